```python
import jax
import jax.numpy as jnp
from jax import lax
import numpy as np

D_MODEL = 1024
BATCH = 4
SEQ = 4096
DEPTH = 1

CTX_LEN = 256
GRID_W = 64
CHUNK = 64
EPS = 1e-6
GLA_HEADS = 4
GLA_DK = D_MODEL // 2
GLA_DV = D_MODEL
GLA_DK_HEAD = GLA_DK // GLA_HEADS
GLA_DV_HEAD = GLA_DV // GLA_HEADS
GLA_RANK = 16
GLA_TAU = 16.0
MLSTM_HEADS = 4
MLSTM_DK = D_MODEL // 2
MLSTM_DV = D_MODEL
MLSTM_DK_HEAD = MLSTM_DK // MLSTM_HEADS
MLSTM_DV_HEAD = MLSTM_DV // MLSTM_HEADS
MLSTM_CONV = 3
D_FF = -(-8 * D_MODEL // (3 * 256)) * 256
IN_SIZES = (GLA_DK, GLA_DK, GLA_DV, GLA_DV, 2 * GLA_RANK,
            MLSTM_DK, MLSTM_DK, MLSTM_DV, MLSTM_DV, 4 * MLSTM_HEADS,
            2 * D_MODEL)
N_IN = sum(IN_SIZES)

kernel_name = 'hybrid_gla_mlstm_prefix_dit_block'


def _rms_norm(x, g):
    xf = x.astype(jnp.float32)
    y = xf * lax.rsqrt(jnp.mean(xf * xf, axis=-1, keepdims=True) + EPS)
    return (y * g.astype(jnp.float32)).astype(x.dtype)


def _modulate(h, shift, scale):
    return h * (1 + scale) + shift


def _swiglu(u, w_in, w_out):
    a, b = jnp.split(u @ w_in, 2, axis=-1)
    return (jax.nn.silu(a) * b) @ w_out


def _split_cols(a):
    idx = [int(s) for s in np.cumsum(IN_SIZES)[:-1]]
    return jnp.split(a, idx, axis=-1)


def _heads(a, n):
    B, T, C = a.shape
    return a.reshape(B, T, n, C // n).transpose(0, 2, 1, 3)


def _head_rms_norm(h, g, dtype):
    h = h * lax.rsqrt(jnp.mean(h * h, axis=-1, keepdims=True) + EPS)
    B, H, T, d = h.shape
    return (h.transpose(0, 2, 1, 3).reshape(B, T, H * d) * g.astype(jnp.float32)).astype(dtype)


def _to_colmajor(a):
    B, T, C = a.shape
    rows = T // GRID_W
    return a.reshape(B, rows, GRID_W, C).swapaxes(1, 2).reshape(B, T, C)


def _from_colmajor(a):
    B, T, C = a.shape
    rows = T // GRID_W
    return a.reshape(B, GRID_W, rows, C).swapaxes(1, 2).reshape(B, T, C)


def _centred_conv(a, w, b):
    pad = MLSTM_CONV // 2
    T = a.shape[1]
    ap = jnp.pad(a, ((0, 0), (pad, pad), (0, 0)))
    return sum(ap[:, j:j + T] * w[j] for j in range(MLSTM_CONV)) + b


def _chunks(a):
    B, H, T = a.shape[:3]
    a = a.reshape((B, H, T // CHUNK, CHUNK) + a.shape[3:])
    return jnp.moveaxis(a, 2, 0)


def _unchunk(o, T):
    o = jnp.moveaxis(o, 0, 2)
    return o.reshape(o.shape[:2] + (T,) + o.shape[4:])


def _gla_scan(q, k, v, log_a, s0):
    mask = jnp.tril(jnp.ones((CHUNK, CHUNK), dtype=bool))

    def step(s, inp):
        qc, kc, vc, gc = inp
        b = jnp.cumsum(gc, axis=2)
        b_last = b[:, :, -1:, :]
        q_dec = qc * jnp.exp(b)
        scores = jnp.einsum('bhld,bhmd->bhlm', q_dec, kc * jnp.exp(-b))
        scores = jnp.where(mask, scores, 0.0)
        o = (jnp.einsum('bhlm,bhmv->bhlv', scores, vc)
             + jnp.einsum('bhld,bhdv->bhlv', q_dec, s))
        s_new = (jnp.exp(b_last[:, :, 0, :, None]) * s
                 + jnp.einsum('bhld,bhlv->bhdv', kc * jnp.exp(b_last - b), vc))
        return s_new, o

    s_fin, o = lax.scan(step, s0, tuple(_chunks(a) for a in (q, k, v, log_a)))
    return _unchunk(o, q.shape[2]), s_fin


def _mlstm_scan(q, k, v, log_i, log_f, state0):
    mask = jnp.tril(jnp.ones((CHUNK, CHUNK), dtype=bool))

    def step(carry, inp):
        c, nv, m = carry
        qc, kc, vc, ic, fc = inp
        b = jnp.cumsum(fc, axis=-1)
        log_intra = b[..., :, None] - b[..., None, :] + ic[..., None, :]
        log_intra = jnp.where(mask, log_intra, -jnp.inf)
        log_inter = b + m[..., None]
        m_q = jnp.maximum(log_inter, jnp.max(log_intra, axis=-1))
        w_intra = jnp.exp(log_intra - m_q[..., None]) * jnp.einsum('bhld,bhmd->bhlm', qc, kc)
        w_inter = jnp.exp(log_inter - m_q)
        num = (jnp.einsum('bhlm,bhmv->bhlv', w_intra, vc)
               + w_inter[..., None] * jnp.einsum('bhld,bhdv->bhlv', qc, c))
        den = jnp.sum(w_intra, axis=-1) + w_inter * jnp.einsum('bhld,bhd->bhl', qc, nv)
        h = num / jnp.maximum(jnp.abs(den), jnp.exp(-m_q))[..., None]
        b_last = b[..., -1]
        log_key = b_last[..., None] - b + ic
        m_new = jnp.maximum(b_last + m, jnp.max(log_key, axis=-1))
        w_key = jnp.exp(log_key - m_new[..., None])
        decay = jnp.exp(b_last + m - m_new)
        c_new = decay[..., None, None] * c + jnp.einsum('bhl,bhld,bhlv->bhdv', w_key, kc, vc)
        n_new = decay[..., None] * nv + jnp.einsum('bhl,bhld->bhd', w_key, kc)
        return (c_new, n_new, m_new), h

    state, h = lax.scan(step, state0, tuple(_chunks(a) for a in (q, k, v, log_i, log_f)))
    return _unchunk(h, q.shape[2]), state


def _flip_t(a):
    return jnp.flip(a, axis=2)


def _identity(a):
    return a


def _bidirectional(scan_fn, lat_dirs, ctx_dirs, init):
    outs_l, outs_c = [], []
    for d in range(2):
        flip = _flip_t if d == 1 else _identity
        o_c, st = scan_fn(*[flip(a) for a in ctx_dirs[d]], init)
        o_l, _ = scan_fn(*[flip(a) for a in lat_dirs[d]], st)
        outs_l.append(flip(o_l))
        outs_c.append(flip(o_c))
    return outs_l[0] + outs_l[1], outs_c[0] + outs_c[1]


def _gla_inputs(q, k, v, lr, w_up, b_dec):
    B, T, _ = q.shape
    q = _heads(q, GLA_HEADS).astype(jnp.float32) * GLA_DK_HEAD ** -0.5
    k = _heads(k, GLA_HEADS).astype(jnp.float32)
    v = _heads(v, GLA_HEADS).astype(jnp.float32)
    lr = lr.reshape(B, T, 2, GLA_RANK)
    z = jnp.einsum('btdr,drk->dbtk', lr, w_up) + b_dec[:, None, None, :]
    log_a = jax.nn.log_sigmoid(z.astype(jnp.float32)) / GLA_TAU
    return ((q, k, v, _heads(log_a[0], GLA_HEADS)),
            (q, k, v, _heads(log_a[1], GLA_HEADS)))


def _mlstm_inputs(q, k, v, gates, conv_w, conv_b, b_gate):
    qk = jax.nn.silu(_centred_conv(jnp.concatenate([q, k], axis=-1), conv_w, conv_b))
    q, k = jnp.split(qk, 2, axis=-1)
    q = _heads(q, MLSTM_HEADS).astype(jnp.float32)
    k = _heads(k, MLSTM_HEADS).astype(jnp.float32) * MLSTM_DK_HEAD ** -0.5
    v = _heads(v, MLSTM_HEADS).astype(jnp.float32)
    B, T, _ = gates.shape
    g = (gates.reshape(B, T, 4, MLSTM_HEADS) + b_gate).astype(jnp.float32)
    g = g.transpose(2, 0, 3, 1)
    return ((q, k, v, g[0], jax.nn.log_sigmoid(g[1])),
            (q, k, v, g[2], jax.nn.log_sigmoid(g[3])))


def _mixer(u, u_c, w_in, gla_w_up, gla_b_dec, gla_norm_g, conv_w, conv_b, b_gate,
           mlstm_norm_g, w_br_gla, w_br_mlstm, w_out, need_ctx):
    B = u.shape[0]
    p = _split_cols(u @ w_in)
    pc = _split_cols(u_c @ w_in)
    gla_lat = _gla_inputs(p[0], p[1], p[2], p[4], gla_w_up, gla_b_dec)
    gla_ctx = _gla_inputs(pc[0], pc[1], pc[2], pc[4], gla_w_up, gla_b_dec)
    s0 = jnp.zeros((B, GLA_HEADS, GLA_DK_HEAD, GLA_DV_HEAD), jnp.float32)
    o_l, o_c = _bidirectional(_gla_scan, gla_lat, gla_ctx, s0)
    cm = [_to_colmajor(p[j]) for j in (5, 6, 7, 9)]
    m_lat = _mlstm_inputs(*cm, conv_w, conv_b, b_gate)
    m_ctx = _mlstm_inputs(pc[5], pc[6], pc[7], pc[9], conv_w, conv_b, b_gate)
    st0 = (jnp.zeros((B, MLSTM_HEADS, MLSTM_DK_HEAD, MLSTM_DV_HEAD), jnp.float32),
           jnp.zeros((B, MLSTM_HEADS, MLSTM_DK_HEAD), jnp.float32),
           jnp.zeros((B, MLSTM_HEADS), jnp.float32))
    h_l, h_c = _bidirectional(_mlstm_scan, m_lat, m_ctx, st0)

    def merge(pp, o, h):
        y_gla = _head_rms_norm(o, gla_norm_g, u.dtype) * jax.nn.silu(pp[3])
        y_m = h * jax.nn.sigmoid(pp[8])
        gate_gla, gate_m = jnp.split(jax.nn.sigmoid(pp[10]), 2, axis=-1)
        y = gate_gla * (y_gla @ w_br_gla) + gate_m * (y_m @ w_br_mlstm)
        return y @ w_out

    mix = merge(p, o_l, _from_colmajor(_head_rms_norm(h_l, mlstm_norm_g, u.dtype)))
    mix_c = merge(pc, o_c, _head_rms_norm(h_c, mlstm_norm_g, u.dtype)) if need_ctx else None
    return mix, mix_c


def setup_inputs(seed: int = 0) -> dict:
    key = jax.random.key(seed)
    ks = jax.random.split(key, 24)

    def nrm(k, shape, scale):
        return jax.random.normal(k, shape, jnp.float32) * scale

    D, L = D_MODEL, DEPTH
    f_bias = jnp.stack([jnp.zeros((MLSTM_HEADS,), jnp.float32),
                        jnp.linspace(3.0, 6.0, MLSTM_HEADS, dtype=jnp.float32),
                        jnp.zeros((MLSTM_HEADS,), jnp.float32),
                        jnp.linspace(3.0, 6.0, MLSTM_HEADS, dtype=jnp.float32)])
    return {
        'x': nrm(ks[0], (BATCH, SEQ, D), 1.0),
        'c': nrm(ks[1], (BATCH, D), 1.0),
        'ctx': nrm(ks[2], (BATCH, CTX_LEN, D), 1.0),
        'c_ctx': nrm(ks[3], (D,), 1.0),
        'w_ada': nrm(ks[4], (L, D, 6 * D), D ** -0.5),
        'b_ada': nrm(ks[5], (L, 6 * D), 0.01),
        'norm1_g': 1.0 + nrm(ks[6], (L, D), 0.01),
        'w_in': nrm(ks[7], (L, D, N_IN), D ** -0.5),
        'gla_w_up': nrm(ks[8], (L, 2, GLA_RANK, GLA_DK), GLA_RANK ** -0.5),
        'gla_b_dec': nrm(ks[9], (L, 2, GLA_DK), 0.1),
        'gla_norm_g': 1.0 + nrm(ks[10], (L, GLA_DV), 0.01),
        'mlstm_conv_w': nrm(ks[11], (L, MLSTM_CONV, 2 * MLSTM_DK), MLSTM_CONV ** -0.5),
        'mlstm_conv_b': nrm(ks[12], (L, 2 * MLSTM_DK), 0.01),
        'mlstm_b_gate': nrm(ks[13], (L, 4, MLSTM_HEADS), 0.1) + f_bias,
        'mlstm_norm_g': 1.0 + nrm(ks[14], (L, MLSTM_DV), 0.01),
        'w_br_gla': nrm(ks[15], (L, GLA_DV, D), GLA_DV ** -0.5),
        'w_br_mlstm': nrm(ks[16], (L, MLSTM_DV, D), MLSTM_DV ** -0.5),
        'w_out': nrm(ks[17], (L, D, D), D ** -0.5),
        'norm2_g': 1.0 + nrm(ks[18], (L, D), 0.01),
        'w_ffn_in': nrm(ks[19], (L, D, 2 * D_FF), D ** -0.5),
        'w_ffn_out': nrm(ks[20], (L, D_FF, D), D_FF ** -0.5),
        'final_g': 1.0 + nrm(ks[21], (D,), 0.01),
    }


def reference(x, c, ctx, c_ctx, w_ada, b_ada, norm1_g, w_in, gla_w_up, gla_b_dec,
              gla_norm_g, mlstm_conv_w, mlstm_conv_b, mlstm_b_gate, mlstm_norm_g,
              w_br_gla, w_br_mlstm, w_out, norm2_g, w_ffn_in, w_ffn_out, final_g):
    h_ctx = ctx
    silu_c = jax.nn.silu(c)
    silu_cc = jax.nn.silu(c_ctx)
    for i in range(DEPTH):
        need_ctx = i + 1 < DEPTH
        mod = (silu_c @ w_ada[i] + b_ada[i])[:, None, :]
        mod_c = silu_cc @ w_ada[i] + b_ada[i]
        sh1, sc1, g1, sh2, sc2, g2 = jnp.split(mod, 6, axis=-1)
        sh1c, sc1c, g1c, sh2c, sc2c, g2c = jnp.split(mod_c, 6, axis=-1)
        u = _modulate(_rms_norm(x, norm1_g[i]), sh1, sc1)
        u_c = _modulate(_rms_norm(h_ctx, norm1_g[i]), sh1c, sc1c)
        mix, mix_c = _mixer(u, u_c, w_in[i], gla_w_up[i], gla_b_dec[i], gla_norm_g[i],
                            mlstm_conv_w[i], mlstm_conv_b[i], mlstm_b_gate[i], mlstm_norm_g[i],
                            w_br_gla[i], w_br_mlstm[i], w_out[i], need_ctx)
        x = x + g1 * mix
        x = x + g2 * _swiglu(_modulate(_rms_norm(x, norm2_g[i]), sh2, sc2),
                             w_ffn_in[i], w_ffn_out[i])
        if need_ctx:
            h_ctx = h_ctx + g1c * mix_c
            h_ctx = h_ctx + g2c * _swiglu(_modulate(_rms_norm(h_ctx, norm2_g[i]), sh2c, sc2c),
                                          w_ffn_in[i], w_ffn_out[i])
    return _rms_norm(x, final_g)
```

```python
import functools

import jax
import jax.numpy as jnp
from jax import lax
from jax.experimental import pallas as pl
from jax.experimental.pallas import tpu as pltpu

D = 1024
BATCH = 4
SEQ = 4096
CTX = 256
GRID_W = 64
CHUNK = 64
EPS = 1e-6
HEADS = 4
DK = 128
DV = 256
RANK = 16
TAU = 16.0
D_FF = 2816
N_MOD = 6 * D
TAIL = 256
GATE_LANE = 32

F32 = jnp.float32
BF16 = jnp.bfloat16
VMEM_LIMIT = 56 * 1024 * 1024


def _dot(a, b):
    return jnp.dot(a, b, preferred_element_type=F32)


def _dot_nt(a, b):
    return lax.dot_general(a, b, (((1,), (1,)), ((), ())), preferred_element_type=F32)


def _dot_tn(a, b):
    return lax.dot_general(a, b, (((0,), (0,)), ((), ())), preferred_element_type=F32)


def _sigmoid(x):
    return 1.0 / (1.0 + jnp.exp(-x))


def _log_sigmoid(x):
    return jnp.minimum(x, 0.0) - jnp.log1p(jnp.exp(-jnp.abs(x)))


def _cumsum_rows(tri, g):
    g1 = g.astype(BF16)
    r1 = g - g1.astype(F32)
    g2 = r1.astype(BF16)
    g3 = (r1 - g2.astype(F32)).astype(BF16)
    return _dot(tri, g1) + _dot(tri, g2) + _dot(tri, g3)


def _resident(shape):
    n = len(shape)
    return pl.BlockSpec(shape, lambda *_: (0,) * n, pipeline_mode=pl.Buffered(1))


def _params(sem):
    return pltpu.CompilerParams(dimension_semantics=sem, vmem_limit_bytes=VMEM_LIMIT)


def _ada_kernel(c_ref, w_ref, b_ref, o_ref):
    cv = c_ref[...]
    s = (cv * _sigmoid(cv)).astype(BF16)
    o_ref[...] = _dot(s, w_ref[...].astype(BF16)) + b_ref[...]


def _ada(cvec, w_ada, b_ada):
    tn = 1024
    return pl.pallas_call(
        _ada_kernel,
        out_shape=jax.ShapeDtypeStruct((8, N_MOD), F32),
        grid=(N_MOD // tn,),
        in_specs=[pl.BlockSpec((8, D), lambda j: (0, 0)),
                  pl.BlockSpec((D, tn), lambda j: (0, j)),
                  pl.BlockSpec((1, tn), lambda j: (0, j))],
        out_specs=pl.BlockSpec((8, tn), lambda j: (0, j)),
        compiler_params=_params(("arbitrary",)),
        name="adaln",
    )(cvec, w_ada, b_ada)


def _inproj_kernel(x_ref, sh_ref, sc_ref, g_ref, wg_ref, wm_ref, wo_ref, wt_ref,
                   og_ref, om_ref, oo_ref, ot_ref):
    x = x_ref[...]
    y = x * lax.rsqrt(jnp.mean(x * x, axis=-1, keepdims=True) + EPS) * g_ref[...]
    u = (y * (1.0 + sc_ref[...]) + sh_ref[...]).astype(BF16)
    for w_ref, o_ref in ((wg_ref, og_ref), (wm_ref, om_ref), (wo_ref, oo_ref)):
        for j in range(w_ref.shape[1] // 1024):
            cs = slice(j * 1024, (j + 1) * 1024)
            o_ref[:, cs] = _dot(u, w_ref[:, cs]).astype(BF16)
    ot_ref[...] = _dot(u, wt_ref[...])


def _inproj(x2, mod3, row_of_block, norm_g, wg, wm, wo, wt, tm):
    m = x2.shape[0]
    return pl.pallas_call(
        _inproj_kernel,
        out_shape=(jax.ShapeDtypeStruct((m, 2048), BF16),
                   jax.ShapeDtypeStruct((m, 2048), BF16),
                   jax.ShapeDtypeStruct((m, 4096), BF16),
                   jax.ShapeDtypeStruct((m, TAIL), F32)),
        grid=(m // tm,),
        in_specs=[pl.BlockSpec((tm, D), lambda i: (i, 0)),
                  pl.BlockSpec((None, 1, D), lambda i: (row_of_block(i), 0, 0)),
                  pl.BlockSpec((None, 1, D), lambda i: (row_of_block(i), 0, 1)),
                  pl.BlockSpec((1, D), lambda i: (0, 0)),
                  _resident((D, 2048)), _resident((D, 2048)), _resident((D, 4096)),
                  _resident((D, TAIL))],
        out_specs=(pl.BlockSpec((tm, 2048), lambda i: (i, 0)),
                   pl.BlockSpec((tm, 2048), lambda i: (i, 0)),
                   pl.BlockSpec((tm, 4096), lambda i: (i, 0)),
                   pl.BlockSpec((tm, TAIL), lambda i: (i, 0))),
        compiler_params=_params(("arbitrary",)),
        name="inproj",
    )(x2, mod3, mod3, norm_g, wg, wm, wo, wt)


def _gla_kernel(pf_ref, pb_ref, tf_ref, tb_ref, wup_ref, bdec_ref, s0_ref, *out_refs, emit_out):
    if emit_out:
        of_ref, ob_ref, st_ref = out_refs
    else:
        (st_ref,) = out_refs
        of_ref = ob_ref = None

    @pl.when(pl.program_id(1) == 0)
    def _():
        st_ref[...] = s0_ref[...]

    row = lax.broadcasted_iota(jnp.int32, (CHUNK, CHUNK), 0)
    col = lax.broadcasted_iota(jnp.int32, (CHUNK, CHUNK), 1)
    for d, (p_ref, t_ref, o_ref) in enumerate(((pf_ref, tf_ref, of_ref), (pb_ref, tb_ref, ob_ref))):
        causal = (col <= row) if d == 0 else (col >= row)
        tri = jnp.where(causal, 1.0, 0.0).astype(BF16)
        z = _dot(t_ref[...].astype(BF16), wup_ref[d]) + bdec_ref[d]
        g = _log_sigmoid(z) * (1.0 / TAU)
        b = _cumsum_rows(tri, g)
        b_last = b[CHUNK - 1:CHUNK, :] if d == 0 else b[0:1, :]
        q = p_ref[:, 0:512].astype(F32) * (DK ** -0.5)
        k = p_ref[:, 512:1024].astype(F32)
        qd = (q * jnp.exp(b)).astype(BF16)
        kd = (k * jnp.exp(-b)).astype(BF16)
        kl = (k * jnp.exp(b_last - b)).astype(BF16)
        dec = jnp.exp(b_last)
        for h in range(HEADS):
            ks = slice(h * DK, (h + 1) * DK)
            vs = slice(1024 + h * DV, 1024 + (h + 1) * DV)
            v = p_ref[:, vs]
            st = st_ref[d * HEADS + h]
            if emit_out:
                s = jnp.where(causal, _dot_nt(qd[:, ks], kd[:, ks]), 0.0).astype(BF16)
                o = _dot(s, v) + _dot_nt(qd[:, ks], st.astype(BF16))
                o_ref[:, h * DV:(h + 1) * DV] = o
            st_ref[d * HEADS + h] = st * dec[:, ks] + _dot_tn(v, kl[:, ks])


def _gla_scan(pg, tail, wup, bdec, s0, emit_out):
    bn, tn, _ = pg.shape
    nc = tn // CHUNK
    fwd = lambda b, i: (b, i, 0)
    bwd = lambda b, i: (b, nc - 1 - i, 0)
    st_shape = jax.ShapeDtypeStruct((bn, 2 * HEADS, DV, DK), F32)
    st_spec = pl.BlockSpec((None, 2 * HEADS, DV, DK), lambda b, i: (b, 0, 0, 0))
    o_shape = jax.ShapeDtypeStruct((bn, tn, HEADS * DV), F32)
    out_shape = (o_shape, o_shape, st_shape) if emit_out else (st_shape,)
    out_specs = ((pl.BlockSpec((None, CHUNK, HEADS * DV), fwd),
                  pl.BlockSpec((None, CHUNK, HEADS * DV), bwd), st_spec) if emit_out else (st_spec,))
    return pl.pallas_call(
        functools.partial(_gla_kernel, emit_out=emit_out),
        out_shape=out_shape,
        grid=(bn, nc),
        in_specs=[pl.BlockSpec((None, CHUNK, 2048), fwd),
                  pl.BlockSpec((None, CHUNK, 2048), bwd),
                  pl.BlockSpec((None, CHUNK, 128), fwd),
                  pl.BlockSpec((None, CHUNK, 128), bwd),
                  pl.BlockSpec((2, 128, 512), lambda b, i: (0, 0, 0)),
                  pl.BlockSpec((2, 1, 512), lambda b, i: (0, 0, 0)),
                  st_spec],
        out_specs=out_specs,
        compiler_params=_params(("arbitrary", "arbitrary")),
        name="gla_scan_lat" if emit_out else "gla_scan_ctx",
    )(pg, pg, tail, tail, wup, bdec, s0)


def _mlstm_kernel(pf_ref, pb_ref, hpf_ref, hnf_ref, hpb_ref, hnb_ref, tf_ref, tb_ref,
                  cw_ref, cb_ref, bg_ref, c0_ref, n0_ref, m0_ref, *out_refs, emit_out, nc):
    if emit_out:
        of_ref, ob_ref, c_ref, n_ref, m_ref = out_refs
    else:
        c_ref, n_ref, m_ref = out_refs
        of_ref = ob_ref = None
    i = pl.program_id(1)

    @pl.when(i == 0)
    def _():
        c_ref[...] = c0_ref[...]
        n_ref[...] = n0_ref[...]
        m_ref[...] = m0_ref[...]

    row = lax.broadcasted_iota(jnp.int32, (CHUNK, CHUNK), 0)
    col = lax.broadcasted_iota(jnp.int32, (CHUNK, CHUNK), 1)
    rowq = lax.broadcasted_iota(jnp.int32, (CHUNK, 2 * HEADS * DK), 0)
    dirs = ((pf_ref, hpf_ref, hnf_ref, tf_ref, of_ref, i),
            (pb_ref, hpb_ref, hnb_ref, tb_ref, ob_ref, nc - 1 - i))
    for d, (p_ref, hp_ref, hn_ref, t_ref, o_ref, cidx) in enumerate(dirs):
        causal = (col <= row) if d == 0 else (col >= row)
        tri = jnp.where(causal, 1.0, 0.0).astype(BF16)
        x = p_ref[:, 0:1024].astype(F32)
        prev_row = jnp.where(cidx > 0, hp_ref[7:8, :].astype(F32), 0.0)
        next_row = jnp.where(cidx < nc - 1, hn_ref[0:1, :].astype(F32), 0.0)
        xp = jnp.where(rowq == 0, prev_row, pltpu.roll(x, 1, axis=0))
        xn = jnp.where(rowq == CHUNK - 1, next_row, pltpu.roll(x, CHUNK - 1, axis=0))
        conv = xp * cw_ref[0:1, :] + x * cw_ref[1:2, :] + xn * cw_ref[2:3, :] + cb_ref[...]
        qk = conv * _sigmoid(conv)
        q = qk[:, 0:512]
        k = qk[:, 512:1024] * (DK ** -0.5)
        qb = q.astype(BF16)
        ga = t_ref[:, 0:128] + bg_ref[:, 0:128]
        gb = t_ref[:, 128:256] + bg_ref[:, 128:256]
        bc = _cumsum_rows(tri, _log_sigmoid(gb))
        rt = jnp.concatenate([ga - bc, jnp.zeros((CHUNK, 128), F32)], axis=0).T
        for h in range(HEADS):
            idx = d * HEADS + h
            lane = GATE_LANE + 8 * d + h
            ks = slice(h * DK, (h + 1) * DK)
            v = p_ref[:, 1024 + h * DV:1024 + (h + 1) * DV]
            bcol = bc[:, lane:lane + 1]
            icol = ga[:, lane:lane + 1]
            rrow = rt[lane:lane + 1, 0:CHUNK]
            mval = m_ref[idx:idx + 1, 0:1]
            nvec = n_ref[idx:idx + 1, :]
            cmat = c_ref[idx]
            log_intra = jnp.where(causal, bcol + rrow, -jnp.inf)
            log_inter = bcol + mval
            b_last = bcol[CHUNK - 1:CHUNK, :] if d == 0 else bcol[0:1, :]
            if emit_out:
                m_q = jnp.maximum(log_inter, jnp.max(log_intra, axis=1, keepdims=True))
                w_intra = jnp.exp(log_intra - m_q) * _dot_nt(qb[:, ks], k[:, ks].astype(BF16))
                w_inter = jnp.exp(log_inter - m_q)
                num = _dot(w_intra.astype(BF16), v) + w_inter * _dot(qb[:, ks], cmat.astype(BF16))
                den = (jnp.sum(w_intra, axis=1, keepdims=True)
                       + w_inter * jnp.sum(q[:, ks] * nvec, axis=1, keepdims=True))
                o_ref[:, h * DV:(h + 1) * DV] = num / jnp.maximum(jnp.abs(den), jnp.exp(-m_q))
            log_key = b_last - bcol + icol
            m_new = jnp.maximum(b_last + mval, jnp.max(log_key, axis=0, keepdims=True))
            kw = k[:, ks] * jnp.exp(log_key - m_new)
            decay = jnp.exp(b_last + mval - m_new)
            c_ref[idx] = decay * cmat + _dot_tn(kw.astype(BF16), v)
            n_ref[idx:idx + 1, :] = decay * nvec + jnp.sum(kw, axis=0, keepdims=True)
            m_ref[idx:idx + 1, :] = jnp.broadcast_to(m_new, (1, 128))


def _mlstm_scan(pm_v, tail_v, colmajor, conv_w, conv_b, bgate, c0, n0, m0, emit_out):
    bn = pm_v.shape[0]
    if colmajor:
        nc = GRID_W
        blk = lambda c: (lambda b, i: (b, 0, c(i)))
        prev = lambda c: (lambda b, i: (b, 7, 2 * jnp.maximum(c(i) - 1, 0)))
        nxt = lambda c: (lambda b, i: (b, 0, 2 * jnp.minimum(c(i) + 1, nc - 1)))
    else:
        nc = pm_v.shape[1] // CHUNK
        blk = lambda c: (lambda b, i: (b, c(i), 0))
        prev = lambda c: (lambda b, i: (b, jnp.maximum(8 * c(i) - 1, 0), 0))
        nxt = lambda c: (lambda b, i: (b, jnp.minimum(8 * (c(i) + 1), 8 * nc - 1), 0))
    cf = lambda i: i
    cb = lambda i: nc - 1 - i
    c_shape = jax.ShapeDtypeStruct((bn, 2 * HEADS, DK, DV), F32)
    v_shape = jax.ShapeDtypeStruct((bn, 2 * HEADS, 128), F32)
    c_spec = pl.BlockSpec((None, 2 * HEADS, DK, DV), lambda b, i: (b, 0, 0, 0))
    v_spec = pl.BlockSpec((None, 2 * HEADS, 128), lambda b, i: (b, 0, 0))
    if emit_out:
        o_shape = jax.ShapeDtypeStruct((bn, GRID_W, GRID_W * HEADS * DV), F32)
        out_shape = (o_shape, o_shape, c_shape, v_shape, v_shape)
        out_specs = (pl.BlockSpec((None, CHUNK, HEADS * DV), blk(cf)),
                     pl.BlockSpec((None, CHUNK, HEADS * DV), blk(cb)), c_spec, v_spec, v_spec)
    else:
        out_shape = (c_shape, v_shape, v_shape)
        out_specs = (c_spec, v_spec, v_spec)
    return pl.pallas_call(
        functools.partial(_mlstm_kernel, emit_out=emit_out, nc=nc),
        out_shape=out_shape,
        grid=(bn, nc),
        in_specs=[pl.BlockSpec((None, CHUNK, 2048), blk(cf)),
                  pl.BlockSpec((None, CHUNK, 2048), blk(cb)),
                  pl.BlockSpec((None, 8, 1024), prev(cf)),
                  pl.BlockSpec((None, 8, 1024), nxt(cf)),
                  pl.BlockSpec((None, 8, 1024), prev(cb)),
                  pl.BlockSpec((None, 8, 1024), nxt(cb)),
                  pl.BlockSpec((None, CHUNK, TAIL), blk(cf)),
                  pl.BlockSpec((None, CHUNK, TAIL), blk(cb)),
                  pl.BlockSpec((3, 1024), lambda b, i: (0, 0)),
                  pl.BlockSpec((1, 1024), lambda b, i: (0, 0)),
                  pl.BlockSpec((1, TAIL), lambda b, i: (0, 0)),
                  c_spec, v_spec, v_spec],
        out_specs=out_specs,
        compiler_params=_params(("arbitrary", "arbitrary")),
        name="mlstm_scan_lat" if emit_out else "mlstm_scan_ctx",
    )(pm_v, pm_v, pm_v, pm_v, pm_v, pm_v, tail_v, tail_v, conv_w, conv_b, bgate, c0, n0, m0)


def _head_norm(o, g):
    parts = []
    for h in range(HEADS):
        oh = o[:, h * DV:(h + 1) * DV]
        parts.append(oh * lax.rsqrt(jnp.mean(oh * oh, axis=-1, keepdims=True) + EPS))
    return jnp.concatenate(parts, axis=-1) * g


def _merge_kernel(of_ref, ob_ref, hf_ref, hb_ref, po_ref, x_ref, g1_ref, gg_ref, gm_ref,
                  wbg_ref, wbm_ref, wo_ref, o_ref):
    og = po_ref[:, 0:1024].astype(F32)
    y_gla = _head_norm(of_ref[...] + ob_ref[...], gg_ref[...]) * (og * _sigmoid(og))
    y_m = _head_norm(hf_ref[...] + hb_ref[...], gm_ref[...]) * _sigmoid(po_ref[:, 1024:2048].astype(F32))
    gate_g = _sigmoid(po_ref[:, 2048:3072].astype(F32))
    gate_m = _sigmoid(po_ref[:, 3072:4096].astype(F32))
    y = (gate_g * _dot(y_gla.astype(BF16), wbg_ref[...])
         + gate_m * _dot(y_m.astype(BF16), wbm_ref[...]))
    mix = _dot(y.astype(BF16), wo_ref[...])
    o_ref[...] = x_ref[...] + g1_ref[...] * mix


def _merge(of, ob, hf, hb, po, x2, mod3, gg, gm, wbg, wbm, wo, tm):
    m = x2.shape[0]
    per_b = SEQ // tm
    tok = lambda i: (i, 0)
    return pl.pallas_call(
        _merge_kernel,
        out_shape=jax.ShapeDtypeStruct((m, D), F32),
        grid=(m // tm,),
        in_specs=[pl.BlockSpec((tm, D), tok), pl.BlockSpec((tm, D), tok),
                  pl.BlockSpec((tm, D), tok), pl.BlockSpec((tm, D), tok),
                  pl.BlockSpec((tm, 4096), tok), pl.BlockSpec((tm, D), tok),
                  pl.BlockSpec((None, 1, D), lambda i: (i // per_b, 0, 2)),
                  pl.BlockSpec((1, D), lambda i: (0, 0)), pl.BlockSpec((1, D), lambda i: (0, 0)),
                  _resident((D, D)), _resident((D, D)), _resident((D, D))],
        out_specs=pl.BlockSpec((tm, D), tok),
        compiler_params=_params(("arbitrary",)),
        name="merge",
    )(of, ob, hf, hb, po, x2, mod3, gg, gm, wbg, wbm, wo)


FF_TILE = 1408


def _ffn_kernel(x_ref, sh_ref, sc_ref, g2_ref, ng_ref, fg_ref, wi_ref, wo_ref, o_ref):
    x = x_ref[...]
    y = x * lax.rsqrt(jnp.mean(x * x, axis=-1, keepdims=True) + EPS) * ng_ref[...]
    u = (y * (1.0 + sc_ref[...]) + sh_ref[...]).astype(BF16)
    acc = None
    for j in range(D_FF // FF_TILE):
        a = _dot(u, wi_ref[:, j * FF_TILE:(j + 1) * FF_TILE])
        b = _dot(u, wi_ref[:, D_FF + j * FF_TILE:D_FF + (j + 1) * FF_TILE])
        hid = (a * _sigmoid(a) * b).astype(BF16)
        part = _dot(hid, wo_ref[j * FF_TILE:(j + 1) * FF_TILE, :])
        acc = part if acc is None else acc + part
    x2 = x + g2_ref[...] * acc
    o_ref[...] = x2 * lax.rsqrt(jnp.mean(x2 * x2, axis=-1, keepdims=True) + EPS) * fg_ref[...]


def _ffn(x1, mod3, ng, fg, wi, wo, tm):
    m = x1.shape[0]
    per_b = SEQ // tm
    tok = lambda i: (i, 0)
    modspec = lambda c: pl.BlockSpec((None, 1, D), lambda i: (i // per_b, 0, c))
    return pl.pallas_call(
        _ffn_kernel,
        out_shape=jax.ShapeDtypeStruct((m, D), F32),
        grid=(m // tm,),
        in_specs=[pl.BlockSpec((tm, D), tok), modspec(3), modspec(4), modspec(5),
                  pl.BlockSpec((1, D), lambda i: (0, 0)), pl.BlockSpec((1, D), lambda i: (0, 0)),
                  _resident((D, 2 * D_FF)), _resident((D_FF, D))],
        out_specs=pl.BlockSpec((tm, D), tok),
        compiler_params=_params(("arbitrary",)),
        name="ffn",
    )(x1, mod3, mod3, mod3, ng, fg, wi, wo)


def _split_w_in(w_in):
    gq, gk, gv, gog, lr, mq, mk, mv, mog, gates, bg = jnp.split(
        w_in, [512, 1024, 2048, 3072, 3104, 3616, 4128, 5152, 6176, 6192], axis=1)
    z = lambda n: jnp.zeros((D, n), w_in.dtype)
    wg = jnp.concatenate([gq, gk, gv], axis=1).astype(BF16)
    wm = jnp.concatenate([mq, mk, mv], axis=1).astype(BF16)
    wo = jnp.concatenate([gog, mog, bg], axis=1).astype(BF16)
    wt = jnp.concatenate([lr, gates, z(128 - 48),
                          z(GATE_LANE), gates[:, 4:8], z(4), gates[:, 12:16], z(128 - GATE_LANE - 12)],
                         axis=1).astype(BF16)
    return wg, wm, wo, wt


def kernel(x, c, ctx, c_ctx, w_ada, b_ada, norm1_g, w_in, gla_w_up, gla_b_dec, gla_norm_g,
           mlstm_conv_w, mlstm_conv_b, mlstm_b_gate, mlstm_norm_g, w_br_gla, w_br_mlstm, w_out,
           norm2_g, w_ffn_in, w_ffn_out, final_g):
    bsz = x.shape[0]
    row = lambda a: a.reshape(1, -1)

    cvec = jnp.concatenate([c, c_ctx[None, :], jnp.zeros((8 - bsz - 1, D), F32)], axis=0)
    mod3 = _ada(cvec, w_ada[0], row(b_ada[0])).reshape(8, 1, N_MOD)

    wg, wm, wo, wt = _split_w_in(w_in[0])
    x2 = x.reshape(bsz * SEQ, D)
    ctx2 = ctx.reshape(bsz * CTX, D)
    tm = 256
    g1n = row(norm1_g[0])
    pg, pm, po, tail = _inproj(x2, mod3, lambda i: i // (SEQ // tm), g1n, wg, wm, wo, wt, tm)
    pg_c, pm_c, _, tail_c = _inproj(ctx2, mod3, lambda i: bsz, g1n, wg, wm, wo, wt, tm)

    wup = jnp.zeros((2, 128, HEADS * DK), F32)
    wup = wup.at[0, 0:RANK].set(gla_w_up[0, 0]).at[1, RANK:2 * RANK].set(gla_w_up[0, 1]).astype(BF16)
    bdec = gla_b_dec[0].reshape(2, 1, HEADS * DK)
    s0 = jnp.zeros((bsz, 2 * HEADS, DV, DK), F32)
    (s_ctx,) = _gla_scan(pg_c.reshape(bsz, CTX, 2048), tail_c.reshape(bsz, CTX, TAIL),
                         wup, bdec, s0, emit_out=False)
    o_f, o_b, _ = _gla_scan(pg.reshape(bsz, SEQ, 2048), tail.reshape(bsz, SEQ, TAIL),
                            wup, bdec, s_ctx, emit_out=True)

    bgate = mlstm_b_gate[0].reshape(1, 16)
    zg = lambda n: jnp.zeros((1, n), F32)
    bg2 = jnp.concatenate([zg(GATE_LANE), bgate, zg(128 - GATE_LANE - 16),
                           zg(GATE_LANE), bgate[:, 4:8], zg(4), bgate[:, 12:16],
                           zg(128 - GATE_LANE - 12)], axis=1)
    conv_w = mlstm_conv_w[0]
    conv_b = row(mlstm_conv_b[0])
    c0 = jnp.zeros((bsz, 2 * HEADS, DK, DV), F32)
    v0 = jnp.zeros((bsz, 2 * HEADS, 128), F32)
    c_ctx_s, n_ctx_s, m_ctx_s = _mlstm_scan(
        pm_c.reshape(bsz, CTX, 2048), tail_c.reshape(bsz, CTX, TAIL), False,
        conv_w, conv_b, bg2, c0, v0, v0, emit_out=False)
    rows = SEQ // GRID_W
    h_f, h_b, _, _, _ = _mlstm_scan(
        pm.reshape(bsz, rows, GRID_W * 2048), tail.reshape(bsz, rows, GRID_W * TAIL), True,
        conv_w, conv_b, bg2, c_ctx_s, n_ctx_s, m_ctx_s, emit_out=True)

    x1 = _merge(o_f.reshape(bsz * SEQ, D), o_b.reshape(bsz * SEQ, D),
                h_f.reshape(bsz * SEQ, D), h_b.reshape(bsz * SEQ, D), po, x2, mod3,
                row(gla_norm_g[0]), row(mlstm_norm_g[0]),
                w_br_gla[0].astype(BF16), w_br_mlstm[0].astype(BF16), w_out[0].astype(BF16), tm)
    out = _ffn(x1, mod3, row(norm2_g[0]), row(final_g),
               w_ffn_in[0].astype(BF16), w_ffn_out[0].astype(BF16), tm)
    return out.reshape(bsz, SEQ, D)
```

```python
import functools

import jax
import jax.numpy as jnp
from jax import lax
from jax.experimental import pallas as pl
from jax.experimental.pallas import tpu as pltpu

D = 1024
SEQ = 4096
CTX = 256
GRID_W = 64
EPS = 1e-6
HEADS = 4
DK = 128
DV = 256
RANK = 16
TAU = 16.0
D_FF = 2816
N_MOD = 6 * D
GATE_LANE = 32
STEP = 256
GLA_CHUNK = 128
COL_BLOCK = 8

F32 = jnp.float32
BF16 = jnp.bfloat16
VMEM_LIMIT = 56 * 1024 * 1024


def _dot(a, b):
    return jnp.dot(a, b, preferred_element_type=F32)


def _dot_nt(a, b):
    return lax.dot_general(a, b, (((1,), (1,)), ((), ())), preferred_element_type=F32)


def _dot_tn(a, b):
    return lax.dot_general(a, b, (((0,), (0,)), ((), ())), preferred_element_type=F32)


def _sigmoid(x):
    return 1.0 / (1.0 + jnp.exp(-x))


def _log_sigmoid(x):
    return jnp.minimum(x, 0.0) - jnp.log1p(jnp.exp(-jnp.abs(x)))


def _cumsum_rows(tri, g):
    g1 = g.astype(BF16)
    g2 = (g - g1.astype(F32)).astype(BF16)
    return _dot(tri, g1) + _dot(tri, g2)


def _tri(n, d):
    row = lax.broadcasted_iota(jnp.int32, (n, n), 0)
    col = lax.broadcasted_iota(jnp.int32, (n, n), 1)
    causal = (col <= row) if d == 0 else (col >= row)
    return causal, jnp.where(causal, 1.0, 0.0).astype(BF16)


def _resident(shape):
    n = len(shape)
    return pl.BlockSpec(shape, lambda *_: (0,) * n, pipeline_mode=pl.Buffered(1))


def _params(sem):
    return pltpu.CompilerParams(dimension_semantics=sem, vmem_limit_bytes=VMEM_LIMIT)


def _ada_kernel(c_ref, w_ref, b_ref, o_ref):
    cv = c_ref[...]
    s = (cv * _sigmoid(cv)).astype(BF16)
    o_ref[...] = _dot(s, w_ref[...].astype(BF16)) + b_ref[...]


def _ada(cvec, w_ada, b_ada):
    tn = 1024
    return pl.pallas_call(
        _ada_kernel,
        out_shape=jax.ShapeDtypeStruct((8, N_MOD), F32),
        grid=(N_MOD // tn,),
        in_specs=[pl.BlockSpec((8, D), lambda j: (0, 0)),
                  pl.BlockSpec((D, tn), lambda j: (0, j)),
                  pl.BlockSpec((1, tn), lambda j: (0, j))],
        out_specs=pl.BlockSpec((8, tn), lambda j: (0, j)),
        compiler_params=_params(("arbitrary",)),
        name="adaln",
    )(cvec, w_ada, b_ada)


def _inproj_kernel(x_ref, sh_ref, sc_ref, g_ref, *refs, colmajor):
    n_out = len(refs) // 2
    if colmajor:
        x = jnp.concatenate([x_ref[:, cl, :] for cl in range(COL_BLOCK)], axis=0)
    else:
        x = x_ref[...]
    y = x * lax.rsqrt(jnp.mean(x * x, axis=-1, keepdims=True) + EPS) * g_ref[...]
    u = (y * (1.0 + sc_ref[...]) + sh_ref[...]).astype(BF16)
    for w_ref, o_ref in zip(refs[:n_out], refs[n_out:]):
        n = w_ref.shape[1]
        for j in range(0, n, 1024):
            cs = slice(j, min(j + 1024, n))
            o_ref[:, cs] = _dot(u, w_ref[:, cs]).astype(o_ref.dtype)


def _inproj(xv, mod3, mod_row, norm_g, weights, out_dtypes, tm, colmajor):
    if colmajor:
        bsz = xv.shape[0]
        m = bsz * SEQ
        per_b = GRID_W // COL_BLOCK
        grid = (bsz * per_b,)
        x_spec = pl.BlockSpec((None, GRID_W, COL_BLOCK, D), lambda i: (i // per_b, 0, i % per_b, 0))
    else:
        m = xv.shape[0]
        grid = (m // tm,)
        x_spec = pl.BlockSpec((tm, D), lambda i: (i, 0))
    return pl.pallas_call(
        functools.partial(_inproj_kernel, colmajor=colmajor),
        out_shape=tuple(jax.ShapeDtypeStruct((m, w.shape[1]), dt) for w, dt in zip(weights, out_dtypes)),
        grid=grid,
        in_specs=[x_spec,
                  pl.BlockSpec((None, 1, D), lambda i: (mod_row(i), 0, 0)),
                  pl.BlockSpec((None, 1, D), lambda i: (mod_row(i), 0, 1)),
                  pl.BlockSpec((1, D), lambda i: (0, 0))] + [_resident(w.shape) for w in weights],
        out_specs=tuple(pl.BlockSpec((tm, w.shape[1]), lambda i: (i, 0)) for w in weights),
        compiler_params=_params(("arbitrary",)),
        name="inproj_cm" if colmajor else "inproj",
    )(xv, mod3, mod3, norm_g, *weights)


def _gla_kernel(pf_ref, pb_ref, tf_ref, tb_ref, wup_ref, bdec_ref, s0_ref, *out_refs, emit_out):
    if emit_out:
        of_ref, ob_ref, st_ref = out_refs
    else:
        (st_ref,) = out_refs
        of_ref = ob_ref = None

    @pl.when(pl.program_id(1) == 0)
    def _():
        st_ref[...] = s0_ref[...]

    lc = GLA_CHUNK
    n_sub = STEP // lc
    for d, (p_ref, t_ref, o_ref) in enumerate(((pf_ref, tf_ref, of_ref), (pb_ref, tb_ref, ob_ref))):
        causal, tri = _tri(lc, d)
        z = _dot(t_ref[...].astype(BF16), wup_ref[d]) + bdec_ref[d]
        g = _log_sigmoid(z) * (1.0 / TAU)
        order = range(n_sub) if d == 0 else range(n_sub - 1, -1, -1)
        for j in order:
            rs = slice(j * lc, (j + 1) * lc)
            b = _cumsum_rows(tri, g[rs])
            b_last = b[lc - 1:lc, :] if d == 0 else b[0:1, :]
            b_mid = b[lc // 2 - 1:lc // 2, :] if d == 0 else b[lc // 2:lc // 2 + 1, :]
            q = p_ref[rs, 0:512].astype(F32) * (DK ** -0.5)
            k = p_ref[rs, 512:1024].astype(F32)
            qi = (q * jnp.exp(b)).astype(BF16)
            kl = (k * jnp.exp(b_last - b)).astype(BF16)
            dec = jnp.exp(b_last)
            if emit_out:
                qd = (q * jnp.exp(b - b_mid)).astype(BF16)
                kd = (k * jnp.exp(b_mid - b)).astype(BF16)
            for h in range(HEADS):
                ks = slice(h * DK, (h + 1) * DK)
                v = p_ref[rs, 1024 + h * DV:1024 + (h + 1) * DV]
                st = st_ref[d * HEADS + h]
                if emit_out:
                    s = jnp.where(causal, _dot_nt(qd[:, ks], kd[:, ks]), 0.0).astype(BF16)
                    o_ref[rs, h * DV:(h + 1) * DV] = _dot(s, v) + _dot_nt(qi[:, ks], st.astype(BF16))
                st_ref[d * HEADS + h] = st * dec[:, ks] + _dot_tn(v, kl[:, ks])


def _gla_scan(pg, tlr, wup, bdec, s0, emit_out):
    bn, tn, _ = pg.shape
    ns = tn // STEP
    fwd = lambda b, i: (b, i, 0)
    bwd = lambda b, i: (b, ns - 1 - i, 0)
    st_shape = jax.ShapeDtypeStruct((bn, 2 * HEADS, DV, DK), F32)
    st_spec = pl.BlockSpec((None, 2 * HEADS, DV, DK), lambda b, i: (b, 0, 0, 0))
    o_shape = jax.ShapeDtypeStruct((bn, tn, HEADS * DV), F32)
    out_shape = (o_shape, o_shape, st_shape) if emit_out else (st_shape,)
    out_specs = ((pl.BlockSpec((None, STEP, HEADS * DV), fwd),
                  pl.BlockSpec((None, STEP, HEADS * DV), bwd), st_spec) if emit_out else (st_spec,))
    return pl.pallas_call(
        functools.partial(_gla_kernel, emit_out=emit_out),
        out_shape=out_shape,
        grid=(bn, ns),
        in_specs=[pl.BlockSpec((None, STEP, 2048), fwd),
                  pl.BlockSpec((None, STEP, 2048), bwd),
                  pl.BlockSpec((None, STEP, 128), fwd),
                  pl.BlockSpec((None, STEP, 128), bwd),
                  pl.BlockSpec((2, 128, 512), lambda b, i: (0, 0, 0)),
                  pl.BlockSpec((2, 1, 512), lambda b, i: (0, 0, 0)),
                  st_spec],
        out_specs=out_specs,
        compiler_params=_params(("arbitrary", "arbitrary")),
        name="gla_scan_lat" if emit_out else "gla_scan_ctx",
    )(pg, pg, tlr, tlr, wup, bdec, s0)


def _mlstm_kernel(pf_ref, pb_ref, hpf_ref, hnf_ref, hpb_ref, hnb_ref, tf_ref, tb_ref,
                  cw_ref, cb_ref, bg_ref, c0_ref, n0_ref, m0_ref, *out_refs, emit_out, ns):
    if emit_out:
        of_ref, ob_ref, c_ref, n_ref, m_ref = out_refs
    else:
        c_ref, n_ref, m_ref = out_refs
        of_ref = ob_ref = None
    i = pl.program_id(1)

    @pl.when(i == 0)
    def _():
        c_ref[...] = c0_ref[...]
        n_ref[...] = n0_ref[...]
        m_ref[...] = m0_ref[...]

    lc = STEP
    rowq = lax.broadcasted_iota(jnp.int32, (lc, 2 * HEADS * DK), 0)
    dirs = ((pf_ref, hpf_ref, hnf_ref, tf_ref, of_ref, i),
            (pb_ref, hpb_ref, hnb_ref, tb_ref, ob_ref, ns - 1 - i))
    for d, (p_ref, hp_ref, hn_ref, t_ref, o_ref, sidx) in enumerate(dirs):
        causal, tri = _tri(lc, d)
        x = p_ref[:, 0:1024].astype(F32)
        prev_row = jnp.where(sidx > 0, hp_ref[7:8, :].astype(F32), 0.0)
        next_row = jnp.where(sidx < ns - 1, hn_ref[0:1, :].astype(F32), 0.0)
        xp = jnp.where(rowq == 0, prev_row, pltpu.roll(x, 1, axis=0))
        xn = jnp.where(rowq == lc - 1, next_row, pltpu.roll(x, lc - 1, axis=0))
        conv = xp * cw_ref[0:1, :] + x * cw_ref[1:2, :] + xn * cw_ref[2:3, :] + cb_ref[...]
        qk = conv * _sigmoid(conv)
        q = qk[:, 0:512]
        k = qk[:, 512:1024] * (DK ** -0.5)
        qb = q.astype(BF16)
        ga = t_ref[:, 0:128] + bg_ref[:, 0:128]
        gb = t_ref[:, 128:256] + bg_ref[:, 128:256]
        bc = _cumsum_rows(tri, _log_sigmoid(gb))
        rt = (ga - bc).T
        for h in range(HEADS):
            idx = d * HEADS + h
            lane = GATE_LANE + 8 * d + h
            ks = slice(h * DK, (h + 1) * DK)
            v = p_ref[:, 1024 + h * DV:1024 + (h + 1) * DV]
            bcol = bc[:, lane:lane + 1]
            icol = ga[:, lane:lane + 1]
            rrow = rt[lane:lane + 1, :]
            mval = m_ref[idx:idx + 1, 0:1]
            nvec = n_ref[idx:idx + 1, :]
            cmat = c_ref[idx]
            b_last = bcol[lc - 1:lc, :] if d == 0 else bcol[0:1, :]
            if emit_out:
                log_intra = jnp.where(causal, bcol + rrow, -jnp.inf)
                log_inter = bcol + mval
                m_q = jnp.maximum(log_inter, jnp.max(log_intra, axis=1, keepdims=True))
                w_intra = jnp.exp(log_intra - m_q) * _dot_nt(qb[:, ks], k[:, ks].astype(BF16))
                w_inter = jnp.exp(log_inter - m_q)
                num = _dot(w_intra.astype(BF16), v) + w_inter * _dot(qb[:, ks], cmat.astype(BF16))
                den = (jnp.sum(w_intra, axis=1, keepdims=True)
                       + w_inter * jnp.sum(q[:, ks] * nvec, axis=1, keepdims=True))
                inv = 1.0 / jnp.maximum(jnp.abs(den), jnp.exp(-m_q))
                o_ref[:, h * DV:(h + 1) * DV] = num * inv
            log_key = b_last - bcol + icol
            m_new = jnp.maximum(b_last + mval, jnp.max(log_key, axis=0, keepdims=True))
            kw = k[:, ks] * jnp.exp(log_key - m_new)
            decay = jnp.exp(b_last + mval - m_new)
            c_ref[idx] = decay * cmat + _dot_tn(kw.astype(BF16), v)
            n_ref[idx:idx + 1, :] = decay * nvec + jnp.sum(kw, axis=0, keepdims=True)
            m_ref[idx:idx + 1, :] = jnp.broadcast_to(m_new, (1, 128))


def _mlstm_scan(pm, tg, conv_w, conv_b, bgate, c0, n0, m0, emit_out):
    bn, tn, _ = pm.shape
    ns = tn // STEP
    hb = STEP // 8
    sf = lambda i: i
    sb = lambda i: ns - 1 - i
    blk = lambda s: (lambda b, i: (b, s(i), 0))
    prev = lambda s: (lambda b, i: (b, jnp.maximum(hb * s(i) - 1, 0), 0))
    nxt = lambda s: (lambda b, i: (b, jnp.minimum(hb * (s(i) + 1), hb * ns - 1), 0))
    c_shape = jax.ShapeDtypeStruct((bn, 2 * HEADS, DK, DV), F32)
    v_shape = jax.ShapeDtypeStruct((bn, 2 * HEADS, 128), F32)
    c_spec = pl.BlockSpec((None, 2 * HEADS, DK, DV), lambda b, i: (b, 0, 0, 0))
    v_spec = pl.BlockSpec((None, 2 * HEADS, 128), lambda b, i: (b, 0, 0))
    if emit_out:
        o_shape = jax.ShapeDtypeStruct((bn, tn, HEADS * DV), F32)
        out_shape = (o_shape, o_shape, c_shape, v_shape, v_shape)
        out_specs = (pl.BlockSpec((None, STEP, HEADS * DV), blk(sf)),
                     pl.BlockSpec((None, STEP, HEADS * DV), blk(sb)), c_spec, v_spec, v_spec)
    else:
        out_shape = (c_shape, v_shape, v_shape)
        out_specs = (c_spec, v_spec, v_spec)
    return pl.pallas_call(
        functools.partial(_mlstm_kernel, emit_out=emit_out, ns=ns),
        out_shape=out_shape,
        grid=(bn, ns),
        in_specs=[pl.BlockSpec((None, STEP, 2048), blk(sf)),
                  pl.BlockSpec((None, STEP, 2048), blk(sb)),
                  pl.BlockSpec((None, 8, 1024), prev(sf)),
                  pl.BlockSpec((None, 8, 1024), nxt(sf)),
                  pl.BlockSpec((None, 8, 1024), prev(sb)),
                  pl.BlockSpec((None, 8, 1024), nxt(sb)),
                  pl.BlockSpec((None, STEP, 256), blk(sf)),
                  pl.BlockSpec((None, STEP, 256), blk(sb)),
                  pl.BlockSpec((3, 1024), lambda b, i: (0, 0)),
                  pl.BlockSpec((1, 1024), lambda b, i: (0, 0)),
                  pl.BlockSpec((1, 256), lambda b, i: (0, 0)),
                  c_spec, v_spec, v_spec],
        out_specs=out_specs,
        compiler_params=_params(("arbitrary", "arbitrary")),
        name="mlstm_scan_lat" if emit_out else "mlstm_scan_ctx",
    )(pm, pm, pm, pm, pm, pm, tg, tg, conv_w, conv_b, bgate, c0, n0, m0)


def _head_norm(o, g):
    parts = []
    for h in range(HEADS):
        oh = o[:, h * DV:(h + 1) * DV]
        parts.append(oh * lax.rsqrt(jnp.mean(oh * oh, axis=-1, keepdims=True) + EPS))
    return jnp.concatenate(parts, axis=-1) * g


def _merge_kernel(of_ref, ob_ref, hf_ref, hb_ref, po_ref, x_ref, g1_ref, gg_ref, gm_ref,
                  wbg_ref, wbm_ref, wo_ref, o_ref, hsum_ref):
    hsum_ref[...] = hf_ref[...] + hb_ref[...]
    hm = jnp.concatenate([hsum_ref[:, rl, :] for rl in range(COL_BLOCK)], axis=0)
    og = po_ref[:, 0:1024].astype(F32)
    y_gla = _head_norm(of_ref[...] + ob_ref[...], gg_ref[...]) * (og * _sigmoid(og))
    y_m = _head_norm(hm, gm_ref[...]) * _sigmoid(po_ref[:, 1024:2048].astype(F32))
    gate_g = _sigmoid(po_ref[:, 2048:3072].astype(F32))
    gate_m = _sigmoid(po_ref[:, 3072:4096].astype(F32))
    y = (gate_g * _dot(y_gla.astype(BF16), wbg_ref[...])
         + gate_m * _dot(y_m.astype(BF16), wbm_ref[...]))
    mix = _dot(y.astype(BF16), wo_ref[...])
    o_ref[...] = x_ref[...] + g1_ref[...] * mix


def _merge(of, ob, hf4, hb4, po, x2, mod3, gg, gm, wbg, wbm, wo):
    m = x2.shape[0]
    tm = GRID_W * COL_BLOCK
    per_b = SEQ // tm
    tok = lambda i: (i, 0)
    hspec = pl.BlockSpec((None, GRID_W, COL_BLOCK, D), lambda i: (i // per_b, 0, i % per_b, 0))
    return pl.pallas_call(
        _merge_kernel,
        out_shape=jax.ShapeDtypeStruct((m, D), F32),
        grid=(m // tm,),
        in_specs=[pl.BlockSpec((tm, D), tok), pl.BlockSpec((tm, D), tok), hspec, hspec,
                  pl.BlockSpec((tm, 4096), tok), pl.BlockSpec((tm, D), tok),
                  pl.BlockSpec((None, 1, D), lambda i: (i // per_b, 0, 2)),
                  pl.BlockSpec((1, D), lambda i: (0, 0)), pl.BlockSpec((1, D), lambda i: (0, 0)),
                  _resident((D, D)), _resident((D, D)), _resident((D, D))],
        out_specs=pl.BlockSpec((tm, D), tok),
        scratch_shapes=[pltpu.VMEM((GRID_W, COL_BLOCK, D), F32)],
        compiler_params=_params(("arbitrary",)),
        name="merge",
    )(of, ob, hf4, hb4, po, x2, mod3, gg, gm, wbg, wbm, wo)


FF_TILE = 1408


def _ffn_kernel(x_ref, sh_ref, sc_ref, g2_ref, ng_ref, fg_ref, wi_ref, wo_ref, o_ref):
    x = x_ref[...]
    y = x * lax.rsqrt(jnp.mean(x * x, axis=-1, keepdims=True) + EPS) * ng_ref[...]
    u = (y * (1.0 + sc_ref[...]) + sh_ref[...]).astype(BF16)
    acc = None
    for j in range(D_FF // FF_TILE):
        a = _dot(u, wi_ref[:, j * FF_TILE:(j + 1) * FF_TILE])
        b = _dot(u, wi_ref[:, D_FF + j * FF_TILE:D_FF + (j + 1) * FF_TILE])
        hid = (a * _sigmoid(a) * b).astype(BF16)
        part = _dot(hid, wo_ref[j * FF_TILE:(j + 1) * FF_TILE, :])
        acc = part if acc is None else acc + part
    x2 = x + g2_ref[...] * acc
    o_ref[...] = x2 * lax.rsqrt(jnp.mean(x2 * x2, axis=-1, keepdims=True) + EPS) * fg_ref[...]


def _ffn(x1, mod3, ng, fg, wi, wo, tm):
    m = x1.shape[0]
    per_b = SEQ // tm
    tok = lambda i: (i, 0)
    modspec = lambda c: pl.BlockSpec((None, 1, D), lambda i: (i // per_b, 0, c))
    return pl.pallas_call(
        _ffn_kernel,
        out_shape=jax.ShapeDtypeStruct((m, D), F32),
        grid=(m // tm,),
        in_specs=[pl.BlockSpec((tm, D), tok), modspec(3), modspec(4), modspec(5),
                  pl.BlockSpec((1, D), lambda i: (0, 0)), pl.BlockSpec((1, D), lambda i: (0, 0)),
                  _resident((D, 2 * D_FF)), _resident((D_FF, D))],
        out_specs=pl.BlockSpec((tm, D), tok),
        compiler_params=_params(("arbitrary",)),
        name="ffn",
    )(x1, mod3, mod3, mod3, ng, fg, wi, wo)


def _split_w_in(w_in):
    gq, gk, gv, gog, lr, mq, mk, mv, mog, gates, bg = jnp.split(
        w_in, [512, 1024, 2048, 3072, 3104, 3616, 4128, 5152, 6176, 6192], axis=1)
    z = lambda n: jnp.zeros((D, n), w_in.dtype)
    wg = jnp.concatenate([gq, gk, gv], axis=1).astype(BF16)
    wm = jnp.concatenate([mq, mk, mv], axis=1).astype(BF16)
    wo = jnp.concatenate([gog, mog, bg], axis=1).astype(BF16)
    wtl = jnp.concatenate([lr, z(128 - 2 * RANK)], axis=1).astype(BF16)
    wtg = jnp.concatenate([z(GATE_LANE), gates, z(128 - GATE_LANE - 16),
                           z(GATE_LANE), gates[:, 4:8], z(4), gates[:, 12:16], z(128 - GATE_LANE - 12)],
                          axis=1).astype(BF16)
    return wg, wm, wo, wtl, wtg


def kernel(x, c, ctx, c_ctx, w_ada, b_ada, norm1_g, w_in, gla_w_up, gla_b_dec, gla_norm_g,
           mlstm_conv_w, mlstm_conv_b, mlstm_b_gate, mlstm_norm_g, w_br_gla, w_br_mlstm, w_out,
           norm2_g, w_ffn_in, w_ffn_out, final_g):
    bsz = x.shape[0]
    row = lambda a: a.reshape(1, -1)

    cvec = jnp.concatenate([c, c_ctx[None, :], jnp.zeros((8 - bsz - 1, D), F32)], axis=0)
    mod3 = _ada(cvec, w_ada[0], row(b_ada[0])).reshape(8, 1, N_MOD)

    wg, wm, wo, wtl, wtg = _split_w_in(w_in[0])
    x2 = x.reshape(bsz * SEQ, D)
    ctx2 = ctx.reshape(bsz * CTX, D)
    g1n = row(norm1_g[0])
    tm = GRID_W * COL_BLOCK
    pg, po, tlr = _inproj(x2, mod3, lambda i: i // (SEQ // tm), g1n, (wg, wo, wtl),
                          (BF16, BF16, F32), tm, False)
    pm, tg = _inproj(x.reshape(bsz, SEQ // GRID_W, GRID_W, D), mod3,
                     lambda i: i // (GRID_W // COL_BLOCK), g1n, (wm, wtg), (BF16, F32), tm, True)
    pg_c, pm_c, tlr_c, tg_c = _inproj(ctx2, mod3, lambda i: bsz, g1n, (wg, wm, wtl, wtg),
                                      (BF16, BF16, F32, F32), CTX, False)

    wup = jnp.zeros((2, 128, HEADS * DK), F32)
    wup = wup.at[0, 0:RANK].set(gla_w_up[0, 0]).at[1, RANK:2 * RANK].set(gla_w_up[0, 1]).astype(BF16)
    bdec = gla_b_dec[0].reshape(2, 1, HEADS * DK)
    s0 = jnp.zeros((bsz, 2 * HEADS, DV, DK), F32)
    (s_ctx,) = _gla_scan(pg_c.reshape(bsz, CTX, 2048), tlr_c.reshape(bsz, CTX, 128),
                         wup, bdec, s0, emit_out=False)
    o_f, o_b, _ = _gla_scan(pg.reshape(bsz, SEQ, 2048), tlr.reshape(bsz, SEQ, 128),
                            wup, bdec, s_ctx, emit_out=True)

    bgate = mlstm_b_gate[0].reshape(1, 16)
    zg = lambda n: jnp.zeros((1, n), F32)
    bg2 = jnp.concatenate([zg(GATE_LANE), bgate, zg(128 - GATE_LANE - 16),
                           zg(GATE_LANE), bgate[:, 4:8], zg(4), bgate[:, 12:16],
                           zg(128 - GATE_LANE - 12)], axis=1)
    conv_w = mlstm_conv_w[0]
    conv_b = row(mlstm_conv_b[0])
    c0 = jnp.zeros((bsz, 2 * HEADS, DK, DV), F32)
    v0 = jnp.zeros((bsz, 2 * HEADS, 128), F32)
    c_ctx_s, n_ctx_s, m_ctx_s = _mlstm_scan(
        pm_c.reshape(bsz, CTX, 2048), tg_c.reshape(bsz, CTX, 256),
        conv_w, conv_b, bg2, c0, v0, v0, emit_out=False)
    h_f, h_b, _, _, _ = _mlstm_scan(
        pm.reshape(bsz, SEQ, 2048), tg.reshape(bsz, SEQ, 256),
        conv_w, conv_b, bg2, c_ctx_s, n_ctx_s, m_ctx_s, emit_out=True)

    cm4 = lambda a: a.reshape(bsz, GRID_W, SEQ // GRID_W, D)
    x1 = _merge(o_f.reshape(bsz * SEQ, D), o_b.reshape(bsz * SEQ, D), cm4(h_f), cm4(h_b), po, x2, mod3,
                row(gla_norm_g[0]), row(mlstm_norm_g[0]),
                w_br_gla[0].astype(BF16), w_br_mlstm[0].astype(BF16), w_out[0].astype(BF16))
    out = _ffn(x1, mod3, row(norm2_g[0]), row(final_g),
               w_ffn_in[0].astype(BF16), w_ffn_out[0].astype(BF16), tm)
    return out.reshape(bsz, SEQ, D)
```

```python
import functools

import jax
import jax.numpy as jnp
from jax import lax
from jax.experimental import pallas as pl
from jax.experimental.pallas import tpu as pltpu

D = 1024
SEQ = 4096
CTX = 256
GRID_W = 64
EPS = 1e-6
HEADS = 4
DK = 128
DV = 256
RANK = 16
TAU = 16.0
D_FF = 2816
N_MOD = 6 * D
GATE_LANE = 32
STEP = 256
GLA_CHUNK = 128
MLSTM_CHUNK = 128
COL_BLOCK = 8

F32 = jnp.float32
BF16 = jnp.bfloat16
VMEM_LIMIT = 56 * 1024 * 1024


def _dot(a, b):
    return jnp.dot(a, b, preferred_element_type=F32)


def _dot_nt(a, b):
    return lax.dot_general(a, b, (((1,), (1,)), ((), ())), preferred_element_type=F32)


def _dot_tn(a, b):
    return lax.dot_general(a, b, (((0,), (0,)), ((), ())), preferred_element_type=F32)


def _sigmoid(x):
    return 1.0 / (1.0 + jnp.exp(-x))


def _log_sigmoid(x):
    return jnp.minimum(x, 0.0) - jnp.log1p(jnp.exp(-jnp.abs(x)))


def _cumsum_rows(tri, g):
    g1 = g.astype(BF16)
    g2 = (g - g1.astype(F32)).astype(BF16)
    return _dot(tri, g1) + _dot(tri, g2)


def _tri(n, d):
    row = lax.broadcasted_iota(jnp.int32, (n, n), 0)
    col = lax.broadcasted_iota(jnp.int32, (n, n), 1)
    causal = (col <= row) if d == 0 else (col >= row)
    return causal, jnp.where(causal, 1.0, 0.0).astype(BF16)


def _resident(shape):
    n = len(shape)
    return pl.BlockSpec(shape, lambda *_: (0,) * n, pipeline_mode=pl.Buffered(1))


def _params(sem):
    return pltpu.CompilerParams(dimension_semantics=sem, vmem_limit_bytes=VMEM_LIMIT)


def _scan_units(n_sub):
    return [(d, j) for jj in range(n_sub) for d, j in ((0, jj), (1, n_sub - 1 - jj))]


def _ada_kernel(c_ref, w_ref, b_ref, o_ref):
    cv = c_ref[...]
    s = (cv * _sigmoid(cv)).astype(BF16)
    o_ref[...] = _dot(s, w_ref[...].astype(BF16)) + b_ref[...]


def _ada(cvec, w_ada, b_ada):
    tn = 1024
    return pl.pallas_call(
        _ada_kernel,
        out_shape=jax.ShapeDtypeStruct((8, N_MOD), F32),
        grid=(N_MOD // tn,),
        in_specs=[pl.BlockSpec((8, D), lambda j: (0, 0)),
                  pl.BlockSpec((D, tn), lambda j: (0, j)),
                  pl.BlockSpec((1, tn), lambda j: (0, j))],
        out_specs=pl.BlockSpec((8, tn), lambda j: (0, j)),
        compiler_params=_params(("arbitrary",)),
        name="adaln",
    )(cvec, w_ada, b_ada)


def _inproj_kernel(x_ref, sh_ref, sc_ref, g_ref, *refs):
    n_out = len(refs) // 2
    x = x_ref[...]
    y = x * lax.rsqrt(jnp.mean(x * x, axis=-1, keepdims=True) + EPS) * g_ref[...]
    u = (y * (1.0 + sc_ref[...]) + sh_ref[...]).astype(BF16)
    for w_ref, o_ref in zip(refs[:n_out], refs[n_out:]):
        n = w_ref.shape[1]
        for j in range(0, n, 1024):
            cs = slice(j, min(j + 1024, n))
            o_ref[:, cs] = _dot(u, w_ref[:, cs]).astype(o_ref.dtype)


def _inproj(x2, mod3, mod_row, norm_g, weights, out_dtypes, tm):
    m = x2.shape[0]
    return pl.pallas_call(
        _inproj_kernel,
        out_shape=tuple(jax.ShapeDtypeStruct((m, w.shape[1]), dt) for w, dt in zip(weights, out_dtypes)),
        grid=(m // tm,),
        in_specs=[pl.BlockSpec((tm, D), lambda i: (i, 0)),
                  pl.BlockSpec((None, 1, D), lambda i: (mod_row(i), 0, 0)),
                  pl.BlockSpec((None, 1, D), lambda i: (mod_row(i), 0, 1)),
                  pl.BlockSpec((1, D), lambda i: (0, 0))] + [_resident(w.shape) for w in weights],
        out_specs=tuple(pl.BlockSpec((tm, w.shape[1]), lambda i: (i, 0)) for w in weights),
        compiler_params=_params(("arbitrary",)),
        name="inproj",
    )(x2, mod3, mod3, norm_g, *weights)


def _inproj_m_kernel(*refs, colmajor, nblk):
    if colmajor:
        x_ref, hp_ref, hn_ref, sh_ref, sc_ref, g_ref, wm_ref, wtg_ref, cw_ref, cb_ref, pm_ref, tg_ref = refs
        x = jnp.concatenate([x_ref[:, cl, :] for cl in range(COL_BLOCK)] + [hp_ref[7], hn_ref[0]], axis=0)
    else:
        x_ref, sh_ref, sc_ref, g_ref, wm_ref, wtg_ref, cw_ref, cb_ref, pm_ref, tg_ref = refs
        x = x_ref[...]
    n = pm_ref.shape[0]
    y = x * lax.rsqrt(jnp.mean(x * x, axis=-1, keepdims=True) + EPS) * g_ref[...]
    u = (y * (1.0 + sc_ref[...]) + sh_ref[...]).astype(BF16)
    pre = _dot(u, wm_ref[:, 0:1024])
    a = pre[0:n]
    if colmajor:
        j = pl.program_id(0) % nblk
        prev_row = jnp.where(j > 0, pre[n + 7:n + 8], 0.0)
        next_row = jnp.where(j < nblk - 1, pre[n + 8:n + 9], 0.0)
    else:
        prev_row = next_row = jnp.zeros((1, 1024), F32)
    rowq = lax.broadcasted_iota(jnp.int32, (n, 1024), 0)
    ap = jnp.where(rowq == 0, prev_row, pltpu.roll(a, 1, axis=0))
    an = jnp.where(rowq == n - 1, next_row, pltpu.roll(a, n - 1, axis=0))
    conv = ap * cw_ref[0:1, :] + a * cw_ref[1:2, :] + an * cw_ref[2:3, :] + cb_ref[...]
    qk = conv * _sigmoid(conv)
    pm_ref[:, 0:512] = qk[:, 0:512].astype(BF16)
    pm_ref[:, 512:1024] = (qk[:, 512:1024] * (DK ** -0.5)).astype(BF16)
    um = u[0:n]
    pm_ref[:, 1024:2048] = _dot(um, wm_ref[:, 1024:2048]).astype(BF16)
    tg_ref[...] = _dot(um, wtg_ref[...])


def _inproj_m(xv, mod3, mod_row, norm_g, wm, wtg, conv_w, conv_b, tm, colmajor):
    full = lambda shape: pl.BlockSpec(shape, lambda i: (0,) * len(shape))
    if colmajor:
        bsz = xv.shape[0]
        m = bsz * SEQ
        nblk = GRID_W // COL_BLOCK
        grid = (bsz * nblk,)
        blk = (None, GRID_W, COL_BLOCK, D)
        halo = (None, 8, COL_BLOCK, D)
        x_specs = [pl.BlockSpec(blk, lambda i: (i // nblk, 0, i % nblk, 0)),
                   pl.BlockSpec(halo, lambda i: (i // nblk, GRID_W // 8 - 1, jnp.maximum(i % nblk - 1, 0), 0)),
                   pl.BlockSpec(halo, lambda i: (i // nblk, 0, jnp.minimum(i % nblk + 1, nblk - 1), 0))]
        xs = (xv, xv, xv)
    else:
        m = xv.shape[0]
        nblk = 1
        grid = (m // tm,)
        x_specs = [pl.BlockSpec((tm, D), lambda i: (i, 0))]
        xs = (xv,)
    return pl.pallas_call(
        functools.partial(_inproj_m_kernel, colmajor=colmajor, nblk=nblk),
        out_shape=(jax.ShapeDtypeStruct((m, 2048), BF16), jax.ShapeDtypeStruct((m, 256), F32)),
        grid=grid,
        in_specs=x_specs + [pl.BlockSpec((None, 1, D), lambda i: (mod_row(i), 0, 0)),
                            pl.BlockSpec((None, 1, D), lambda i: (mod_row(i), 0, 1)),
                            full((1, D)), _resident(wm.shape), _resident(wtg.shape),
                            full((3, 1024)), full((1, 1024))],
        out_specs=(pl.BlockSpec((tm, 2048), lambda i: (i, 0)), pl.BlockSpec((tm, 256), lambda i: (i, 0))),
        compiler_params=_params(("arbitrary",)),
        name="inproj_m_cm" if colmajor else "inproj_m",
    )(*xs, mod3, mod3, norm_g, wm, wtg, conv_w, conv_b)


def _gla_kernel(pf_ref, pb_ref, tf_ref, tb_ref, wup_ref, bdec_ref, s0_ref, *out_refs, emit_out):
    if emit_out:
        of_ref, ob_ref, st_ref = out_refs
    else:
        (st_ref,) = out_refs
        of_ref = ob_ref = None

    @pl.when(pl.program_id(1) == 0)
    def _():
        st_ref[...] = s0_ref[...]

    lc = GLA_CHUNK
    dirs = ((pf_ref, tf_ref, of_ref), (pb_ref, tb_ref, ob_ref))
    units = _scan_units(STEP // lc)
    masks = [_tri(lc, d) for d in range(2)]
    gs = []
    for d, (p_ref, t_ref, o_ref) in enumerate(dirs):
        z = _dot(t_ref[...].astype(BF16), wup_ref[d]) + bdec_ref[d]
        gs.append(_log_sigmoid(z) * (1.0 / TAU))
    bs = {}
    for d, j in units:
        bs[d, j] = _cumsum_rows(masks[d][1], gs[d][j * lc:(j + 1) * lc])
    ops = {}
    for d, j in units:
        p_ref = dirs[d][0]
        rs = slice(j * lc, (j + 1) * lc)
        b = bs[d, j]
        b_last = b[lc - 1:lc, :] if d == 0 else b[0:1, :]
        b_mid = b[lc // 2 - 1:lc // 2, :] if d == 0 else b[lc // 2:lc // 2 + 1, :]
        q = p_ref[rs, 0:512].astype(F32) * (DK ** -0.5)
        k = p_ref[rs, 512:1024].astype(F32)
        qi = (q * jnp.exp(b)).astype(BF16)
        kl = (k * jnp.exp(b_last - b)).astype(BF16)
        dec = jnp.exp(b_last)
        if emit_out:
            qd = (q * jnp.exp(b - b_mid)).astype(BF16)
            kd = (k * jnp.exp(b_mid - b)).astype(BF16)
        else:
            qd = kd = None
        ops[d, j] = (qi, kl, dec, qd, kd)
    sc, us, dcols = {}, {}, {}
    for d, j in units:
        p_ref = dirs[d][0]
        rs = slice(j * lc, (j + 1) * lc)
        qi, kl, dec, qd, kd = ops[d, j]
        for h in range(HEADS):
            ks = slice(h * DK, (h + 1) * DK)
            v = p_ref[rs, 1024 + h * DV:1024 + (h + 1) * DV]
            if emit_out:
                sc[d, j, h] = jnp.where(masks[d][0], _dot_nt(qd[:, ks], kd[:, ks]), 0.0).astype(BF16)
            us[d, j, h] = _dot_tn(kl[:, ks], v)
            dcols[d, j, h] = jnp.broadcast_to(dec[:, ks], (8, DK)).T[:, 0:1]
    for d in range(2):
        p_ref, _, o_ref = dirs[d]
        for h in range(HEADS):
            ks = slice(h * DK, (h + 1) * DK)
            st = st_ref[d * HEADS + h]
            for dd, j in units:
                if dd != d:
                    continue
                rs = slice(j * lc, (j + 1) * lc)
                if emit_out:
                    v = p_ref[rs, 1024 + h * DV:1024 + (h + 1) * DV]
                    o_ref[rs, h * DV:(h + 1) * DV] = _dot(
                        jnp.concatenate([sc[d, j, h], ops[d, j][0][:, ks]], axis=1),
                        jnp.concatenate([v, st.astype(BF16)], axis=0))
                st = st * dcols[d, j, h] + us[d, j, h]
            st_ref[d * HEADS + h] = st


def _gla_scan(pg, tlr, wup, bdec, s0, emit_out):
    bn, tn, _ = pg.shape
    ns = tn // STEP
    fwd = lambda b, i: (b, i, 0)
    bwd = lambda b, i: (b, ns - 1 - i, 0)
    st_shape = jax.ShapeDtypeStruct((bn, 2 * HEADS, DK, DV), F32)
    st_spec = pl.BlockSpec((None, 2 * HEADS, DK, DV), lambda b, i: (b, 0, 0, 0))
    o_shape = jax.ShapeDtypeStruct((bn, tn, HEADS * DV), F32)
    out_shape = (o_shape, o_shape, st_shape) if emit_out else (st_shape,)
    out_specs = ((pl.BlockSpec((None, STEP, HEADS * DV), fwd),
                  pl.BlockSpec((None, STEP, HEADS * DV), bwd), st_spec) if emit_out else (st_spec,))
    return pl.pallas_call(
        functools.partial(_gla_kernel, emit_out=emit_out),
        out_shape=out_shape,
        grid=(bn, ns),
        in_specs=[pl.BlockSpec((None, STEP, 2048), fwd),
                  pl.BlockSpec((None, STEP, 2048), bwd),
                  pl.BlockSpec((None, STEP, 128), fwd),
                  pl.BlockSpec((None, STEP, 128), bwd),
                  pl.BlockSpec((2, 128, 512), lambda b, i: (0, 0, 0)),
                  pl.BlockSpec((2, 1, 512), lambda b, i: (0, 0, 0)),
                  st_spec],
        out_specs=out_specs,
        compiler_params=_params(("arbitrary", "arbitrary")),
        name="gla_scan_lat" if emit_out else "gla_scan_ctx",
    )(pg, pg, tlr, tlr, wup, bdec, s0)


def _mlstm_kernel(pf_ref, pb_ref, tf_ref, tb_ref, bg_ref, c0_ref, n0_ref, m0_ref, *rest, emit_out):
    if emit_out:
        of_ref, ob_ref, c_ref, n_ref, m_ref = rest
    else:
        c_ref, n_ref, m_ref = rest
        of_ref = ob_ref = None

    @pl.when(pl.program_id(1) == 0)
    def _():
        c_ref[...] = c0_ref[...]
        n_ref[...] = n0_ref[...]
        m_ref[...] = m0_ref[...]

    lc = MLSTM_CHUNK
    dirs = ((pf_ref, tf_ref, of_ref), (pb_ref, tb_ref, ob_ref))
    units = _scan_units(STEP // lc)
    tris = [_tri(lc, d) for d in range(2)]
    row8 = lax.broadcasted_iota(jnp.int32, (8, lc), 0)
    lane_of = lambda d, h: GATE_LANE + 8 * d + h

    mrow = [m_ref[d:d + 1, :] for d in range(2)]
    tiles = {}
    for d, j in units:
        t_ref = dirs[d][1]
        rs = slice(j * lc, (j + 1) * lc)
        ga = t_ref[rs, 0:128] + bg_ref[:, 0:128]
        gb = t_ref[rs, 128:256] + bg_ref[:, 128:256]
        bc = _cumsum_rows(tris[d][1], _log_sigmoid(gb))
        b_last = bc[lc - 1:lc, :] if d == 0 else bc[0:1, :]
        log_key = b_last - bc + ga
        m_new = jnp.maximum(b_last + mrow[d], jnp.max(log_key, axis=0, keepdims=True))
        tiles[d, j] = dict(rmat=ga - bc, bct=bc.T, m_in=mrow[d], w_key=jnp.exp(log_key - m_new),
                           decay=jnp.exp(b_last + mrow[d] - m_new))
        mrow[d] = m_new
    for d in range(2):
        m_ref[d:d + 1, :] = mrow[d]

    us, ncols = {}, {}
    for d, j in units:
        p_ref = dirs[d][0]
        rs = slice(j * lc, (j + 1) * lc)
        for h in range(HEADS):
            lane = lane_of(d, h)
            k = p_ref[rs, 512 + h * DK:512 + (h + 1) * DK]
            v = p_ref[rs, 1024 + h * DV:1024 + (h + 1) * DV]
            kw = k.astype(F32) * tiles[d, j]["w_key"][:, lane:lane + 1]
            us[d, j, h] = _dot_tn(kw.astype(BF16), v)
            ncols[d, j, h] = jnp.sum(kw, axis=0, keepdims=True)

    n_in = {}
    for d in range(2):
        for h in range(HEADS):
            idx = d * HEADS + h
            lane = lane_of(d, h)
            nvec = n_ref[idx:idx + 1, :]
            for dd, j in units:
                if dd == d:
                    n_in[d, j, h] = nvec
                    nvec = tiles[d, j]["decay"][:, lane:lane + 1] * nvec + ncols[d, j, h]
            n_ref[idx:idx + 1, :] = nvec

    pre = {}
    if emit_out:
        for d, j in units:
            p_ref = dirs[d][0]
            rs = slice(j * lc, (j + 1) * lc)
            t = tiles[d, j]
            causal_t = tris[1 - d][0]
            for h in range(HEADS):
                lane = lane_of(d, h)
                q = p_ref[rs, h * DK:(h + 1) * DK]
                k = p_ref[rs, 512 + h * DK:512 + (h + 1) * DK]
                kq = _dot_nt(jnp.concatenate(
                    [k, jnp.broadcast_to(n_in[d, j, h], (16, DK)).astype(BF16)], axis=0), q)
                rm = jnp.where(causal_t, t["rmat"][:, lane:lane + 1], -jnp.inf)
                mval = t["m_in"][:, lane:lane + 1]
                mx = jnp.maximum(mval, jnp.max(rm, axis=0, keepdims=True))
                wt = jnp.exp(rm - mx) * kq[0:lc]
                w_inter = jnp.exp(mval - mx)
                den = jnp.sum(wt, axis=0, keepdims=True) + w_inter * kq[lc:lc + 1]
                inv = 1.0 / jnp.maximum(jnp.abs(den), jnp.exp(-(t["bct"][lane:lane + 1, :] + mx)))
                stat = jnp.where(row8 == 0, w_inter, jnp.where(row8 == 1, inv, 0.0)).T
                pre[d, j, h] = (wt.astype(BF16), (q.astype(F32) * stat[:, 0:1]).astype(BF16), stat[:, 1:2])

    for d in range(2):
        p_ref, _, o_ref = dirs[d]
        for h in range(HEADS):
            idx = d * HEADS + h
            lane = lane_of(d, h)
            cmat = c_ref[idx]
            for dd, j in units:
                if dd != d:
                    continue
                rs = slice(j * lc, (j + 1) * lc)
                if emit_out:
                    v = p_ref[rs, 1024 + h * DV:1024 + (h + 1) * DV]
                    wtb, qs, inv_col = pre[d, j, h]
                    num = _dot_tn(wtb, v) + _dot(qs, cmat.astype(BF16))
                    o_ref[rs, h * DV:(h + 1) * DV] = num * inv_col
                cmat = tiles[d, j]["decay"][:, lane:lane + 1] * cmat + us[d, j, h]
            c_ref[idx] = cmat


def _mlstm_scan(pm, tg, bgate, c0, n0, m0, emit_out):
    bn, tn, _ = pm.shape
    ns = tn // STEP
    fwd = lambda b, i: (b, i, 0)
    bwd = lambda b, i: (b, ns - 1 - i, 0)
    c_shape = jax.ShapeDtypeStruct((bn, 2 * HEADS, DK, DV), F32)
    v_shape = jax.ShapeDtypeStruct((bn, 2 * HEADS, 128), F32)
    c_spec = pl.BlockSpec((None, 2 * HEADS, DK, DV), lambda b, i: (b, 0, 0, 0))
    v_spec = pl.BlockSpec((None, 2 * HEADS, 128), lambda b, i: (b, 0, 0))
    if emit_out:
        o_shape = jax.ShapeDtypeStruct((bn, tn, HEADS * DV), F32)
        out_shape = (o_shape, o_shape, c_shape, v_shape, v_shape)
        out_specs = (pl.BlockSpec((None, STEP, HEADS * DV), fwd),
                     pl.BlockSpec((None, STEP, HEADS * DV), bwd), c_spec, v_spec, v_spec)
    else:
        out_shape = (c_shape, v_shape, v_shape)
        out_specs = (c_spec, v_spec, v_spec)
    return pl.pallas_call(
        functools.partial(_mlstm_kernel, emit_out=emit_out),
        out_shape=out_shape,
        grid=(bn, ns),
        in_specs=[pl.BlockSpec((None, STEP, 2048), fwd),
                  pl.BlockSpec((None, STEP, 2048), bwd),
                  pl.BlockSpec((None, STEP, 256), fwd),
                  pl.BlockSpec((None, STEP, 256), bwd),
                  pl.BlockSpec((1, 256), lambda b, i: (0, 0)),
                  c_spec, v_spec, v_spec],
        out_specs=out_specs,
        compiler_params=_params(("arbitrary", "arbitrary")),
        name="mlstm_scan_lat" if emit_out else "mlstm_scan_ctx",
    )(pm, pm, tg, tg, bgate, c0, n0, m0)


def _head_norm(o, g):
    parts = []
    for h in range(HEADS):
        oh = o[:, h * DV:(h + 1) * DV]
        parts.append(oh * lax.rsqrt(jnp.mean(oh * oh, axis=-1, keepdims=True) + EPS))
    return jnp.concatenate(parts, axis=-1) * g


def _merge_kernel(of_ref, ob_ref, hf_ref, hb_ref, po_ref, x_ref, g1_ref, gg_ref, gm_ref,
                  wbg_ref, wbm_ref, wo_ref, o_ref, hsum_ref):
    hsum_ref[...] = hf_ref[...] + hb_ref[...]
    hm = jnp.concatenate([hsum_ref[:, rl, :] for rl in range(COL_BLOCK)], axis=0)
    og = po_ref[:, 0:1024].astype(F32)
    y_gla = _head_norm(of_ref[...] + ob_ref[...], gg_ref[...]) * (og * _sigmoid(og))
    y_m = _head_norm(hm, gm_ref[...]) * _sigmoid(po_ref[:, 1024:2048].astype(F32))
    gate_g = _sigmoid(po_ref[:, 2048:3072].astype(F32))
    gate_m = _sigmoid(po_ref[:, 3072:4096].astype(F32))
    y = (gate_g * _dot(y_gla.astype(BF16), wbg_ref[...])
         + gate_m * _dot(y_m.astype(BF16), wbm_ref[...]))
    mix = _dot(y.astype(BF16), wo_ref[...])
    o_ref[...] = x_ref[...] + g1_ref[...] * mix


def _merge(of, ob, hf4, hb4, po, x2, mod3, gg, gm, wbg, wbm, wo):
    m = x2.shape[0]
    tm = GRID_W * COL_BLOCK
    per_b = SEQ // tm
    tok = lambda i: (i, 0)
    hspec = pl.BlockSpec((None, GRID_W, COL_BLOCK, D), lambda i: (i // per_b, 0, i % per_b, 0))
    return pl.pallas_call(
        _merge_kernel,
        out_shape=jax.ShapeDtypeStruct((m, D), F32),
        grid=(m // tm,),
        in_specs=[pl.BlockSpec((tm, D), tok), pl.BlockSpec((tm, D), tok), hspec, hspec,
                  pl.BlockSpec((tm, 4096), tok), pl.BlockSpec((tm, D), tok),
                  pl.BlockSpec((None, 1, D), lambda i: (i // per_b, 0, 2)),
                  pl.BlockSpec((1, D), lambda i: (0, 0)), pl.BlockSpec((1, D), lambda i: (0, 0)),
                  _resident((D, D)), _resident((D, D)), _resident((D, D))],
        out_specs=pl.BlockSpec((tm, D), tok),
        scratch_shapes=[pltpu.VMEM((GRID_W, COL_BLOCK, D), F32)],
        compiler_params=_params(("arbitrary",)),
        name="merge",
    )(of, ob, hf4, hb4, po, x2, mod3, gg, gm, wbg, wbm, wo)


FF_TILE = 1408


def _ffn_kernel(x_ref, sh_ref, sc_ref, g2_ref, ng_ref, fg_ref, wi_ref, wo_ref, o_ref):
    x = x_ref[...]
    y = x * lax.rsqrt(jnp.mean(x * x, axis=-1, keepdims=True) + EPS) * ng_ref[...]
    u = (y * (1.0 + sc_ref[...]) + sh_ref[...]).astype(BF16)
    acc = None
    for j in range(D_FF // FF_TILE):
        a = _dot(u, wi_ref[:, j * FF_TILE:(j + 1) * FF_TILE])
        b = _dot(u, wi_ref[:, D_FF + j * FF_TILE:D_FF + (j + 1) * FF_TILE])
        hid = (a * _sigmoid(a) * b).astype(BF16)
        part = _dot(hid, wo_ref[j * FF_TILE:(j + 1) * FF_TILE, :])
        acc = part if acc is None else acc + part
    x2 = x + g2_ref[...] * acc
    o_ref[...] = x2 * lax.rsqrt(jnp.mean(x2 * x2, axis=-1, keepdims=True) + EPS) * fg_ref[...]


def _ffn(x1, mod3, ng, fg, wi, wo, tm):
    m = x1.shape[0]
    per_b = SEQ // tm
    tok = lambda i: (i, 0)
    modspec = lambda c: pl.BlockSpec((None, 1, D), lambda i: (i // per_b, 0, c))
    return pl.pallas_call(
        _ffn_kernel,
        out_shape=jax.ShapeDtypeStruct((m, D), F32),
        grid=(m // tm,),
        in_specs=[pl.BlockSpec((tm, D), tok), modspec(3), modspec(4), modspec(5),
                  pl.BlockSpec((1, D), lambda i: (0, 0)), pl.BlockSpec((1, D), lambda i: (0, 0)),
                  _resident((D, 2 * D_FF)), _resident((D_FF, D))],
        out_specs=pl.BlockSpec((tm, D), tok),
        compiler_params=_params(("arbitrary",)),
        name="ffn",
    )(x1, mod3, mod3, mod3, ng, fg, wi, wo)


def _split_w_in(w_in):
    gq, gk, gv, gog, lr, mq, mk, mv, mog, gates, bg = jnp.split(
        w_in, [512, 1024, 2048, 3072, 3104, 3616, 4128, 5152, 6176, 6192], axis=1)
    z = lambda n: jnp.zeros((D, n), w_in.dtype)
    wg = jnp.concatenate([gq, gk, gv], axis=1).astype(BF16)
    wm = jnp.concatenate([mq, mk, mv], axis=1).astype(BF16)
    wo = jnp.concatenate([gog, mog, bg], axis=1).astype(BF16)
    wtl = jnp.concatenate([lr, z(128 - 2 * RANK)], axis=1).astype(BF16)
    wtg = jnp.concatenate([z(GATE_LANE), gates, z(128 - GATE_LANE - 16),
                           z(GATE_LANE), gates[:, 4:8], z(4), gates[:, 12:16], z(128 - GATE_LANE - 12)],
                          axis=1).astype(BF16)
    return wg, wm, wo, wtl, wtg


def kernel(x, c, ctx, c_ctx, w_ada, b_ada, norm1_g, w_in, gla_w_up, gla_b_dec, gla_norm_g,
           mlstm_conv_w, mlstm_conv_b, mlstm_b_gate, mlstm_norm_g, w_br_gla, w_br_mlstm, w_out,
           norm2_g, w_ffn_in, w_ffn_out, final_g):
    bsz = x.shape[0]
    row = lambda a: a.reshape(1, -1)

    cvec = jnp.concatenate([c, c_ctx[None, :], jnp.zeros((8 - bsz - 1, D), F32)], axis=0)
    mod3 = _ada(cvec, w_ada[0], row(b_ada[0])).reshape(8, 1, N_MOD)

    wg, wm, wo, wtl, wtg = _split_w_in(w_in[0])
    x2 = x.reshape(bsz * SEQ, D)
    ctx2 = ctx.reshape(bsz * CTX, D)
    g1n = row(norm1_g[0])
    tm = GRID_W * COL_BLOCK
    pg, po, tlr = _inproj(x2, mod3, lambda i: i // (SEQ // tm), g1n, (wg, wo, wtl), (BF16, BF16, F32), tm)
    conv_w = mlstm_conv_w[0]
    conv_b = row(mlstm_conv_b[0])
    pm, tg = _inproj_m(x.reshape(bsz, SEQ // GRID_W, GRID_W, D), mod3,
                       lambda i: i // (GRID_W // COL_BLOCK), g1n, wm, wtg, conv_w, conv_b, tm, True)
    pg_c, tlr_c = _inproj(ctx2, mod3, lambda i: bsz, g1n, (wg, wtl), (BF16, F32), CTX)
    pm_c, tg_c = _inproj_m(ctx2, mod3, lambda i: bsz, g1n, wm, wtg, conv_w, conv_b, CTX, False)

    wup = jnp.zeros((2, 128, HEADS * DK), F32)
    wup = wup.at[0, 0:RANK].set(gla_w_up[0, 0]).at[1, RANK:2 * RANK].set(gla_w_up[0, 1]).astype(BF16)
    bdec = gla_b_dec[0].reshape(2, 1, HEADS * DK)
    s0 = jnp.zeros((bsz, 2 * HEADS, DK, DV), F32)
    (s_ctx,) = _gla_scan(pg_c.reshape(bsz, CTX, 2048), tlr_c.reshape(bsz, CTX, 128),
                         wup, bdec, s0, emit_out=False)
    o_f, o_b, _ = _gla_scan(pg.reshape(bsz, SEQ, 2048), tlr.reshape(bsz, SEQ, 128),
                            wup, bdec, s_ctx, emit_out=True)

    bgate = mlstm_b_gate[0].reshape(1, 16)
    zg = lambda n: jnp.zeros((1, n), F32)
    bg2 = jnp.concatenate([zg(GATE_LANE), bgate, zg(128 - GATE_LANE - 16),
                           zg(GATE_LANE), bgate[:, 4:8], zg(4), bgate[:, 12:16],
                           zg(128 - GATE_LANE - 12)], axis=1)
    c0 = jnp.zeros((bsz, 2 * HEADS, DK, DV), F32)
    v0 = jnp.zeros((bsz, 2 * HEADS, 128), F32)
    c_ctx_s, n_ctx_s, m_ctx_s = _mlstm_scan(
        pm_c.reshape(bsz, CTX, 2048), tg_c.reshape(bsz, CTX, 256), bg2, c0, v0, v0, emit_out=False)
    h_f, h_b, _, _, _ = _mlstm_scan(
        pm.reshape(bsz, SEQ, 2048), tg.reshape(bsz, SEQ, 256), bg2, c_ctx_s, n_ctx_s, m_ctx_s, emit_out=True)

    cm4 = lambda a: a.reshape(bsz, GRID_W, SEQ // GRID_W, D)
    x1 = _merge(o_f.reshape(bsz * SEQ, D), o_b.reshape(bsz * SEQ, D), cm4(h_f), cm4(h_b), po, x2, mod3,
                row(gla_norm_g[0]), row(mlstm_norm_g[0]),
                w_br_gla[0].astype(BF16), w_br_mlstm[0].astype(BF16), w_out[0].astype(BF16))
    out = _ffn(x1, mod3, row(norm2_g[0]), row(final_g),
               w_ffn_in[0].astype(BF16), w_ffn_out[0].astype(BF16), tm)
    return out.reshape(bsz, SEQ, D)
```

```python
import functools

import jax
import jax.numpy as jnp
from jax import lax
from jax.experimental import pallas as pl
from jax.experimental.pallas import tpu as pltpu

D = 1024
SEQ = 4096
CTX = 256
GRID_W = 64
EPS = 1e-6
HEADS = 4
DK = 128
DV = 256
RANK = 16
TAU = 16.0
D_FF = 2816
N_MOD = 6 * D
GATE_LANE = 32
STEP = 256
GLA_CHUNK = 128
MLSTM_CHUNK = 128
COL_BLOCK = 8

F32 = jnp.float32
BF16 = jnp.bfloat16
VMEM_LIMIT = 56 * 1024 * 1024


def _dot(a, b):
    return jnp.dot(a, b, preferred_element_type=F32)


def _dot_nt(a, b):
    return lax.dot_general(a, b, (((1,), (1,)), ((), ())), preferred_element_type=F32)


def _dot_tn(a, b):
    return lax.dot_general(a, b, (((0,), (0,)), ((), ())), preferred_element_type=F32)


def _sigmoid(x):
    return 1.0 / (1.0 + jnp.exp(-x))


def _log_sigmoid(x):
    return jnp.minimum(x, 0.0) - jnp.log1p(jnp.exp(-jnp.abs(x)))


def _cumsum_rows(tri, g):
    g1 = g.astype(BF16)
    g2 = (g - g1.astype(F32)).astype(BF16)
    return _dot(tri, g1) + _dot(tri, g2)


def _tri(n, d):
    row = lax.broadcasted_iota(jnp.int32, (n, n), 0)
    col = lax.broadcasted_iota(jnp.int32, (n, n), 1)
    causal = (col <= row) if d == 0 else (col >= row)
    return causal, jnp.where(causal, 1.0, 0.0).astype(BF16)


def _resident(shape):
    n = len(shape)
    return pl.BlockSpec(shape, lambda *_: (0,) * n, pipeline_mode=pl.Buffered(1))


def _params(sem):
    return pltpu.CompilerParams(dimension_semantics=sem, vmem_limit_bytes=VMEM_LIMIT)


def _scan_units(n_sub):
    return [(d, j) for jj in range(n_sub) for d, j in ((0, jj), (1, n_sub - 1 - jj))]


def _ada_kernel(c_ref, w_ref, b_ref, o_ref):
    cv = c_ref[...]
    s = (cv * _sigmoid(cv)).astype(BF16)
    o_ref[...] = _dot(s, w_ref[...].astype(BF16)) + b_ref[...]


def _ada(cvec, w_ada, b_ada):
    tn = 1024
    return pl.pallas_call(
        _ada_kernel,
        out_shape=jax.ShapeDtypeStruct((8, N_MOD), F32),
        grid=(N_MOD // tn,),
        in_specs=[pl.BlockSpec((8, D), lambda j: (0, 0)),
                  pl.BlockSpec((D, tn), lambda j: (0, j)),
                  pl.BlockSpec((1, tn), lambda j: (0, j))],
        out_specs=pl.BlockSpec((8, tn), lambda j: (0, j)),
        compiler_params=_params(("arbitrary",)),
        name="adaln",
    )(cvec, w_ada, b_ada)


def _norm_mod(x, g, sh, sc):
    y = x * lax.rsqrt(jnp.mean(x * x, axis=-1, keepdims=True) + EPS) * g
    return (y * (1.0 + sc) + sh).astype(BF16)


def _act(v, kind):
    if kind == "silu":
        return v * _sigmoid(v)
    if kind == "sigmoid":
        return _sigmoid(v)
    return v


def _inproj_kernel(x_ref, sh_ref, sc_ref, g_ref, *refs, acts):
    n_out = len(refs) // 2
    u = _norm_mod(x_ref[...], g_ref[...], sh_ref[...], sc_ref[...])
    for w_ref, o_ref, act in zip(refs[:n_out], refs[n_out:], acts):
        n = w_ref.shape[1]
        for jc, j in enumerate(range(0, n, 1024)):
            cs = slice(j, min(j + 1024, n))
            o_ref[:, cs] = _act(_dot(u, w_ref[:, cs]), act[jc] if act else None).astype(o_ref.dtype)


def _inproj(x2, mod3, mod_row, norm_g, weights, out_dtypes, tm, acts=None):
    m = x2.shape[0]
    acts = acts or (None,) * len(weights)
    return pl.pallas_call(
        functools.partial(_inproj_kernel, acts=acts),
        out_shape=tuple(jax.ShapeDtypeStruct((m, w.shape[1]), dt) for w, dt in zip(weights, out_dtypes)),
        grid=(m // tm,),
        in_specs=[pl.BlockSpec((tm, D), lambda i: (i, 0)),
                  pl.BlockSpec((None, 1, D), lambda i: (mod_row(i), 0, 0)),
                  pl.BlockSpec((None, 1, D), lambda i: (mod_row(i), 0, 1)),
                  pl.BlockSpec((1, D), lambda i: (0, 0))] + [_resident(w.shape) for w in weights],
        out_specs=tuple(pl.BlockSpec((tm, w.shape[1]), lambda i: (i, 0)) for w in weights),
        compiler_params=_params(("arbitrary",)),
        name="inproj",
    )(x2, mod3, mod3, norm_g, *weights)


def _inproj_m_kernel(*refs, colmajor, nblk):
    if colmajor:
        x_ref, hp_ref, hn_ref, sh_ref, sc_ref, g_ref, wm_ref, wtg_ref, cw_ref, cb_ref = refs[:10]
        x = jnp.concatenate([x_ref[:, cl, :] for cl in range(COL_BLOCK)] + [hp_ref[7], hn_ref[0]], axis=0)
    else:
        x_ref, sh_ref, sc_ref, g_ref, wm_ref, wtg_ref, cw_ref, cb_ref = refs[:8]
        x = x_ref[...]
    k_ref, qt_ref, kt_ref, v_ref, tg_ref = refs[-5:]
    n = k_ref.shape[0]
    u = _norm_mod(x, g_ref[...], sh_ref[...], sc_ref[...])
    pre = _dot(u, wm_ref[:, 0:1024])
    a = pre[0:n]
    if colmajor:
        j = pl.program_id(0) % nblk
        prev_row = jnp.where(j > 0, pre[n + 7:n + 8], 0.0)
        next_row = jnp.where(j < nblk - 1, pre[n + 8:n + 9], 0.0)
    else:
        prev_row = next_row = jnp.zeros((1, 1024), F32)
    rowq = lax.broadcasted_iota(jnp.int32, (n, 1024), 0)
    ap = jnp.where(rowq == 0, prev_row, pltpu.roll(a, 1, axis=0))
    an = jnp.where(rowq == n - 1, next_row, pltpu.roll(a, n - 1, axis=0))
    conv = ap * cw_ref[0:1, :] + a * cw_ref[1:2, :] + an * cw_ref[2:3, :] + cb_ref[...]
    qk = conv * _sigmoid(conv)
    q = qk[:, 0:512]
    k = qk[:, 512:1024] * (DK ** -0.5)
    k_ref[...] = k.astype(BF16)
    qt_ref[...] = q.T.astype(BF16)
    kt_ref[...] = k.T.astype(BF16)
    um = u[0:n]
    v_ref[...] = _dot(um, wm_ref[:, 1024:2048]).astype(BF16)
    tg_ref[...] = _dot(um, wtg_ref[...])


def _inproj_m(xv, mod3, mod_row, norm_g, wm, wtg, conv_w, conv_b, tm, colmajor):
    full = lambda shape: pl.BlockSpec(shape, lambda i: (0,) * len(shape))
    if colmajor:
        bsz = xv.shape[0]
        tn = SEQ
        nblk = GRID_W // COL_BLOCK
        blk = (None, GRID_W, COL_BLOCK, D)
        halo = (None, 8, COL_BLOCK, D)
        x_specs = [pl.BlockSpec(blk, lambda i: (i // nblk, 0, i % nblk, 0)),
                   pl.BlockSpec(halo, lambda i: (i // nblk, GRID_W // 8 - 1, jnp.maximum(i % nblk - 1, 0), 0)),
                   pl.BlockSpec(halo, lambda i: (i // nblk, 0, jnp.minimum(i % nblk + 1, nblk - 1), 0))]
        xs = (xv, xv, xv)
    else:
        tn = tm
        bsz = xv.shape[0] // tn
        nblk = 1
        x_specs = [pl.BlockSpec((tm, D), lambda i: (i, 0))]
        xs = (xv,)
    tok = lambda w: pl.BlockSpec((None, tm, w), lambda i: (i // nblk, i % nblk, 0))
    tr = pl.BlockSpec((None, 512, tm), lambda i: (i // nblk, 0, i % nblk))
    sds = jax.ShapeDtypeStruct
    return pl.pallas_call(
        functools.partial(_inproj_m_kernel, colmajor=colmajor, nblk=nblk),
        out_shape=(sds((bsz, tn, 512), BF16), sds((bsz, 512, tn), BF16), sds((bsz, 512, tn), BF16),
                   sds((bsz, tn, 1024), BF16), sds((bsz, tn, 256), F32)),
        grid=(bsz * nblk,),
        in_specs=x_specs + [pl.BlockSpec((None, 1, D), lambda i: (mod_row(i), 0, 0)),
                            pl.BlockSpec((None, 1, D), lambda i: (mod_row(i), 0, 1)),
                            full((1, D)), _resident(wm.shape), _resident(wtg.shape),
                            full((3, 1024)), full((1, 1024))],
        out_specs=(tok(512), tr, tr, tok(1024), tok(256)),
        compiler_params=_params(("arbitrary",)),
        name="inproj_m_cm" if colmajor else "inproj_m",
    )(*xs, mod3, mod3, norm_g, wm, wtg, conv_w, conv_b)


def _gla_kernel(pf_ref, pb_ref, tf_ref, tb_ref, wup_ref, bdec_ref, s0_ref, *out_refs, emit_out):
    if emit_out:
        of_ref, ob_ref, st_ref = out_refs
    else:
        (st_ref,) = out_refs
        of_ref = ob_ref = None

    @pl.when(pl.program_id(1) == 0)
    def _():
        st_ref[...] = s0_ref[...]

    lc = GLA_CHUNK
    dirs = ((pf_ref, tf_ref, of_ref), (pb_ref, tb_ref, ob_ref))
    units = _scan_units(STEP // lc)
    masks = [_tri(lc, d) for d in range(2)]
    gs = []
    for d, (p_ref, t_ref, o_ref) in enumerate(dirs):
        z = _dot(t_ref[...].astype(BF16), wup_ref[d]) + bdec_ref[d]
        gs.append(_log_sigmoid(z) * (1.0 / TAU))
    bs = {}
    for d, j in units:
        bs[d, j] = _cumsum_rows(masks[d][1], gs[d][j * lc:(j + 1) * lc])
    ops = {}
    for d, j in units:
        p_ref = dirs[d][0]
        rs = slice(j * lc, (j + 1) * lc)
        b = bs[d, j]
        b_last = b[lc - 1:lc, :] if d == 0 else b[0:1, :]
        b_mid = b[lc // 2 - 1:lc // 2, :] if d == 0 else b[lc // 2:lc // 2 + 1, :]
        q = p_ref[rs, 0:512].astype(F32) * (DK ** -0.5)
        k = p_ref[rs, 512:1024].astype(F32)
        qi = (q * jnp.exp(b)).astype(BF16)
        kl = (k * jnp.exp(b_last - b)).astype(BF16)
        dec = jnp.exp(b_last)
        if emit_out:
            qd = (q * jnp.exp(b - b_mid)).astype(BF16)
            kd = (k * jnp.exp(b_mid - b)).astype(BF16)
        else:
            qd = kd = None
        ops[d, j] = (qi, kl, dec, qd, kd)
    sc, us, dcols = {}, {}, {}
    for d, j in units:
        p_ref = dirs[d][0]
        rs = slice(j * lc, (j + 1) * lc)
        qi, kl, dec, qd, kd = ops[d, j]
        for h in range(HEADS):
            ks = slice(h * DK, (h + 1) * DK)
            v = p_ref[rs, 1024 + h * DV:1024 + (h + 1) * DV]
            if emit_out:
                sc[d, j, h] = jnp.where(masks[d][0], _dot_nt(qd[:, ks], kd[:, ks]), 0.0).astype(BF16)
            us[d, j, h] = _dot_tn(kl[:, ks], v)
            dcols[d, j, h] = jnp.broadcast_to(dec[:, ks], (8, DK)).T[:, 0:1]
    for d in range(2):
        p_ref, _, o_ref = dirs[d]
        for h in range(HEADS):
            ks = slice(h * DK, (h + 1) * DK)
            st = st_ref[d * HEADS + h]
            for dd, j in units:
                if dd != d:
                    continue
                rs = slice(j * lc, (j + 1) * lc)
                if emit_out:
                    v = p_ref[rs, 1024 + h * DV:1024 + (h + 1) * DV]
                    o_ref[rs, h * DV:(h + 1) * DV] = _dot(
                        jnp.concatenate([sc[d, j, h], ops[d, j][0][:, ks]], axis=1),
                        jnp.concatenate([v, st.astype(BF16)], axis=0))
                st = st * dcols[d, j, h] + us[d, j, h]
            st_ref[d * HEADS + h] = st


def _gla_scan(pg, tlr, wup, bdec, s0, emit_out):
    bn, tn, _ = pg.shape
    ns = tn // STEP
    fwd = lambda b, i: (b, i, 0)
    bwd = lambda b, i: (b, ns - 1 - i, 0)
    st_shape = jax.ShapeDtypeStruct((bn, 2 * HEADS, DK, DV), F32)
    st_spec = pl.BlockSpec((None, 2 * HEADS, DK, DV), lambda b, i: (b, 0, 0, 0))
    o_shape = jax.ShapeDtypeStruct((bn, tn, HEADS * DV), F32)
    out_shape = (o_shape, o_shape, st_shape) if emit_out else (st_shape,)
    out_specs = ((pl.BlockSpec((None, STEP, HEADS * DV), fwd),
                  pl.BlockSpec((None, STEP, HEADS * DV), bwd), st_spec) if emit_out else (st_spec,))
    return pl.pallas_call(
        functools.partial(_gla_kernel, emit_out=emit_out),
        out_shape=out_shape,
        grid=(bn, ns),
        in_specs=[pl.BlockSpec((None, STEP, 2048), fwd),
                  pl.BlockSpec((None, STEP, 2048), bwd),
                  pl.BlockSpec((None, STEP, 128), fwd),
                  pl.BlockSpec((None, STEP, 128), bwd),
                  pl.BlockSpec((2, 128, 512), lambda b, i: (0, 0, 0)),
                  pl.BlockSpec((2, 1, 512), lambda b, i: (0, 0, 0)),
                  st_spec],
        out_specs=out_specs,
        compiler_params=_params(("arbitrary", "arbitrary")),
        name="gla_scan_lat" if emit_out else "gla_scan_ctx",
    )(pg, pg, tlr, tlr, wup, bdec, s0)


def _mlstm_kernel(kf_ref, kb_ref, qtf_ref, qtb_ref, ktf_ref, ktb_ref, vf_ref, vb_ref, tf_ref, tb_ref,
                  bg_ref, c0_ref, n0_ref, m0_ref, *rest, emit_out):
    if emit_out:
        of_ref, ob_ref, c_ref, n_ref, m_ref = rest
    else:
        c_ref, n_ref, m_ref = rest
        of_ref = ob_ref = None

    @pl.when(pl.program_id(1) == 0)
    def _():
        c_ref[...] = c0_ref[...]
        n_ref[...] = n0_ref[...]
        m_ref[...] = m0_ref[...]

    lc = MLSTM_CHUNK
    dirs = ((kf_ref, qtf_ref, ktf_ref, vf_ref, tf_ref, of_ref), (kb_ref, qtb_ref, ktb_ref, vb_ref, tb_ref, ob_ref))
    units = _scan_units(STEP // lc)
    tris = [_tri(lc, d) for d in range(2)]
    lane_of = lambda d, h: GATE_LANE + 8 * d + h
    hs = lambda h: slice(h * DK, (h + 1) * DK)
    vs = lambda h: slice(h * DV, (h + 1) * DV)

    mrow = [m_ref[d:d + 1, :] for d in range(2)]
    tiles = {}
    for d, j in units:
        t_ref = dirs[d][4]
        rs = slice(j * lc, (j + 1) * lc)
        ga = t_ref[rs, 0:128] + bg_ref[:, 0:128]
        gb = t_ref[rs, 128:256] + bg_ref[:, 128:256]
        bc = _cumsum_rows(tris[d][1], _log_sigmoid(gb))
        b_last = bc[lc - 1:lc, :] if d == 0 else bc[0:1, :]
        log_key = b_last - bc + ga
        m_new = jnp.maximum(b_last + mrow[d], jnp.max(log_key, axis=0, keepdims=True))
        tiles[d, j] = dict(rmat=ga - bc, bct=bc.T, m_in=mrow[d], wkt=jnp.exp(log_key - m_new).T,
                           decay=jnp.exp(b_last + mrow[d] - m_new))
        mrow[d] = m_new
    for d in range(2):
        m_ref[d:d + 1, :] = mrow[d]

    us, ncols = {}, {}
    for d, j in units:
        k_ref, _, kt_ref, v_ref = dirs[d][:4]
        rs = slice(j * lc, (j + 1) * lc)
        for h in range(HEADS):
            lane = lane_of(d, h)
            wk = tiles[d, j]["wkt"][lane:lane + 1, :]
            kwt = (kt_ref[hs(h), rs].astype(F32) * wk).astype(BF16)
            us[d, j, h] = _dot(kwt, v_ref[rs, vs(h)])
            ncols[d, j, h] = _dot(jnp.broadcast_to(wk, (16, lc)).astype(BF16), k_ref[rs, hs(h)])[0:1]

    n_in = {}
    for d in range(2):
        for h in range(HEADS):
            idx = d * HEADS + h
            lane = lane_of(d, h)
            nvec = n_ref[idx:idx + 1, :]
            for dd, j in units:
                if dd == d:
                    n_in[d, j, h] = nvec
                    nvec = tiles[d, j]["decay"][:, lane:lane + 1] * nvec + ncols[d, j, h]
            n_ref[idx:idx + 1, :] = nvec

    lhs = {}
    if emit_out:
        for d, j in units:
            k_ref, qt_ref = dirs[d][:2]
            rs = slice(j * lc, (j + 1) * lc)
            t = tiles[d, j]
            causal_t = tris[1 - d][0]
            for h in range(HEADS):
                lane = lane_of(d, h)
                qt = qt_ref[hs(h), rs]
                kq = _dot(jnp.concatenate(
                    [k_ref[rs, hs(h)], jnp.broadcast_to(n_in[d, j, h], (16, DK)).astype(BF16)], axis=0), qt)
                rm = jnp.where(causal_t, t["rmat"][:, lane:lane + 1], -jnp.inf)
                mval = t["m_in"][:, lane:lane + 1]
                mx = jnp.maximum(mval, jnp.max(rm, axis=0, keepdims=True))
                wt = jnp.exp(rm - mx) * kq[0:lc]
                w_inter = jnp.exp(mval - mx)
                den = jnp.sum(wt, axis=0, keepdims=True) + w_inter * kq[lc:lc + 1]
                inv = 1.0 / jnp.maximum(jnp.abs(den), jnp.exp(-(t["bct"][lane:lane + 1, :] + mx)))
                lhs[d, j, h] = jnp.concatenate(
                    [(wt * inv).astype(BF16), (qt.astype(F32) * (w_inter * inv)).astype(BF16)], axis=0)

    for d in range(2):
        v_ref, o_ref = dirs[d][3], dirs[d][5]
        for h in range(HEADS):
            idx = d * HEADS + h
            lane = lane_of(d, h)
            cmat = c_ref[idx]
            for dd, j in units:
                if dd != d:
                    continue
                rs = slice(j * lc, (j + 1) * lc)
                if emit_out:
                    o_ref[rs, vs(h)] = _dot_tn(
                        lhs[d, j, h], jnp.concatenate([v_ref[rs, vs(h)], cmat.astype(BF16)], axis=0))
                cmat = tiles[d, j]["decay"][:, lane:lane + 1] * cmat + us[d, j, h]
            c_ref[idx] = cmat


def _mlstm_scan(k, qt, kt, v, tg, bgate, c0, n0, m0, emit_out):
    bn, tn, _ = k.shape
    ns = tn // STEP
    fwd = lambda b, i: (b, i, 0)
    bwd = lambda b, i: (b, ns - 1 - i, 0)
    fwd_t = lambda b, i: (b, 0, i)
    bwd_t = lambda b, i: (b, 0, ns - 1 - i)
    c_shape = jax.ShapeDtypeStruct((bn, 2 * HEADS, DK, DV), F32)
    v_shape = jax.ShapeDtypeStruct((bn, 2 * HEADS, 128), F32)
    c_spec = pl.BlockSpec((None, 2 * HEADS, DK, DV), lambda b, i: (b, 0, 0, 0))
    v_spec = pl.BlockSpec((None, 2 * HEADS, 128), lambda b, i: (b, 0, 0))
    if emit_out:
        o_shape = jax.ShapeDtypeStruct((bn, tn, HEADS * DV), F32)
        out_shape = (o_shape, o_shape, c_shape, v_shape, v_shape)
        out_specs = (pl.BlockSpec((None, STEP, HEADS * DV), fwd),
                     pl.BlockSpec((None, STEP, HEADS * DV), bwd), c_spec, v_spec, v_spec)
    else:
        out_shape = (c_shape, v_shape, v_shape)
        out_specs = (c_spec, v_spec, v_spec)
    both = lambda shape, f, g: [pl.BlockSpec(shape, f), pl.BlockSpec(shape, g)]
    return pl.pallas_call(
        functools.partial(_mlstm_kernel, emit_out=emit_out),
        out_shape=out_shape,
        grid=(bn, ns),
        in_specs=(both((None, STEP, 512), fwd, bwd) + both((None, 512, STEP), fwd_t, bwd_t)
                  + both((None, 512, STEP), fwd_t, bwd_t) + both((None, STEP, 1024), fwd, bwd)
                  + both((None, STEP, 256), fwd, bwd)
                  + [pl.BlockSpec((1, 256), lambda b, i: (0, 0)), c_spec, v_spec, v_spec]),
        out_specs=out_specs,
        compiler_params=_params(("arbitrary", "arbitrary")),
        name="mlstm_scan_lat" if emit_out else "mlstm_scan_ctx",
    )(k, k, qt, qt, kt, kt, v, v, tg, tg, bgate, c0, n0, m0)


def _head_norm(o, g):
    parts = []
    for h in range(HEADS):
        oh = o[:, h * DV:(h + 1) * DV]
        parts.append(oh * lax.rsqrt(jnp.mean(oh * oh, axis=-1, keepdims=True) + EPS))
    return jnp.concatenate(parts, axis=-1) * g


def _merge_kernel(of_ref, ob_ref, hf_ref, hb_ref, po_ref, x_ref, g1_ref, gg_ref, gm_ref,
                  wbg_ref, wbm_ref, wo_ref, o_ref, hsum_ref):
    hsum_ref[...] = hf_ref[...] + hb_ref[...]
    hm = jnp.concatenate([hsum_ref[:, rl, :] for rl in range(COL_BLOCK)], axis=0)
    y_gla = _head_norm(of_ref[...] + ob_ref[...], gg_ref[...]) * po_ref[:, 0:1024].astype(F32)
    y_m = _head_norm(hm, gm_ref[...]) * po_ref[:, 1024:2048].astype(F32)
    gate_g = po_ref[:, 2048:3072].astype(F32)
    gate_m = po_ref[:, 3072:4096].astype(F32)
    y = (gate_g * _dot(y_gla.astype(BF16), wbg_ref[...])
         + gate_m * _dot(y_m.astype(BF16), wbm_ref[...]))
    mix = _dot(y.astype(BF16), wo_ref[...])
    o_ref[...] = x_ref[...] + g1_ref[...] * mix


def _merge(of, ob, hf4, hb4, po, x2, mod3, gg, gm, wbg, wbm, wo):
    m = x2.shape[0]
    tm = GRID_W * COL_BLOCK
    per_b = SEQ // tm
    tok = lambda i: (i, 0)
    hspec = pl.BlockSpec((None, GRID_W, COL_BLOCK, D), lambda i: (i // per_b, 0, i % per_b, 0))
    return pl.pallas_call(
        _merge_kernel,
        out_shape=jax.ShapeDtypeStruct((m, D), F32),
        grid=(m // tm,),
        in_specs=[pl.BlockSpec((tm, D), tok), pl.BlockSpec((tm, D), tok), hspec, hspec,
                  pl.BlockSpec((tm, 4096), tok), pl.BlockSpec((tm, D), tok),
                  pl.BlockSpec((None, 1, D), lambda i: (i // per_b, 0, 2)),
                  pl.BlockSpec((1, D), lambda i: (0, 0)), pl.BlockSpec((1, D), lambda i: (0, 0)),
                  _resident((D, D)), _resident((D, D)), _resident((D, D))],
        out_specs=pl.BlockSpec((tm, D), tok),
        scratch_shapes=[pltpu.VMEM((GRID_W, COL_BLOCK, D), F32)],
        compiler_params=_params(("arbitrary",)),
        name="merge",
    )(of, ob, hf4, hb4, po, x2, mod3, gg, gm, wbg, wbm, wo)


FF_TILES = ((0, 1280), (1280, 2816))


def _ffn_kernel(x_ref, sh_ref, sc_ref, g2_ref, ng_ref, fg_ref, wi_ref, wo_ref, o_ref):
    x = x_ref[...]
    u = _norm_mod(x, ng_ref[...], sh_ref[...], sc_ref[...])
    acc = None
    for lo, hi in FF_TILES:
        a = _dot(u, wi_ref[:, lo:hi])
        b = _dot(u, wi_ref[:, D_FF + lo:D_FF + hi])
        hid = (a * _sigmoid(a) * b).astype(BF16)
        part = _dot(hid, wo_ref[lo:hi, :])
        acc = part if acc is None else acc + part
    x2 = x + g2_ref[...] * acc
    o_ref[...] = x2 * lax.rsqrt(jnp.mean(x2 * x2, axis=-1, keepdims=True) + EPS) * fg_ref[...]


def _ffn(x1, mod3, ng, fg, wi, wo, tm):
    m = x1.shape[0]
    per_b = SEQ // tm
    tok = lambda i: (i, 0)
    modspec = lambda c: pl.BlockSpec((None, 1, D), lambda i: (i // per_b, 0, c))
    return pl.pallas_call(
        _ffn_kernel,
        out_shape=jax.ShapeDtypeStruct((m, D), F32),
        grid=(m // tm,),
        in_specs=[pl.BlockSpec((tm, D), tok), modspec(3), modspec(4), modspec(5),
                  pl.BlockSpec((1, D), lambda i: (0, 0)), pl.BlockSpec((1, D), lambda i: (0, 0)),
                  _resident((D, 2 * D_FF)), _resident((D_FF, D))],
        out_specs=pl.BlockSpec((tm, D), tok),
        compiler_params=_params(("arbitrary",)),
        name="ffn",
    )(x1, mod3, mod3, mod3, ng, fg, wi, wo)


def _split_w_in(w_in):
    gq, gk, gv, gog, lr, mq, mk, mv, mog, gates, bg = jnp.split(
        w_in, [512, 1024, 2048, 3072, 3104, 3616, 4128, 5152, 6176, 6192], axis=1)
    z = lambda n: jnp.zeros((D, n), w_in.dtype)
    wg = jnp.concatenate([gq, gk, gv], axis=1).astype(BF16)
    wm = jnp.concatenate([mq, mk, mv], axis=1).astype(BF16)
    wo = jnp.concatenate([gog, mog, bg], axis=1).astype(BF16)
    wtl = jnp.concatenate([lr, z(128 - 2 * RANK)], axis=1).astype(BF16)
    wtg = jnp.concatenate([z(GATE_LANE), gates, z(128 - GATE_LANE - 16),
                           z(GATE_LANE), gates[:, 4:8], z(4), gates[:, 12:16], z(128 - GATE_LANE - 12)],
                          axis=1).astype(BF16)
    return wg, wm, wo, wtl, wtg


def kernel(x, c, ctx, c_ctx, w_ada, b_ada, norm1_g, w_in, gla_w_up, gla_b_dec, gla_norm_g,
           mlstm_conv_w, mlstm_conv_b, mlstm_b_gate, mlstm_norm_g, w_br_gla, w_br_mlstm, w_out,
           norm2_g, w_ffn_in, w_ffn_out, final_g):
    bsz = x.shape[0]
    row = lambda a: a.reshape(1, -1)

    cvec = jnp.concatenate([c, c_ctx[None, :], jnp.zeros((8 - bsz - 1, D), F32)], axis=0)
    mod3 = _ada(cvec, w_ada[0], row(b_ada[0])).reshape(8, 1, N_MOD)

    wg, wm, wo, wtl, wtg = _split_w_in(w_in[0])
    x2 = x.reshape(bsz * SEQ, D)
    ctx2 = ctx.reshape(bsz * CTX, D)
    g1n = row(norm1_g[0])
    tm = GRID_W * COL_BLOCK
    po, pg, tlr = _inproj(x2, mod3, lambda i: i // (SEQ // tm), g1n, (wo, wg, wtl), (BF16, BF16, F32), tm,
                          acts=(("silu", "sigmoid", "sigmoid", "sigmoid"), None, None))
    conv_w = mlstm_conv_w[0]
    conv_b = row(mlstm_conv_b[0])
    m_lat = _inproj_m(x.reshape(bsz, SEQ // GRID_W, GRID_W, D), mod3,
                      lambda i: i // (GRID_W // COL_BLOCK), g1n, wm, wtg, conv_w, conv_b, tm, True)
    pg_c, tlr_c = _inproj(ctx2, mod3, lambda i: bsz, g1n, (wg, wtl), (BF16, F32), CTX)
    m_ctx = _inproj_m(ctx2, mod3, lambda i: bsz, g1n, wm, wtg, conv_w, conv_b, CTX, False)

    wup = jnp.zeros((2, 128, HEADS * DK), F32)
    wup = wup.at[0, 0:RANK].set(gla_w_up[0, 0]).at[1, RANK:2 * RANK].set(gla_w_up[0, 1]).astype(BF16)
    bdec = gla_b_dec[0].reshape(2, 1, HEADS * DK)
    s0 = jnp.zeros((bsz, 2 * HEADS, DK, DV), F32)
    (s_ctx,) = _gla_scan(pg_c.reshape(bsz, CTX, 2048), tlr_c.reshape(bsz, CTX, 128),
                         wup, bdec, s0, emit_out=False)
    o_f, o_b, _ = _gla_scan(pg.reshape(bsz, SEQ, 2048), tlr.reshape(bsz, SEQ, 128),
                            wup, bdec, s_ctx, emit_out=True)

    bgate = mlstm_b_gate[0].reshape(1, 16)
    zg = lambda n: jnp.zeros((1, n), F32)
    bg2 = jnp.concatenate([zg(GATE_LANE), bgate, zg(128 - GATE_LANE - 16),
                           zg(GATE_LANE), bgate[:, 4:8], zg(4), bgate[:, 12:16],
                           zg(128 - GATE_LANE - 12)], axis=1)
    c0 = jnp.zeros((bsz, 2 * HEADS, DK, DV), F32)
    v0 = jnp.zeros((bsz, 2 * HEADS, 128), F32)
    c_ctx_s, n_ctx_s, m_ctx_s = _mlstm_scan(*m_ctx, bg2, c0, v0, v0, emit_out=False)
    h_f, h_b, _, _, _ = _mlstm_scan(*m_lat, bg2, c_ctx_s, n_ctx_s, m_ctx_s, emit_out=True)

    cm4 = lambda a: a.reshape(bsz, GRID_W, SEQ // GRID_W, D)
    x1 = _merge(o_f.reshape(bsz * SEQ, D), o_b.reshape(bsz * SEQ, D), cm4(h_f), cm4(h_b), po, x2, mod3,
                row(gla_norm_g[0]), row(mlstm_norm_g[0]),
                w_br_gla[0].astype(BF16), w_br_mlstm[0].astype(BF16), w_out[0].astype(BF16))
    out = _ffn(x1, mod3, row(norm2_g[0]), row(final_g),
               w_ffn_in[0].astype(BF16), w_ffn_out[0].astype(BF16), tm)
    return out.reshape(bsz, SEQ, D)
```

```python
import functools
import math

import jax
import jax.numpy as jnp
from jax import lax
from jax.experimental import pallas as pl
from jax.experimental.pallas import tpu as pltpu

D = 1024
SEQ = 4096
CTX = 256
GRID_W = 64
EPS = 1e-6
HEADS = 4
DK = 128
DV = 256
RANK = 16
TAU = 16.0
D_FF = 2816
N_MOD = 6 * D
GATE_LANE = 32
STEP = 256
GLA_CHUNK = 128
MLSTM_CHUNK = 128
COL_BLOCK = 8

LOG2E = math.log2(math.e)
LOG2_QSCALE = -0.5 * math.log2(DK)

F32 = jnp.float32
BF16 = jnp.bfloat16
VMEM_LIMIT = 56 * 1024 * 1024


def _dot(a, b):
    return jnp.dot(a, b, preferred_element_type=F32)


def _dot_nt(a, b):
    return lax.dot_general(a, b, (((1,), (1,)), ((), ())), preferred_element_type=F32)


def _dot_tn(a, b):
    return lax.dot_general(a, b, (((0,), (0,)), ((), ())), preferred_element_type=F32)


def _sigmoid(x):
    return 1.0 / (1.0 + jnp.exp(-x))


def _log_sigmoid(x):
    return jnp.minimum(x, 0.0) - jnp.log1p(jnp.exp(-jnp.abs(x)))


def _cumsum_rows(tri, g):
    g1 = g.astype(BF16)
    g2 = (g - g1.astype(F32)).astype(BF16)
    return _dot(tri, g1) + _dot(tri, g2)


def _tri(n, d):
    row = lax.broadcasted_iota(jnp.int32, (n, n), 0)
    col = lax.broadcasted_iota(jnp.int32, (n, n), 1)
    causal = (col <= row) if d == 0 else (col >= row)
    return causal, jnp.where(causal, 1.0, 0.0).astype(BF16)


def _resident(shape):
    n = len(shape)
    return pl.BlockSpec(shape, lambda *_: (0,) * n, pipeline_mode=pl.Buffered(1))


def _params(sem):
    return pltpu.CompilerParams(dimension_semantics=sem, vmem_limit_bytes=VMEM_LIMIT)


def _scan_units(n_sub):
    return [(d, j) for jj in range(n_sub) for d, j in ((0, jj), (1, n_sub - 1 - jj))]


def _ada_kernel(c_ref, w_ref, b_ref, o_ref):
    cv = c_ref[...]
    s = (cv * _sigmoid(cv)).astype(BF16)
    o_ref[...] = _dot(s, w_ref[...].astype(BF16)) + b_ref[...]


def _ada(cvec, w_ada, b_ada):
    tn = 1024
    return pl.pallas_call(
        _ada_kernel,
        out_shape=jax.ShapeDtypeStruct((8, N_MOD), F32),
        grid=(N_MOD // tn,),
        in_specs=[pl.BlockSpec((8, D), lambda j: (0, 0)),
                  pl.BlockSpec((D, tn), lambda j: (0, j)),
                  pl.BlockSpec((1, tn), lambda j: (0, j))],
        out_specs=pl.BlockSpec((8, tn), lambda j: (0, j)),
        compiler_params=_params(("arbitrary",)),
        name="adaln",
    )(cvec, w_ada, b_ada)


def _norm_mod(x, g, sh, sc):
    y = x * lax.rsqrt(jnp.mean(x * x, axis=-1, keepdims=True) + EPS) * g
    return (y * (1.0 + sc) + sh).astype(BF16)


def _act(v, kind):
    if kind == "silu":
        return v * _sigmoid(v)
    if kind == "sigmoid":
        return _sigmoid(v)
    return v


def _inproj_kernel(x_ref, sh_ref, sc_ref, g_ref, *refs, acts):
    n_out = len(refs) // 2
    u = _norm_mod(x_ref[...], g_ref[...], sh_ref[...], sc_ref[...])
    for w_ref, o_ref, act in zip(refs[:n_out], refs[n_out:], acts):
        n = w_ref.shape[0]
        for jc, j in enumerate(range(0, n, 1024)):
            cs = slice(j, min(j + 1024, n))
            o_ref[:, cs] = _act(_dot_nt(u, w_ref[cs, :]), act[jc] if act else None).astype(o_ref.dtype)


def _inproj(x2, mod3, mod_row, norm_g, weights, out_dtypes, tm, acts=None):
    m = x2.shape[0]
    acts = acts or (None,) * len(weights)
    return pl.pallas_call(
        functools.partial(_inproj_kernel, acts=acts),
        out_shape=tuple(jax.ShapeDtypeStruct((m, w.shape[0]), dt) for w, dt in zip(weights, out_dtypes)),
        grid=(m // tm,),
        in_specs=[pl.BlockSpec((tm, D), lambda i: (i, 0)),
                  pl.BlockSpec((None, 1, D), lambda i: (mod_row(i), 0, 0)),
                  pl.BlockSpec((None, 1, D), lambda i: (mod_row(i), 0, 1)),
                  pl.BlockSpec((1, D), lambda i: (0, 0))] + [_resident(w.shape) for w in weights],
        out_specs=tuple(pl.BlockSpec((tm, w.shape[0]), lambda i: (i, 0)) for w in weights),
        compiler_params=_params(("arbitrary",)),
        name="inproj",
    )(x2, mod3, mod3, norm_g, *weights)


def _inproj_m_kernel(*refs, colmajor, nblk):
    if colmajor:
        x_ref, hp_ref, hn_ref, sh_ref, sc_ref, g_ref, wm_ref, wtg_ref, cw_ref, cb_ref = refs[:10]
        x = jnp.concatenate([x_ref[:, cl, :] for cl in range(COL_BLOCK)] + [hp_ref[7], hn_ref[0]], axis=0)
    else:
        x_ref, sh_ref, sc_ref, g_ref, wm_ref, wtg_ref, cw_ref, cb_ref = refs[:8]
        x = x_ref[...]
    k_ref, qt_ref, kt_ref, v_ref, tg_ref = refs[-5:]
    n = k_ref.shape[0]
    u = _norm_mod(x, g_ref[...], sh_ref[...], sc_ref[...])
    pre = _dot_nt(u, wm_ref[0:1024, :])
    a = pre[0:n]
    if colmajor:
        j = pl.program_id(0) % nblk
        prev_row = jnp.where(j > 0, pre[n + 7:n + 8], 0.0)
        next_row = jnp.where(j < nblk - 1, pre[n + 8:n + 9], 0.0)
    else:
        prev_row = next_row = jnp.zeros((1, 1024), F32)
    rowq = lax.broadcasted_iota(jnp.int32, (n, 1024), 0)
    ap = jnp.where(rowq == 0, prev_row, pltpu.roll(a, 1, axis=0))
    an = jnp.where(rowq == n - 1, next_row, pltpu.roll(a, n - 1, axis=0))
    conv = ap * cw_ref[0:1, :] + a * cw_ref[1:2, :] + an * cw_ref[2:3, :] + cb_ref[...]
    qk = conv * _sigmoid(conv)
    q = qk[:, 0:512]
    k = qk[:, 512:1024] * (DK ** -0.5)
    k_ref[...] = k.astype(BF16)
    qt_ref[...] = q.T.astype(BF16)
    kt_ref[...] = k.T.astype(BF16)
    um = u[0:n]
    v_ref[...] = _dot_nt(um, wm_ref[1024:2048, :]).astype(BF16)
    tg_ref[...] = _dot_nt(um, wtg_ref[...])


def _inproj_m(xv, mod3, mod_row, norm_g, wm, wtg, conv_w, conv_b, tm, colmajor):
    full = lambda shape: pl.BlockSpec(shape, lambda i: (0,) * len(shape))
    if colmajor:
        bsz = xv.shape[0]
        tn = SEQ
        nblk = GRID_W // COL_BLOCK
        blk = (None, GRID_W, COL_BLOCK, D)
        halo = (None, 8, COL_BLOCK, D)
        x_specs = [pl.BlockSpec(blk, lambda i: (i // nblk, 0, i % nblk, 0)),
                   pl.BlockSpec(halo, lambda i: (i // nblk, GRID_W // 8 - 1, jnp.maximum(i % nblk - 1, 0), 0)),
                   pl.BlockSpec(halo, lambda i: (i // nblk, 0, jnp.minimum(i % nblk + 1, nblk - 1), 0))]
        xs = (xv, xv, xv)
    else:
        tn = tm
        bsz = xv.shape[0] // tn
        nblk = 1
        x_specs = [pl.BlockSpec((tm, D), lambda i: (i, 0))]
        xs = (xv,)
    tok = lambda w: pl.BlockSpec((None, tm, w), lambda i: (i // nblk, i % nblk, 0))
    tr = pl.BlockSpec((None, 512, tm), lambda i: (i // nblk, 0, i % nblk))
    sds = jax.ShapeDtypeStruct
    return pl.pallas_call(
        functools.partial(_inproj_m_kernel, colmajor=colmajor, nblk=nblk),
        out_shape=(sds((bsz, tn, 512), BF16), sds((bsz, 512, tn), BF16), sds((bsz, 512, tn), BF16),
                   sds((bsz, tn, 1024), BF16), sds((bsz, tn, 256), F32)),
        grid=(bsz * nblk,),
        in_specs=x_specs + [pl.BlockSpec((None, 1, D), lambda i: (mod_row(i), 0, 0)),
                            pl.BlockSpec((None, 1, D), lambda i: (mod_row(i), 0, 1)),
                            full((1, D)), _resident(wm.shape), _resident(wtg.shape),
                            full((3, 1024)), full((1, 1024))],
        out_specs=(tok(512), tr, tr, tok(1024), tok(256)),
        compiler_params=_params(("arbitrary",)),
        name="inproj_m_cm" if colmajor else "inproj_m",
    )(*xs, mod3, mod3, norm_g, wm, wtg, conv_w, conv_b)


def _sum_directions(acc_ref, out_ref, blk, rs, cs, val):
    tot = acc_ref[blk, rs, cs] + val
    acc_ref[blk, rs, cs] = tot
    out_ref[rs, cs] = tot.astype(out_ref.dtype)


def _gla_kernel(pf_ref, pb_ref, tf_ref, tb_ref, wup_ref, bdec_ref, s0_ref, *rest, emit_out, ns):
    i = pl.program_id(1)
    if emit_out:
        olo_ref, ohi_ref, st_ref, acc_ref = rest
    else:
        (st_ref,) = rest
        olo_ref = ohi_ref = acc_ref = None

    @pl.when(i == 0)
    def _():
        st_ref[...] = s0_ref[...]
        if emit_out:
            acc_ref[...] = jnp.zeros(acc_ref.shape, F32)

    lc = GLA_CHUNK
    dirs = ((pf_ref, tf_ref, ohi_ref), (pb_ref, tb_ref, olo_ref))
    blks = (i, ns - 1 - i)
    units = _scan_units(STEP // lc)
    masks = [_tri(lc, d) for d in range(2)]
    gs = []
    for d, (p_ref, t_ref, o_ref) in enumerate(dirs):
        z = _dot(t_ref[...].astype(BF16), wup_ref[d]) + bdec_ref[d]
        gs.append(_log_sigmoid(z) * (LOG2E / TAU))
    bs = {}
    for d, j in units:
        bs[d, j] = _cumsum_rows(masks[d][1], gs[d][j * lc:(j + 1) * lc])
    ops = {}
    for d, j in units:
        p_ref = dirs[d][0]
        rs = slice(j * lc, (j + 1) * lc)
        b = bs[d, j]
        b_last = b[lc - 1:lc, :] if d == 0 else b[0:1, :]
        b_mid = b[lc // 2 - 1:lc // 2, :] if d == 0 else b[lc // 2:lc // 2 + 1, :]
        q = p_ref[rs, 0:512]
        k = p_ref[rs, 512:1024]
        qi = q * jnp.exp2(b).astype(BF16)
        kl = k * jnp.exp2((b_last + LOG2_QSCALE) - b).astype(BF16)
        dec = jnp.exp2(b_last)
        if emit_out:
            qd = q * jnp.exp2(b - b_mid).astype(BF16)
            kd = k * jnp.exp2((b_mid + LOG2_QSCALE) - b).astype(BF16)
        else:
            qd = kd = None
        ops[d, j] = (qi, kl, dec, qd, kd)
    sc, us, dcols = {}, {}, {}
    for d, j in units:
        p_ref = dirs[d][0]
        rs = slice(j * lc, (j + 1) * lc)
        qi, kl, dec, qd, kd = ops[d, j]
        for h in range(HEADS):
            ks = slice(h * DK, (h + 1) * DK)
            v = p_ref[rs, 1024 + h * DV:1024 + (h + 1) * DV]
            if emit_out:
                sc[d, j, h] = jnp.where(masks[d][0], _dot_nt(qd[:, ks], kd[:, ks]), 0.0).astype(BF16)
            us[d, j, h] = _dot_tn(kl[:, ks], v)
            dcols[d, j, h] = jnp.broadcast_to(dec[:, ks], (8, DK)).T[:, 0:1]
    for d in range(2):
        p_ref, _, o_ref = dirs[d]
        for h in range(HEADS):
            ks = slice(h * DK, (h + 1) * DK)
            st = st_ref[d * HEADS + h]
            for dd, j in units:
                if dd != d:
                    continue
                rs = slice(j * lc, (j + 1) * lc)
                if emit_out:
                    v = p_ref[rs, 1024 + h * DV:1024 + (h + 1) * DV]
                    o = _dot(jnp.concatenate([sc[d, j, h], ops[d, j][0][:, ks]], axis=1),
                             jnp.concatenate([v, st.astype(BF16)], axis=0))
                    _sum_directions(acc_ref, o_ref, blks[d], rs, slice(h * DV, (h + 1) * DV), o)
                st = st * dcols[d, j, h] + us[d, j, h]
            st_ref[d * HEADS + h] = st


def _half_specs(ns, width):
    half = ns // 2
    lo = pl.BlockSpec((None, STEP, width), lambda b, i: (b, jnp.minimum(ns - 1 - i, half - 1), 0))
    hi = pl.BlockSpec((None, STEP, width), lambda b, i: (b, jnp.maximum(i - half, 0), 0))
    return lo, hi


def _gla_scan(pg, tlr, wup, bdec, s0, emit_out):
    bn, tn, _ = pg.shape
    ns = tn // STEP
    fwd = lambda b, i: (b, i, 0)
    bwd = lambda b, i: (b, ns - 1 - i, 0)
    st_shape = jax.ShapeDtypeStruct((bn, 2 * HEADS, DK, DV), F32)
    st_spec = pl.BlockSpec((None, 2 * HEADS, DK, DV), lambda b, i: (b, 0, 0, 0))
    o_shape = jax.ShapeDtypeStruct((bn, tn // 2, HEADS * DV), BF16)
    out_shape = (o_shape, o_shape, st_shape) if emit_out else (st_shape,)
    out_specs = (*_half_specs(ns, HEADS * DV), st_spec) if emit_out else (st_spec,)
    return pl.pallas_call(
        functools.partial(_gla_kernel, emit_out=emit_out, ns=ns),
        out_shape=out_shape,
        grid=(bn, ns),
        scratch_shapes=[pltpu.VMEM((ns, STEP, HEADS * DV), F32)] if emit_out else [],
        in_specs=[pl.BlockSpec((None, STEP, 2048), fwd),
                  pl.BlockSpec((None, STEP, 2048), bwd),
                  pl.BlockSpec((None, STEP, 128), fwd),
                  pl.BlockSpec((None, STEP, 128), bwd),
                  pl.BlockSpec((2, 128, 512), lambda b, i: (0, 0, 0)),
                  pl.BlockSpec((2, 1, 512), lambda b, i: (0, 0, 0)),
                  st_spec],
        out_specs=out_specs,
        compiler_params=_params(("arbitrary", "arbitrary")),
        name="gla_scan_lat" if emit_out else "gla_scan_ctx",
    )(pg, pg, tlr, tlr, wup, bdec, s0)


def _mlstm_kernel(kf_ref, kb_ref, qtf_ref, qtb_ref, ktf_ref, ktb_ref, vf_ref, vb_ref, tf_ref, tb_ref,
                  bg_ref, c0_ref, n0_ref, m0_ref, *rest, emit_out, ns):
    i = pl.program_id(1)
    if emit_out:
        hlo_ref, hhi_ref, c_ref, n_ref, m_ref, acc_ref = rest
    else:
        c_ref, n_ref, m_ref = rest
        hlo_ref = hhi_ref = acc_ref = None

    @pl.when(i == 0)
    def _():
        c_ref[...] = c0_ref[...]
        n_ref[...] = n0_ref[...]
        m_ref[...] = m0_ref[...]
        if emit_out:
            acc_ref[...] = jnp.zeros(acc_ref.shape, F32)

    lc = MLSTM_CHUNK
    dirs = ((kf_ref, qtf_ref, ktf_ref, vf_ref, tf_ref, hhi_ref), (kb_ref, qtb_ref, ktb_ref, vb_ref, tb_ref, hlo_ref))
    blks = (i, ns - 1 - i)
    units = _scan_units(STEP // lc)
    tris = [_tri(lc, d) for d in range(2)]
    lane_of = lambda d, h: GATE_LANE + 8 * d + h
    hs = lambda h: slice(h * DK, (h + 1) * DK)
    vs = lambda h: slice(h * DV, (h + 1) * DV)

    mrow = [m_ref[d:d + 1, :] for d in range(2)]
    tiles = {}
    for d, j in units:
        t_ref = dirs[d][4]
        rs = slice(j * lc, (j + 1) * lc)
        ga = (t_ref[rs, 0:128] + bg_ref[:, 0:128]) * LOG2E
        gb = t_ref[rs, 128:256] + bg_ref[:, 128:256]
        bc = _cumsum_rows(tris[d][1], _log_sigmoid(gb) * LOG2E)
        b_last = bc[lc - 1:lc, :] if d == 0 else bc[0:1, :]
        log_key = b_last - bc + ga
        m_new = jnp.maximum(b_last + mrow[d], jnp.max(log_key, axis=0, keepdims=True))
        tiles[d, j] = dict(rmat=ga - bc, bct=bc.T, m_in=mrow[d], wkt=jnp.exp2(log_key - m_new).T,
                           decay=jnp.exp2(b_last + mrow[d] - m_new))
        mrow[d] = m_new
    for d in range(2):
        m_ref[d:d + 1, :] = mrow[d]

    us, ncols = {}, {}
    for d, j in units:
        k_ref, _, kt_ref, v_ref = dirs[d][:4]
        rs = slice(j * lc, (j + 1) * lc)
        for h in range(HEADS):
            lane = lane_of(d, h)
            wk = tiles[d, j]["wkt"][lane:lane + 1, :]
            kwt = (kt_ref[hs(h), rs].astype(F32) * wk).astype(BF16)
            us[d, j, h] = _dot(kwt, v_ref[rs, vs(h)])
            ncols[d, j, h] = _dot(jnp.broadcast_to(wk, (16, lc)).astype(BF16), k_ref[rs, hs(h)])[0:1]

    n_in = {}
    for d in range(2):
        for h in range(HEADS):
            idx = d * HEADS + h
            lane = lane_of(d, h)
            nvec = n_ref[idx:idx + 1, :]
            for dd, j in units:
                if dd == d:
                    n_in[d, j, h] = nvec
                    nvec = tiles[d, j]["decay"][:, lane:lane + 1] * nvec + ncols[d, j, h]
            n_ref[idx:idx + 1, :] = nvec

    lhs = {}
    if emit_out:
        for d, j in units:
            k_ref, qt_ref = dirs[d][:2]
            rs = slice(j * lc, (j + 1) * lc)
            t = tiles[d, j]
            causal_t = tris[1 - d][0]
            for h in range(HEADS):
                lane = lane_of(d, h)
                qt = qt_ref[hs(h), rs]
                kq = _dot(jnp.concatenate(
                    [k_ref[rs, hs(h)], jnp.broadcast_to(n_in[d, j, h], (16, DK)).astype(BF16)], axis=0), qt)
                rm = jnp.where(causal_t, t["rmat"][:, lane:lane + 1], -jnp.inf)
                mval = t["m_in"][:, lane:lane + 1]
                mx = jnp.maximum(mval, jnp.max(rm, axis=0, keepdims=True))
                wt = jnp.exp2(rm - mx) * kq[0:lc]
                w_inter = jnp.exp2(mval - mx)
                den = jnp.sum(wt, axis=0, keepdims=True) + w_inter * kq[lc:lc + 1]
                inv = 1.0 / jnp.maximum(jnp.abs(den), jnp.exp2(-(t["bct"][lane:lane + 1, :] + mx)))
                lhs[d, j, h] = jnp.concatenate(
                    [(wt * inv).astype(BF16), (qt.astype(F32) * (w_inter * inv)).astype(BF16)], axis=0)

    for d in range(2):
        v_ref, o_ref = dirs[d][3], dirs[d][5]
        for h in range(HEADS):
            idx = d * HEADS + h
            lane = lane_of(d, h)
            cmat = c_ref[idx]
            for dd, j in units:
                if dd != d:
                    continue
                rs = slice(j * lc, (j + 1) * lc)
                if emit_out:
                    o = _dot_tn(lhs[d, j, h], jnp.concatenate([v_ref[rs, vs(h)], cmat.astype(BF16)], axis=0))
                    _sum_directions(acc_ref, o_ref, blks[d], rs, vs(h), o)
                cmat = tiles[d, j]["decay"][:, lane:lane + 1] * cmat + us[d, j, h]
            c_ref[idx] = cmat


def _mlstm_scan(k, qt, kt, v, tg, bgate, c0, n0, m0, emit_out):
    bn, tn, _ = k.shape
    ns = tn // STEP
    fwd = lambda b, i: (b, i, 0)
    bwd = lambda b, i: (b, ns - 1 - i, 0)
    fwd_t = lambda b, i: (b, 0, i)
    bwd_t = lambda b, i: (b, 0, ns - 1 - i)
    c_shape = jax.ShapeDtypeStruct((bn, 2 * HEADS, DK, DV), F32)
    v_shape = jax.ShapeDtypeStruct((bn, 2 * HEADS, 128), F32)
    c_spec = pl.BlockSpec((None, 2 * HEADS, DK, DV), lambda b, i: (b, 0, 0, 0))
    v_spec = pl.BlockSpec((None, 2 * HEADS, 128), lambda b, i: (b, 0, 0))
    if emit_out:
        o_shape = jax.ShapeDtypeStruct((bn, tn // 2, HEADS * DV), F32)
        out_shape = (o_shape, o_shape, c_shape, v_shape, v_shape)
        out_specs = (*_half_specs(ns, HEADS * DV), c_spec, v_spec, v_spec)
    else:
        out_shape = (c_shape, v_shape, v_shape)
        out_specs = (c_spec, v_spec, v_spec)
    both = lambda shape, f, g: [pl.BlockSpec(shape, f), pl.BlockSpec(shape, g)]
    return pl.pallas_call(
        functools.partial(_mlstm_kernel, emit_out=emit_out, ns=ns),
        out_shape=out_shape,
        grid=(bn, ns),
        scratch_shapes=[pltpu.VMEM((ns, STEP, HEADS * DV), F32)] if emit_out else [],
        in_specs=(both((None, STEP, 512), fwd, bwd) + both((None, 512, STEP), fwd_t, bwd_t)
                  + both((None, 512, STEP), fwd_t, bwd_t) + both((None, STEP, 1024), fwd, bwd)
                  + both((None, STEP, 256), fwd, bwd)
                  + [pl.BlockSpec((1, 256), lambda b, i: (0, 0)), c_spec, v_spec, v_spec]),
        out_specs=out_specs,
        compiler_params=_params(("arbitrary", "arbitrary")),
        name="mlstm_scan_lat" if emit_out else "mlstm_scan_ctx",
    )(k, k, qt, qt, kt, kt, v, v, tg, tg, bgate, c0, n0, m0)


def _head_norm(o, g):
    parts = []
    for h in range(HEADS):
        oh = o[:, h * DV:(h + 1) * DV]
        parts.append(oh * lax.rsqrt(jnp.mean(oh * oh, axis=-1, keepdims=True) + EPS))
    return jnp.concatenate(parts, axis=-1) * g


def _merge_kernel(olo_ref, ohi_ref, hlo_ref, hhi_ref, po_ref, x_ref, g1_ref, gg_ref, gm_ref,
                  wbg_ref, wbm_ref, wo_ref, o_ref, *, per_b):
    hm = jnp.concatenate([r[:, rl, :] for rl in range(COL_BLOCK) for r in (hlo_ref, hhi_ref)], axis=0)
    lower = pl.program_id(0) % per_b < per_b // 2
    o = jnp.where(lower, olo_ref[...], ohi_ref[...]).astype(F32)
    y_gla = _head_norm(o, gg_ref[...]) * po_ref[:, 0:1024].astype(F32)
    y_m = _head_norm(hm, gm_ref[...]) * po_ref[:, 1024:2048].astype(F32)
    gate_g = po_ref[:, 2048:3072].astype(F32)
    gate_m = po_ref[:, 3072:4096].astype(F32)
    y = (gate_g * _dot(y_gla.astype(BF16), wbg_ref[...])
         + gate_m * _dot(y_m.astype(BF16), wbm_ref[...]))
    mix = _dot(y.astype(BF16), wo_ref[...])
    o_ref[...] = x_ref[...] + g1_ref[...] * mix


def _merge(olo, ohi, hlo4, hhi4, po, x2, mod3, gg, gm, wbg, wbm, wo):
    m = x2.shape[0]
    tm = GRID_W * COL_BLOCK
    per_b = SEQ // tm
    half = per_b // 2
    tok = lambda i: (i, 0)
    hspec = pl.BlockSpec((None, GRID_W // 2, COL_BLOCK, D), lambda i: (i // per_b, 0, i % per_b, 0))
    lo_spec = pl.BlockSpec((None, tm, D), lambda i: (i // per_b, jnp.minimum(i % per_b, half - 1), 0))
    hi_spec = pl.BlockSpec((None, tm, D), lambda i: (i // per_b, jnp.maximum(i % per_b - half, 0), 0))
    return pl.pallas_call(
        functools.partial(_merge_kernel, per_b=per_b),
        out_shape=jax.ShapeDtypeStruct((m, D), F32),
        grid=(m // tm,),
        in_specs=[lo_spec, hi_spec, hspec, hspec,
                  pl.BlockSpec((tm, 4096), tok), pl.BlockSpec((tm, D), tok),
                  pl.BlockSpec((None, 1, D), lambda i: (i // per_b, 0, 2)),
                  pl.BlockSpec((1, D), lambda i: (0, 0)), pl.BlockSpec((1, D), lambda i: (0, 0)),
                  _resident((D, D)), _resident((D, D)), _resident((D, D))],
        out_specs=pl.BlockSpec((tm, D), tok),
        compiler_params=_params(("arbitrary",)),
        name="merge",
    )(olo, ohi, hlo4, hhi4, po, x2, mod3, gg, gm, wbg, wbm, wo)


FF_TILES = ((0, 1280), (1280, 2816))


def _ffn_kernel(x_ref, sh_ref, sc_ref, g2_ref, ng_ref, fg_ref, wi_ref, wo_ref, o_ref):
    x = x_ref[...]
    u = _norm_mod(x, ng_ref[...], sh_ref[...], sc_ref[...])
    acc = None
    for lo, hi in FF_TILES:
        a = _dot(u, wi_ref[:, lo:hi])
        b = _dot(u, wi_ref[:, D_FF + lo:D_FF + hi])
        hid = (a * _sigmoid(a) * b).astype(BF16)
        part = _dot(hid, wo_ref[lo:hi, :])
        acc = part if acc is None else acc + part
    x2 = x + g2_ref[...] * acc
    o_ref[...] = x2 * lax.rsqrt(jnp.mean(x2 * x2, axis=-1, keepdims=True) + EPS) * fg_ref[...]


def _ffn(x1, mod3, ng, fg, wi, wo, tm):
    m = x1.shape[0]
    per_b = SEQ // tm
    tok = lambda i: (i, 0)
    modspec = lambda c: pl.BlockSpec((None, 1, D), lambda i: (i // per_b, 0, c))
    return pl.pallas_call(
        _ffn_kernel,
        out_shape=jax.ShapeDtypeStruct((m, D), F32),
        grid=(m // tm,),
        in_specs=[pl.BlockSpec((tm, D), tok), modspec(3), modspec(4), modspec(5),
                  pl.BlockSpec((1, D), lambda i: (0, 0)), pl.BlockSpec((1, D), lambda i: (0, 0)),
                  _resident((D, 2 * D_FF)), _resident((D_FF, D))],
        out_specs=pl.BlockSpec((tm, D), tok),
        compiler_params=_params(("arbitrary",)),
        name="ffn",
    )(x1, mod3, mod3, mod3, ng, fg, wi, wo)


def _split_w_in_kernel(w_ref, wg_ref, wm_ref, wo_ref, wtl_ref, wtg_ref):
    cols = w_ref.shape[1]
    c = lambda a, b: w_ref[a:b, :].astype(BF16)
    z = lambda n: jnp.zeros((n, cols), BF16)
    wg_ref[...] = c(0, 2048)
    wm_ref[...] = c(3104, 5152)
    wo_ref[0:1024, :] = c(2048, 3072)
    wo_ref[1024:2048, :] = c(5152, 6176)
    wo_ref[2048:4096, :] = c(6192, 8240)
    wtl_ref[...] = jnp.concatenate([c(3072, 3104), z(128 - 2 * RANK)], axis=0)
    wtg_ref[...] = jnp.concatenate(
        [z(GATE_LANE), c(6176, 6192), z(128 - GATE_LANE - 16),
         z(GATE_LANE), c(6180, 6184), z(4), c(6188, 6192), z(128 - GATE_LANE - 12)], axis=0)


def _split_w_in(w_in_t):
    tc = 256
    heights = (2048, 2048, 4096, 128, 256)
    return pl.pallas_call(
        _split_w_in_kernel,
        out_shape=tuple(jax.ShapeDtypeStruct((n, D), BF16) for n in heights),
        grid=(D // tc,),
        in_specs=[pl.BlockSpec((w_in_t.shape[0], tc), lambda i: (0, i))],
        out_specs=tuple(pl.BlockSpec((n, tc), lambda i: (0, i)) for n in heights),
        compiler_params=_params(("arbitrary",)),
        name="split_w_in",
    )(w_in_t)


def kernel(x, c, ctx, c_ctx, w_ada, b_ada, norm1_g, w_in, gla_w_up, gla_b_dec, gla_norm_g,
           mlstm_conv_w, mlstm_conv_b, mlstm_b_gate, mlstm_norm_g, w_br_gla, w_br_mlstm, w_out,
           norm2_g, w_ffn_in, w_ffn_out, final_g):
    bsz = x.shape[0]
    row = lambda a: a.reshape(1, -1)

    cvec = jnp.concatenate([c, c_ctx[None, :], jnp.zeros((8 - bsz - 1, D), F32)], axis=0)
    mod3 = _ada(cvec, w_ada[0], row(b_ada[0])).reshape(8, 1, N_MOD)

    wg, wm, wo, wtl, wtg = _split_w_in(w_in[0].T)
    x2 = x.reshape(bsz * SEQ, D)
    ctx2 = ctx.reshape(bsz * CTX, D)
    g1n = row(norm1_g[0])
    tm = GRID_W * COL_BLOCK
    po, pg, tlr = _inproj(x2, mod3, lambda i: i // (SEQ // tm), g1n, (wo, wg, wtl), (BF16, BF16, F32), tm,
                          acts=(("silu", "sigmoid", "sigmoid", "sigmoid"), None, None))
    conv_w = mlstm_conv_w[0]
    conv_b = row(mlstm_conv_b[0])
    m_lat = _inproj_m(x.reshape(bsz, SEQ // GRID_W, GRID_W, D), mod3,
                      lambda i: i // (GRID_W // COL_BLOCK), g1n, wm, wtg, conv_w, conv_b, tm, True)
    pg_c, tlr_c = _inproj(ctx2, mod3, lambda i: bsz, g1n, (wg, wtl), (BF16, F32), CTX)
    m_ctx = _inproj_m(ctx2, mod3, lambda i: bsz, g1n, wm, wtg, conv_w, conv_b, CTX, False)

    wup = jnp.zeros((2, 128, HEADS * DK), F32)
    wup = wup.at[0, 0:RANK].set(gla_w_up[0, 0]).at[1, RANK:2 * RANK].set(gla_w_up[0, 1]).astype(BF16)
    bdec = gla_b_dec[0].reshape(2, 1, HEADS * DK)
    s0 = jnp.zeros((bsz, 2 * HEADS, DK, DV), F32)
    (s_ctx,) = _gla_scan(pg_c.reshape(bsz, CTX, 2048), tlr_c.reshape(bsz, CTX, 128),
                         wup, bdec, s0, emit_out=False)
    o_lo, o_hi, _ = _gla_scan(pg.reshape(bsz, SEQ, 2048), tlr.reshape(bsz, SEQ, 128),
                              wup, bdec, s_ctx, emit_out=True)

    bgate = mlstm_b_gate[0].reshape(1, 16)
    zg = lambda n: jnp.zeros((1, n), F32)
    bg2 = jnp.concatenate([zg(GATE_LANE), bgate, zg(128 - GATE_LANE - 16),
                           zg(GATE_LANE), bgate[:, 4:8], zg(4), bgate[:, 12:16],
                           zg(128 - GATE_LANE - 12)], axis=1)
    c0 = jnp.zeros((bsz, 2 * HEADS, DK, DV), F32)
    v0 = jnp.zeros((bsz, 2 * HEADS, 128), F32)
    c_ctx_s, n_ctx_s, m_ctx_s = _mlstm_scan(*m_ctx, bg2, c0, v0, v0, emit_out=False)
    h_lo, h_hi, _, _, _ = _mlstm_scan(*m_lat, bg2, c_ctx_s, n_ctx_s, m_ctx_s, emit_out=True)

    cm4 = lambda a: a.reshape(bsz, GRID_W // 2, SEQ // GRID_W, D)
    x1 = _merge(o_lo, o_hi, cm4(h_lo), cm4(h_hi), po, x2, mod3,
                row(gla_norm_g[0]), row(mlstm_norm_g[0]),
                w_br_gla[0].astype(BF16), w_br_mlstm[0].astype(BF16), w_out[0].astype(BF16))
    out = _ffn(x1, mod3, row(norm2_g[0]), row(final_g),
               w_ffn_in[0].astype(BF16), w_ffn_out[0].astype(BF16), tm)
    return out.reshape(bsz, SEQ, D)
```

```python
import functools
import math

import jax
import jax.numpy as jnp
from jax import lax
from jax.experimental import pallas as pl
from jax.experimental.pallas import tpu as pltpu

D = 1024
SEQ = 4096
CTX = 256
GRID_W = 64
EPS = 1e-6
HEADS = 4
DK = 128
DV = 256
RANK = 16
TAU = 16.0
D_FF = 2816
N_MOD = 6 * D
GATE_LANE = 32
STEP = 512
GLA_CHUNK = 128
MLSTM_CHUNK = 128
COL_BLOCK = 8

LOG2E = math.log2(math.e)
LOG2_QSCALE = -0.5 * math.log2(DK)

F32 = jnp.float32
BF16 = jnp.bfloat16
VMEM_LIMIT = 56 * 1024 * 1024


def _dot(a, b):
    return jnp.dot(a, b, preferred_element_type=F32)


def _dot_nt(a, b):
    return lax.dot_general(a, b, (((1,), (1,)), ((), ())), preferred_element_type=F32)


def _dot_tn(a, b):
    return lax.dot_general(a, b, (((0,), (0,)), ((), ())), preferred_element_type=F32)


def _sigmoid(x):
    return 1.0 / (1.0 + jnp.exp(-x))


def _log_sigmoid(x):
    return jnp.minimum(x, 0.0) - jnp.log1p(jnp.exp(-jnp.abs(x)))


def _cumsum_rows(tri, g):
    g1 = g.astype(BF16)
    g2 = (g - g1.astype(F32)).astype(BF16)
    return _dot(tri, g1) + _dot(tri, g2)


def _tri(n, d):
    row = lax.broadcasted_iota(jnp.int32, (n, n), 0)
    col = lax.broadcasted_iota(jnp.int32, (n, n), 1)
    causal = (col <= row) if d == 0 else (col >= row)
    return causal, jnp.where(causal, 1.0, 0.0).astype(BF16)


def _resident(shape):
    n = len(shape)
    return pl.BlockSpec(shape, lambda *_: (0,) * n, pipeline_mode=pl.Buffered(1))


def _params(sem):
    return pltpu.CompilerParams(dimension_semantics=sem, vmem_limit_bytes=VMEM_LIMIT)


def _scan_units(n_sub):
    return [(d, j) for jj in range(n_sub) for d, j in ((0, jj), (1, n_sub - 1 - jj))]


def _ada_kernel(c_ref, w_ref, b_ref, o_ref):
    cv = c_ref[...]
    s = (cv * _sigmoid(cv)).astype(BF16)
    o_ref[...] = _dot(s, w_ref[...].astype(BF16)) + b_ref[...]


def _ada(cvec, w_ada, b_ada):
    tn = 1024
    return pl.pallas_call(
        _ada_kernel,
        out_shape=jax.ShapeDtypeStruct((8, N_MOD), F32),
        grid=(N_MOD // tn,),
        in_specs=[pl.BlockSpec((8, D), lambda j: (0, 0)),
                  pl.BlockSpec((D, tn), lambda j: (0, j)),
                  pl.BlockSpec((1, tn), lambda j: (0, j))],
        out_specs=pl.BlockSpec((8, tn), lambda j: (0, j)),
        compiler_params=_params(("arbitrary",)),
        name="adaln",
    )(cvec, w_ada, b_ada)


def _norm_mod(x, g, sh, sc):
    y = x * lax.rsqrt(jnp.mean(x * x, axis=-1, keepdims=True) + EPS) * g
    return (y * (1.0 + sc) + sh).astype(BF16)


def _act(v, kind):
    if kind == "silu":
        return v * _sigmoid(v)
    if kind == "sigmoid":
        return _sigmoid(v)
    return v


def _inproj_kernel(x_ref, sh_ref, sc_ref, g_ref, *refs, acts):
    n_out = len(refs) // 2
    u = _norm_mod(x_ref[...], g_ref[...], sh_ref[...], sc_ref[...])
    for w_ref, o_ref, act in zip(refs[:n_out], refs[n_out:], acts):
        n = w_ref.shape[0]
        for jc, j in enumerate(range(0, n, 1024)):
            cs = slice(j, min(j + 1024, n))
            o_ref[:, cs] = _act(_dot_nt(u, w_ref[cs, :]), act[jc] if act else None).astype(o_ref.dtype)


def _inproj(x2, mod3, mod_row, norm_g, weights, out_dtypes, tm, acts=None):
    m = x2.shape[0]
    acts = acts or (None,) * len(weights)
    return pl.pallas_call(
        functools.partial(_inproj_kernel, acts=acts),
        out_shape=tuple(jax.ShapeDtypeStruct((m, w.shape[0]), dt) for w, dt in zip(weights, out_dtypes)),
        grid=(m // tm,),
        in_specs=[pl.BlockSpec((tm, D), lambda i: (i, 0)),
                  pl.BlockSpec((None, 1, D), lambda i: (mod_row(i), 0, 0)),
                  pl.BlockSpec((None, 1, D), lambda i: (mod_row(i), 0, 1)),
                  pl.BlockSpec((1, D), lambda i: (0, 0))] + [_resident(w.shape) for w in weights],
        out_specs=tuple(pl.BlockSpec((tm, w.shape[0]), lambda i: (i, 0)) for w in weights),
        compiler_params=_params(("arbitrary",)),
        name="inproj",
    )(x2, mod3, mod3, norm_g, *weights)


def _inproj_m_kernel(*refs, colmajor, nblk):
    if colmajor:
        x_ref, hp_ref, hn_ref, sh_ref, sc_ref, g_ref, wm_ref, wtg_ref, cw_ref, cb_ref = refs[:10]
        x = jnp.concatenate([x_ref[:, cl, :] for cl in range(COL_BLOCK)] + [hp_ref[7], hn_ref[0]], axis=0)
    else:
        x_ref, sh_ref, sc_ref, g_ref, wm_ref, wtg_ref, cw_ref, cb_ref = refs[:8]
        x = x_ref[...]
    k_ref, qt_ref, kt_ref, v_ref, tg_ref = refs[-5:]
    n = k_ref.shape[0]
    u = _norm_mod(x, g_ref[...], sh_ref[...], sc_ref[...])
    pre = _dot_nt(u, wm_ref[0:1024, :])
    a = pre[0:n]
    if colmajor:
        j = pl.program_id(0) % nblk
        prev_row = jnp.where(j > 0, pre[n + 7:n + 8], 0.0)
        next_row = jnp.where(j < nblk - 1, pre[n + 8:n + 9], 0.0)
    else:
        prev_row = next_row = jnp.zeros((1, 1024), F32)
    rowq = lax.broadcasted_iota(jnp.int32, (n, 1024), 0)
    ap = jnp.where(rowq == 0, prev_row, pltpu.roll(a, 1, axis=0))
    an = jnp.where(rowq == n - 1, next_row, pltpu.roll(a, n - 1, axis=0))
    conv = ap * cw_ref[0:1, :] + a * cw_ref[1:2, :] + an * cw_ref[2:3, :] + cb_ref[...]
    qk = conv * _sigmoid(conv)
    q = qk[:, 0:512]
    k = qk[:, 512:1024] * (DK ** -0.5)
    k_ref[...] = k.astype(BF16)
    qt_ref[...] = q.T.astype(BF16)
    kt_ref[...] = k.T.astype(BF16)
    um = u[0:n]
    v_ref[...] = _dot_nt(um, wm_ref[1024:2048, :]).astype(BF16)
    tg_ref[...] = _dot_nt(um, wtg_ref[...])


def _inproj_m(xv, mod3, mod_row, norm_g, wm, wtg, conv_w, conv_b, tm, colmajor):
    full = lambda shape: pl.BlockSpec(shape, lambda i: (0,) * len(shape))
    if colmajor:
        bsz = xv.shape[0]
        tn = SEQ
        nblk = GRID_W // COL_BLOCK
        blk = (None, GRID_W, COL_BLOCK, D)
        halo = (None, 8, COL_BLOCK, D)
        x_specs = [pl.BlockSpec(blk, lambda i: (i // nblk, 0, i % nblk, 0)),
                   pl.BlockSpec(halo, lambda i: (i // nblk, GRID_W // 8 - 1, jnp.maximum(i % nblk - 1, 0), 0)),
                   pl.BlockSpec(halo, lambda i: (i // nblk, 0, jnp.minimum(i % nblk + 1, nblk - 1), 0))]
        xs = (xv, xv, xv)
    else:
        tn = tm
        bsz = xv.shape[0] // tn
        nblk = 1
        x_specs = [pl.BlockSpec((tm, D), lambda i: (i, 0))]
        xs = (xv,)
    tok = lambda w: pl.BlockSpec((None, tm, w), lambda i: (i // nblk, i % nblk, 0))
    tr = pl.BlockSpec((None, 512, tm), lambda i: (i // nblk, 0, i % nblk))
    sds = jax.ShapeDtypeStruct
    return pl.pallas_call(
        functools.partial(_inproj_m_kernel, colmajor=colmajor, nblk=nblk),
        out_shape=(sds((bsz, tn, 512), BF16), sds((bsz, 512, tn), BF16), sds((bsz, 512, tn), BF16),
                   sds((bsz, tn, 1024), BF16), sds((bsz, tn, 256), F32)),
        grid=(bsz * nblk,),
        in_specs=x_specs + [pl.BlockSpec((None, 1, D), lambda i: (mod_row(i), 0, 0)),
                            pl.BlockSpec((None, 1, D), lambda i: (mod_row(i), 0, 1)),
                            full((1, D)), _resident(wm.shape), _resident(wtg.shape),
                            full((3, 1024)), full((1, 1024))],
        out_specs=(tok(512), tr, tr, tok(1024), tok(256)),
        compiler_params=_params(("arbitrary",)),
        name="inproj_m_cm" if colmajor else "inproj_m",
    )(*xs, mod3, mod3, norm_g, wm, wtg, conv_w, conv_b)


def _sum_directions(acc_ref, out_ref, blk, rs, cs, val):
    tot = acc_ref[blk, rs, cs] + val
    acc_ref[blk, rs, cs] = tot
    out_ref[rs, cs] = tot.astype(out_ref.dtype)


def _gla_kernel(pf_ref, pb_ref, tf_ref, tb_ref, wup_ref, bdec_ref, s0_ref, *rest, emit_out, ns):
    i = pl.program_id(1)
    if emit_out:
        olo_ref, ohi_ref, st_ref, acc_ref = rest
    else:
        (st_ref,) = rest
        olo_ref = ohi_ref = acc_ref = None

    @pl.when(i == 0)
    def _():
        st_ref[...] = s0_ref[...]
        if emit_out:
            acc_ref[...] = jnp.zeros(acc_ref.shape, F32)

    lc = GLA_CHUNK
    dirs = ((pf_ref, tf_ref, ohi_ref), (pb_ref, tb_ref, olo_ref))
    blks = (i, ns - 1 - i)
    units = _scan_units(pf_ref.shape[0] // lc)
    masks =[_tri(lc, d) for d in range(2)]
    gs = []
    for d, (p_ref, t_ref, o_ref) in enumerate(dirs):
        z = _dot(t_ref[...].astype(BF16), wup_ref[d]) + bdec_ref[d]
        gs.append(_log_sigmoid(z) * (LOG2E / TAU))
    bs = {}
    for d, j in units:
        bs[d, j] = _cumsum_rows(masks[d][1], gs[d][j * lc:(j + 1) * lc])
    ops = {}
    for d, j in units:
        p_ref = dirs[d][0]
        rs = slice(j * lc, (j + 1) * lc)
        b = bs[d, j]
        b_last = b[lc - 1:lc, :] if d == 0 else b[0:1, :]
        b_mid = b[lc // 2 - 1:lc // 2, :] if d == 0 else b[lc // 2:lc // 2 + 1, :]
        q = p_ref[rs, 0:512]
        k = p_ref[rs, 512:1024]
        qi = q * jnp.exp2(b).astype(BF16)
        kl = k * jnp.exp2((b_last + LOG2_QSCALE) - b).astype(BF16)
        dec = jnp.exp2(b_last)
        if emit_out:
            qd = q * jnp.exp2(b - b_mid).astype(BF16)
            kd = k * jnp.exp2((b_mid + LOG2_QSCALE) - b).astype(BF16)
        else:
            qd = kd = None
        ops[d, j] = (qi, kl, dec, qd, kd)
    sc, us, dcols = {}, {}, {}
    for d, j in units:
        p_ref = dirs[d][0]
        rs = slice(j * lc, (j + 1) * lc)
        qi, kl, dec, qd, kd = ops[d, j]
        for h in range(HEADS):
            ks = slice(h * DK, (h + 1) * DK)
            v = p_ref[rs, 1024 + h * DV:1024 + (h + 1) * DV]
            if emit_out:
                sc[d, j, h] = jnp.where(masks[d][0], _dot_nt(qd[:, ks], kd[:, ks]), 0.0).astype(BF16)
            us[d, j, h] = _dot_tn(kl[:, ks], v)
            dcols[d, j, h] = jnp.broadcast_to(dec[:, ks], (8, DK)).T[:, 0:1]
    for d in range(2):
        p_ref, _, o_ref = dirs[d]
        for h in range(HEADS):
            ks = slice(h * DK, (h + 1) * DK)
            st = st_ref[d * HEADS + h]
            for dd, j in units:
                if dd != d:
                    continue
                rs = slice(j * lc, (j + 1) * lc)
                if emit_out:
                    v = p_ref[rs, 1024 + h * DV:1024 + (h + 1) * DV]
                    o = _dot(jnp.concatenate([sc[d, j, h], ops[d, j][0][:, ks]], axis=1),
                             jnp.concatenate([v, st.astype(BF16)], axis=0))
                    _sum_directions(acc_ref, o_ref, blks[d], rs, slice(h * DV, (h + 1) * DV), o)
                st = st * dcols[d, j, h] + us[d, j, h]
            st_ref[d * HEADS + h] = st


def _half_specs(ns, step, width):
    half = ns // 2
    lo = pl.BlockSpec((None, step, width), lambda b, i: (b, jnp.minimum(ns - 1 - i, half - 1), 0))
    hi = pl.BlockSpec((None, step, width), lambda b, i: (b, jnp.maximum(i - half, 0), 0))
    return lo, hi


def _gla_scan(pg, tlr, wup, bdec, s0, emit_out):
    bn, tn, _ = pg.shape
    step = min(STEP, tn)
    ns = tn // step
    fwd = lambda b, i: (b, i, 0)
    bwd = lambda b, i: (b, ns - 1 - i, 0)
    st_shape = jax.ShapeDtypeStruct((bn, 2 * HEADS, DK, DV), F32)
    st_spec = pl.BlockSpec((None, 2 * HEADS, DK, DV), lambda b, i: (b, 0, 0, 0))
    o_shape = jax.ShapeDtypeStruct((bn, tn // 2, HEADS * DV), BF16)
    out_shape = (o_shape, o_shape, st_shape) if emit_out else (st_shape,)
    out_specs = (*_half_specs(ns, step, HEADS * DV), st_spec) if emit_out else (st_spec,)
    return pl.pallas_call(
        functools.partial(_gla_kernel, emit_out=emit_out, ns=ns),
        out_shape=out_shape,
        grid=(bn, ns),
        scratch_shapes=[pltpu.VMEM((ns, step, HEADS * DV), F32)] if emit_out else [],
        in_specs=[pl.BlockSpec((None, step, 2048), fwd),
                  pl.BlockSpec((None, step, 2048), bwd),
                  pl.BlockSpec((None, step, 128), fwd),
                  pl.BlockSpec((None, step, 128), bwd),
                  pl.BlockSpec((2, 128, 512), lambda b, i: (0, 0, 0)),
                  pl.BlockSpec((2, 1, 512), lambda b, i: (0, 0, 0)),
                  st_spec],
        out_specs=out_specs,
        compiler_params=_params(("arbitrary", "arbitrary")),
        name="gla_scan_lat" if emit_out else "gla_scan_ctx",
    )(pg, pg, tlr, tlr, wup, bdec, s0)


def _mlstm_kernel(kf_ref, kb_ref, qtf_ref, qtb_ref, ktf_ref, ktb_ref, vf_ref, vb_ref, tf_ref, tb_ref,
                  bg_ref, c0_ref, n0_ref, m0_ref, *rest, emit_out, ns):
    i = pl.program_id(1)
    if emit_out:
        hlo_ref, hhi_ref, c_ref, n_ref, m_ref, acc_ref = rest
    else:
        c_ref, n_ref, m_ref = rest
        hlo_ref = hhi_ref = acc_ref = None

    @pl.when(i == 0)
    def _():
        c_ref[...] = c0_ref[...]
        n_ref[...] = n0_ref[...]
        m_ref[...] = m0_ref[...]
        if emit_out:
            acc_ref[...] = jnp.zeros(acc_ref.shape, F32)

    lc = MLSTM_CHUNK
    dirs = ((kf_ref, qtf_ref, ktf_ref, vf_ref, tf_ref, hhi_ref), (kb_ref, qtb_ref, ktb_ref, vb_ref, tb_ref, hlo_ref))
    blks = (i, ns - 1 - i)
    units = _scan_units(kf_ref.shape[0] // lc)
    tris =[_tri(lc, d) for d in range(2)]
    lane_of = lambda d, h: GATE_LANE + 8 * d + h
    hs = lambda h: slice(h * DK, (h + 1) * DK)
    vs = lambda h: slice(h * DV, (h + 1) * DV)

    mrow = [m_ref[d:d + 1, :] for d in range(2)]
    tiles = {}
    for d, j in units:
        t_ref = dirs[d][4]
        rs = slice(j * lc, (j + 1) * lc)
        ga = (t_ref[rs, 0:128] + bg_ref[:, 0:128]) * LOG2E
        gb = t_ref[rs, 128:256] + bg_ref[:, 128:256]
        bc = _cumsum_rows(tris[d][1], _log_sigmoid(gb) * LOG2E)
        b_last = bc[lc - 1:lc, :] if d == 0 else bc[0:1, :]
        log_key = b_last - bc + ga
        m_new = jnp.maximum(b_last + mrow[d], jnp.max(log_key, axis=0, keepdims=True))
        tiles[d, j] = dict(rmat=ga - bc, bct=bc.T, m_in=mrow[d], wkt=jnp.exp2(log_key - m_new).T,
                           decay=jnp.exp2(b_last + mrow[d] - m_new))
        mrow[d] = m_new
    for d in range(2):
        m_ref[d:d + 1, :] = mrow[d]

    us, ncols = {}, {}
    for d, j in units:
        k_ref, _, kt_ref, v_ref = dirs[d][:4]
        rs = slice(j * lc, (j + 1) * lc)
        for h in range(HEADS):
            lane = lane_of(d, h)
            wk = tiles[d, j]["wkt"][lane:lane + 1, :]
            kwt = kt_ref[hs(h), rs] * wk.astype(BF16)
            us[d, j, h] = _dot(kwt, v_ref[rs, vs(h)])
            ncols[d, j, h] = _dot(jnp.broadcast_to(wk, (16, lc)).astype(BF16), k_ref[rs, hs(h)])[0:1]

    n_in = {}
    for d in range(2):
        for h in range(HEADS):
            idx = d * HEADS + h
            lane = lane_of(d, h)
            nvec = n_ref[idx:idx + 1, :]
            for dd, j in units:
                if dd == d:
                    n_in[d, j, h] = nvec
                    nvec = tiles[d, j]["decay"][:, lane:lane + 1] * nvec + ncols[d, j, h]
            n_ref[idx:idx + 1, :] = nvec

    lhs = {}
    if emit_out:
        for d, j in units:
            k_ref, qt_ref = dirs[d][:2]
            rs = slice(j * lc, (j + 1) * lc)
            t = tiles[d, j]
            causal_t = tris[1 - d][0]
            for h in range(HEADS):
                lane = lane_of(d, h)
                qt = qt_ref[hs(h), rs]
                kq = _dot(jnp.concatenate(
                    [k_ref[rs, hs(h)], jnp.broadcast_to(n_in[d, j, h], (16, DK)).astype(BF16)], axis=0), qt)
                rm = jnp.where(causal_t, t["rmat"][:, lane:lane + 1], -jnp.inf)
                mval = t["m_in"][:, lane:lane + 1]
                mx = jnp.maximum(mval, jnp.max(rm, axis=0, keepdims=True))
                wt = jnp.exp2(rm - mx) * kq[0:lc]
                w_inter = jnp.exp2(mval - mx)
                den = jnp.sum(wt, axis=0, keepdims=True) + w_inter * kq[lc:lc + 1]
                inv = 1.0 / jnp.maximum(jnp.abs(den), jnp.exp2(-(t["bct"][lane:lane + 1, :] + mx)))
                lhs[d, j, h] = jnp.concatenate(
                    [(wt * inv).astype(BF16), qt * (w_inter * inv).astype(BF16)], axis=0)

    for d in range(2):
        v_ref, o_ref = dirs[d][3], dirs[d][5]
        for h in range(HEADS):
            idx = d * HEADS + h
            lane = lane_of(d, h)
            cmat = c_ref[idx]
            for dd, j in units:
                if dd != d:
                    continue
                rs = slice(j * lc, (j + 1) * lc)
                if emit_out:
                    o = _dot_tn(lhs[d, j, h], jnp.concatenate([v_ref[rs, vs(h)], cmat.astype(BF16)], axis=0))
                    _sum_directions(acc_ref, o_ref, blks[d], rs, vs(h), o)
                cmat = tiles[d, j]["decay"][:, lane:lane + 1] * cmat + us[d, j, h]
            c_ref[idx] = cmat


def _mlstm_scan(k, qt, kt, v, tg, bgate, c0, n0, m0, emit_out):
    bn, tn, _ = k.shape
    step = min(STEP, tn)
    ns = tn // step
    fwd = lambda b, i: (b, i, 0)
    bwd = lambda b, i: (b, ns - 1 - i, 0)
    fwd_t = lambda b, i: (b, 0, i)
    bwd_t = lambda b, i: (b, 0, ns - 1 - i)
    c_shape = jax.ShapeDtypeStruct((bn, 2 * HEADS, DK, DV), F32)
    v_shape = jax.ShapeDtypeStruct((bn, 2 * HEADS, 128), F32)
    c_spec = pl.BlockSpec((None, 2 * HEADS, DK, DV), lambda b, i: (b, 0, 0, 0))
    v_spec = pl.BlockSpec((None, 2 * HEADS, 128), lambda b, i: (b, 0, 0))
    if emit_out:
        o_shape = jax.ShapeDtypeStruct((bn, tn // 2, HEADS * DV), F32)
        out_shape = (o_shape, o_shape, c_shape, v_shape, v_shape)
        out_specs = (*_half_specs(ns, step, HEADS * DV), c_spec, v_spec, v_spec)
    else:
        out_shape = (c_shape, v_shape, v_shape)
        out_specs = (c_spec, v_spec, v_spec)
    both = lambda shape, f, g: [pl.BlockSpec(shape, f), pl.BlockSpec(shape, g)]
    return pl.pallas_call(
        functools.partial(_mlstm_kernel, emit_out=emit_out, ns=ns),
        out_shape=out_shape,
        grid=(bn, ns),
        scratch_shapes=[pltpu.VMEM((ns, step, HEADS * DV), F32)] if emit_out else [],
        in_specs=(both((None, step, 512), fwd, bwd) + both((None, 512, step), fwd_t, bwd_t)
                  + both((None, 512, step), fwd_t, bwd_t) + both((None, step, 1024), fwd, bwd)
                  + both((None, step, 256), fwd, bwd)
                  + [pl.BlockSpec((1, 256), lambda b, i: (0, 0)), c_spec, v_spec, v_spec]),
        out_specs=out_specs,
        compiler_params=_params(("arbitrary", "arbitrary")),
        name="mlstm_scan_lat" if emit_out else "mlstm_scan_ctx",
    )(k, k, qt, qt, kt, kt, v, v, tg, tg, bgate, c0, n0, m0)


def _head_norm(o, g):
    parts = []
    for h in range(HEADS):
        oh = o[:, h * DV:(h + 1) * DV]
        parts.append(oh * lax.rsqrt(jnp.mean(oh * oh, axis=-1, keepdims=True) + EPS))
    return jnp.concatenate(parts, axis=-1) * g


def _merge_kernel(olo_ref, ohi_ref, hlo_ref, hhi_ref, po_ref, x_ref, g1_ref, gg_ref, gm_ref,
                  wbg_ref, wbm_ref, wo_ref, o_ref, *, per_b):
    hm = jnp.concatenate([r[:, rl, :] for rl in range(COL_BLOCK) for r in (hlo_ref, hhi_ref)], axis=0)
    lower = pl.program_id(0) % per_b < per_b // 2
    o = jnp.where(lower, olo_ref[...], ohi_ref[...]).astype(F32)
    y_gla = _head_norm(o, gg_ref[...]) * po_ref[:, 0:1024].astype(F32)
    y_m = _head_norm(hm, gm_ref[...]) * po_ref[:, 1024:2048].astype(F32)
    gate_g = po_ref[:, 2048:3072].astype(F32)
    gate_m = po_ref[:, 3072:4096].astype(F32)
    y = (gate_g * _dot(y_gla.astype(BF16), wbg_ref[...])
         + gate_m * _dot(y_m.astype(BF16), wbm_ref[...]))
    mix = _dot(y.astype(BF16), wo_ref[...])
    o_ref[...] = x_ref[...] + g1_ref[...] * mix


def _merge(olo, ohi, hlo4, hhi4, po, x2, mod3, gg, gm, wbg, wbm, wo):
    m = x2.shape[0]
    tm = GRID_W * COL_BLOCK
    per_b = SEQ // tm
    half = per_b // 2
    tok = lambda i: (i, 0)
    hspec = pl.BlockSpec((None, GRID_W // 2, COL_BLOCK, D), lambda i: (i // per_b, 0, i % per_b, 0))
    lo_spec = pl.BlockSpec((None, tm, D), lambda i: (i // per_b, jnp.minimum(i % per_b, half - 1), 0))
    hi_spec = pl.BlockSpec((None, tm, D), lambda i: (i // per_b, jnp.maximum(i % per_b - half, 0), 0))
    return pl.pallas_call(
        functools.partial(_merge_kernel, per_b=per_b),
        out_shape=jax.ShapeDtypeStruct((m, D), F32),
        grid=(m // tm,),
        in_specs=[lo_spec, hi_spec, hspec, hspec,
                  pl.BlockSpec((tm, 4096), tok), pl.BlockSpec((tm, D), tok),
                  pl.BlockSpec((None, 1, D), lambda i: (i // per_b, 0, 2)),
                  pl.BlockSpec((1, D), lambda i: (0, 0)), pl.BlockSpec((1, D), lambda i: (0, 0)),
                  _resident((D, D)), _resident((D, D)), _resident((D, D))],
        out_specs=pl.BlockSpec((tm, D), tok),
        compiler_params=_params(("arbitrary",)),
        name="merge",
    )(olo, ohi, hlo4, hhi4, po, x2, mod3, gg, gm, wbg, wbm, wo)


FF_TILES = ((0, 1280), (1280, 2816))


def _ffn_kernel(x_ref, sh_ref, sc_ref, g2_ref, ng_ref, fg_ref, wi_ref, wo_ref, o_ref):
    x = x_ref[...]
    u = _norm_mod(x, ng_ref[...], sh_ref[...], sc_ref[...])
    acc = None
    for lo, hi in FF_TILES:
        a = _dot(u, wi_ref[:, lo:hi])
        b = _dot(u, wi_ref[:, D_FF + lo:D_FF + hi])
        hid = (a * _sigmoid(a) * b).astype(BF16)
        part = _dot(hid, wo_ref[lo:hi, :])
        acc = part if acc is None else acc + part
    x2 = x + g2_ref[...] * acc
    o_ref[...] = x2 * lax.rsqrt(jnp.mean(x2 * x2, axis=-1, keepdims=True) + EPS) * fg_ref[...]


def _ffn(x1, mod3, ng, fg, wi, wo, tm):
    m = x1.shape[0]
    per_b = SEQ // tm
    tok = lambda i: (i, 0)
    modspec = lambda c: pl.BlockSpec((None, 1, D), lambda i: (i // per_b, 0, c))
    return pl.pallas_call(
        _ffn_kernel,
        out_shape=jax.ShapeDtypeStruct((m, D), F32),
        grid=(m // tm,),
        in_specs=[pl.BlockSpec((tm, D), tok), modspec(3), modspec(4), modspec(5),
                  pl.BlockSpec((1, D), lambda i: (0, 0)), pl.BlockSpec((1, D), lambda i: (0, 0)),
                  _resident((D, 2 * D_FF)), _resident((D_FF, D))],
        out_specs=pl.BlockSpec((tm, D), tok),
        compiler_params=_params(("arbitrary",)),
        name="ffn",
    )(x1, mod3, mod3, mod3, ng, fg, wi, wo)


def _split_w_in_kernel(w_ref, wg_ref, wm_ref, wo_ref, wtl_ref, wtg_ref):
    cols = w_ref.shape[1]
    c = lambda a, b: w_ref[a:b, :].astype(BF16)
    z = lambda n: jnp.zeros((n, cols), BF16)
    wg_ref[...] = c(0, 2048)
    wm_ref[...] = c(3104, 5152)
    wo_ref[0:1024, :] = c(2048, 3072)
    wo_ref[1024:2048, :] = c(5152, 6176)
    wo_ref[2048:4096, :] = c(6192, 8240)
    wtl_ref[...] = jnp.concatenate([c(3072, 3104), z(128 - 2 * RANK)], axis=0)
    wtg_ref[...] = jnp.concatenate(
        [z(GATE_LANE), c(6176, 6192), z(128 - GATE_LANE - 16),
         z(GATE_LANE), c(6180, 6184), z(4), c(6188, 6192), z(128 - GATE_LANE - 12)], axis=0)


def _split_w_in(w_in_t):
    tc = 256
    heights = (2048, 2048, 4096, 128, 256)
    return pl.pallas_call(
        _split_w_in_kernel,
        out_shape=tuple(jax.ShapeDtypeStruct((n, D), BF16) for n in heights),
        grid=(D // tc,),
        in_specs=[pl.BlockSpec((w_in_t.shape[0], tc), lambda i: (0, i))],
        out_specs=tuple(pl.BlockSpec((n, tc), lambda i: (0, i)) for n in heights),
        compiler_params=_params(("arbitrary",)),
        name="split_w_in",
    )(w_in_t)


def kernel(x, c, ctx, c_ctx, w_ada, b_ada, norm1_g, w_in, gla_w_up, gla_b_dec, gla_norm_g,
           mlstm_conv_w, mlstm_conv_b, mlstm_b_gate, mlstm_norm_g, w_br_gla, w_br_mlstm, w_out,
           norm2_g, w_ffn_in, w_ffn_out, final_g):
    bsz = x.shape[0]
    row = lambda a: a.reshape(1, -1)

    cvec = jnp.concatenate([c, c_ctx[None, :], jnp.zeros((8 - bsz - 1, D), F32)], axis=0)
    mod3 = _ada(cvec, w_ada[0], row(b_ada[0])).reshape(8, 1, N_MOD)

    wg, wm, wo, wtl, wtg = _split_w_in(w_in[0].T)
    x2 = x.reshape(bsz * SEQ, D)
    ctx2 = ctx.reshape(bsz * CTX, D)
    g1n = row(norm1_g[0])
    tm = GRID_W * COL_BLOCK
    po, pg, tlr = _inproj(x2, mod3, lambda i: i // (SEQ // tm), g1n, (wo, wg, wtl), (BF16, BF16, F32), tm,
                          acts=(("silu", "sigmoid", "sigmoid", "sigmoid"), None, None))
    conv_w = mlstm_conv_w[0]
    conv_b = row(mlstm_conv_b[0])
    m_lat = _inproj_m(x.reshape(bsz, SEQ // GRID_W, GRID_W, D), mod3,
                      lambda i: i // (GRID_W // COL_BLOCK), g1n, wm, wtg, conv_w, conv_b, tm, True)
    pg_c, tlr_c = _inproj(ctx2, mod3, lambda i: bsz, g1n, (wg, wtl), (BF16, F32), CTX)
    m_ctx = _inproj_m(ctx2, mod3, lambda i: bsz, g1n, wm, wtg, conv_w, conv_b, CTX, False)

    wup = jnp.zeros((2, 128, HEADS * DK), F32)
    wup = wup.at[0, 0:RANK].set(gla_w_up[0, 0]).at[1, RANK:2 * RANK].set(gla_w_up[0, 1]).astype(BF16)
    bdec = gla_b_dec[0].reshape(2, 1, HEADS * DK)
    s0 = jnp.zeros((bsz, 2 * HEADS, DK, DV), F32)
    (s_ctx,) = _gla_scan(pg_c.reshape(bsz, CTX, 2048), tlr_c.reshape(bsz, CTX, 128),
                         wup, bdec, s0, emit_out=False)
    o_lo, o_hi, _ = _gla_scan(pg.reshape(bsz, SEQ, 2048), tlr.reshape(bsz, SEQ, 128),
                              wup, bdec, s_ctx, emit_out=True)

    bgate = mlstm_b_gate[0].reshape(1, 16)
    zg = lambda n: jnp.zeros((1, n), F32)
    bg2 = jnp.concatenate([zg(GATE_LANE), bgate, zg(128 - GATE_LANE - 16),
                           zg(GATE_LANE), bgate[:, 4:8], zg(4), bgate[:, 12:16],
                           zg(128 - GATE_LANE - 12)], axis=1)
    c0 = jnp.zeros((bsz, 2 * HEADS, DK, DV), F32)
    v0 = jnp.zeros((bsz, 2 * HEADS, 128), F32)
    c_ctx_s, n_ctx_s, m_ctx_s = _mlstm_scan(*m_ctx, bg2, c0, v0, v0, emit_out=False)
    h_lo, h_hi, _, _, _ = _mlstm_scan(*m_lat, bg2, c_ctx_s, n_ctx_s, m_ctx_s, emit_out=True)

    cm4 = lambda a: a.reshape(bsz, GRID_W // 2, SEQ // GRID_W, D)
    x1 = _merge(o_lo, o_hi, cm4(h_lo), cm4(h_hi), po, x2, mod3,
                row(gla_norm_g[0]), row(mlstm_norm_g[0]),
                w_br_gla[0].astype(BF16), w_br_mlstm[0].astype(BF16), w_out[0].astype(BF16))
    out = _ffn(x1, mod3, row(norm2_g[0]), row(final_g),
               w_ffn_in[0].astype(BF16), w_ffn_out[0].astype(BF16), tm)
    return out.reshape(bsz, SEQ, D)
```

```python
import functools
import math

import jax
import jax.numpy as jnp
from jax import lax
from jax.experimental import pallas as pl
from jax.experimental.pallas import tpu as pltpu

D = 1024
SEQ = 4096
CTX = 256
GRID_W = 64
EPS = 1e-6
HEADS = 4
DK = 128
DV = 256
RANK = 16
TAU = 16.0
D_FF = 2816
N_MOD = 6 * D
GATE_LANE = 32
STEP = 512
GLA_CHUNK = 128
MLSTM_CHUNK = 128
COL_BLOCK = 8
MERGE_GROUPS = 2

LOG2E = math.log2(math.e)
LOG2_QSCALE = -0.5 * math.log2(DK)

F32 = jnp.float32
BF16 = jnp.bfloat16
VMEM_LIMIT = 56 * 1024 * 1024


def _dot(a, b):
    return jnp.dot(a, b, preferred_element_type=F32)


def _dot_nt(a, b):
    return lax.dot_general(a, b, (((1,), (1,)), ((), ())), preferred_element_type=F32)


def _dot_tn(a, b):
    return lax.dot_general(a, b, (((0,), (0,)), ((), ())), preferred_element_type=F32)


def _sigmoid(x):
    return 1.0 / (1.0 + jnp.exp(-x))


def _log2_sigmoid(x):
    return jnp.minimum(x, 0.0) * LOG2E - jnp.log2(1.0 + jnp.exp2(jnp.abs(x) * (-LOG2E)))


def _cumsum_rows(tri, g):
    g1 = g.astype(BF16)
    g2 = (g - g1.astype(F32)).astype(BF16)
    return _dot(tri, g1) + _dot(tri, g2)


def _tri(n, d):
    row = lax.broadcasted_iota(jnp.int32, (n, n), 0)
    col = lax.broadcasted_iota(jnp.int32, (n, n), 1)
    causal = (col <= row) if d == 0 else (col >= row)
    return causal, jnp.where(causal, 1.0, 0.0).astype(BF16)


def _resident(shape):
    n = len(shape)
    return pl.BlockSpec(shape, lambda *_: (0,) * n, pipeline_mode=pl.Buffered(1))


def _params(sem):
    return pltpu.CompilerParams(dimension_semantics=sem, vmem_limit_bytes=VMEM_LIMIT)


def _scan_units(n_sub):
    return [(d, j) for jj in range(n_sub) for d, j in ((0, jj), (1, n_sub - 1 - jj))]


def _ada_kernel(c_ref, w_ref, b_ref, o_ref):
    cv = c_ref[...]
    s = (cv * _sigmoid(cv)).astype(BF16)
    o_ref[...] = _dot(s, w_ref[...].astype(BF16)) + b_ref[...]


def _ada(cvec, w_ada, b_ada):
    tn = 1024
    return pl.pallas_call(
        _ada_kernel,
        out_shape=jax.ShapeDtypeStruct((8, N_MOD), F32),
        grid=(N_MOD // tn,),
        in_specs=[pl.BlockSpec((8, D), lambda j: (0, 0)),
                  pl.BlockSpec((D, tn), lambda j: (0, j)),
                  pl.BlockSpec((1, tn), lambda j: (0, j))],
        out_specs=pl.BlockSpec((8, tn), lambda j: (0, j)),
        compiler_params=_params(("arbitrary",)),
        name="adaln",
    )(cvec, w_ada, b_ada)


def _norm_mod(x, g, sh, sc):
    return (x * lax.rsqrt(jnp.mean(x * x, axis=-1, keepdims=True) + EPS) * (g * (1.0 + sc)) + sh).astype(BF16)


def _act(v, kind):
    if kind == "silu":
        return v * _sigmoid(v)
    if kind == "sigmoid":
        return _sigmoid(v)
    return v


def _inproj_kernel(x_ref, sh_ref, sc_ref, g_ref, *refs, acts):
    n_out = len(refs) // 2
    u = _norm_mod(x_ref[...], g_ref[...], sh_ref[...], sc_ref[...])
    for w_ref, o_ref, act in zip(refs[:n_out], refs[n_out:], acts):
        n = w_ref.shape[0]
        for jc, j in enumerate(range(0, n, 1024)):
            cs = slice(j, min(j + 1024, n))
            o_ref[:, cs] = _act(_dot_nt(u, w_ref[cs, :]), act[jc] if act else None).astype(o_ref.dtype)


def _inproj(x2, mod3, mod_row, norm_g, weights, out_dtypes, tm, acts=None):
    m = x2.shape[0]
    acts = acts or (None,) * len(weights)
    return pl.pallas_call(
        functools.partial(_inproj_kernel, acts=acts),
        out_shape=tuple(jax.ShapeDtypeStruct((m, w.shape[0]), dt) for w, dt in zip(weights, out_dtypes)),
        grid=(m // tm,),
        in_specs=[pl.BlockSpec((tm, D), lambda i: (i, 0)),
                  pl.BlockSpec((None, 1, D), lambda i: (mod_row(i), 0, 0)),
                  pl.BlockSpec((None, 1, D), lambda i: (mod_row(i), 0, 1)),
                  pl.BlockSpec((1, D), lambda i: (0, 0))] + [_resident(w.shape) for w in weights],
        out_specs=tuple(pl.BlockSpec((tm, w.shape[0]), lambda i: (i, 0)) for w in weights),
        compiler_params=_params(("arbitrary",)),
        name="inproj",
    )(x2, mod3, mod3, norm_g, *weights)


def _inproj_m_kernel(*refs, colmajor, nblk):
    if colmajor:
        x_ref, hp_ref, hn_ref, sh_ref, sc_ref, g_ref, wm_ref, wtg_ref, cw_ref, cb_ref = refs[:10]
        x = jnp.concatenate([x_ref[:, cl, :] for cl in range(COL_BLOCK)] + [hp_ref[7], hn_ref[0]], axis=0)
    else:
        x_ref, sh_ref, sc_ref, g_ref, wm_ref, wtg_ref, cw_ref, cb_ref = refs[:8]
        x = x_ref[...]
    k_ref, qt_ref, kt_ref, v_ref, tg_ref = refs[-5:]
    n = k_ref.shape[0]
    u = _norm_mod(x, g_ref[...], sh_ref[...], sc_ref[...])
    pre = _dot_nt(u, wm_ref[0:1024, :])
    a = pre[0:n]
    if colmajor:
        j = pl.program_id(0) % nblk
        prev_row = jnp.where(j > 0, pre[n + 7:n + 8], 0.0)
        next_row = jnp.where(j < nblk - 1, pre[n + 8:n + 9], 0.0)
    else:
        prev_row = next_row = jnp.zeros((1, 1024), F32)
    row8 = lax.broadcasted_iota(jnp.int32, (8, 1024), 0)
    ap = pltpu.roll(a, 1, axis=0)
    ap = jnp.concatenate([jnp.where(row8 == 0, prev_row, ap[0:8]), ap[8:]], axis=0)
    an = pltpu.roll(a, n - 1, axis=0)
    an = jnp.concatenate([an[0:n - 8], jnp.where(row8 == 7, next_row, an[n - 8:])], axis=0)
    conv = ap * cw_ref[0:1, :] + a * cw_ref[1:2, :] + an * cw_ref[2:3, :] + cb_ref[...]
    qk = conv * _sigmoid(conv)
    q = qk[:, 0:512]
    k = qk[:, 512:1024] * (DK ** -0.5)
    k_ref[...] = k.astype(BF16)
    qt_ref[...] = q.T.astype(BF16)
    kt_ref[...] = k.T.astype(BF16)
    um = u[0:n]
    v_ref[...] = _dot_nt(um, wm_ref[1024:2048, :]).astype(BF16)
    tg_ref[...] = _dot_nt(um, wtg_ref[...])


def _inproj_m(xv, mod3, mod_row, norm_g, wm, wtg, conv_w, conv_b, tm, colmajor):
    full = lambda shape: pl.BlockSpec(shape, lambda i: (0,) * len(shape))
    if colmajor:
        bsz = xv.shape[0]
        tn = SEQ
        nblk = GRID_W // COL_BLOCK
        blk = (None, GRID_W, COL_BLOCK, D)
        halo = (None, 8, COL_BLOCK, D)
        x_specs = [pl.BlockSpec(blk, lambda i: (i // nblk, 0, i % nblk, 0)),
                   pl.BlockSpec(halo, lambda i: (i // nblk, GRID_W // 8 - 1, jnp.maximum(i % nblk - 1, 0), 0)),
                   pl.BlockSpec(halo, lambda i: (i // nblk, 0, jnp.minimum(i % nblk + 1, nblk - 1), 0))]
        xs = (xv, xv, xv)
    else:
        tn = tm
        bsz = xv.shape[0] // tn
        nblk = 1
        x_specs = [pl.BlockSpec((tm, D), lambda i: (i, 0))]
        xs = (xv,)
    tok = lambda w: pl.BlockSpec((None, tm, w), lambda i: (i // nblk, i % nblk, 0))
    tr = pl.BlockSpec((None, 512, tm), lambda i: (i // nblk, 0, i % nblk))
    sds = jax.ShapeDtypeStruct
    return pl.pallas_call(
        functools.partial(_inproj_m_kernel, colmajor=colmajor, nblk=nblk),
        out_shape=(sds((bsz, tn, 512), BF16), sds((bsz, 512, tn), BF16), sds((bsz, 512, tn), BF16),
                   sds((bsz, tn, 1024), BF16), sds((bsz, tn, 256), F32)),
        grid=(bsz * nblk,),
        in_specs=x_specs + [pl.BlockSpec((None, 1, D), lambda i: (mod_row(i), 0, 0)),
                            pl.BlockSpec((None, 1, D), lambda i: (mod_row(i), 0, 1)),
                            full((1, D)), _resident(wm.shape), _resident(wtg.shape),
                            full((3, 1024)), full((1, 1024))],
        out_specs=(tok(512), tr, tr, tok(1024), tok(256)),
        compiler_params=_params(("arbitrary",)),
        name="inproj_m_cm" if colmajor else "inproj_m",
    )(*xs, mod3, mod3, norm_g, wm, wtg, conv_w, conv_b)


def _sum_directions(acc_ref, out_ref, blk, rs, cs, val):
    tot = acc_ref[blk, rs, cs] + val
    acc_ref[blk, rs, cs] = tot
    out_ref[rs, cs] = tot.astype(out_ref.dtype)


def _gla_kernel(pf_ref, pb_ref, tf_ref, tb_ref, wup_ref, bdec_ref, s0_ref, *rest, emit_out, ns):
    i = pl.program_id(1)
    if emit_out:
        olo_ref, ohi_ref, st_ref, acc_ref = rest
    else:
        (st_ref,) = rest
        olo_ref = ohi_ref = acc_ref = None

    @pl.when(i == 0)
    def _():
        st_ref[...] = s0_ref[...]
        if emit_out:
            acc_ref[...] = jnp.zeros(acc_ref.shape, F32)

    lc = GLA_CHUNK
    dirs = ((pf_ref, tf_ref, ohi_ref), (pb_ref, tb_ref, olo_ref))
    blks = (i, ns - 1 - i)
    units = _scan_units(pf_ref.shape[0] // lc)
    masks =[_tri(lc, d) for d in range(2)]
    gs = []
    for d, (p_ref, t_ref, o_ref) in enumerate(dirs):
        z = _dot(t_ref[...].astype(BF16), wup_ref[d]) + bdec_ref[d]
        gs.append(_log2_sigmoid(z) * (1.0 / TAU))
    bs = {}
    for d, j in units:
        bs[d, j] = _cumsum_rows(masks[d][1], gs[d][j * lc:(j + 1) * lc])
    ops = {}
    for d, j in units:
        p_ref = dirs[d][0]
        rs = slice(j * lc, (j + 1) * lc)
        b = bs[d, j]
        b_last = b[lc - 1:lc, :] if d == 0 else b[0:1, :]
        b_mid = b[lc // 2 - 1:lc // 2, :] if d == 0 else b[lc // 2:lc // 2 + 1, :]
        q = p_ref[rs, 0:512]
        k = p_ref[rs, 512:1024]
        qi = q * jnp.exp2(b).astype(BF16)
        kl = k * jnp.exp2((b_last + LOG2_QSCALE) - b).astype(BF16)
        dec = jnp.exp2(b_last)
        if emit_out:
            qd = q * jnp.exp2(b - b_mid).astype(BF16)
            kd = k * jnp.exp2((b_mid + LOG2_QSCALE) - b).astype(BF16)
        else:
            qd = kd = None
        ops[d, j] = (qi, kl, dec, qd, kd)
    sc, us, dcols = {}, {}, {}
    for d, j in units:
        p_ref = dirs[d][0]
        rs = slice(j * lc, (j + 1) * lc)
        qi, kl, dec, qd, kd = ops[d, j]
        for h in range(HEADS):
            ks = slice(h * DK, (h + 1) * DK)
            v = p_ref[rs, 1024 + h * DV:1024 + (h + 1) * DV]
            if emit_out:
                sc[d, j, h] = jnp.where(masks[d][0], _dot_nt(qd[:, ks], kd[:, ks]), 0.0).astype(BF16)
            us[d, j, h] = _dot_tn(kl[:, ks], v)
            dcols[d, j, h] = jnp.broadcast_to(dec[:, ks], (8, DK)).T[:, 0:1]
    for d in range(2):
        p_ref, _, o_ref = dirs[d]
        for h in range(HEADS):
            ks = slice(h * DK, (h + 1) * DK)
            st = st_ref[d * HEADS + h]
            for dd, j in units:
                if dd != d:
                    continue
                rs = slice(j * lc, (j + 1) * lc)
                if emit_out:
                    v = p_ref[rs, 1024 + h * DV:1024 + (h + 1) * DV]
                    o = _dot(jnp.concatenate([sc[d, j, h], ops[d, j][0][:, ks]], axis=1),
                             jnp.concatenate([v, st.astype(BF16)], axis=0))
                    _sum_directions(acc_ref, o_ref, blks[d], rs, slice(h * DV, (h + 1) * DV), o)
                st = st * dcols[d, j, h] + us[d, j, h]
            st_ref[d * HEADS + h] = st


def _half_specs(ns, step, width):
    half = ns // 2
    lo = pl.BlockSpec((None, step, width), lambda b, i: (b, jnp.minimum(ns - 1 - i, half - 1), 0))
    hi = pl.BlockSpec((None, step, width), lambda b, i: (b, jnp.maximum(i - half, 0), 0))
    return lo, hi


def _gla_scan(pg, tlr, wup, bdec, s0, emit_out):
    bn, tn, _ = pg.shape
    step = min(STEP, tn)
    ns = tn // step
    fwd = lambda b, i: (b, i, 0)
    bwd = lambda b, i: (b, ns - 1 - i, 0)
    st_shape = jax.ShapeDtypeStruct((bn, 2 * HEADS, DK, DV), F32)
    st_spec = pl.BlockSpec((None, 2 * HEADS, DK, DV), lambda b, i: (b, 0, 0, 0))
    o_shape = jax.ShapeDtypeStruct((bn, tn // 2, HEADS * DV), BF16)
    out_shape = (o_shape, o_shape, st_shape) if emit_out else (st_shape,)
    out_specs = (*_half_specs(ns, step, HEADS * DV), st_spec) if emit_out else (st_spec,)
    return pl.pallas_call(
        functools.partial(_gla_kernel, emit_out=emit_out, ns=ns),
        out_shape=out_shape,
        grid=(bn, ns),
        scratch_shapes=[pltpu.VMEM((ns, step, HEADS * DV), F32)] if emit_out else [],
        in_specs=[pl.BlockSpec((None, step, 2048), fwd),
                  pl.BlockSpec((None, step, 2048), bwd),
                  pl.BlockSpec((None, step, 128), fwd),
                  pl.BlockSpec((None, step, 128), bwd),
                  pl.BlockSpec((2, 128, 512), lambda b, i: (0, 0, 0)),
                  pl.BlockSpec((2, 1, 512), lambda b, i: (0, 0, 0)),
                  st_spec],
        out_specs=out_specs,
        compiler_params=_params(("arbitrary", "arbitrary")),
        name="gla_scan_lat" if emit_out else "gla_scan_ctx",
    )(pg, pg, tlr, tlr, wup, bdec, s0)


def _mlstm_kernel(kf_ref, kb_ref, qtf_ref, qtb_ref, ktf_ref, ktb_ref, vf_ref, vb_ref, tf_ref, tb_ref,
                  bg_ref, c0_ref, n0_ref, m0_ref, *rest, emit_out, ns):
    i = pl.program_id(1)
    if emit_out:
        hlo_ref, hhi_ref, c_ref, n_ref, m_ref, acc_ref = rest
    else:
        c_ref, n_ref, m_ref = rest
        hlo_ref = hhi_ref = acc_ref = None

    @pl.when(i == 0)
    def _():
        c_ref[...] = c0_ref[...]
        n_ref[...] = n0_ref[...]
        m_ref[...] = m0_ref[...]
        if emit_out:
            acc_ref[...] = jnp.zeros(acc_ref.shape, F32)

    lc = MLSTM_CHUNK
    dirs = ((kf_ref, qtf_ref, ktf_ref, vf_ref, tf_ref, hhi_ref), (kb_ref, qtb_ref, ktb_ref, vb_ref, tb_ref, hlo_ref))
    blks = (i, ns - 1 - i)
    units = _scan_units(kf_ref.shape[0] // lc)
    tris =[_tri(lc, d) for d in range(2)]
    lane_of = lambda d, h: GATE_LANE + 8 * d + h
    hs = lambda h: slice(h * DK, (h + 1) * DK)
    vs = lambda h: slice(h * DV, (h + 1) * DV)

    mrow = [m_ref[d:d + 1, :] for d in range(2)]
    tiles = {}
    for d, j in units:
        t_ref = dirs[d][4]
        rs = slice(j * lc, (j + 1) * lc)
        ga = (t_ref[rs, 0:128] + bg_ref[:, 0:128]) * LOG2E
        gb = t_ref[rs, 128:256] + bg_ref[:, 128:256]
        bc = _cumsum_rows(tris[d][1], _log2_sigmoid(gb))
        b_last = bc[lc - 1:lc, :] if d == 0 else bc[0:1, :]
        log_key = b_last - bc + ga
        m_new = jnp.maximum(b_last + mrow[d], jnp.max(log_key, axis=0, keepdims=True))
        tiles[d, j] = dict(rmat=ga - bc, bct=bc.T, m_in=mrow[d], wkt=jnp.exp2(log_key - m_new).T,
                           decay=jnp.exp2(b_last + mrow[d] - m_new))
        mrow[d] = m_new
    for d in range(2):
        m_ref[d:d + 1, :] = mrow[d]

    us, ncols = {}, {}
    for d, j in units:
        k_ref, _, kt_ref, v_ref = dirs[d][:4]
        rs = slice(j * lc, (j + 1) * lc)
        for h in range(HEADS):
            lane = lane_of(d, h)
            wk = tiles[d, j]["wkt"][lane:lane + 1, :]
            kwt = kt_ref[hs(h), rs] * wk.astype(BF16)
            us[d, j, h] = _dot(kwt, v_ref[rs, vs(h)])
            ncols[d, j, h] = _dot(jnp.broadcast_to(wk, (16, lc)).astype(BF16), k_ref[rs, hs(h)])[0:1]

    n_in = {}
    for d in range(2):
        for h in range(HEADS):
            idx = d * HEADS + h
            lane = lane_of(d, h)
            nvec = n_ref[idx:idx + 1, :]
            for dd, j in units:
                if dd == d:
                    n_in[d, j, h] = nvec
                    nvec = tiles[d, j]["decay"][:, lane:lane + 1] * nvec + ncols[d, j, h]
            n_ref[idx:idx + 1, :] = nvec

    lhs = {}
    if emit_out:
        for d, j in units:
            k_ref, qt_ref = dirs[d][:2]
            rs = slice(j * lc, (j + 1) * lc)
            t = tiles[d, j]
            causal_t = tris[1 - d][0]
            for h in range(HEADS):
                lane = lane_of(d, h)
                qt = qt_ref[hs(h), rs]
                kq = _dot(jnp.concatenate(
                    [k_ref[rs, hs(h)], jnp.broadcast_to(n_in[d, j, h], (16, DK)).astype(BF16)], axis=0), qt)
                rm = jnp.where(causal_t, t["rmat"][:, lane:lane + 1], -jnp.inf)
                mval = t["m_in"][:, lane:lane + 1]
                mx = jnp.maximum(mval, jnp.max(rm, axis=0, keepdims=True))
                wt = jnp.exp2(rm - mx) * kq[0:lc]
                w_inter = jnp.exp2(mval - mx)
                den = jnp.sum(wt, axis=0, keepdims=True) + w_inter * kq[lc:lc + 1]
                inv = 1.0 / jnp.maximum(jnp.abs(den), jnp.exp2(-(t["bct"][lane:lane + 1, :] + mx)))
                lhs[d, j, h] = jnp.concatenate(
                    [(wt * inv).astype(BF16), qt * (w_inter * inv).astype(BF16)], axis=0)

    for d in range(2):
        v_ref, o_ref = dirs[d][3], dirs[d][5]
        for h in range(HEADS):
            idx = d * HEADS + h
            lane = lane_of(d, h)
            cmat = c_ref[idx]
            for dd, j in units:
                if dd != d:
                    continue
                rs = slice(j * lc, (j + 1) * lc)
                if emit_out:
                    o = _dot_tn(lhs[d, j, h], jnp.concatenate([v_ref[rs, vs(h)], cmat.astype(BF16)], axis=0))
                    _sum_directions(acc_ref, o_ref, blks[d], rs, vs(h), o)
                cmat = tiles[d, j]["decay"][:, lane:lane + 1] * cmat + us[d, j, h]
            c_ref[idx] = cmat


def _mlstm_scan(k, qt, kt, v, tg, bgate, c0, n0, m0, emit_out):
    bn, tn, _ = k.shape
    step = min(STEP, tn)
    ns = tn // step
    fwd = lambda b, i: (b, i, 0)
    bwd = lambda b, i: (b, ns - 1 - i, 0)
    fwd_t = lambda b, i: (b, 0, i)
    bwd_t = lambda b, i: (b, 0, ns - 1 - i)
    c_shape = jax.ShapeDtypeStruct((bn, 2 * HEADS, DK, DV), F32)
    v_shape = jax.ShapeDtypeStruct((bn, 2 * HEADS, 128), F32)
    c_spec = pl.BlockSpec((None, 2 * HEADS, DK, DV), lambda b, i: (b, 0, 0, 0))
    v_spec = pl.BlockSpec((None, 2 * HEADS, 128), lambda b, i: (b, 0, 0))
    if emit_out:
        o_shape = jax.ShapeDtypeStruct((bn, tn // 2, HEADS * DV), F32)
        out_shape = (o_shape, o_shape, c_shape, v_shape, v_shape)
        out_specs = (*_half_specs(ns, step, HEADS * DV), c_spec, v_spec, v_spec)
    else:
        out_shape = (c_shape, v_shape, v_shape)
        out_specs = (c_spec, v_spec, v_spec)
    both = lambda shape, f, g: [pl.BlockSpec(shape, f), pl.BlockSpec(shape, g)]
    return pl.pallas_call(
        functools.partial(_mlstm_kernel, emit_out=emit_out, ns=ns),
        out_shape=out_shape,
        grid=(bn, ns),
        scratch_shapes=[pltpu.VMEM((ns, step, HEADS * DV), F32)] if emit_out else [],
        in_specs=(both((None, step, 512), fwd, bwd) + both((None, 512, step), fwd_t, bwd_t)
                  + both((None, 512, step), fwd_t, bwd_t) + both((None, step, 1024), fwd, bwd)
                  + both((None, step, 256), fwd, bwd)
                  + [pl.BlockSpec((1, 256), lambda b, i: (0, 0)), c_spec, v_spec, v_spec]),
        out_specs=out_specs,
        compiler_params=_params(("arbitrary", "arbitrary")),
        name="mlstm_scan_lat" if emit_out else "mlstm_scan_ctx",
    )(k, k, qt, qt, kt, kt, v, v, tg, tg, bgate, c0, n0, m0)


def _head_norm(o, g):
    parts = []
    for h in range(HEADS):
        oh = o[:, h * DV:(h + 1) * DV]
        parts.append(oh * lax.rsqrt(jnp.mean(oh * oh, axis=-1, keepdims=True) + EPS))
    return jnp.concatenate(parts, axis=-1) * g


def _merge_kernel(olo_ref, ohi_ref, hlo_ref, hhi_ref, po_ref, x_ref, g1_ref, gg_ref, gm_ref,
                  wbg_ref, wbm_ref, wo_ref, o_ref, *, per_b):
    groups = [(g * COL_BLOCK // MERGE_GROUPS, (g + 1) * COL_BLOCK // MERGE_GROUPS) for g in range(MERGE_GROUPS)]
    rows = [slice(a * GRID_W, b * GRID_W) for a, b in groups]
    lower = pl.program_id(0) % per_b < per_b // 2
    ys = []
    for (a, b), rs in zip(groups, rows):
        hm = jnp.concatenate([r[:, rl, :] for rl in range(a, b) for r in (hlo_ref, hhi_ref)], axis=0)
        o = jnp.where(lower, olo_ref[rs, :], ohi_ref[rs, :]).astype(F32)
        y_gla = _head_norm(o, gg_ref[...]) * po_ref[rs, 0:1024].astype(F32)
        y_m = _head_norm(hm, gm_ref[...]) * po_ref[rs, 1024:2048].astype(F32)
        ys.append((y_gla.astype(BF16), y_m.astype(BF16)))
    ds = [(_dot(y_gla, wbg_ref[...]), _dot(y_m, wbm_ref[...])) for y_gla, y_m in ys]
    ys = [(po_ref[rs, 2048:3072].astype(F32) * d_g + po_ref[rs, 3072:4096].astype(F32) * d_m).astype(BF16)
          for rs, (d_g, d_m) in zip(rows, ds)]
    mixes = [_dot(y, wo_ref[...]) for y in ys]
    for rs, mix in zip(rows, mixes):
        o_ref[rs, :] = x_ref[rs, :] + g1_ref[...] * mix


def _merge(olo, ohi, hlo4, hhi4, po, x2, mod3, gg, gm, wbg, wbm, wo):
    m = x2.shape[0]
    tm = GRID_W * COL_BLOCK
    per_b = SEQ // tm
    half = per_b // 2
    tok = lambda i: (i, 0)
    hspec = pl.BlockSpec((None, GRID_W // 2, COL_BLOCK, D), lambda i: (i // per_b, 0, i % per_b, 0))
    lo_spec = pl.BlockSpec((None, tm, D), lambda i: (i // per_b, jnp.minimum(i % per_b, half - 1), 0))
    hi_spec = pl.BlockSpec((None, tm, D), lambda i: (i // per_b, jnp.maximum(i % per_b - half, 0), 0))
    return pl.pallas_call(
        functools.partial(_merge_kernel, per_b=per_b),
        out_shape=jax.ShapeDtypeStruct((m, D), F32),
        grid=(m // tm,),
        in_specs=[lo_spec, hi_spec, hspec, hspec,
                  pl.BlockSpec((tm, 4096), tok), pl.BlockSpec((tm, D), tok),
                  pl.BlockSpec((None, 1, D), lambda i: (i // per_b, 0, 2)),
                  pl.BlockSpec((1, D), lambda i: (0, 0)), pl.BlockSpec((1, D), lambda i: (0, 0)),
                  _resident((D, D)), _resident((D, D)), _resident((D, D))],
        out_specs=pl.BlockSpec((tm, D), tok),
        compiler_params=_params(("arbitrary",)),
        name="merge",
    )(olo, ohi, hlo4, hhi4, po, x2, mod3, gg, gm, wbg, wbm, wo)


FF_TILES = ((0, 1280), (1280, 2816))


def _ffn_kernel(x_ref, sh_ref, sc_ref, g2_ref, ng_ref, fg_ref, wi_ref, wo_ref, o_ref):
    x = x_ref[...]
    u = _norm_mod(x, ng_ref[...], sh_ref[...], sc_ref[...])
    acc = None
    for lo, hi in FF_TILES:
        a = _dot(u, wi_ref[:, lo:hi])
        b = _dot(u, wi_ref[:, D_FF + lo:D_FF + hi])
        hid = (a * _sigmoid(a) * b).astype(BF16)
        part = _dot(hid, wo_ref[lo:hi, :])
        acc = part if acc is None else acc + part
    x2 = x + g2_ref[...] * acc
    o_ref[...] = x2 * lax.rsqrt(jnp.mean(x2 * x2, axis=-1, keepdims=True) + EPS) * fg_ref[...]


def _ffn(x1, mod3, ng, fg, wi, wo, tm):
    m = x1.shape[0]
    per_b = SEQ // tm
    tok = lambda i: (i, 0)
    modspec = lambda c: pl.BlockSpec((None, 1, D), lambda i: (i // per_b, 0, c))
    return pl.pallas_call(
        _ffn_kernel,
        out_shape=jax.ShapeDtypeStruct((m, D), F32),
        grid=(m // tm,),
        in_specs=[pl.BlockSpec((tm, D), tok), modspec(3), modspec(4), modspec(5),
                  pl.BlockSpec((1, D), lambda i: (0, 0)), pl.BlockSpec((1, D), lambda i: (0, 0)),
                  _resident((D, 2 * D_FF)), _resident((D_FF, D))],
        out_specs=pl.BlockSpec((tm, D), tok),
        compiler_params=_params(("arbitrary",)),
        name="ffn",
    )(x1, mod3, mod3, mod3, ng, fg, wi, wo)


def _split_w_in_kernel(w_ref, wg_ref, wm_ref, wo_ref, wtl_ref, wtg_ref):
    cols = w_ref.shape[1]
    c = lambda a, b: w_ref[a:b, :].astype(BF16)
    z = lambda n: jnp.zeros((n, cols), BF16)
    wg_ref[...] = c(0, 2048)
    wm_ref[...] = c(3104, 5152)
    wo_ref[0:1024, :] = c(2048, 3072)
    wo_ref[1024:2048, :] = c(5152, 6176)
    wo_ref[2048:4096, :] = c(6192, 8240)
    wtl_ref[...] = jnp.concatenate([c(3072, 3104), z(128 - 2 * RANK)], axis=0)
    wtg_ref[...] = jnp.concatenate(
        [z(GATE_LANE), c(6176, 6192), z(128 - GATE_LANE - 16),
         z(GATE_LANE), c(6180, 6184), z(4), c(6188, 6192), z(128 - GATE_LANE - 12)], axis=0)


def _split_w_in(w_in_t):
    tc = 256
    heights = (2048, 2048, 4096, 128, 256)
    return pl.pallas_call(
        _split_w_in_kernel,
        out_shape=tuple(jax.ShapeDtypeStruct((n, D), BF16) for n in heights),
        grid=(D // tc,),
        in_specs=[pl.BlockSpec((w_in_t.shape[0], tc), lambda i: (0, i))],
        out_specs=tuple(pl.BlockSpec((n, tc), lambda i: (0, i)) for n in heights),
        compiler_params=_params(("arbitrary",)),
        name="split_w_in",
    )(w_in_t)


def kernel(x, c, ctx, c_ctx, w_ada, b_ada, norm1_g, w_in, gla_w_up, gla_b_dec, gla_norm_g,
           mlstm_conv_w, mlstm_conv_b, mlstm_b_gate, mlstm_norm_g, w_br_gla, w_br_mlstm, w_out,
           norm2_g, w_ffn_in, w_ffn_out, final_g):
    bsz = x.shape[0]
    row = lambda a: a.reshape(1, -1)

    cvec = jnp.concatenate([c, c_ctx[None, :], jnp.zeros((8 - bsz - 1, D), F32)], axis=0)
    mod3 = _ada(cvec, w_ada[0], row(b_ada[0])).reshape(8, 1, N_MOD)

    wg, wm, wo, wtl, wtg = _split_w_in(w_in[0].T)
    x2 = x.reshape(bsz * SEQ, D)
    ctx2 = ctx.reshape(bsz * CTX, D)
    g1n = row(norm1_g[0])
    tm = GRID_W * COL_BLOCK
    po, pg, tlr = _inproj(x2, mod3, lambda i: i // (SEQ // tm), g1n, (wo, wg, wtl), (BF16, BF16, F32), tm,
                          acts=(("silu", "sigmoid", "sigmoid", "sigmoid"), None, None))
    conv_w = mlstm_conv_w[0]
    conv_b = row(mlstm_conv_b[0])
    m_lat = _inproj_m(x.reshape(bsz, SEQ // GRID_W, GRID_W, D), mod3,
                      lambda i: i // (GRID_W // COL_BLOCK), g1n, wm, wtg, conv_w, conv_b, tm, True)
    pg_c, tlr_c = _inproj(ctx2, mod3, lambda i: bsz, g1n, (wg, wtl), (BF16, F32), CTX)
    m_ctx = _inproj_m(ctx2, mod3, lambda i: bsz, g1n, wm, wtg, conv_w, conv_b, CTX, False)

    wup = jnp.zeros((2, 128, HEADS * DK), F32)
    wup = wup.at[0, 0:RANK].set(gla_w_up[0, 0]).at[1, RANK:2 * RANK].set(gla_w_up[0, 1]).astype(BF16)
    bdec = gla_b_dec[0].reshape(2, 1, HEADS * DK)
    s0 = jnp.zeros((bsz, 2 * HEADS, DK, DV), F32)
    (s_ctx,) = _gla_scan(pg_c.reshape(bsz, CTX, 2048), tlr_c.reshape(bsz, CTX, 128),
                         wup, bdec, s0, emit_out=False)
    o_lo, o_hi, _ = _gla_scan(pg.reshape(bsz, SEQ, 2048), tlr.reshape(bsz, SEQ, 128),
                              wup, bdec, s_ctx, emit_out=True)

    bgate = mlstm_b_gate[0].reshape(1, 16)
    zg = lambda n: jnp.zeros((1, n), F32)
    bg2 = jnp.concatenate([zg(GATE_LANE), bgate, zg(128 - GATE_LANE - 16),
                           zg(GATE_LANE), bgate[:, 4:8], zg(4), bgate[:, 12:16],
                           zg(128 - GATE_LANE - 12)], axis=1)
    c0 = jnp.zeros((bsz, 2 * HEADS, DK, DV), F32)
    v0 = jnp.zeros((bsz, 2 * HEADS, 128), F32)
    c_ctx_s, n_ctx_s, m_ctx_s = _mlstm_scan(*m_ctx, bg2, c0, v0, v0, emit_out=False)
    h_lo, h_hi, _, _, _ = _mlstm_scan(*m_lat, bg2, c_ctx_s, n_ctx_s, m_ctx_s, emit_out=True)

    cm4 = lambda a: a.reshape(bsz, GRID_W // 2, SEQ // GRID_W, D)
    x1 = _merge(o_lo, o_hi, cm4(h_lo), cm4(h_hi), po, x2, mod3,
                row(gla_norm_g[0]), row(mlstm_norm_g[0]),
                w_br_gla[0].astype(BF16), w_br_mlstm[0].astype(BF16), w_out[0].astype(BF16))
    out = _ffn(x1, mod3, row(norm2_g[0]), row(final_g),
               w_ffn_in[0].astype(BF16), w_ffn_out[0].astype(BF16), tm)
    return out.reshape(bsz, SEQ, D)
```

```python
import functools
import math

import jax
import jax.numpy as jnp
from jax import lax
from jax.experimental import pallas as pl
from jax.experimental.pallas import tpu as pltpu

D = 1024
SEQ = 4096
CTX = 256
GRID_W = 64
EPS = 1e-6
HEADS = 4
DK = 128
DV = 256
RANK = 16
TAU = 16.0
D_FF = 2816
N_MOD = 6 * D
GATE_LANE = 32
STEP = 512
GLA_CHUNK = 128
MLSTM_CHUNK = 128
COL_BLOCK = 8
MERGE_GROUPS = 2

LOG2E = math.log2(math.e)
LOG2_QSCALE = -0.5 * math.log2(DK)

F32 = jnp.float32
BF16 = jnp.bfloat16
VMEM_LIMIT = 56 * 1024 * 1024


def _dot(a, b):
    return jnp.dot(a, b, preferred_element_type=F32)


def _dot_nt(a, b):
    return lax.dot_general(a, b, (((1,), (1,)), ((), ())), preferred_element_type=F32)


def _dot_tn(a, b):
    return lax.dot_general(a, b, (((0,), (0,)), ((), ())), preferred_element_type=F32)


def _sigmoid(x):
    return 1.0 / (1.0 + jnp.exp(-x))


def _log2_sigmoid(x):
    return jnp.minimum(x, 0.0) * LOG2E - jnp.log2(1.0 + jnp.exp2(jnp.abs(x) * (-LOG2E)))


def _cumsum_rows(tri, g):
    g1 = g.astype(BF16)
    g2 = (g - g1.astype(F32)).astype(BF16)
    return _dot(tri, g1) + _dot(tri, g2)


def _tri(n, d):
    row = lax.broadcasted_iota(jnp.int32, (n, n), 0)
    col = lax.broadcasted_iota(jnp.int32, (n, n), 1)
    causal = (col <= row) if d == 0 else (col >= row)
    return causal, jnp.where(causal, 1.0, 0.0).astype(BF16)


def _resident(shape):
    n = len(shape)
    return pl.BlockSpec(shape, lambda *_: (0,) * n, pipeline_mode=pl.Buffered(1))


def _params(sem):
    return pltpu.CompilerParams(dimension_semantics=sem, vmem_limit_bytes=VMEM_LIMIT)


def _skewed(units, stages):
    for t in range(len(units) + len(stages) - 1):
        for s_idx, stage in enumerate(stages):
            if 0 <= t - s_idx < len(units):
                stage(*units[t - s_idx])


def _scan_units(n_sub):
    return [(d, j) for jj in range(n_sub) for d, j in ((0, jj), (1, n_sub - 1 - jj))]


def _ada_kernel(c_ref, w_ref, b_ref, o_ref):
    cv = c_ref[...]
    s = (cv * _sigmoid(cv)).astype(BF16)
    o_ref[...] = _dot(s, w_ref[...].astype(BF16)) + b_ref[...]


def _ada(cvec, w_ada, b_ada):
    tn = 1024
    return pl.pallas_call(
        _ada_kernel,
        out_shape=jax.ShapeDtypeStruct((8, N_MOD), F32),
        grid=(N_MOD // tn,),
        in_specs=[pl.BlockSpec((8, D), lambda j: (0, 0)),
                  pl.BlockSpec((D, tn), lambda j: (0, j)),
                  pl.BlockSpec((1, tn), lambda j: (0, j))],
        out_specs=pl.BlockSpec((8, tn), lambda j: (0, j)),
        compiler_params=_params(("arbitrary",)),
        name="adaln",
    )(cvec, w_ada, b_ada)


def _norm_mod_f32(x, g, sh, sc):
    return x * lax.rsqrt(jnp.mean(x * x, axis=-1, keepdims=True) + EPS) * (g * (1.0 + sc)) + sh


def _norm_mod(x, g, sh, sc):
    return _norm_mod_f32(x, g, sh, sc).astype(BF16)


def _act(v, kind):
    if kind == "silu":
        return v * _sigmoid(v)
    if kind == "sigmoid":
        return _sigmoid(v)
    return v


def _inproj_kernel(x_ref, sh_ref, sc_ref, g_ref, *refs, acts, emit_u):
    uf = _norm_mod_f32(x_ref[...], g_ref[...], sh_ref[...], sc_ref[...])
    if emit_u:
        refs[-1][...] = uf
        refs = refs[:-1]
    n_out = len(refs) // 2
    u = uf.astype(BF16)
    for w_ref, o_ref, act in zip(refs[:n_out], refs[n_out:], acts):
        n = w_ref.shape[0]
        for jc, j in enumerate(range(0, n, 1024)):
            cs = slice(j, min(j + 1024, n))
            o_ref[:, cs] = _act(_dot_nt(u, w_ref[cs, :]), act[jc] if act else None).astype(o_ref.dtype)


def _inproj(x2, mod3, mod_row, norm_g, weights, out_dtypes, tm, acts=None, emit_u=False):
    m = x2.shape[0]
    acts = acts or (None,) * len(weights)
    widths = [w.shape[0] for w in weights] + ([D] if emit_u else [])
    dtypes = list(out_dtypes) + ([F32] if emit_u else [])
    return pl.pallas_call(
        functools.partial(_inproj_kernel, acts=acts, emit_u=emit_u),
        out_shape=tuple(jax.ShapeDtypeStruct((m, n), dt) for n, dt in zip(widths, dtypes)),
        grid=(m // tm,),
        in_specs=[pl.BlockSpec((tm, D), lambda i: (i, 0)),
                  pl.BlockSpec((None, 1, D), lambda i: (mod_row(i), 0, 0)),
                  pl.BlockSpec((None, 1, D), lambda i: (mod_row(i), 0, 1)),
                  pl.BlockSpec((1, D), lambda i: (0, 0))] + [_resident(w.shape) for w in weights],
        out_specs=tuple(pl.BlockSpec((tm, n), lambda i: (i, 0)) for n in widths),
        compiler_params=_params(("arbitrary",)),
        name="inproj",
    )(x2, mod3, mod3, norm_g, *weights)


def _inproj_m_kernel(*refs, colmajor, nblk):
    if colmajor:
        u_ref, hp_ref, hn_ref, wm_ref, wtg_ref, cw_ref, cb_ref = refs[:7]
        u = jnp.concatenate([u_ref[:, cl, :] for cl in range(COL_BLOCK)] + [hp_ref[7], hn_ref[0]],
                            axis=0).astype(BF16)
    else:
        x_ref, sh_ref, sc_ref, g_ref, wm_ref, wtg_ref, cw_ref, cb_ref = refs[:8]
        u = _norm_mod(x_ref[...], g_ref[...], sh_ref[...], sc_ref[...])
    k_ref, qt_ref, kt_ref, v_ref, tg_ref = refs[-5:]
    n = k_ref.shape[0]
    pre = _dot_nt(u, wm_ref[0:1024, :])
    a = pre[0:n]
    if colmajor:
        j = pl.program_id(0) % nblk
        prev_row = jnp.where(j > 0, pre[n + 7:n + 8], 0.0)
        next_row = jnp.where(j < nblk - 1, pre[n + 8:n + 9], 0.0)
    else:
        prev_row = next_row = jnp.zeros((1, 1024), F32)
    row8 = lax.broadcasted_iota(jnp.int32, (8, 1024), 0)
    ap = pltpu.roll(a, 1, axis=0)
    ap = jnp.concatenate([jnp.where(row8 == 0, prev_row, ap[0:8]), ap[8:]], axis=0)
    an = pltpu.roll(a, n - 1, axis=0)
    an = jnp.concatenate([an[0:n - 8], jnp.where(row8 == 7, next_row, an[n - 8:])], axis=0)
    conv = ap * cw_ref[0:1, :] + a * cw_ref[1:2, :] + an * cw_ref[2:3, :] + cb_ref[...]
    qk = conv * _sigmoid(conv)
    q = qk[:, 0:512]
    k = qk[:, 512:1024] * (DK ** -0.5)
    k_ref[...] = k.astype(BF16)
    qt_ref[...] = q.T.astype(BF16)
    kt_ref[...] = k.T.astype(BF16)
    um = u[0:n]
    v_ref[...] = _dot_nt(um, wm_ref[1024:2048, :]).astype(BF16)
    tg_ref[...] = _dot_nt(um, wtg_ref[...])


def _inproj_m(xv, mod3, mod_row, norm_g, wm, wtg, conv_w, conv_b, tm, colmajor):
    full = lambda shape: pl.BlockSpec(shape, lambda i: (0,) * len(shape))
    mod_specs = [pl.BlockSpec((None, 1, D), lambda i: (mod_row(i), 0, 0)),
                 pl.BlockSpec((None, 1, D), lambda i: (mod_row(i), 0, 1)), full((1, D))]
    mod_args = (mod3, mod3, norm_g)
    if colmajor:
        mod_specs, mod_args = [], ()
        bsz = xv.shape[0]
        tn = SEQ
        nblk = GRID_W // COL_BLOCK
        blk = (None, GRID_W, COL_BLOCK, D)
        halo = (None, 8, COL_BLOCK, D)
        x_specs = [pl.BlockSpec(blk, lambda i: (i // nblk, 0, i % nblk, 0)),
                   pl.BlockSpec(halo, lambda i: (i // nblk, GRID_W // 8 - 1, jnp.maximum(i % nblk - 1, 0), 0)),
                   pl.BlockSpec(halo, lambda i: (i // nblk, 0, jnp.minimum(i % nblk + 1, nblk - 1), 0))]
        xs = (xv, xv, xv)
    else:
        tn = tm
        bsz = xv.shape[0] // tn
        nblk = 1
        x_specs = [pl.BlockSpec((tm, D), lambda i: (i, 0))]
        xs = (xv,)
    tok = lambda w: pl.BlockSpec((None, tm, w), lambda i: (i // nblk, i % nblk, 0))
    tr = pl.BlockSpec((None, 512, tm), lambda i: (i // nblk, 0, i % nblk))
    sds = jax.ShapeDtypeStruct
    return pl.pallas_call(
        functools.partial(_inproj_m_kernel, colmajor=colmajor, nblk=nblk),
        out_shape=(sds((bsz, tn, 512), BF16), sds((bsz, 512, tn), BF16), sds((bsz, 512, tn), BF16),
                   sds((bsz, tn, 1024), BF16), sds((bsz, tn, 256), F32)),
        grid=(bsz * nblk,),
        in_specs=x_specs + mod_specs + [_resident(wm.shape), _resident(wtg.shape),
                                        full((3, 1024)), full((1, 1024))],
        out_specs=(tok(512), tr, tr, tok(1024), tok(256)),
        compiler_params=_params(("arbitrary",)),
        name="inproj_m_cm" if colmajor else "inproj_m",
    )(*xs, *mod_args, wm, wtg, conv_w, conv_b)


def _sum_directions(acc_ref, out_ref, blk, rs, cs, val):
    tot = acc_ref[blk, rs, cs] + val
    acc_ref[blk, rs, cs] = tot
    out_ref[rs, cs] = tot.astype(out_ref.dtype)


def _gla_kernel(pf_ref, pb_ref, tf_ref, tb_ref, wup_ref, bdec_ref, s0_ref, *rest, emit_out, ns):
    i = pl.program_id(1)
    if emit_out:
        olo_ref, ohi_ref, st_ref, acc_ref = rest
    else:
        (st_ref,) = rest
        olo_ref = ohi_ref = acc_ref = None

    @pl.when(i == 0)
    def _():
        st_ref[...] = s0_ref[...]
        if emit_out:
            acc_ref[...] = jnp.zeros(acc_ref.shape, F32)

    lc = GLA_CHUNK
    dirs = ((pf_ref, tf_ref, ohi_ref), (pb_ref, tb_ref, olo_ref))
    blks = (i, ns - 1 - i)
    units = _scan_units(pf_ref.shape[0] // lc)
    masks =[_tri(lc, d) for d in range(2)]
    gs = []
    for d, (p_ref, t_ref, o_ref) in enumerate(dirs):
        z = _dot(t_ref[...].astype(BF16), wup_ref[d]) + bdec_ref[d]
        gs.append(_log2_sigmoid(z) * (1.0 / TAU))
    bs, ops, sc, us, dcols = {}, {}, {}, {}, {}
    st = {(d, h): st_ref[d * HEADS + h] for d in range(2) for h in range(HEADS)}

    def stage2(d, j):
        bs[d, j] = _cumsum_rows(masks[d][1], gs[d][j * lc:(j + 1) * lc])

    def stage3(d, j):
        p_ref = dirs[d][0]
        rs = slice(j * lc, (j + 1) * lc)
        b = bs[d, j]
        b_last = b[lc - 1:lc, :] if d == 0 else b[0:1, :]
        b_mid = b[lc // 2 - 1:lc // 2, :] if d == 0 else b[lc // 2:lc // 2 + 1, :]
        q = p_ref[rs, 0:512]
        k = p_ref[rs, 512:1024]
        qd = q * jnp.exp2(b - b_mid).astype(BF16)
        kd = k * jnp.exp2((b_mid + LOG2_QSCALE) - b).astype(BF16)
        qi = qd * jnp.exp2(b_mid).astype(BF16)
        kl = kd * jnp.exp2(b_last - b_mid).astype(BF16)
        dec = jnp.exp2(b_last)
        ops[d, j] = (qi, kl, dec, qd, kd)

    def stage4(d, j):
        p_ref = dirs[d][0]
        rs = slice(j * lc, (j + 1) * lc)
        qi, kl, dec, qd, kd = ops[d, j]
        for h in range(HEADS):
            ks = slice(h * DK, (h + 1) * DK)
            v = p_ref[rs, 1024 + h * DV:1024 + (h + 1) * DV]
            if emit_out:
                sc[d, j, h] = jnp.where(masks[d][0], _dot_nt(qd[:, ks], kd[:, ks]), 0.0).astype(BF16)
            us[d, j, h] = _dot_tn(kl[:, ks], v)
            dcols[d, j, h] = jnp.broadcast_to(dec[:, ks], (8, DK)).T[:, 0:1]
    def stage5(d, j):
        p_ref, _, o_ref = dirs[d]
        rs = slice(j * lc, (j + 1) * lc)
        for h in range(HEADS):
            ks = slice(h * DK, (h + 1) * DK)
            if emit_out:
                v = p_ref[rs, 1024 + h * DV:1024 + (h + 1) * DV]
                o = _dot(jnp.concatenate([sc[d, j, h], ops[d, j][0][:, ks]], axis=1),
                         jnp.concatenate([v, st[d, h].astype(BF16)], axis=0))
                _sum_directions(acc_ref, o_ref, blks[d], rs, slice(h * DV, (h + 1) * DV), o)
            st[d, h] = st[d, h] * dcols[d, j, h] + us[d, j, h]

    _skewed(units, (stage2, stage3, stage4, stage5))
    for (d, h), val in st.items():
        st_ref[d * HEADS + h] = val


def _half_specs(ns, step, width):
    half = ns // 2
    lo = pl.BlockSpec((None, step, width), lambda b, i: (b, jnp.minimum(ns - 1 - i, half - 1), 0))
    hi = pl.BlockSpec((None, step, width), lambda b, i: (b, jnp.maximum(i - half, 0), 0))
    return lo, hi


def _gla_scan(pg, tlr, wup, bdec, s0, emit_out):
    bn, tn, _ = pg.shape
    step = min(STEP, tn)
    ns = tn // step
    fwd = lambda b, i: (b, i, 0)
    bwd = lambda b, i: (b, ns - 1 - i, 0)
    st_shape = jax.ShapeDtypeStruct((bn, 2 * HEADS, DK, DV), F32)
    st_spec = pl.BlockSpec((None, 2 * HEADS, DK, DV), lambda b, i: (b, 0, 0, 0))
    o_shape = jax.ShapeDtypeStruct((bn, tn // 2, HEADS * DV), BF16)
    out_shape = (o_shape, o_shape, st_shape) if emit_out else (st_shape,)
    out_specs = (*_half_specs(ns, step, HEADS * DV), st_spec) if emit_out else (st_spec,)
    return pl.pallas_call(
        functools.partial(_gla_kernel, emit_out=emit_out, ns=ns),
        out_shape=out_shape,
        grid=(bn, ns),
        scratch_shapes=[pltpu.VMEM((ns, step, HEADS * DV), F32)] if emit_out else [],
        in_specs=[pl.BlockSpec((None, step, 2048), fwd),
                  pl.BlockSpec((None, step, 2048), bwd),
                  pl.BlockSpec((None, step, 128), fwd),
                  pl.BlockSpec((None, step, 128), bwd),
                  pl.BlockSpec((2, 128, 512), lambda b, i: (0, 0, 0)),
                  pl.BlockSpec((2, 1, 512), lambda b, i: (0, 0, 0)),
                  st_spec],
        out_specs=out_specs,
        compiler_params=_params(("arbitrary", "arbitrary")),
        name="gla_scan_lat" if emit_out else "gla_scan_ctx",
    )(pg, pg, tlr, tlr, wup, bdec, s0)


def _mlstm_kernel(kf_ref, kb_ref, qtf_ref, qtb_ref, ktf_ref, ktb_ref, vf_ref, vb_ref, tf_ref, tb_ref,
                  bg_ref, c0_ref, n0_ref, m0_ref, *rest, emit_out, ns):
    i = pl.program_id(1)
    if emit_out:
        hlo_ref, hhi_ref, c_ref, n_ref, m_ref, acc_ref = rest
    else:
        c_ref, n_ref, m_ref = rest
        hlo_ref = hhi_ref = acc_ref = None

    @pl.when(i == 0)
    def _():
        c_ref[...] = c0_ref[...]
        n_ref[...] = n0_ref[...]
        m_ref[...] = m0_ref[...]
        if emit_out:
            acc_ref[...] = jnp.zeros(acc_ref.shape, F32)

    lc = MLSTM_CHUNK
    dirs = ((kf_ref, qtf_ref, ktf_ref, vf_ref, tf_ref, hhi_ref), (kb_ref, qtb_ref, ktb_ref, vb_ref, tb_ref, hlo_ref))
    blks = (i, ns - 1 - i)
    units = _scan_units(kf_ref.shape[0] // lc)
    tris =[_tri(lc, d) for d in range(2)]
    lane_of = lambda d, h: GATE_LANE + 8 * d + h
    hs = lambda h: slice(h * DK, (h + 1) * DK)
    vs = lambda h: slice(h * DV, (h + 1) * DV)

    mrow = [m_ref[d:d + 1, :] for d in range(2)]
    tiles = {}
    for d, j in units:
        t_ref = dirs[d][4]
        rs = slice(j * lc, (j + 1) * lc)
        ga = (t_ref[rs, 0:128] + bg_ref[:, 0:128]) * LOG2E
        gb = t_ref[rs, 128:256] + bg_ref[:, 128:256]
        bc = _cumsum_rows(tris[d][1], _log2_sigmoid(gb))
        b_last = bc[lc - 1:lc, :] if d == 0 else bc[0:1, :]
        log_key = b_last - bc + ga
        m_new = jnp.maximum(b_last + mrow[d], jnp.max(log_key, axis=0, keepdims=True))
        tiles[d, j] = dict(rmat=ga - bc, bct=bc.T, m_in=mrow[d], wkt=jnp.exp2(log_key - m_new).T,
                           decay=jnp.exp2(b_last + mrow[d] - m_new))
        mrow[d] = m_new
    for d in range(2):
        m_ref[d:d + 1, :] = mrow[d]

    us, ncols = {}, {}
    for d, j in units:
        k_ref, _, kt_ref, v_ref = dirs[d][:4]
        rs = slice(j * lc, (j + 1) * lc)
        for h in range(HEADS):
            lane = lane_of(d, h)
            wk = tiles[d, j]["wkt"][lane:lane + 1, :]
            kwt = kt_ref[hs(h), rs] * wk.astype(BF16)
            us[d, j, h] = _dot(kwt, v_ref[rs, vs(h)])
            ncols[d, j, h] = _dot(jnp.broadcast_to(wk, (16, lc)).astype(BF16), k_ref[rs, hs(h)])[0:1]

    n_in = {}
    for d in range(2):
        for h in range(HEADS):
            idx = d * HEADS + h
            lane = lane_of(d, h)
            nvec = n_ref[idx:idx + 1, :]
            for dd, j in units:
                if dd == d:
                    n_in[d, j, h] = nvec
                    nvec = tiles[d, j]["decay"][:, lane:lane + 1] * nvec + ncols[d, j, h]
            n_ref[idx:idx + 1, :] = nvec

    lhs = {}
    if emit_out:
        for d, j in units:
            k_ref, qt_ref = dirs[d][:2]
            rs = slice(j * lc, (j + 1) * lc)
            t = tiles[d, j]
            causal_t = tris[1 - d][0]
            for h in range(HEADS):
                lane = lane_of(d, h)
                qt = qt_ref[hs(h), rs]
                kq = _dot(jnp.concatenate(
                    [k_ref[rs, hs(h)], jnp.broadcast_to(n_in[d, j, h], (16, DK)).astype(BF16)], axis=0), qt)
                rm = jnp.where(causal_t, t["rmat"][:, lane:lane + 1], -jnp.inf)
                mval = t["m_in"][:, lane:lane + 1]
                mx = jnp.maximum(mval, jnp.max(rm, axis=0, keepdims=True))
                wt = jnp.exp2(rm - mx) * kq[0:lc]
                w_inter = jnp.exp2(mval - mx)
                den = jnp.sum(wt, axis=0, keepdims=True) + w_inter * kq[lc:lc + 1]
                inv = 1.0 / jnp.maximum(jnp.abs(den), jnp.exp2(-(t["bct"][lane:lane + 1, :] + mx)))
                lhs[d, j, h] = jnp.concatenate(
                    [(wt * inv).astype(BF16), qt * (w_inter * inv).astype(BF16)], axis=0)

    for d in range(2):
        v_ref, o_ref = dirs[d][3], dirs[d][5]
        for h in range(HEADS):
            idx = d * HEADS + h
            lane = lane_of(d, h)
            cmat = c_ref[idx]
            for dd, j in units:
                if dd != d:
                    continue
                rs = slice(j * lc, (j + 1) * lc)
                if emit_out:
                    o = _dot_tn(lhs[d, j, h], jnp.concatenate([v_ref[rs, vs(h)], cmat.astype(BF16)], axis=0))
                    _sum_directions(acc_ref, o_ref, blks[d], rs, vs(h), o)
                cmat = tiles[d, j]["decay"][:, lane:lane + 1] * cmat + us[d, j, h]
            c_ref[idx] = cmat


def _mlstm_scan(k, qt, kt, v, tg, bgate, c0, n0, m0, emit_out):
    bn, tn, _ = k.shape
    step = min(STEP, tn)
    ns = tn // step
    fwd = lambda b, i: (b, i, 0)
    bwd = lambda b, i: (b, ns - 1 - i, 0)
    fwd_t = lambda b, i: (b, 0, i)
    bwd_t = lambda b, i: (b, 0, ns - 1 - i)
    c_shape = jax.ShapeDtypeStruct((bn, 2 * HEADS, DK, DV), F32)
    v_shape = jax.ShapeDtypeStruct((bn, 2 * HEADS, 128), F32)
    c_spec = pl.BlockSpec((None, 2 * HEADS, DK, DV), lambda b, i: (b, 0, 0, 0))
    v_spec = pl.BlockSpec((None, 2 * HEADS, 128), lambda b, i: (b, 0, 0))
    if emit_out:
        o_shape = jax.ShapeDtypeStruct((bn, tn // 2, HEADS * DV), F32)
        out_shape = (o_shape, o_shape, c_shape, v_shape, v_shape)
        out_specs = (*_half_specs(ns, step, HEADS * DV), c_spec, v_spec, v_spec)
    else:
        out_shape = (c_shape, v_shape, v_shape)
        out_specs = (c_spec, v_spec, v_spec)
    both = lambda shape, f, g: [pl.BlockSpec(shape, f), pl.BlockSpec(shape, g)]
    return pl.pallas_call(
        functools.partial(_mlstm_kernel, emit_out=emit_out, ns=ns),
        out_shape=out_shape,
        grid=(bn, ns),
        scratch_shapes=[pltpu.VMEM((ns, step, HEADS * DV), F32)] if emit_out else [],
        in_specs=(both((None, step, 512), fwd, bwd) + both((None, 512, step), fwd_t, bwd_t)
                  + both((None, 512, step), fwd_t, bwd_t) + both((None, step, 1024), fwd, bwd)
                  + both((None, step, 256), fwd, bwd)
                  + [pl.BlockSpec((1, 256), lambda b, i: (0, 0)), c_spec, v_spec, v_spec]),
        out_specs=out_specs,
        compiler_params=_params(("arbitrary", "arbitrary")),
        name="mlstm_scan_lat" if emit_out else "mlstm_scan_ctx",
    )(k, k, qt, qt, kt, kt, v, v, tg, tg, bgate, c0, n0, m0)


def _head_norm(o, g):
    parts = []
    for h in range(HEADS):
        oh = o[:, h * DV:(h + 1) * DV]
        parts.append(oh * lax.rsqrt(jnp.mean(oh * oh, axis=-1, keepdims=True) + EPS))
    return jnp.concatenate(parts, axis=-1) * g


def _merge_kernel(olo_ref, ohi_ref, hlo_ref, hhi_ref, po_ref, x_ref, g1_ref, gg_ref, gm_ref,
                  wbg_ref, wbm_ref, wo_ref, o_ref, *, per_b):
    groups = [(g * COL_BLOCK // MERGE_GROUPS, (g + 1) * COL_BLOCK // MERGE_GROUPS) for g in range(MERGE_GROUPS)]
    rows = [slice(a * GRID_W, b * GRID_W) for a, b in groups]
    lower = pl.program_id(0) % per_b < per_b // 2
    ys = []
    for (a, b), rs in zip(groups, rows):
        hm = jnp.concatenate([r[:, rl, :] for rl in range(a, b) for r in (hlo_ref, hhi_ref)], axis=0)
        o = jnp.where(lower, olo_ref[rs, :], ohi_ref[rs, :]).astype(F32)
        y_gla = _head_norm(o, gg_ref[...]) * po_ref[rs, 0:1024].astype(F32)
        y_m = _head_norm(hm, gm_ref[...]) * po_ref[rs, 1024:2048].astype(F32)
        ys.append((y_gla.astype(BF16), y_m.astype(BF16)))
    ds = [(_dot(y_gla, wbg_ref[...]), _dot(y_m, wbm_ref[...])) for y_gla, y_m in ys]
    ys = [(po_ref[rs, 2048:3072].astype(F32) * d_g + po_ref[rs, 3072:4096].astype(F32) * d_m).astype(BF16)
          for rs, (d_g, d_m) in zip(rows, ds)]
    mixes = [_dot(y, wo_ref[...]) for y in ys]
    for rs, mix in zip(rows, mixes):
        o_ref[rs, :] = x_ref[rs, :] + g1_ref[...] * mix


def _merge(olo, ohi, hlo4, hhi4, po, x2, mod3, gg, gm, wbg, wbm, wo):
    m = x2.shape[0]
    tm = GRID_W * COL_BLOCK
    per_b = SEQ // tm
    half = per_b // 2
    tok = lambda i: (i, 0)
    hspec = pl.BlockSpec((None, GRID_W // 2, COL_BLOCK, D), lambda i: (i // per_b, 0, i % per_b, 0))
    lo_spec = pl.BlockSpec((None, tm, D), lambda i: (i // per_b, jnp.minimum(i % per_b, half - 1), 0))
    hi_spec = pl.BlockSpec((None, tm, D), lambda i: (i // per_b, jnp.maximum(i % per_b - half, 0), 0))
    return pl.pallas_call(
        functools.partial(_merge_kernel, per_b=per_b),
        out_shape=jax.ShapeDtypeStruct((m, D), F32),
        grid=(m // tm,),
        in_specs=[lo_spec, hi_spec, hspec, hspec,
                  pl.BlockSpec((tm, 4096), tok), pl.BlockSpec((tm, D), tok),
                  pl.BlockSpec((None, 1, D), lambda i: (i // per_b, 0, 2)),
                  pl.BlockSpec((1, D), lambda i: (0, 0)), pl.BlockSpec((1, D), lambda i: (0, 0)),
                  _resident((D, D)), _resident((D, D)), _resident((D, D))],
        out_specs=pl.BlockSpec((tm, D), tok),
        compiler_params=_params(("arbitrary",)),
        name="merge",
    )(olo, ohi, hlo4, hhi4, po, x2, mod3, gg, gm, wbg, wbm, wo)


FF_TILES = ((0, 1280), (1280, 2816))


def _ffn_kernel(x_ref, sh_ref, sc_ref, g2_ref, ng_ref, fg_ref, wi_ref, wo_ref, o_ref):
    x = x_ref[...]
    u = _norm_mod(x, ng_ref[...], sh_ref[...], sc_ref[...])
    acc = None
    for lo, hi in FF_TILES:
        a = _dot(u, wi_ref[:, lo:hi])
        b = _dot(u, wi_ref[:, D_FF + lo:D_FF + hi])
        hid = (a * _sigmoid(a) * b).astype(BF16)
        part = _dot(hid, wo_ref[lo:hi, :])
        acc = part if acc is None else acc + part
    x2 = x + g2_ref[...] * acc
    o_ref[...] = x2 * lax.rsqrt(jnp.mean(x2 * x2, axis=-1, keepdims=True) + EPS) * fg_ref[...]


def _ffn(x1, mod3, ng, fg, wi, wo, tm):
    m = x1.shape[0]
    per_b = SEQ // tm
    tok = lambda i: (i, 0)
    modspec = lambda c: pl.BlockSpec((None, 1, D), lambda i: (i // per_b, 0, c))
    return pl.pallas_call(
        _ffn_kernel,
        out_shape=jax.ShapeDtypeStruct((m, D), F32),
        grid=(m // tm,),
        in_specs=[pl.BlockSpec((tm, D), tok), modspec(3), modspec(4), modspec(5),
                  pl.BlockSpec((1, D), lambda i: (0, 0)), pl.BlockSpec((1, D), lambda i: (0, 0)),
                  _resident((D, 2 * D_FF)), _resident((D_FF, D))],
        out_specs=pl.BlockSpec((tm, D), tok),
        compiler_params=_params(("arbitrary",)),
        name="ffn",
    )(x1, mod3, mod3, mod3, ng, fg, wi, wo)


def _split_w_in_kernel(w_ref, wg_ref, wm_ref, wo_ref, wtl_ref, wtg_ref):
    cols = w_ref.shape[1]
    c = lambda a, b: w_ref[a:b, :].astype(BF16)
    z = lambda n: jnp.zeros((n, cols), BF16)
    wg_ref[...] = c(0, 2048)
    wm_ref[...] = c(3104, 5152)
    wo_ref[0:1024, :] = c(2048, 3072)
    wo_ref[1024:2048, :] = c(5152, 6176)
    wo_ref[2048:4096, :] = c(6192, 8240)
    wtl_ref[...] = jnp.concatenate([c(3072, 3104), z(128 - 2 * RANK)], axis=0)
    wtg_ref[...] = jnp.concatenate(
        [z(GATE_LANE), c(6176, 6192), z(128 - GATE_LANE - 16),
         z(GATE_LANE), c(6180, 6184), z(4), c(6188, 6192), z(128 - GATE_LANE - 12)], axis=0)


def _split_w_in(w_in_t):
    tc = 256
    heights = (2048, 2048, 4096, 128, 256)
    return pl.pallas_call(
        _split_w_in_kernel,
        out_shape=tuple(jax.ShapeDtypeStruct((n, D), BF16) for n in heights),
        grid=(D // tc,),
        in_specs=[pl.BlockSpec((w_in_t.shape[0], tc), lambda i: (0, i))],
        out_specs=tuple(pl.BlockSpec((n, tc), lambda i: (0, i)) for n in heights),
        compiler_params=_params(("arbitrary",)),
        name="split_w_in",
    )(w_in_t)


def kernel(x, c, ctx, c_ctx, w_ada, b_ada, norm1_g, w_in, gla_w_up, gla_b_dec, gla_norm_g,
           mlstm_conv_w, mlstm_conv_b, mlstm_b_gate, mlstm_norm_g, w_br_gla, w_br_mlstm, w_out,
           norm2_g, w_ffn_in, w_ffn_out, final_g):
    bsz = x.shape[0]
    row = lambda a: a.reshape(1, -1)

    cvec = jnp.concatenate([c, c_ctx[None, :], jnp.zeros((8 - bsz - 1, D), F32)], axis=0)
    mod3 = _ada(cvec, w_ada[0], row(b_ada[0])).reshape(8, 1, N_MOD)

    wg, wm, wo, wtl, wtg = _split_w_in(w_in[0].T)
    x2 = x.reshape(bsz * SEQ, D)
    ctx2 = ctx.reshape(bsz * CTX, D)
    g1n = row(norm1_g[0])
    tm = GRID_W * COL_BLOCK
    po, pg, tlr, u_lat = _inproj(x2, mod3, lambda i: i // (SEQ // tm), g1n, (wo, wg, wtl), (BF16, BF16, F32), tm,
                                 acts=(("silu", "sigmoid", "sigmoid", "sigmoid"), None, None), emit_u=True)
    conv_w = mlstm_conv_w[0]
    conv_b = row(mlstm_conv_b[0])
    m_lat = _inproj_m(u_lat.reshape(bsz, SEQ // GRID_W, GRID_W, D), None, None, None,
                      wm, wtg, conv_w, conv_b, tm, True)
    pg_c, tlr_c = _inproj(ctx2, mod3, lambda i: bsz, g1n, (wg, wtl), (BF16, F32), CTX)
    m_ctx = _inproj_m(ctx2, mod3, lambda i: bsz, g1n, wm, wtg, conv_w, conv_b, CTX, False)

    wup = jnp.zeros((2, 128, HEADS * DK), F32)
    wup = wup.at[0, 0:RANK].set(gla_w_up[0, 0]).at[1, RANK:2 * RANK].set(gla_w_up[0, 1]).astype(BF16)
    bdec = gla_b_dec[0].reshape(2, 1, HEADS * DK)
    s0 = jnp.zeros((bsz, 2 * HEADS, DK, DV), F32)
    (s_ctx,) = _gla_scan(pg_c.reshape(bsz, CTX, 2048), tlr_c.reshape(bsz, CTX, 128),
                         wup, bdec, s0, emit_out=False)
    o_lo, o_hi, _ = _gla_scan(pg.reshape(bsz, SEQ, 2048), tlr.reshape(bsz, SEQ, 128),
                              wup, bdec, s_ctx, emit_out=True)

    bgate = mlstm_b_gate[0].reshape(1, 16)
    zg = lambda n: jnp.zeros((1, n), F32)
    bg2 = jnp.concatenate([zg(GATE_LANE), bgate, zg(128 - GATE_LANE - 16),
                           zg(GATE_LANE), bgate[:, 4:8], zg(4), bgate[:, 12:16],
                           zg(128 - GATE_LANE - 12)], axis=1)
    c0 = jnp.zeros((bsz, 2 * HEADS, DK, DV), F32)
    v0 = jnp.zeros((bsz, 2 * HEADS, 128), F32)
    c_ctx_s, n_ctx_s, m_ctx_s = _mlstm_scan(*m_ctx, bg2, c0, v0, v0, emit_out=False)
    h_lo, h_hi, _, _, _ = _mlstm_scan(*m_lat, bg2, c_ctx_s, n_ctx_s, m_ctx_s, emit_out=True)

    cm4 = lambda a: a.reshape(bsz, GRID_W // 2, SEQ // GRID_W, D)
    x1 = _merge(o_lo, o_hi, cm4(h_lo), cm4(h_hi), po, x2, mod3,
                row(gla_norm_g[0]), row(mlstm_norm_g[0]),
                w_br_gla[0].astype(BF16), w_br_mlstm[0].astype(BF16), w_out[0].astype(BF16))
    out = _ffn(x1, mod3, row(norm2_g[0]), row(final_g),
               w_ffn_in[0].astype(BF16), w_ffn_out[0].astype(BF16), tm)
    return out.reshape(bsz, SEQ, D)
```

```python
import functools
import math

import jax
import jax.numpy as jnp
from jax import lax
from jax.experimental import pallas as pl
from jax.experimental.pallas import tpu as pltpu

D = 1024
SEQ = 4096
CTX = 256
GRID_W = 64
EPS = 1e-6
HEADS = 4
DK = 128
DV = 256
RANK = 16
TAU = 16.0
D_FF = 2816
N_MOD = 6 * D
GATE_LANE = 32
STEP = 512
GLA_CHUNK = 128
MLSTM_CHUNK = 128
COL_BLOCK = 8
MERGE_GROUPS = 2

LOG2E = math.log2(math.e)
LOG2_QSCALE = -0.5 * math.log2(DK)

F32 = jnp.float32
BF16 = jnp.bfloat16
VMEM_LIMIT = 56 * 1024 * 1024


def _dot(a, b):
    return jnp.dot(a, b, preferred_element_type=F32)


def _dot_nt(a, b):
    return lax.dot_general(a, b, (((1,), (1,)), ((), ())), preferred_element_type=F32)


def _dot_tn(a, b):
    return lax.dot_general(a, b, (((0,), (0,)), ((), ())), preferred_element_type=F32)


def _sigmoid(x):
    return 0.5 * jnp.tanh(0.5 * x) + 0.5


def _log2_sigmoid(x):
    return jnp.minimum(x, 0.0) * LOG2E - jnp.log2(1.0 + jnp.exp2(jnp.abs(x) * (-LOG2E)))


def _cumsum_rows(tri, g):
    g1 = g.astype(BF16)
    g2 = (g - g1.astype(F32)).astype(BF16)
    return _dot(tri, g1) + _dot(tri, g2)


def _tri(n, d):
    row = lax.broadcasted_iota(jnp.int32, (n, n), 0)
    col = lax.broadcasted_iota(jnp.int32, (n, n), 1)
    causal = (col <= row) if d == 0 else (col >= row)
    return causal, jnp.where(causal, 1.0, 0.0).astype(BF16)


def _resident(shape):
    n = len(shape)
    return pl.BlockSpec(shape, lambda *_: (0,) * n, pipeline_mode=pl.Buffered(1))


def _params(sem):
    return pltpu.CompilerParams(dimension_semantics=sem, vmem_limit_bytes=VMEM_LIMIT)


def _skewed(units, stages):
    for t in range(len(units) + len(stages) - 1):
        for s_idx, stage in enumerate(stages):
            if 0 <= t - s_idx < len(units):
                stage(*units[t - s_idx])


def _scan_units(n_sub):
    return [(d, j) for jj in range(n_sub) for d, j in ((0, jj), (1, n_sub - 1 - jj))]


def _ada_kernel(c_ref, w_ref, b_ref, o_ref):
    cv = c_ref[...]
    s = (cv * _sigmoid(cv)).astype(BF16)
    o_ref[...] = _dot(s, w_ref[...].astype(BF16)) + b_ref[...]


def _ada(cvec, w_ada, b_ada):
    tn = 1024
    return pl.pallas_call(
        _ada_kernel,
        out_shape=jax.ShapeDtypeStruct((8, N_MOD), F32),
        grid=(N_MOD // tn,),
        in_specs=[pl.BlockSpec((8, D), lambda j: (0, 0)),
                  pl.BlockSpec((D, tn), lambda j: (0, j)),
                  pl.BlockSpec((1, tn), lambda j: (0, j))],
        out_specs=pl.BlockSpec((8, tn), lambda j: (0, j)),
        compiler_params=_params(("arbitrary",)),
        name="adaln",
    )(cvec, w_ada, b_ada)


def _norm_mod_f32(x, g, sh, sc):
    return x * lax.rsqrt(jnp.mean(x * x, axis=-1, keepdims=True) + EPS) * (g * (1.0 + sc)) + sh


def _norm_mod(x, g, sh, sc):
    return _norm_mod_f32(x, g, sh, sc).astype(BF16)


def _act(v, kind):
    if kind == "silu":
        return v * _sigmoid(v)
    if kind == "sigmoid":
        return _sigmoid(v)
    return v


def _inproj_kernel(x_ref, sh_ref, sc_ref, g_ref, *refs, acts, emit_u):
    uf = _norm_mod_f32(x_ref[...], g_ref[...], sh_ref[...], sc_ref[...])
    if emit_u:
        refs[-1][...] = uf
        refs = refs[:-1]
    n_out = len(refs) // 2
    u = uf.astype(BF16)
    for w_ref, o_ref, act in zip(refs[:n_out], refs[n_out:], acts):
        n = w_ref.shape[0]
        for jc, j in enumerate(range(0, n, 1024)):
            cs = slice(j, min(j + 1024, n))
            o_ref[:, cs] = _act(_dot_nt(u, w_ref[cs, :]), act[jc] if act else None).astype(o_ref.dtype)


def _inproj(x2, mod3, mod_row, norm_g, weights, out_dtypes, tm, acts=None, emit_u=False):
    m = x2.shape[0]
    acts = acts or (None,) * len(weights)
    widths = [w.shape[0] for w in weights] + ([D] if emit_u else [])
    dtypes = list(out_dtypes) + ([F32] if emit_u else [])
    return pl.pallas_call(
        functools.partial(_inproj_kernel, acts=acts, emit_u=emit_u),
        out_shape=tuple(jax.ShapeDtypeStruct((m, n), dt) for n, dt in zip(widths, dtypes)),
        grid=(m // tm,),
        in_specs=[pl.BlockSpec((tm, D), lambda i: (i, 0)),
                  pl.BlockSpec((None, 1, D), lambda i: (mod_row(i), 0, 0)),
                  pl.BlockSpec((None, 1, D), lambda i: (mod_row(i), 0, 1)),
                  pl.BlockSpec((1, D), lambda i: (0, 0))] + [_resident(w.shape) for w in weights],
        out_specs=tuple(pl.BlockSpec((tm, n), lambda i: (i, 0)) for n in widths),
        compiler_params=_params(("arbitrary",)),
        name="inproj",
    )(x2, mod3, mod3, norm_g, *weights)


def _inproj_m_kernel(*refs, colmajor, nblk):
    if colmajor:
        u_ref, hp_ref, hn_ref, wm_ref, wtg_ref, cw_ref, cb_ref = refs[:7]
        u = jnp.concatenate([u_ref[:, cl, :] for cl in range(COL_BLOCK)] + [hp_ref[7], hn_ref[0]],
                            axis=0).astype(BF16)
    else:
        x_ref, sh_ref, sc_ref, g_ref, wm_ref, wtg_ref, cw_ref, cb_ref = refs[:8]
        u = _norm_mod(x_ref[...], g_ref[...], sh_ref[...], sc_ref[...])
    k_ref, qt_ref, kt_ref, v_ref, tg_ref = refs[-5:]
    n = k_ref.shape[0]
    um = u[0:n]
    pres = [_dot_nt(u, wm_ref[c * 512:(c + 1) * 512, :]) for c in range(2)]
    v_ref[...] = _dot_nt(um, wm_ref[1024:2048, :]).astype(BF16)
    tg_ref[...] = _dot_nt(um, wtg_ref[...])
    row8 = lax.broadcasted_iota(jnp.int32, (8, 512), 0)
    j = pl.program_id(0) % nblk
    for c, pre in enumerate(pres):
        cs = slice(c * 512, (c + 1) * 512)
        a = pre[0:n]
        if colmajor:
            prev_row = jnp.where(j > 0, pre[n + 7:n + 8], 0.0)
            next_row = jnp.where(j < nblk - 1, pre[n + 8:n + 9], 0.0)
        else:
            prev_row = next_row = jnp.zeros((1, 512), F32)
        ap = pltpu.roll(a, 1, axis=0)
        ap = jnp.concatenate([jnp.where(row8 == 0, prev_row, ap[0:8]), ap[8:]], axis=0)
        an = pltpu.roll(a, n - 1, axis=0)
        an = jnp.concatenate([an[0:n - 8], jnp.where(row8 == 7, next_row, an[n - 8:])], axis=0)
        conv = ap * cw_ref[0:1, cs] + a * cw_ref[1:2, cs] + an * cw_ref[2:3, cs] + cb_ref[:, cs]
        y = conv * _sigmoid(conv)
        if c == 0:
            qt_ref[...] = y.T.astype(BF16)
        else:
            y = y * (DK ** -0.5)
            k_ref[...] = y.astype(BF16)
            kt_ref[...] = y.T.astype(BF16)


def _inproj_m(xv, mod3, mod_row, norm_g, wm, wtg, conv_w, conv_b, tm, colmajor):
    full = lambda shape: pl.BlockSpec(shape, lambda i: (0,) * len(shape))
    mod_specs = [pl.BlockSpec((None, 1, D), lambda i: (mod_row(i), 0, 0)),
                 pl.BlockSpec((None, 1, D), lambda i: (mod_row(i), 0, 1)), full((1, D))]
    mod_args = (mod3, mod3, norm_g)
    if colmajor:
        mod_specs, mod_args = [], ()
        bsz = xv.shape[0]
        tn = SEQ
        nblk = GRID_W // COL_BLOCK
        blk = (None, GRID_W, COL_BLOCK, D)
        halo = (None, 8, COL_BLOCK, D)
        x_specs = [pl.BlockSpec(blk, lambda i: (i // nblk, 0, i % nblk, 0)),
                   pl.BlockSpec(halo, lambda i: (i // nblk, GRID_W // 8 - 1, jnp.maximum(i % nblk - 1, 0), 0)),
                   pl.BlockSpec(halo, lambda i: (i // nblk, 0, jnp.minimum(i % nblk + 1, nblk - 1), 0))]
        xs = (xv, xv, xv)
    else:
        tn = tm
        bsz = xv.shape[0] // tn
        nblk = 1
        x_specs = [pl.BlockSpec((tm, D), lambda i: (i, 0))]
        xs = (xv,)
    tok = lambda w: pl.BlockSpec((None, tm, w), lambda i: (i // nblk, i % nblk, 0))
    tr = pl.BlockSpec((None, 512, tm), lambda i: (i // nblk, 0, i % nblk))
    sds = jax.ShapeDtypeStruct
    return pl.pallas_call(
        functools.partial(_inproj_m_kernel, colmajor=colmajor, nblk=nblk),
        out_shape=(sds((bsz, tn, 512), BF16), sds((bsz, 512, tn), BF16), sds((bsz, 512, tn), BF16),
                   sds((bsz, tn, 1024), BF16), sds((bsz, tn, 256), F32)),
        grid=(bsz * nblk,),
        in_specs=x_specs + mod_specs + [_resident(wm.shape), _resident(wtg.shape),
                                        full((3, 1024)), full((1, 1024))],
        out_specs=(tok(512), tr, tr, tok(1024), tok(256)),
        compiler_params=_params(("arbitrary",)),
        name="inproj_m_cm" if colmajor else "inproj_m",
    )(*xs, *mod_args, wm, wtg, conv_w, conv_b)


def _sum_directions(acc_ref, out_ref, blk, rs, cs, val):
    tot = acc_ref[blk, rs, cs] + val
    acc_ref[blk, rs, cs] = tot
    out_ref[rs, cs] = tot.astype(out_ref.dtype)


def _gla_kernel(pf_ref, pb_ref, tf_ref, tb_ref, wup_ref, bdec_ref, s0_ref, *rest, emit_out, ns):
    i = pl.program_id(1)
    if emit_out:
        olo_ref, ohi_ref, st_ref, acc_ref = rest
    else:
        (st_ref,) = rest
        olo_ref = ohi_ref = acc_ref = None

    @pl.when(i == 0)
    def _():
        st_ref[...] = s0_ref[...]
        if emit_out:
            acc_ref[...] = jnp.zeros(acc_ref.shape, F32)

    lc = GLA_CHUNK
    dirs = ((pf_ref, tf_ref, ohi_ref), (pb_ref, tb_ref, olo_ref))
    blks = (i, ns - 1 - i)
    units = _scan_units(pf_ref.shape[0] // lc)
    masks =[_tri(lc, d) for d in range(2)]
    gs = []
    for d, (p_ref, t_ref, o_ref) in enumerate(dirs):
        z = _dot(t_ref[...].astype(BF16), wup_ref[d]) + bdec_ref[d]
        gs.append(_log2_sigmoid(z) * (1.0 / TAU))
    bs, ops, sc, us, dcols = {}, {}, {}, {}, {}
    st = {(d, h): st_ref[d * HEADS + h] for d in range(2) for h in range(HEADS)}

    def stage2(d, j):
        bs[d, j] = _cumsum_rows(masks[d][1], gs[d][j * lc:(j + 1) * lc])

    def stage3(d, j):
        p_ref = dirs[d][0]
        rs = slice(j * lc, (j + 1) * lc)
        b = bs[d, j]
        b_last = b[lc - 1:lc, :] if d == 0 else b[0:1, :]
        b_mid = b[lc // 2 - 1:lc // 2, :] if d == 0 else b[lc // 2:lc // 2 + 1, :]
        q = p_ref[rs, 0:512]
        k = p_ref[rs, 512:1024]
        qd = q * jnp.exp2(b - b_mid).astype(BF16)
        kd = k * jnp.exp2((b_mid + LOG2_QSCALE) - b).astype(BF16)
        qi = qd * jnp.exp2(b_mid).astype(BF16)
        kl = kd * jnp.exp2(b_last - b_mid).astype(BF16)
        dec = jnp.exp2(b_last)
        ops[d, j] = (qi, kl, dec, qd, kd)

    def stage4(d, j):
        p_ref = dirs[d][0]
        rs = slice(j * lc, (j + 1) * lc)
        qi, kl, dec, qd, kd = ops[d, j]
        for h in range(HEADS):
            ks = slice(h * DK, (h + 1) * DK)
            v = p_ref[rs, 1024 + h * DV:1024 + (h + 1) * DV]
            if emit_out:
                sc[d, j, h] = jnp.where(masks[d][0], _dot_nt(qd[:, ks], kd[:, ks]), 0.0).astype(BF16)
            us[d, j, h] = _dot_tn(kl[:, ks], v)
            dcols[d, j, h] = jnp.broadcast_to(dec[:, ks], (8, DK)).T[:, 0:1]
    def stage5(d, j):
        p_ref, _, o_ref = dirs[d]
        rs = slice(j * lc, (j + 1) * lc)
        for h in range(HEADS):
            ks = slice(h * DK, (h + 1) * DK)
            if emit_out:
                v = p_ref[rs, 1024 + h * DV:1024 + (h + 1) * DV]
                o = _dot(jnp.concatenate([sc[d, j, h], ops[d, j][0][:, ks]], axis=1),
                         jnp.concatenate([v, st[d, h].astype(BF16)], axis=0))
                _sum_directions(acc_ref, o_ref, blks[d], rs, slice(h * DV, (h + 1) * DV), o)
            st[d, h] = st[d, h] * dcols[d, j, h] + us[d, j, h]

    _skewed(units, (stage2, stage3, stage4, stage5))
    for (d, h), val in st.items():
        st_ref[d * HEADS + h] = val


def _half_specs(ns, step, width):
    half = ns // 2
    lo = pl.BlockSpec((None, step, width), lambda b, i: (b, jnp.minimum(ns - 1 - i, half - 1), 0))
    hi = pl.BlockSpec((None, step, width), lambda b, i: (b, jnp.maximum(i - half, 0), 0))
    return lo, hi


def _gla_scan(pg, tlr, wup, bdec, s0, emit_out):
    bn, tn, _ = pg.shape
    step = min(STEP, tn)
    ns = tn // step
    fwd = lambda b, i: (b, i, 0)
    bwd = lambda b, i: (b, ns - 1 - i, 0)
    st_shape = jax.ShapeDtypeStruct((bn, 2 * HEADS, DK, DV), F32)
    st_spec = pl.BlockSpec((None, 2 * HEADS, DK, DV), lambda b, i: (b, 0, 0, 0))
    o_shape = jax.ShapeDtypeStruct((bn, tn // 2, HEADS * DV), BF16)
    out_shape = (o_shape, o_shape, st_shape) if emit_out else (st_shape,)
    out_specs = (*_half_specs(ns, step, HEADS * DV), st_spec) if emit_out else (st_spec,)
    return pl.pallas_call(
        functools.partial(_gla_kernel, emit_out=emit_out, ns=ns),
        out_shape=out_shape,
        grid=(bn, ns),
        scratch_shapes=[pltpu.VMEM((ns, step, HEADS * DV), F32)] if emit_out else [],
        in_specs=[pl.BlockSpec((None, step, 2048), fwd),
                  pl.BlockSpec((None, step, 2048), bwd),
                  pl.BlockSpec((None, step, 128), fwd),
                  pl.BlockSpec((None, step, 128), bwd),
                  pl.BlockSpec((2, 128, 512), lambda b, i: (0, 0, 0)),
                  pl.BlockSpec((2, 1, 512), lambda b, i: (0, 0, 0)),
                  st_spec],
        out_specs=out_specs,
        compiler_params=_params(("arbitrary", "arbitrary")),
        name="gla_scan_lat" if emit_out else "gla_scan_ctx",
    )(pg, pg, tlr, tlr, wup, bdec, s0)


def _mlstm_kernel(kf_ref, kb_ref, qtf_ref, qtb_ref, ktf_ref, ktb_ref, vf_ref, vb_ref, tf_ref, tb_ref,
                  bg_ref, c0_ref, n0_ref, m0_ref, *rest, emit_out, ns):
    i = pl.program_id(1)
    if emit_out:
        hlo_ref, hhi_ref, c_ref, n_ref, m_ref, acc_ref = rest
    else:
        c_ref, n_ref, m_ref = rest
        hlo_ref = hhi_ref = acc_ref = None

    @pl.when(i == 0)
    def _():
        c_ref[...] = c0_ref[...]
        n_ref[...] = n0_ref[...]
        m_ref[...] = m0_ref[...]
        if emit_out:
            acc_ref[...] = jnp.zeros(acc_ref.shape, F32)

    lc = MLSTM_CHUNK
    dirs = ((kf_ref, qtf_ref, ktf_ref, vf_ref, tf_ref, hhi_ref), (kb_ref, qtb_ref, ktb_ref, vb_ref, tb_ref, hlo_ref))
    blks = (i, ns - 1 - i)
    units = _scan_units(kf_ref.shape[0] // lc)
    tris =[_tri(lc, d) for d in range(2)]
    lane_of = lambda d, h: GATE_LANE + 8 * d + h
    hs = lambda h: slice(h * DK, (h + 1) * DK)
    vs = lambda h: slice(h * DV, (h + 1) * DV)

    mrow = [m_ref[d:d + 1, :] for d in range(2)]
    tiles = {}
    for d, j in units:
        t_ref = dirs[d][4]
        rs = slice(j * lc, (j + 1) * lc)
        ga = (t_ref[rs, 0:128] + bg_ref[:, 0:128]) * LOG2E
        gb = t_ref[rs, 128:256] + bg_ref[:, 128:256]
        bc = _cumsum_rows(tris[d][1], _log2_sigmoid(gb))
        b_last = bc[lc - 1:lc, :] if d == 0 else bc[0:1, :]
        log_key = b_last - bc + ga
        m_new = jnp.maximum(b_last + mrow[d], jnp.max(log_key, axis=0, keepdims=True))
        tiles[d, j] = dict(rmat=ga - bc, bct=bc.T, m_in=mrow[d], wkt=jnp.exp2(log_key - m_new).T,
                           decay=jnp.exp2(b_last + mrow[d] - m_new))
        mrow[d] = m_new
    for d in range(2):
        m_ref[d:d + 1, :] = mrow[d]

    us, ncols = {}, {}
    for d, j in units:
        k_ref, _, kt_ref, v_ref = dirs[d][:4]
        rs = slice(j * lc, (j + 1) * lc)
        for h in range(HEADS):
            lane = lane_of(d, h)
            wk = tiles[d, j]["wkt"][lane:lane + 1, :]
            kwt = kt_ref[hs(h), rs] * wk.astype(BF16)
            us[d, j, h] = _dot(kwt, v_ref[rs, vs(h)])
            ncols[d, j, h] = _dot(jnp.broadcast_to(wk, (16, lc)).astype(BF16), k_ref[rs, hs(h)])[0:1]

    n_in = {}
    for d in range(2):
        for h in range(HEADS):
            idx = d * HEADS + h
            lane = lane_of(d, h)
            nvec = n_ref[idx:idx + 1, :]
            for dd, j in units:
                if dd == d:
                    n_in[d, j, h] = nvec
                    nvec = tiles[d, j]["decay"][:, lane:lane + 1] * nvec + ncols[d, j, h]
            n_ref[idx:idx + 1, :] = nvec

    lhs = {}
    if emit_out:
        for d, j in units:
            k_ref, qt_ref = dirs[d][:2]
            rs = slice(j * lc, (j + 1) * lc)
            t = tiles[d, j]
            causal_t = tris[1 - d][0]
            for h in range(HEADS):
                lane = lane_of(d, h)
                qt = qt_ref[hs(h), rs]
                kq = _dot(jnp.concatenate(
                    [k_ref[rs, hs(h)], jnp.broadcast_to(n_in[d, j, h], (16, DK)).astype(BF16)], axis=0), qt)
                rm = jnp.where(causal_t, t["rmat"][:, lane:lane + 1], -jnp.inf)
                mval = t["m_in"][:, lane:lane + 1]
                mx = jnp.maximum(mval, jnp.max(rm, axis=0, keepdims=True))
                wt = jnp.exp2(rm - mx) * kq[0:lc]
                w_inter = jnp.exp2(mval - mx)
                den = jnp.sum(wt, axis=0, keepdims=True) + w_inter * kq[lc:lc + 1]
                inv = 1.0 / jnp.maximum(jnp.abs(den), jnp.exp2(-(t["bct"][lane:lane + 1, :] + mx)))
                lhs[d, j, h] = jnp.concatenate(
                    [(wt * inv).astype(BF16), qt * (w_inter * inv).astype(BF16)], axis=0)

    for d in range(2):
        v_ref, o_ref = dirs[d][3], dirs[d][5]
        for h in range(HEADS):
            idx = d * HEADS + h
            lane = lane_of(d, h)
            cmat = c_ref[idx]
            for dd, j in units:
                if dd != d:
                    continue
                rs = slice(j * lc, (j + 1) * lc)
                if emit_out:
                    o = _dot_tn(lhs[d, j, h], jnp.concatenate([v_ref[rs, vs(h)], cmat.astype(BF16)], axis=0))
                    _sum_directions(acc_ref, o_ref, blks[d], rs, vs(h), o)
                cmat = tiles[d, j]["decay"][:, lane:lane + 1] * cmat + us[d, j, h]
            c_ref[idx] = cmat


def _mlstm_scan(k, qt, kt, v, tg, bgate, c0, n0, m0, emit_out):
    bn, tn, _ = k.shape
    step = min(STEP, tn)
    ns = tn // step
    fwd = lambda b, i: (b, i, 0)
    bwd = lambda b, i: (b, ns - 1 - i, 0)
    fwd_t = lambda b, i: (b, 0, i)
    bwd_t = lambda b, i: (b, 0, ns - 1 - i)
    c_shape = jax.ShapeDtypeStruct((bn, 2 * HEADS, DK, DV), F32)
    v_shape = jax.ShapeDtypeStruct((bn, 2 * HEADS, 128), F32)
    c_spec = pl.BlockSpec((None, 2 * HEADS, DK, DV), lambda b, i: (b, 0, 0, 0))
    v_spec = pl.BlockSpec((None, 2 * HEADS, 128), lambda b, i: (b, 0, 0))
    if emit_out:
        o_shape = jax.ShapeDtypeStruct((bn, tn // 2, HEADS * DV), F32)
        out_shape = (o_shape, o_shape, c_shape, v_shape, v_shape)
        out_specs = (*_half_specs(ns, step, HEADS * DV), c_spec, v_spec, v_spec)
    else:
        out_shape = (c_shape, v_shape, v_shape)
        out_specs = (c_spec, v_spec, v_spec)
    both = lambda shape, f, g: [pl.BlockSpec(shape, f), pl.BlockSpec(shape, g)]
    return pl.pallas_call(
        functools.partial(_mlstm_kernel, emit_out=emit_out, ns=ns),
        out_shape=out_shape,
        grid=(bn, ns),
        scratch_shapes=[pltpu.VMEM((ns, step, HEADS * DV), F32)] if emit_out else [],
        in_specs=(both((None, step, 512), fwd, bwd) + both((None, 512, step), fwd_t, bwd_t)
                  + both((None, 512, step), fwd_t, bwd_t) + both((None, step, 1024), fwd, bwd)
                  + both((None, step, 256), fwd, bwd)
                  + [pl.BlockSpec((1, 256), lambda b, i: (0, 0)), c_spec, v_spec, v_spec]),
        out_specs=out_specs,
        compiler_params=_params(("arbitrary", "arbitrary")),
        name="mlstm_scan_lat" if emit_out else "mlstm_scan_ctx",
    )(k, k, qt, qt, kt, kt, v, v, tg, tg, bgate, c0, n0, m0)


def _head_norm(o, g):
    parts = []
    for h in range(HEADS):
        oh = o[:, h * DV:(h + 1) * DV]
        parts.append(oh * lax.rsqrt(jnp.mean(oh * oh, axis=-1, keepdims=True) + EPS))
    return jnp.concatenate(parts, axis=-1) * g


def _merge_kernel(olo_ref, ohi_ref, hlo_ref, hhi_ref, po_ref, x_ref, g1_ref, gg_ref, gm_ref,
                  wbg_ref, wbm_ref, wo_ref, o_ref, *, per_b):
    groups = [(g * COL_BLOCK // MERGE_GROUPS, (g + 1) * COL_BLOCK // MERGE_GROUPS) for g in range(MERGE_GROUPS)]
    rows = [slice(a * GRID_W, b * GRID_W) for a, b in groups]
    lower = pl.program_id(0) % per_b < per_b // 2
    ys = []
    for (a, b), rs in zip(groups, rows):
        hm = jnp.concatenate([r[:, rl, :] for rl in range(a, b) for r in (hlo_ref, hhi_ref)], axis=0)
        o = jnp.where(lower, olo_ref[rs, :], ohi_ref[rs, :]).astype(F32)
        y_gla = _head_norm(o, gg_ref[...]) * po_ref[rs, 0:1024].astype(F32)
        y_m = _head_norm(hm, gm_ref[...]) * po_ref[rs, 1024:2048].astype(F32)
        ys.append((y_gla.astype(BF16), y_m.astype(BF16)))
    ds = [(_dot(y_gla, wbg_ref[...]), _dot(y_m, wbm_ref[...])) for y_gla, y_m in ys]
    ys = [(po_ref[rs, 2048:3072].astype(F32) * d_g + po_ref[rs, 3072:4096].astype(F32) * d_m).astype(BF16)
          for rs, (d_g, d_m) in zip(rows, ds)]
    mixes = [_dot(y, wo_ref[...]) for y in ys]
    for rs, mix in zip(rows, mixes):
        o_ref[rs, :] = x_ref[rs, :] + g1_ref[...] * mix


def _merge(olo, ohi, hlo4, hhi4, po, x2, mod3, gg, gm, wbg, wbm, wo):
    m = x2.shape[0]
    tm = GRID_W * COL_BLOCK
    per_b = SEQ // tm
    half = per_b // 2
    tok = lambda i: (i, 0)
    hspec = pl.BlockSpec((None, GRID_W // 2, COL_BLOCK, D), lambda i: (i // per_b, 0, i % per_b, 0))
    lo_spec = pl.BlockSpec((None, tm, D), lambda i: (i // per_b, jnp.minimum(i % per_b, half - 1), 0))
    hi_spec = pl.BlockSpec((None, tm, D), lambda i: (i // per_b, jnp.maximum(i % per_b - half, 0), 0))
    return pl.pallas_call(
        functools.partial(_merge_kernel, per_b=per_b),
        out_shape=jax.ShapeDtypeStruct((m, D), F32),
        grid=(m // tm,),
        in_specs=[lo_spec, hi_spec, hspec, hspec,
                  pl.BlockSpec((tm, 4096), tok), pl.BlockSpec((tm, D), tok),
                  pl.BlockSpec((None, 1, D), lambda i: (i // per_b, 0, 2)),
                  pl.BlockSpec((1, D), lambda i: (0, 0)), pl.BlockSpec((1, D), lambda i: (0, 0)),
                  _resident((D, D)), _resident((D, D)), _resident((D, D))],
        out_specs=pl.BlockSpec((tm, D), tok),
        compiler_params=_params(("arbitrary",)),
        name="merge",
    )(olo, ohi, hlo4, hhi4, po, x2, mod3, gg, gm, wbg, wbm, wo)


FF_TILES = ((0, 1280), (1280, 2816))


def _ffn_kernel(x_ref, sh_ref, sc_ref, g2_ref, ng_ref, fg_ref, wi_ref, wo_ref, o_ref):
    x = x_ref[...]
    u = _norm_mod(x, ng_ref[...], sh_ref[...], sc_ref[...])
    acc = None
    for lo, hi in FF_TILES:
        a = _dot(u, wi_ref[:, lo:hi])
        b = _dot(u, wi_ref[:, D_FF + lo:D_FF + hi])
        hid = (a * _sigmoid(a) * b).astype(BF16)
        part = _dot(hid, wo_ref[lo:hi, :])
        acc = part if acc is None else acc + part
    x2 = x + g2_ref[...] * acc
    o_ref[...] = x2 * lax.rsqrt(jnp.mean(x2 * x2, axis=-1, keepdims=True) + EPS) * fg_ref[...]


def _ffn(x1, mod3, ng, fg, wi, wo, tm):
    m = x1.shape[0]
    per_b = SEQ // tm
    tok = lambda i: (i, 0)
    modspec = lambda c: pl.BlockSpec((None, 1, D), lambda i: (i // per_b, 0, c))
    return pl.pallas_call(
        _ffn_kernel,
        out_shape=jax.ShapeDtypeStruct((m, D), F32),
        grid=(m // tm,),
        in_specs=[pl.BlockSpec((tm, D), tok), modspec(3), modspec(4), modspec(5),
                  pl.BlockSpec((1, D), lambda i: (0, 0)), pl.BlockSpec((1, D), lambda i: (0, 0)),
                  _resident((D, 2 * D_FF)), _resident((D_FF, D))],
        out_specs=pl.BlockSpec((tm, D), tok),
        compiler_params=_params(("arbitrary",)),
        name="ffn",
    )(x1, mod3, mod3, mod3, ng, fg, wi, wo)


def _split_w_in_kernel(w_ref, wg_ref, wm_ref, wo_ref, wtl_ref, wtg_ref):
    cols = w_ref.shape[1]
    c = lambda a, b: w_ref[a:b, :].astype(BF16)
    z = lambda n: jnp.zeros((n, cols), BF16)
    wg_ref[...] = c(0, 2048)
    wm_ref[...] = c(3104, 5152)
    wo_ref[0:1024, :] = c(2048, 3072)
    wo_ref[1024:2048, :] = c(5152, 6176)
    wo_ref[2048:4096, :] = c(6192, 8240)
    wtl_ref[...] = jnp.concatenate([c(3072, 3104), z(128 - 2 * RANK)], axis=0)
    wtg_ref[...] = jnp.concatenate(
        [z(GATE_LANE), c(6176, 6192), z(128 - GATE_LANE - 16),
         z(GATE_LANE), c(6180, 6184), z(4), c(6188, 6192), z(128 - GATE_LANE - 12)], axis=0)


def _split_w_in(w_in_t):
    tc = 256
    heights = (2048, 2048, 4096, 128, 256)
    return pl.pallas_call(
        _split_w_in_kernel,
        out_shape=tuple(jax.ShapeDtypeStruct((n, D), BF16) for n in heights),
        grid=(D // tc,),
        in_specs=[pl.BlockSpec((w_in_t.shape[0], tc), lambda i: (0, i))],
        out_specs=tuple(pl.BlockSpec((n, tc), lambda i: (0, i)) for n in heights),
        compiler_params=_params(("arbitrary",)),
        name="split_w_in",
    )(w_in_t)


def kernel(x, c, ctx, c_ctx, w_ada, b_ada, norm1_g, w_in, gla_w_up, gla_b_dec, gla_norm_g,
           mlstm_conv_w, mlstm_conv_b, mlstm_b_gate, mlstm_norm_g, w_br_gla, w_br_mlstm, w_out,
           norm2_g, w_ffn_in, w_ffn_out, final_g):
    bsz = x.shape[0]
    row = lambda a: a.reshape(1, -1)

    cvec = jnp.concatenate([c, c_ctx[None, :], jnp.zeros((8 - bsz - 1, D), F32)], axis=0)
    mod3 = _ada(cvec, w_ada[0], row(b_ada[0])).reshape(8, 1, N_MOD)

    wg, wm, wo, wtl, wtg = _split_w_in(w_in[0].T)
    x2 = x.reshape(bsz * SEQ, D)
    ctx2 = ctx.reshape(bsz * CTX, D)
    g1n = row(norm1_g[0])
    tm = GRID_W * COL_BLOCK
    po, pg, tlr, u_lat = _inproj(x2, mod3, lambda i: i // (SEQ // tm), g1n, (wo, wg, wtl), (BF16, BF16, F32), tm,
                                 acts=(("silu", "sigmoid", "sigmoid", "sigmoid"), None, None), emit_u=True)
    conv_w = mlstm_conv_w[0]
    conv_b = row(mlstm_conv_b[0])
    m_lat = _inproj_m(u_lat.reshape(bsz, SEQ // GRID_W, GRID_W, D), None, None, None,
                      wm, wtg, conv_w, conv_b, tm, True)
    pg_c, tlr_c = _inproj(ctx2, mod3, lambda i: bsz, g1n, (wg, wtl), (BF16, F32), CTX)
    m_ctx = _inproj_m(ctx2, mod3, lambda i: bsz, g1n, wm, wtg, conv_w, conv_b, CTX, False)

    wup = jnp.zeros((2, 128, HEADS * DK), F32)
    wup = wup.at[0, 0:RANK].set(gla_w_up[0, 0]).at[1, RANK:2 * RANK].set(gla_w_up[0, 1]).astype(BF16)
    bdec = gla_b_dec[0].reshape(2, 1, HEADS * DK)
    s0 = jnp.zeros((bsz, 2 * HEADS, DK, DV), F32)
    (s_ctx,) = _gla_scan(pg_c.reshape(bsz, CTX, 2048), tlr_c.reshape(bsz, CTX, 128),
                         wup, bdec, s0, emit_out=False)
    o_lo, o_hi, _ = _gla_scan(pg.reshape(bsz, SEQ, 2048), tlr.reshape(bsz, SEQ, 128),
                              wup, bdec, s_ctx, emit_out=True)

    bgate = mlstm_b_gate[0].reshape(1, 16)
    zg = lambda n: jnp.zeros((1, n), F32)
    bg2 = jnp.concatenate([zg(GATE_LANE), bgate, zg(128 - GATE_LANE - 16),
                           zg(GATE_LANE), bgate[:, 4:8], zg(4), bgate[:, 12:16],
                           zg(128 - GATE_LANE - 12)], axis=1)
    c0 = jnp.zeros((bsz, 2 * HEADS, DK, DV), F32)
    v0 = jnp.zeros((bsz, 2 * HEADS, 128), F32)
    c_ctx_s, n_ctx_s, m_ctx_s = _mlstm_scan(*m_ctx, bg2, c0, v0, v0, emit_out=False)
    h_lo, h_hi, _, _, _ = _mlstm_scan(*m_lat, bg2, c_ctx_s, n_ctx_s, m_ctx_s, emit_out=True)

    cm4 = lambda a: a.reshape(bsz, GRID_W // 2, SEQ // GRID_W, D)
    x1 = _merge(o_lo, o_hi, cm4(h_lo), cm4(h_hi), po, x2, mod3,
                row(gla_norm_g[0]), row(mlstm_norm_g[0]),
                w_br_gla[0].astype(BF16), w_br_mlstm[0].astype(BF16), w_out[0].astype(BF16))
    out = _ffn(x1, mod3, row(norm2_g[0]), row(final_g),
               w_ffn_in[0].astype(BF16), w_ffn_out[0].astype(BF16), tm)
    return out.reshape(bsz, SEQ, D)
```

```python
import functools
import math

import jax
import jax.numpy as jnp
from jax import lax
from jax.experimental import pallas as pl
from jax.experimental.pallas import tpu as pltpu

D = 1024
SEQ = 4096
CTX = 256
GRID_W = 64
EPS = 1e-6
HEADS = 4
DK = 128
DV = 256
RANK = 16
TAU = 16.0
D_FF = 2816
N_MOD = 6 * D
GATE_LANE = 32
STEP = 512
GLA_CHUNK = 128
MLSTM_CHUNK = 128
COL_BLOCK = 8
MERGE_GROUPS = 2

LOG2E = math.log2(math.e)
LOG2_QSCALE = -0.5 * math.log2(DK)

F32 = jnp.float32
BF16 = jnp.bfloat16
VMEM_LIMIT = 56 * 1024 * 1024


def _dot(a, b):
    return jnp.dot(a, b, preferred_element_type=F32)


def _dot_nt(a, b):
    return lax.dot_general(a, b, (((1,), (1,)), ((), ())), preferred_element_type=F32)


def _dot_tn(a, b):
    return lax.dot_general(a, b, (((0,), (0,)), ((), ())), preferred_element_type=F32)


def _sigmoid(x):
    return 0.5 * jnp.tanh(0.5 * x) + 0.5


def _log2_sigmoid(x):
    return jnp.minimum(x, 0.0) * LOG2E - jnp.log2(1.0 + jnp.exp2(jnp.abs(x) * (-LOG2E)))


def _cumsum_rows(tri, g):
    g1 = g.astype(BF16)
    g2 = (g - g1.astype(F32)).astype(BF16)
    return _dot(tri, g1) + _dot(tri, g2)


def _tri(n, d):
    row = lax.broadcasted_iota(jnp.int32, (n, n), 0)
    col = lax.broadcasted_iota(jnp.int32, (n, n), 1)
    causal = (col <= row) if d == 0 else (col >= row)
    return causal, jnp.where(causal, 1.0, 0.0).astype(BF16)


def _resident(shape):
    n = len(shape)
    return pl.BlockSpec(shape, lambda *_: (0,) * n, pipeline_mode=pl.Buffered(1))


def _params(sem):
    return pltpu.CompilerParams(dimension_semantics=sem, vmem_limit_bytes=VMEM_LIMIT)


def _skewed(units, stages):
    for t in range(len(units) + len(stages) - 1):
        for s_idx, stage in enumerate(stages):
            if 0 <= t - s_idx < len(units):
                stage(*units[t - s_idx])


def _scan_units(n_sub):
    return [(d, j) for jj in range(n_sub) for d, j in ((0, jj), (1, n_sub - 1 - jj))]


def _ada_kernel(c_ref, w_ref, b_ref, o_ref):
    cv = c_ref[...]
    s = (cv * _sigmoid(cv)).astype(BF16)
    o_ref[...] = _dot(s, w_ref[...].astype(BF16)) + b_ref[...]


def _ada(cvec, w_ada, b_ada):
    tn = 1024
    return pl.pallas_call(
        _ada_kernel,
        out_shape=jax.ShapeDtypeStruct((8, N_MOD), F32),
        grid=(N_MOD // tn,),
        in_specs=[pl.BlockSpec((8, D), lambda j: (0, 0)),
                  pl.BlockSpec((D, tn), lambda j: (0, j)),
                  pl.BlockSpec((1, tn), lambda j: (0, j))],
        out_specs=pl.BlockSpec((8, tn), lambda j: (0, j)),
        compiler_params=_params(("arbitrary",)),
        name="adaln",
    )(cvec, w_ada, b_ada)


def _norm_mod_f32(x, g, sh, sc):
    return x * lax.rsqrt(jnp.mean(x * x, axis=-1, keepdims=True) + EPS) * (g * (1.0 + sc)) + sh


def _norm_mod(x, g, sh, sc):
    return _norm_mod_f32(x, g, sh, sc).astype(BF16)


def _act(v, kind):
    if kind == "silu":
        return v * _sigmoid(v)
    if kind == "sigmoid":
        return _sigmoid(v)
    return v


def _inproj_kernel(x_ref, sh_ref, sc_ref, g_ref, *refs, acts, emit_u):
    uf = _norm_mod_f32(x_ref[...], g_ref[...], sh_ref[...], sc_ref[...])
    if emit_u:
        refs[-1][...] = uf
        refs = refs[:-1]
    n_out = len(refs) // 2
    u = uf.astype(BF16)
    for w_ref, o_ref, act in zip(refs[:n_out], refs[n_out:], acts):
        n = w_ref.shape[0]
        for jc, j in enumerate(range(0, n, 1024)):
            cs = slice(j, min(j + 1024, n))
            o_ref[:, cs] = _act(_dot_nt(u, w_ref[cs, :]), act[jc] if act else None).astype(o_ref.dtype)


def _inproj(x2, mod3, mod_row, norm_g, weights, out_dtypes, tm, acts=None, emit_u=False):
    m = x2.shape[0]
    acts = acts or (None,) * len(weights)
    widths = [w.shape[0] for w in weights] + ([D] if emit_u else [])
    dtypes = list(out_dtypes) + ([F32] if emit_u else [])
    return pl.pallas_call(
        functools.partial(_inproj_kernel, acts=acts, emit_u=emit_u),
        out_shape=tuple(jax.ShapeDtypeStruct((m, n), dt) for n, dt in zip(widths, dtypes)),
        grid=(m // tm,),
        in_specs=[pl.BlockSpec((tm, D), lambda i: (i, 0)),
                  pl.BlockSpec((None, 1, D), lambda i: (mod_row(i), 0, 0)),
                  pl.BlockSpec((None, 1, D), lambda i: (mod_row(i), 0, 1)),
                  pl.BlockSpec((1, D), lambda i: (0, 0))] + [_resident(w.shape) for w in weights],
        out_specs=tuple(pl.BlockSpec((tm, n), lambda i: (i, 0)) for n in widths),
        compiler_params=_params(("arbitrary",)),
        name="inproj",
    )(x2, mod3, mod3, norm_g, *weights)


def _inproj_m_kernel(*refs, colmajor, nblk):
    if colmajor:
        u_ref, hp_ref, hn_ref, wm_ref, wtg_ref, cw_ref, cb_ref = refs[:7]
        u = jnp.concatenate([u_ref[:, cl, :] for cl in range(COL_BLOCK)] + [hp_ref[7], hn_ref[0]],
                            axis=0).astype(BF16)
    else:
        x_ref, sh_ref, sc_ref, g_ref, wm_ref, wtg_ref, cw_ref, cb_ref = refs[:8]
        u = _norm_mod(x_ref[...], g_ref[...], sh_ref[...], sc_ref[...])
    k_ref, qt_ref, kt_ref, v_ref, tg_ref = refs[-5:]
    n = k_ref.shape[0]
    um = u[0:n]
    pres = [_dot_nt(u, wm_ref[c * 512:(c + 1) * 512, :]) for c in range(2)]
    v_ref[...] = _dot_nt(um, wm_ref[1024:2048, :]).astype(BF16)
    tg_ref[...] = _dot_nt(um, wtg_ref[...])
    row8 = lax.broadcasted_iota(jnp.int32, (8, 512), 0)
    j = pl.program_id(0) % nblk
    for c, pre in enumerate(pres):
        cs = slice(c * 512, (c + 1) * 512)
        a = pre[0:n]
        if colmajor:
            prev_row = jnp.where(j > 0, pre[n + 7:n + 8], 0.0)
            next_row = jnp.where(j < nblk - 1, pre[n + 8:n + 9], 0.0)
        else:
            prev_row = next_row = jnp.zeros((1, 512), F32)
        ap = pltpu.roll(a, 1, axis=0)
        ap = jnp.concatenate([jnp.where(row8 == 0, prev_row, ap[0:8]), ap[8:]], axis=0)
        an = pltpu.roll(a, n - 1, axis=0)
        an = jnp.concatenate([an[0:n - 8], jnp.where(row8 == 7, next_row, an[n - 8:])], axis=0)
        conv = ap * cw_ref[0:1, cs] + a * cw_ref[1:2, cs] + an * cw_ref[2:3, cs] + cb_ref[:, cs]
        y = conv * _sigmoid(conv)
        if c == 0:
            qt_ref[...] = y.T.astype(BF16)
        else:
            y = y * (DK ** -0.5)
            k_ref[...] = y.astype(BF16)
            kt_ref[...] = y.T.astype(BF16)


def _inproj_m(xv, mod3, mod_row, norm_g, wm, wtg, conv_w, conv_b, tm, colmajor):
    full = lambda shape: pl.BlockSpec(shape, lambda i: (0,) * len(shape))
    mod_specs = [pl.BlockSpec((None, 1, D), lambda i: (mod_row(i), 0, 0)),
                 pl.BlockSpec((None, 1, D), lambda i: (mod_row(i), 0, 1)), full((1, D))]
    mod_args = (mod3, mod3, norm_g)
    if colmajor:
        mod_specs, mod_args = [], ()
        bsz = xv.shape[0]
        tn = SEQ
        nblk = GRID_W // COL_BLOCK
        blk = (None, GRID_W, COL_BLOCK, D)
        halo = (None, 8, COL_BLOCK, D)
        x_specs = [pl.BlockSpec(blk, lambda i: (i // nblk, 0, i % nblk, 0)),
                   pl.BlockSpec(halo, lambda i: (i // nblk, GRID_W // 8 - 1, jnp.maximum(i % nblk - 1, 0), 0)),
                   pl.BlockSpec(halo, lambda i: (i // nblk, 0, jnp.minimum(i % nblk + 1, nblk - 1), 0))]
        xs = (xv, xv, xv)
    else:
        tn = tm
        bsz = xv.shape[0] // tn
        nblk = 1
        x_specs = [pl.BlockSpec((tm, D), lambda i: (i, 0))]
        xs = (xv,)
    tok = lambda w: pl.BlockSpec((None, tm, w), lambda i: (i // nblk, i % nblk, 0))
    tr = pl.BlockSpec((None, 512, tm), lambda i: (i // nblk, 0, i % nblk))
    sds = jax.ShapeDtypeStruct
    return pl.pallas_call(
        functools.partial(_inproj_m_kernel, colmajor=colmajor, nblk=nblk),
        out_shape=(sds((bsz, tn, 512), BF16), sds((bsz, 512, tn), BF16), sds((bsz, 512, tn), BF16),
                   sds((bsz, tn, 1024), BF16), sds((bsz, tn, 256), F32)),
        grid=(bsz * nblk,),
        in_specs=x_specs + mod_specs + [_resident(wm.shape), _resident(wtg.shape),
                                        full((3, 1024)), full((1, 1024))],
        out_specs=(tok(512), tr, tr, tok(1024), tok(256)),
        compiler_params=_params(("arbitrary",)),
        name="inproj_m_cm" if colmajor else "inproj_m",
    )(*xs, *mod_args, wm, wtg, conv_w, conv_b)


def _sum_directions(acc_ref, out_ref, blk, rs, cs, val):
    tot = acc_ref[blk, rs, cs] + val
    acc_ref[blk, rs, cs] = tot
    out_ref[rs, cs] = tot.astype(out_ref.dtype)


def _gla_kernel(pf_ref, pb_ref, tf_ref, tb_ref, cp_ref, ct_ref, wup_ref, bdec_ref, olo_ref, ohi_ref,
                st_ref, acc_ref, *, ns):
    i = pl.program_id(1)

    @pl.when(i == 0)
    def _():
        st_ref[...] = jnp.zeros(st_ref.shape, F32)
        acc_ref[...] = jnp.zeros(acc_ref.shape, F32)
        ctx = (cp_ref, ct_ref, None)
        _gla_step((ctx, ctx), wup_ref, bdec_ref, st_ref, None, None, emit_out=False)

    _gla_step(((pf_ref, tf_ref, ohi_ref), (pb_ref, tb_ref, olo_ref)), wup_ref, bdec_ref, st_ref, acc_ref,
              (i, ns - 1 - i), emit_out=True)


def _gla_step(dirs, wup_ref, bdec_ref, st_ref, acc_ref, blks, *, emit_out):
    lc = GLA_CHUNK
    units = _scan_units(dirs[0][0].shape[0] // lc)
    masks =[_tri(lc, d) for d in range(2)]
    gs = []
    for d, (p_ref, t_ref, o_ref) in enumerate(dirs):
        z = _dot(t_ref[...].astype(BF16), wup_ref[d]) + bdec_ref[d]
        gs.append(_log2_sigmoid(z) * (1.0 / TAU))
    bs, ops, sc, us, dcols = {}, {}, {}, {}, {}
    st = {(d, h): st_ref[d * HEADS + h] for d in range(2) for h in range(HEADS)}

    def stage2(d, j):
        bs[d, j] = _cumsum_rows(masks[d][1], gs[d][j * lc:(j + 1) * lc])

    def stage3(d, j):
        p_ref = dirs[d][0]
        rs = slice(j * lc, (j + 1) * lc)
        b = bs[d, j]
        b_last = b[lc - 1:lc, :] if d == 0 else b[0:1, :]
        b_mid = b[lc // 2 - 1:lc // 2, :] if d == 0 else b[lc // 2:lc // 2 + 1, :]
        q = p_ref[rs, 0:512]
        k = p_ref[rs, 512:1024]
        qd = q * jnp.exp2(b - b_mid).astype(BF16)
        kd = k * jnp.exp2((b_mid + LOG2_QSCALE) - b).astype(BF16)
        qi = qd * jnp.exp2(b_mid).astype(BF16)
        kl = kd * jnp.exp2(b_last - b_mid).astype(BF16)
        dec = jnp.exp2(b_last)
        ops[d, j] = (qi, kl, dec, qd, kd)

    def stage4(d, j):
        p_ref = dirs[d][0]
        rs = slice(j * lc, (j + 1) * lc)
        qi, kl, dec, qd, kd = ops[d, j]
        for h in range(HEADS):
            ks = slice(h * DK, (h + 1) * DK)
            v = p_ref[rs, 1024 + h * DV:1024 + (h + 1) * DV]
            if emit_out:
                sc[d, j, h] = jnp.where(masks[d][0], _dot_nt(qd[:, ks], kd[:, ks]), 0.0).astype(BF16)
            us[d, j, h] = _dot_tn(kl[:, ks], v)
            dcols[d, j, h] = jnp.broadcast_to(dec[:, ks], (8, DK)).T[:, 0:1]
    def stage5(d, j):
        p_ref, _, o_ref = dirs[d]
        rs = slice(j * lc, (j + 1) * lc)
        for h in range(HEADS):
            ks = slice(h * DK, (h + 1) * DK)
            if emit_out:
                v = p_ref[rs, 1024 + h * DV:1024 + (h + 1) * DV]
                o = _dot(jnp.concatenate([sc[d, j, h], ops[d, j][0][:, ks]], axis=1),
                         jnp.concatenate([v, st[d, h].astype(BF16)], axis=0))
                _sum_directions(acc_ref, o_ref, blks[d], rs, slice(h * DV, (h + 1) * DV), o)
            st[d, h] = st[d, h] * dcols[d, j, h] + us[d, j, h]

    _skewed(units, (stage2, stage3, stage4, stage5))
    for (d, h), val in st.items():
        st_ref[d * HEADS + h] = val


def _half_specs(ns, step, width):
    half = ns // 2
    lo = pl.BlockSpec((None, step, width), lambda b, i: (b, jnp.minimum(ns - 1 - i, half - 1), 0))
    hi = pl.BlockSpec((None, step, width), lambda b, i: (b, jnp.maximum(i - half, 0), 0))
    return lo, hi


def _gla_scan(pg, tlr, pg_c, tlr_c, wup, bdec):
    bn, tn, _ = pg.shape
    tc = pg_c.shape[1]
    step = min(STEP, tn)
    ns = tn // step
    fwd = lambda b, i: (b, i, 0)
    bwd = lambda b, i: (b, ns - 1 - i, 0)
    ctx = lambda b, i: (b, 0, 0)
    o_shape = jax.ShapeDtypeStruct((bn, tn // 2, HEADS * DV), BF16)
    return pl.pallas_call(
        functools.partial(_gla_kernel, ns=ns),
        out_shape=(o_shape, o_shape),
        grid=(bn, ns),
        scratch_shapes=[pltpu.VMEM((2 * HEADS, DK, DV), F32), pltpu.VMEM((ns, step, HEADS * DV), F32)],
        in_specs=[pl.BlockSpec((None, step, 2048), fwd),
                  pl.BlockSpec((None, step, 2048), bwd),
                  pl.BlockSpec((None, step, 128), fwd),
                  pl.BlockSpec((None, step, 128), bwd),
                  pl.BlockSpec((None, tc, 2048), ctx),
                  pl.BlockSpec((None, tc, 128), ctx),
                  pl.BlockSpec((2, 128, 512), lambda b, i: (0, 0, 0)),
                  pl.BlockSpec((2, 1, 512), lambda b, i: (0, 0, 0))],
        out_specs=_half_specs(ns, step, HEADS * DV),
        compiler_params=_params(("arbitrary", "arbitrary")),
        name="gla_scan",
    )(pg, pg, tlr, tlr, pg_c, tlr_c, wup, bdec)


def _mlstm_kernel(kf_ref, kb_ref, qtf_ref, qtb_ref, ktf_ref, ktb_ref, vf_ref, vb_ref, tf_ref, tb_ref,
                  kc_ref, qtc_ref, ktc_ref, vc_ref, tc_ref, bg_ref, hlo_ref, hhi_ref,
                  c_ref, n_ref, m_ref, acc_ref, *, ns):
    i = pl.program_id(1)

    @pl.when(i == 0)
    def _():
        c_ref[...] = jnp.zeros(c_ref.shape, F32)
        n_ref[...] = jnp.zeros(n_ref.shape, F32)
        m_ref[...] = jnp.zeros(m_ref.shape, F32)
        acc_ref[...] = jnp.zeros(acc_ref.shape, F32)
        ctx = (kc_ref, qtc_ref, ktc_ref, vc_ref, tc_ref, None)
        _mlstm_step((ctx, ctx), bg_ref, c_ref, n_ref, m_ref, None, None, emit_out=False)

    _mlstm_step(((kf_ref, qtf_ref, ktf_ref, vf_ref, tf_ref, hhi_ref),
                 (kb_ref, qtb_ref, ktb_ref, vb_ref, tb_ref, hlo_ref)),
                bg_ref, c_ref, n_ref, m_ref, acc_ref, (i, ns - 1 - i), emit_out=True)


def _mlstm_step(dirs, bg_ref, c_ref, n_ref, m_ref, acc_ref, blks, *, emit_out):
    lc = MLSTM_CHUNK
    units = _scan_units(dirs[0][0].shape[0] // lc)
    tris =[_tri(lc, d) for d in range(2)]
    lane_of = lambda d, h: GATE_LANE + 8 * d + h
    hs = lambda h: slice(h * DK, (h + 1) * DK)
    vs = lambda h: slice(h * DV, (h + 1) * DV)

    mrow = [m_ref[d:d + 1, :] for d in range(2)]
    tiles = {}
    for d, j in units:
        t_ref = dirs[d][4]
        rs = slice(j * lc, (j + 1) * lc)
        ga = (t_ref[rs, 0:128] + bg_ref[:, 0:128]) * LOG2E
        gb = t_ref[rs, 128:256] + bg_ref[:, 128:256]
        bc = _cumsum_rows(tris[d][1], _log2_sigmoid(gb))
        b_last = bc[lc - 1:lc, :] if d == 0 else bc[0:1, :]
        log_key = b_last - bc + ga
        m_new = jnp.maximum(b_last + mrow[d], jnp.max(log_key, axis=0, keepdims=True))
        tiles[d, j] = dict(rmat=ga - bc, bct=bc.T, m_in=mrow[d], wkt=jnp.exp2(log_key - m_new).T,
                           decay=jnp.exp2(b_last + mrow[d] - m_new))
        mrow[d] = m_new
    for d in range(2):
        m_ref[d:d + 1, :] = mrow[d]

    us, ncols = {}, {}
    for d, j in units:
        k_ref, _, kt_ref, v_ref = dirs[d][:4]
        rs = slice(j * lc, (j + 1) * lc)
        for h in range(HEADS):
            lane = lane_of(d, h)
            wk = tiles[d, j]["wkt"][lane:lane + 1, :]
            kwt = kt_ref[hs(h), rs] * wk.astype(BF16)
            us[d, j, h] = _dot(kwt, v_ref[rs, vs(h)])
            ncols[d, j, h] = _dot(jnp.broadcast_to(wk, (16, lc)).astype(BF16), k_ref[rs, hs(h)])[0:1]

    n_in = {}
    for d in range(2):
        for h in range(HEADS):
            idx = d * HEADS + h
            lane = lane_of(d, h)
            nvec = n_ref[idx:idx + 1, :]
            for dd, j in units:
                if dd == d:
                    n_in[d, j, h] = nvec
                    nvec = tiles[d, j]["decay"][:, lane:lane + 1] * nvec + ncols[d, j, h]
            n_ref[idx:idx + 1, :] = nvec

    lhs = {}
    if emit_out:
        for d, j in units:
            k_ref, qt_ref = dirs[d][:2]
            rs = slice(j * lc, (j + 1) * lc)
            t = tiles[d, j]
            causal_t = tris[1 - d][0]
            for h in range(HEADS):
                lane = lane_of(d, h)
                qt = qt_ref[hs(h), rs]
                kq = _dot(jnp.concatenate(
                    [k_ref[rs, hs(h)], jnp.broadcast_to(n_in[d, j, h], (16, DK)).astype(BF16)], axis=0), qt)
                rm = jnp.where(causal_t, t["rmat"][:, lane:lane + 1], -jnp.inf)
                mval = t["m_in"][:, lane:lane + 1]
                mx = jnp.maximum(mval, jnp.max(rm, axis=0, keepdims=True))
                wt = jnp.exp2(rm - mx) * kq[0:lc]
                w_inter = jnp.exp2(mval - mx)
                den = jnp.sum(wt, axis=0, keepdims=True) + w_inter * kq[lc:lc + 1]
                inv = 1.0 / jnp.maximum(jnp.abs(den), jnp.exp2(-(t["bct"][lane:lane + 1, :] + mx)))
                lhs[d, j, h] = jnp.concatenate(
                    [(wt * inv).astype(BF16), qt * (w_inter * inv).astype(BF16)], axis=0)

    for d in range(2):
        v_ref, o_ref = dirs[d][3], dirs[d][5]
        for h in range(HEADS):
            idx = d * HEADS + h
            lane = lane_of(d, h)
            cmat = c_ref[idx]
            for dd, j in units:
                if dd != d:
                    continue
                rs = slice(j * lc, (j + 1) * lc)
                if emit_out:
                    o = _dot_tn(lhs[d, j, h], jnp.concatenate([v_ref[rs, vs(h)], cmat.astype(BF16)], axis=0))
                    _sum_directions(acc_ref, o_ref, blks[d], rs, vs(h), o)
                cmat = tiles[d, j]["decay"][:, lane:lane + 1] * cmat + us[d, j, h]
            c_ref[idx] = cmat


def _mlstm_scan(lat, ctx, bgate):
    k, qt, kt, v, tg = lat
    bn, tn, _ = k.shape
    tc = ctx[0].shape[1]
    step = min(STEP, tn)
    ns = tn // step
    fwd = lambda b, i: (b, i, 0)
    bwd = lambda b, i: (b, ns - 1 - i, 0)
    fwd_t = lambda b, i: (b, 0, i)
    bwd_t = lambda b, i: (b, 0, ns - 1 - i)
    whole = lambda b, i: (b, 0, 0)
    o_shape = jax.ShapeDtypeStruct((bn, tn // 2, HEADS * DV), F32)
    both = lambda shape, f, g: [pl.BlockSpec(shape, f), pl.BlockSpec(shape, g)]
    ctx_specs = [pl.BlockSpec((None, tc, 512), whole), pl.BlockSpec((None, 512, tc), whole),
                 pl.BlockSpec((None, 512, tc), whole), pl.BlockSpec((None, tc, 1024), whole),
                 pl.BlockSpec((None, tc, 256), whole)]
    return pl.pallas_call(
        functools.partial(_mlstm_kernel, ns=ns),
        out_shape=(o_shape, o_shape),
        grid=(bn, ns),
        scratch_shapes=[pltpu.VMEM((2 * HEADS, DK, DV), F32), pltpu.VMEM((2 * HEADS, 128), F32),
                        pltpu.VMEM((8, 128), F32), pltpu.VMEM((ns, step, HEADS * DV), F32)],
        in_specs=(both((None, step, 512), fwd, bwd) + both((None, 512, step), fwd_t, bwd_t)
                  + both((None, 512, step), fwd_t, bwd_t) + both((None, step, 1024), fwd, bwd)
                  + both((None, step, 256), fwd, bwd) + ctx_specs
                  + [pl.BlockSpec((1, 256), lambda b, i: (0, 0))]),
        out_specs=_half_specs(ns, step, HEADS * DV),
        compiler_params=_params(("arbitrary", "arbitrary")),
        name="mlstm_scan",
    )(k, k, qt, qt, kt, kt, v, v, tg, tg, *ctx, bgate)


def _head_norm(o, g):
    parts = []
    for h in range(HEADS):
        oh = o[:, h * DV:(h + 1) * DV]
        parts.append(oh * lax.rsqrt(jnp.mean(oh * oh, axis=-1, keepdims=True) + EPS))
    return jnp.concatenate(parts, axis=-1) * g


def _merge_kernel(olo_ref, ohi_ref, hlo_ref, hhi_ref, po_ref, x_ref, g1_ref, gg_ref, gm_ref,
                  wbg_ref, wbm_ref, wo_ref, o_ref, *, per_b):
    groups = [(g * COL_BLOCK // MERGE_GROUPS, (g + 1) * COL_BLOCK // MERGE_GROUPS) for g in range(MERGE_GROUPS)]
    rows = [slice(a * GRID_W, b * GRID_W) for a, b in groups]
    lower = pl.program_id(0) % per_b < per_b // 2
    ys = []
    for (a, b), rs in zip(groups, rows):
        hm = jnp.concatenate([r[:, rl, :] for rl in range(a, b) for r in (hlo_ref, hhi_ref)], axis=0)
        o = jnp.where(lower, olo_ref[rs, :], ohi_ref[rs, :]).astype(F32)
        y_gla = _head_norm(o, gg_ref[...]) * po_ref[rs, 0:1024].astype(F32)
        y_m = _head_norm(hm, gm_ref[...]) * po_ref[rs, 1024:2048].astype(F32)
        ys.append((y_gla.astype(BF16), y_m.astype(BF16)))
    ds = [(_dot(y_gla, wbg_ref[...]), _dot(y_m, wbm_ref[...])) for y_gla, y_m in ys]
    ys = [(po_ref[rs, 2048:3072].astype(F32) * d_g + po_ref[rs, 3072:4096].astype(F32) * d_m).astype(BF16)
          for rs, (d_g, d_m) in zip(rows, ds)]
    mixes = [_dot(y, wo_ref[...]) for y in ys]
    for rs, mix in zip(rows, mixes):
        o_ref[rs, :] = x_ref[rs, :] + g1_ref[...] * mix


def _merge(olo, ohi, hlo4, hhi4, po, x2, mod3, gg, gm, wbg, wbm, wo):
    m = x2.shape[0]
    tm = GRID_W * COL_BLOCK
    per_b = SEQ // tm
    half = per_b // 2
    tok = lambda i: (i, 0)
    hspec = pl.BlockSpec((None, GRID_W // 2, COL_BLOCK, D), lambda i: (i // per_b, 0, i % per_b, 0))
    lo_spec = pl.BlockSpec((None, tm, D), lambda i: (i // per_b, jnp.minimum(i % per_b, half - 1), 0))
    hi_spec = pl.BlockSpec((None, tm, D), lambda i: (i // per_b, jnp.maximum(i % per_b - half, 0), 0))
    return pl.pallas_call(
        functools.partial(_merge_kernel, per_b=per_b),
        out_shape=jax.ShapeDtypeStruct((m, D), F32),
        grid=(m // tm,),
        in_specs=[lo_spec, hi_spec, hspec, hspec,
                  pl.BlockSpec((tm, 4096), tok), pl.BlockSpec((tm, D), tok),
                  pl.BlockSpec((None, 1, D), lambda i: (i // per_b, 0, 2)),
                  pl.BlockSpec((1, D), lambda i: (0, 0)), pl.BlockSpec((1, D), lambda i: (0, 0)),
                  _resident((D, D)), _resident((D, D)), _resident((D, D))],
        out_specs=pl.BlockSpec((tm, D), tok),
        compiler_params=_params(("arbitrary",)),
        name="merge",
    )(olo, ohi, hlo4, hhi4, po, x2, mod3, gg, gm, wbg, wbm, wo)


FF_TILES = ((0, 1280), (1280, 2816))


def _ffn_kernel(x_ref, sh_ref, sc_ref, g2_ref, ng_ref, fg_ref, wi_ref, wo_ref, o_ref):
    x = x_ref[...]
    u = _norm_mod(x, ng_ref[...], sh_ref[...], sc_ref[...])
    acc = None
    for lo, hi in FF_TILES:
        a = _dot(u, wi_ref[:, lo:hi])
        b = _dot(u, wi_ref[:, D_FF + lo:D_FF + hi])
        hid = (a * _sigmoid(a) * b).astype(BF16)
        part = _dot(hid, wo_ref[lo:hi, :])
        acc = part if acc is None else acc + part
    x2 = x + g2_ref[...] * acc
    o_ref[...] = x2 * lax.rsqrt(jnp.mean(x2 * x2, axis=-1, keepdims=True) + EPS) * fg_ref[...]


def _ffn(x1, mod3, ng, fg, wi, wo, tm):
    m = x1.shape[0]
    per_b = SEQ // tm
    tok = lambda i: (i, 0)
    modspec = lambda c: pl.BlockSpec((None, 1, D), lambda i: (i // per_b, 0, c))
    return pl.pallas_call(
        _ffn_kernel,
        out_shape=jax.ShapeDtypeStruct((m, D), F32),
        grid=(m // tm,),
        in_specs=[pl.BlockSpec((tm, D), tok), modspec(3), modspec(4), modspec(5),
                  pl.BlockSpec((1, D), lambda i: (0, 0)), pl.BlockSpec((1, D), lambda i: (0, 0)),
                  _resident((D, 2 * D_FF)), _resident((D_FF, D))],
        out_specs=pl.BlockSpec((tm, D), tok),
        compiler_params=_params(("arbitrary",)),
        name="ffn",
    )(x1, mod3, mod3, mod3, ng, fg, wi, wo)


def _split_w_in_kernel(w_ref, wg_ref, wm_ref, wo_ref, wtl_ref, wtg_ref):
    cols = w_ref.shape[1]
    c = lambda a, b: w_ref[a:b, :].astype(BF16)
    z = lambda n: jnp.zeros((n, cols), BF16)
    wg_ref[...] = c(0, 2048)
    wm_ref[...] = c(3104, 5152)
    wo_ref[0:1024, :] = c(2048, 3072)
    wo_ref[1024:2048, :] = c(5152, 6176)
    wo_ref[2048:4096, :] = c(6192, 8240)
    wtl_ref[...] = jnp.concatenate([c(3072, 3104), z(128 - 2 * RANK)], axis=0)
    wtg_ref[...] = jnp.concatenate(
        [z(GATE_LANE), c(6176, 6192), z(128 - GATE_LANE - 16),
         z(GATE_LANE), c(6180, 6184), z(4), c(6188, 6192), z(128 - GATE_LANE - 12)], axis=0)


def _split_w_in(w_in_t):
    tc = 256
    heights = (2048, 2048, 4096, 128, 256)
    return pl.pallas_call(
        _split_w_in_kernel,
        out_shape=tuple(jax.ShapeDtypeStruct((n, D), BF16) for n in heights),
        grid=(D // tc,),
        in_specs=[pl.BlockSpec((w_in_t.shape[0], tc), lambda i: (0, i))],
        out_specs=tuple(pl.BlockSpec((n, tc), lambda i: (0, i)) for n in heights),
        compiler_params=_params(("arbitrary",)),
        name="split_w_in",
    )(w_in_t)


def kernel(x, c, ctx, c_ctx, w_ada, b_ada, norm1_g, w_in, gla_w_up, gla_b_dec, gla_norm_g,
           mlstm_conv_w, mlstm_conv_b, mlstm_b_gate, mlstm_norm_g, w_br_gla, w_br_mlstm, w_out,
           norm2_g, w_ffn_in, w_ffn_out, final_g):
    bsz = x.shape[0]
    row = lambda a: a.reshape(1, -1)

    cvec = jnp.concatenate([c, c_ctx[None, :], jnp.zeros((8 - bsz - 1, D), F32)], axis=0)
    mod3 = _ada(cvec, w_ada[0], row(b_ada[0])).reshape(8, 1, N_MOD)

    wg, wm, wo, wtl, wtg = _split_w_in(w_in[0].T)
    x2 = x.reshape(bsz * SEQ, D)
    ctx2 = ctx.reshape(bsz * CTX, D)
    g1n = row(norm1_g[0])
    tm = GRID_W * COL_BLOCK
    po, pg, tlr, u_lat = _inproj(x2, mod3, lambda i: i // (SEQ // tm), g1n, (wo, wg, wtl), (BF16, BF16, F32), tm,
                                 acts=(("silu", "sigmoid", "sigmoid", "sigmoid"), None, None), emit_u=True)
    conv_w = mlstm_conv_w[0]
    conv_b = row(mlstm_conv_b[0])
    m_lat = _inproj_m(u_lat.reshape(bsz, SEQ // GRID_W, GRID_W, D), None, None, None,
                      wm, wtg, conv_w, conv_b, tm, True)
    pg_c, tlr_c = _inproj(ctx2, mod3, lambda i: bsz, g1n, (wg, wtl), (BF16, F32), CTX)
    m_ctx = _inproj_m(ctx2, mod3, lambda i: bsz, g1n, wm, wtg, conv_w, conv_b, CTX, False)

    wup = jnp.zeros((2, 128, HEADS * DK), F32)
    wup = wup.at[0, 0:RANK].set(gla_w_up[0, 0]).at[1, RANK:2 * RANK].set(gla_w_up[0, 1]).astype(BF16)
    bdec = gla_b_dec[0].reshape(2, 1, HEADS * DK)
    o_lo, o_hi = _gla_scan(pg.reshape(bsz, SEQ, 2048), tlr.reshape(bsz, SEQ, 128),
                           pg_c.reshape(bsz, CTX, 2048), tlr_c.reshape(bsz, CTX, 128), wup, bdec)

    bgate = mlstm_b_gate[0].reshape(1, 16)
    zg = lambda n: jnp.zeros((1, n), F32)
    bg2 = jnp.concatenate([zg(GATE_LANE), bgate, zg(128 - GATE_LANE - 16),
                           zg(GATE_LANE), bgate[:, 4:8], zg(4), bgate[:, 12:16],
                           zg(128 - GATE_LANE - 12)], axis=1)
    h_lo, h_hi = _mlstm_scan(m_lat, m_ctx, bg2)

    cm4 = lambda a: a.reshape(bsz, GRID_W // 2, SEQ // GRID_W, D)
    x1 = _merge(o_lo, o_hi, cm4(h_lo), cm4(h_hi), po, x2, mod3,
                row(gla_norm_g[0]), row(mlstm_norm_g[0]),
                w_br_gla[0].astype(BF16), w_br_mlstm[0].astype(BF16), w_out[0].astype(BF16))
    out = _ffn(x1, mod3, row(norm2_g[0]), row(final_g),
               w_ffn_in[0].astype(BF16), w_ffn_out[0].astype(BF16), tm)
    return out.reshape(bsz, SEQ, D)
```

```python
import functools
import math

import jax
import jax.numpy as jnp
from jax import lax
from jax.experimental import pallas as pl
from jax.experimental.pallas import tpu as pltpu

D = 1024
SEQ = 4096
CTX = 256
GRID_W = 64
EPS = 1e-6
HEADS = 4
DK = 128
DV = 256
RANK = 16
TAU = 16.0
D_FF = 2816
N_MOD = 6 * D
GATE_LANE = 32
STEP = 512
GLA_CHUNK = 128
MLSTM_CHUNK = 128
COL_BLOCK = 8
MERGE_GROUPS = 2

LOG2E = math.log2(math.e)
LOG2_QSCALE = -0.5 * math.log2(DK)

F32 = jnp.float32
BF16 = jnp.bfloat16
VMEM_LIMIT = 56 * 1024 * 1024


def _dot(a, b):
    return jnp.dot(a, b, preferred_element_type=F32)


def _dot_nt(a, b):
    return lax.dot_general(a, b, (((1,), (1,)), ((), ())), preferred_element_type=F32)


def _dot_tn(a, b):
    return lax.dot_general(a, b, (((0,), (0,)), ((), ())), preferred_element_type=F32)


def _sigmoid(x):
    return 0.5 * jnp.tanh(0.5 * x) + 0.5


def _log2_sigmoid(x):
    return jnp.minimum(x, 0.0) * LOG2E - jnp.log2(1.0 + jnp.exp2(jnp.abs(x) * (-LOG2E)))


def _cumsum_rows(tri, g):
    g1 = g.astype(BF16)
    g2 = (g - g1.astype(F32)).astype(BF16)
    return _dot(tri, g1) + _dot(tri, g2)


def _tri(n, d):
    row = lax.broadcasted_iota(jnp.int32, (n, n), 0)
    col = lax.broadcasted_iota(jnp.int32, (n, n), 1)
    causal = (col <= row) if d == 0 else (col >= row)
    return causal, jnp.where(causal, 1.0, 0.0).astype(BF16)


def _resident(shape):
    n = len(shape)
    return pl.BlockSpec(shape, lambda *_: (0,) * n, pipeline_mode=pl.Buffered(1))


def _params(sem):
    return pltpu.CompilerParams(dimension_semantics=sem, vmem_limit_bytes=VMEM_LIMIT)


def _skewed(units, stages):
    for t in range(len(units) + len(stages) - 1):
        for s_idx, stage in enumerate(stages):
            if 0 <= t - s_idx < len(units):
                stage(*units[t - s_idx])


def _scan_units(n_sub):
    return [(d, j) for jj in range(n_sub) for d, j in ((0, jj), (1, n_sub - 1 - jj))]


def _ada_kernel(c_ref, w_ref, b_ref, o_ref):
    cv = c_ref[...]
    s = (cv * _sigmoid(cv)).astype(BF16)
    o_ref[...] = _dot(s, w_ref[...].astype(BF16)) + b_ref[...]


def _ada(cvec, w_ada, b_ada):
    tn = 1024
    return pl.pallas_call(
        _ada_kernel,
        out_shape=jax.ShapeDtypeStruct((8, N_MOD), F32),
        grid=(N_MOD // tn,),
        in_specs=[pl.BlockSpec((8, D), lambda j: (0, 0)),
                  pl.BlockSpec((D, tn), lambda j: (0, j)),
                  pl.BlockSpec((1, tn), lambda j: (0, j))],
        out_specs=pl.BlockSpec((8, tn), lambda j: (0, j)),
        compiler_params=_params(("arbitrary",)),
        name="adaln",
    )(cvec, w_ada, b_ada)


def _norm_mod_f32(x, g, sh, sc):
    return x * lax.rsqrt(jnp.mean(x * x, axis=-1, keepdims=True) + EPS) * (g * (1.0 + sc)) + sh


def _norm_mod(x, g, sh, sc):
    return _norm_mod_f32(x, g, sh, sc).astype(BF16)


def _act(v, kind):
    if kind == "silu":
        return v * _sigmoid(v)
    if kind == "sigmoid":
        return _sigmoid(v)
    return v


def _inproj_kernel(x_ref, sh_ref, sc_ref, g_ref, *refs, acts, emit_u):
    uf = _norm_mod_f32(x_ref[...], g_ref[...], sh_ref[...], sc_ref[...])
    if emit_u:
        refs[-1][...] = uf
        refs = refs[:-1]
    n_out = len(refs) // 2
    u = uf.astype(BF16)
    for w_ref, o_ref, act in zip(refs[:n_out], refs[n_out:], acts):
        n = w_ref.shape[0]
        for jc, j in enumerate(range(0, n, 1024)):
            cs = slice(j, min(j + 1024, n))
            o_ref[:, cs] = _act(_dot_nt(u, w_ref[cs, :]), act[jc] if act else None).astype(o_ref.dtype)


def _inproj(x2, mod3, mod_row, norm_g, weights, out_dtypes, tm, acts=None, emit_u=False):
    m = x2.shape[0]
    acts = acts or (None,) * len(weights)
    widths = [w.shape[0] for w in weights] + ([D] if emit_u else [])
    dtypes = list(out_dtypes) + ([F32] if emit_u else [])
    return pl.pallas_call(
        functools.partial(_inproj_kernel, acts=acts, emit_u=emit_u),
        out_shape=tuple(jax.ShapeDtypeStruct((m, n), dt) for n, dt in zip(widths, dtypes)),
        grid=(m // tm,),
        in_specs=[pl.BlockSpec((tm, D), lambda i: (i, 0)),
                  pl.BlockSpec((None, 1, D), lambda i: (mod_row(i), 0, 0)),
                  pl.BlockSpec((None, 1, D), lambda i: (mod_row(i), 0, 1)),
                  pl.BlockSpec((1, D), lambda i: (0, 0))] + [_resident(w.shape) for w in weights],
        out_specs=tuple(pl.BlockSpec((tm, n), lambda i: (i, 0)) for n in widths),
        compiler_params=_params(("arbitrary",)),
        name="inproj",
    )(x2, mod3, mod3, norm_g, *weights)


def _inproj_m_kernel(*refs, colmajor, nblk):
    if colmajor:
        u_ref, hp_ref, hn_ref, wm_ref, wtg_ref, cw_ref, cb_ref = refs[:7]
        u = jnp.concatenate([u_ref[:, cl, :] for cl in range(COL_BLOCK)] + [hp_ref[7], hn_ref[0]],
                            axis=0).astype(BF16)
    else:
        x_ref, sh_ref, sc_ref, g_ref, wm_ref, wtg_ref, cw_ref, cb_ref = refs[:8]
        u = _norm_mod(x_ref[...], g_ref[...], sh_ref[...], sc_ref[...])
    k_ref, qt_ref, kt_ref, v_ref, tg_ref = refs[-5:]
    n = k_ref.shape[0]
    um = u[0:n]
    pres = [_dot_nt(u, wm_ref[c * 512:(c + 1) * 512, :]) for c in range(2)]
    v_ref[...] = _dot_nt(um, wm_ref[1024:2048, :]).astype(BF16)
    tg_ref[...] = _dot_nt(um, wtg_ref[...])
    row8 = lax.broadcasted_iota(jnp.int32, (8, 512), 0)
    j = pl.program_id(0) % nblk
    for c, pre in enumerate(pres):
        cs = slice(c * 512, (c + 1) * 512)
        a = pre[0:n]
        if colmajor:
            prev_row = jnp.where(j > 0, pre[n + 7:n + 8], 0.0)
            next_row = jnp.where(j < nblk - 1, pre[n + 8:n + 9], 0.0)
        else:
            prev_row = next_row = jnp.zeros((1, 512), F32)
        ap = pltpu.roll(a, 1, axis=0)
        ap = jnp.concatenate([jnp.where(row8 == 0, prev_row, ap[0:8]), ap[8:]], axis=0)
        an = pltpu.roll(a, n - 1, axis=0)
        an = jnp.concatenate([an[0:n - 8], jnp.where(row8 == 7, next_row, an[n - 8:])], axis=0)
        conv = ap * cw_ref[0:1, cs] + a * cw_ref[1:2, cs] + an * cw_ref[2:3, cs] + cb_ref[:, cs]
        y = conv * _sigmoid(conv)
        if c == 0:
            qt_ref[...] = y.T.astype(BF16)
        else:
            y = y * (DK ** -0.5)
            k_ref[...] = y.astype(BF16)
            kt_ref[...] = y.T.astype(BF16)


def _inproj_m(xv, mod3, mod_row, norm_g, wm, wtg, conv_w, conv_b, tm, colmajor):
    full = lambda shape: pl.BlockSpec(shape, lambda i: (0,) * len(shape))
    mod_specs = [pl.BlockSpec((None, 1, D), lambda i: (mod_row(i), 0, 0)),
                 pl.BlockSpec((None, 1, D), lambda i: (mod_row(i), 0, 1)), full((1, D))]
    mod_args = (mod3, mod3, norm_g)
    if colmajor:
        mod_specs, mod_args = [], ()
        bsz = xv.shape[0]
        tn = SEQ
        nblk = GRID_W // COL_BLOCK
        blk = (None, GRID_W, COL_BLOCK, D)
        halo = (None, 8, COL_BLOCK, D)
        x_specs = [pl.BlockSpec(blk, lambda i: (i // nblk, 0, i % nblk, 0)),
                   pl.BlockSpec(halo, lambda i: (i // nblk, GRID_W // 8 - 1, jnp.maximum(i % nblk - 1, 0), 0)),
                   pl.BlockSpec(halo, lambda i: (i // nblk, 0, jnp.minimum(i % nblk + 1, nblk - 1), 0))]
        xs = (xv, xv, xv)
    else:
        tn = tm
        bsz = xv.shape[0] // tn
        nblk = 1
        x_specs = [pl.BlockSpec((tm, D), lambda i: (i, 0))]
        xs = (xv,)
    tok = lambda w: pl.BlockSpec((None, tm, w), lambda i: (i // nblk, i % nblk, 0))
    tr = pl.BlockSpec((None, None, 512, tm), lambda i: (i // nblk, i % nblk, 0, 0))
    sds = jax.ShapeDtypeStruct
    return pl.pallas_call(
        functools.partial(_inproj_m_kernel, colmajor=colmajor, nblk=nblk),
        out_shape=(sds((bsz, tn, 512), BF16), sds((bsz, nblk, 512, tm), BF16), sds((bsz, nblk, 512, tm), BF16),
                   sds((bsz, tn, 1024), BF16), sds((bsz, tn, 256), F32)),
        grid=(bsz * nblk,),
        in_specs=x_specs + mod_specs + [_resident(wm.shape), _resident(wtg.shape),
                                        full((3, 1024)), full((1, 1024))],
        out_specs=(tok(512), tr, tr, tok(1024), tok(256)),
        compiler_params=_params(("arbitrary",)),
        name="inproj_m_cm" if colmajor else "inproj_m",
    )(*xs, *mod_args, wm, wtg, conv_w, conv_b)


def _sum_directions(acc_ref, out_ref, blk, rs, cs, val):
    tot = acc_ref[blk, rs, cs] + val
    acc_ref[blk, rs, cs] = tot
    out_ref[rs, cs] = tot.astype(out_ref.dtype)


def _gla_kernel(pf_ref, pb_ref, tf_ref, tb_ref, cp_ref, ct_ref, wup_ref, bdec_ref, olo_ref, ohi_ref,
                st_ref, acc_ref, *, ns):
    i = pl.program_id(1)

    @pl.when(i == 0)
    def _():
        st_ref[...] = jnp.zeros(st_ref.shape, F32)
        acc_ref[...] = jnp.zeros(acc_ref.shape, F32)
        ctx = (cp_ref, ct_ref, None)
        _gla_step((ctx, ctx), wup_ref, bdec_ref, st_ref, None, None, emit_out=False)

    _gla_step(((pf_ref, tf_ref, ohi_ref), (pb_ref, tb_ref, olo_ref)), wup_ref, bdec_ref, st_ref, acc_ref,
              (i, ns - 1 - i), emit_out=True)


def _gla_step(dirs, wup_ref, bdec_ref, st_ref, acc_ref, blks, *, emit_out):
    lc = GLA_CHUNK
    units = _scan_units(dirs[0][0].shape[0] // lc)
    masks =[_tri(lc, d) for d in range(2)]
    gs = []
    for d, (p_ref, t_ref, o_ref) in enumerate(dirs):
        z = _dot(t_ref[...].astype(BF16), wup_ref[d]) + bdec_ref[d]
        gs.append(_log2_sigmoid(z) * (1.0 / TAU))
    bs, ops, sc, us, dcols = {}, {}, {}, {}, {}
    st = {(d, h): st_ref[d * HEADS + h] for d in range(2) for h in range(HEADS)}

    def stage2(d, j):
        bs[d, j] = _cumsum_rows(masks[d][1], gs[d][j * lc:(j + 1) * lc])

    def stage3(d, j):
        p_ref = dirs[d][0]
        rs = slice(j * lc, (j + 1) * lc)
        b = bs[d, j]
        b_last = b[lc - 1:lc, :] if d == 0 else b[0:1, :]
        b_mid = b[lc // 2 - 1:lc // 2, :] if d == 0 else b[lc // 2:lc // 2 + 1, :]
        q = p_ref[rs, 0:512]
        k = p_ref[rs, 512:1024]
        qd = q * jnp.exp2(b - b_mid).astype(BF16)
        kd = k * jnp.exp2((b_mid + LOG2_QSCALE) - b).astype(BF16)
        qi = qd * jnp.exp2(b_mid).astype(BF16)
        kl = kd * jnp.exp2(b_last - b_mid).astype(BF16)
        dec = jnp.exp2(b_last)
        ops[d, j] = (qi, kl, dec, qd, kd)

    def stage4(d, j):
        p_ref = dirs[d][0]
        rs = slice(j * lc, (j + 1) * lc)
        qi, kl, dec, qd, kd = ops[d, j]
        for h in range(HEADS):
            ks = slice(h * DK, (h + 1) * DK)
            v = p_ref[rs, 1024 + h * DV:1024 + (h + 1) * DV]
            if emit_out:
                sc[d, j, h] = jnp.where(masks[d][0], _dot_nt(qd[:, ks], kd[:, ks]), 0.0).astype(BF16)
            us[d, j, h] = _dot_tn(kl[:, ks], v)
            dcols[d, j, h] = jnp.broadcast_to(dec[:, ks], (8, DK)).T[:, 0:1]
    def stage5(d, j):
        p_ref, _, o_ref = dirs[d]
        rs = slice(j * lc, (j + 1) * lc)
        for h in range(HEADS):
            ks = slice(h * DK, (h + 1) * DK)
            if emit_out:
                v = p_ref[rs, 1024 + h * DV:1024 + (h + 1) * DV]
                o = _dot(jnp.concatenate([sc[d, j, h], ops[d, j][0][:, ks]], axis=1),
                         jnp.concatenate([v, st[d, h].astype(BF16)], axis=0))
                _sum_directions(acc_ref, o_ref, blks[d], rs, slice(h * DV, (h + 1) * DV), o)
            st[d, h] = st[d, h] * dcols[d, j, h] + us[d, j, h]

    _skewed(units, (stage2, stage3, stage4, stage5))
    for (d, h), val in st.items():
        st_ref[d * HEADS + h] = val


def _half_specs(ns, step, width):
    half = ns // 2
    lo = pl.BlockSpec((None, step, width), lambda b, i: (b, jnp.minimum(ns - 1 - i, half - 1), 0))
    hi = pl.BlockSpec((None, step, width), lambda b, i: (b, jnp.maximum(i - half, 0), 0))
    return lo, hi


def _gla_scan(pg, tlr, pg_c, tlr_c, wup, bdec):
    bn, tn, _ = pg.shape
    tc = pg_c.shape[1]
    step = min(STEP, tn)
    ns = tn // step
    fwd = lambda b, i: (b, i, 0)
    bwd = lambda b, i: (b, ns - 1 - i, 0)
    ctx = lambda b, i: (b, 0, 0)
    o_shape = jax.ShapeDtypeStruct((bn, tn // 2, HEADS * DV), BF16)
    return pl.pallas_call(
        functools.partial(_gla_kernel, ns=ns),
        out_shape=(o_shape, o_shape),
        grid=(bn, ns),
        scratch_shapes=[pltpu.VMEM((2 * HEADS, DK, DV), F32), pltpu.VMEM((ns, step, HEADS * DV), F32)],
        in_specs=[pl.BlockSpec((None, step, 2048), fwd),
                  pl.BlockSpec((None, step, 2048), bwd),
                  pl.BlockSpec((None, step, 128), fwd),
                  pl.BlockSpec((None, step, 128), bwd),
                  pl.BlockSpec((None, tc, 2048), ctx),
                  pl.BlockSpec((None, tc, 128), ctx),
                  pl.BlockSpec((2, 128, 512), lambda b, i: (0, 0, 0)),
                  pl.BlockSpec((2, 1, 512), lambda b, i: (0, 0, 0))],
        out_specs=_half_specs(ns, step, HEADS * DV),
        compiler_params=_params(("arbitrary", "arbitrary")),
        name="gla_scan",
    )(pg, pg, tlr, tlr, pg_c, tlr_c, wup, bdec)


def _mlstm_kernel(kf_ref, kb_ref, qtf_ref, qtb_ref, ktf_ref, ktb_ref, vf_ref, vb_ref, tf_ref, tb_ref,
                  kc_ref, qtc_ref, ktc_ref, vc_ref, tc_ref, bg_ref, hlo_ref, hhi_ref,
                  c_ref, n_ref, m_ref, acc_ref, *, ns):
    i = pl.program_id(1)

    @pl.when(i == 0)
    def _():
        c_ref[...] = jnp.zeros(c_ref.shape, F32)
        n_ref[...] = jnp.zeros(n_ref.shape, F32)
        m_ref[...] = jnp.zeros(m_ref.shape, F32)
        acc_ref[...] = jnp.zeros(acc_ref.shape, F32)
        ctx = (kc_ref, qtc_ref, ktc_ref, vc_ref, tc_ref, None)
        _mlstm_step((ctx, ctx), bg_ref, c_ref, n_ref, m_ref, None, None, emit_out=False)

    _mlstm_step(((kf_ref, qtf_ref, ktf_ref, vf_ref, tf_ref, hhi_ref),
                 (kb_ref, qtb_ref, ktb_ref, vb_ref, tb_ref, hlo_ref)),
                bg_ref, c_ref, n_ref, m_ref, acc_ref, (i, ns - 1 - i), emit_out=True)


def _mlstm_step(dirs, bg_ref, c_ref, n_ref, m_ref, acc_ref, blks, *, emit_out):
    lc = MLSTM_CHUNK
    units = _scan_units(dirs[0][0].shape[0] // lc)
    tris =[_tri(lc, d) for d in range(2)]
    lane_of = lambda d, h: GATE_LANE + 8 * d + h
    hs = lambda h: slice(h * DK, (h + 1) * DK)
    vs = lambda h: slice(h * DV, (h + 1) * DV)

    mrow = [m_ref[d:d + 1, :] for d in range(2)]
    tiles = {}
    for d, j in units:
        t_ref = dirs[d][4]
        rs = slice(j * lc, (j + 1) * lc)
        ga = (t_ref[rs, 0:128] + bg_ref[:, 0:128]) * LOG2E
        gb = t_ref[rs, 128:256] + bg_ref[:, 128:256]
        bc = _cumsum_rows(tris[d][1], _log2_sigmoid(gb))
        b_last = bc[lc - 1:lc, :] if d == 0 else bc[0:1, :]
        log_key = b_last - bc + ga
        m_new = jnp.maximum(b_last + mrow[d], jnp.max(log_key, axis=0, keepdims=True))
        tiles[d, j] = dict(rmat=ga - bc, bct=bc.T, m_in=mrow[d], wkt=jnp.exp2(log_key - m_new).T,
                           decay=jnp.exp2(b_last + mrow[d] - m_new))
        mrow[d] = m_new
    for d in range(2):
        m_ref[d:d + 1, :] = mrow[d]

    us, ncols = {}, {}
    for d, j in units:
        k_ref, _, kt_ref, v_ref = dirs[d][:4]
        rs = slice(j * lc, (j + 1) * lc)
        for h in range(HEADS):
            lane = lane_of(d, h)
            wk = tiles[d, j]["wkt"][lane:lane + 1, :]
            kwt = kt_ref[hs(h), rs] * wk.astype(BF16)
            us[d, j, h] = _dot(kwt, v_ref[rs, vs(h)])
            ncols[d, j, h] = _dot(jnp.broadcast_to(wk, (16, lc)).astype(BF16), k_ref[rs, hs(h)])[0:1]

    n_in = {}
    for d in range(2):
        for h in range(HEADS):
            idx = d * HEADS + h
            lane = lane_of(d, h)
            nvec = n_ref[idx:idx + 1, :]
            for dd, j in units:
                if dd == d:
                    n_in[d, j, h] = nvec
                    nvec = tiles[d, j]["decay"][:, lane:lane + 1] * nvec + ncols[d, j, h]
            n_ref[idx:idx + 1, :] = nvec

    lhs = {}
    if emit_out:
        for d, j in units:
            k_ref, qt_ref = dirs[d][:2]
            rs = slice(j * lc, (j + 1) * lc)
            t = tiles[d, j]
            causal_t = tris[1 - d][0]
            for h in range(HEADS):
                lane = lane_of(d, h)
                qt = qt_ref[hs(h), rs]
                kq = _dot(jnp.concatenate(
                    [k_ref[rs, hs(h)], jnp.broadcast_to(n_in[d, j, h], (16, DK)).astype(BF16)], axis=0), qt)
                rm = jnp.where(causal_t, t["rmat"][:, lane:lane + 1], -jnp.inf)
                mval = t["m_in"][:, lane:lane + 1]
                mx = jnp.maximum(mval, jnp.max(rm, axis=0, keepdims=True))
                wt = jnp.exp2(rm - mx) * kq[0:lc]
                w_inter = jnp.exp2(mval - mx)
                den = jnp.sum(wt, axis=0, keepdims=True) + w_inter * kq[lc:lc + 1]
                inv = 1.0 / jnp.maximum(jnp.abs(den), jnp.exp2(-(t["bct"][lane:lane + 1, :] + mx)))
                lhs[d, j, h] = jnp.concatenate(
                    [(wt * inv).astype(BF16), qt * (w_inter * inv).astype(BF16)], axis=0)

    for d in range(2):
        v_ref, o_ref = dirs[d][3], dirs[d][5]
        for h in range(HEADS):
            idx = d * HEADS + h
            lane = lane_of(d, h)
            cmat = c_ref[idx]
            for dd, j in units:
                if dd != d:
                    continue
                rs = slice(j * lc, (j + 1) * lc)
                if emit_out:
                    o = _dot_tn(lhs[d, j, h], jnp.concatenate([v_ref[rs, vs(h)], cmat.astype(BF16)], axis=0))
                    _sum_directions(acc_ref, o_ref, blks[d], rs, vs(h), o)
                cmat = tiles[d, j]["decay"][:, lane:lane + 1] * cmat + us[d, j, h]
            c_ref[idx] = cmat


def _mlstm_scan(lat, ctx, bgate):
    k, qt, kt, v, tg = lat
    bn, tn, _ = k.shape
    tc = ctx[0].shape[1]
    step = min(STEP, tn)
    ns = tn // step
    fwd = lambda b, i: (b, i, 0)
    bwd = lambda b, i: (b, ns - 1 - i, 0)
    assert qt.shape[1:] == (ns, 512, step) and ctx[1].shape[1:] == (1, 512, tc)
    fwd_t = lambda b, i: (b, i, 0, 0)
    bwd_t = lambda b, i: (b, ns - 1 - i, 0, 0)
    whole = lambda b, i: (b, 0, 0)
    o_shape = jax.ShapeDtypeStruct((bn, tn // 2, HEADS * DV), F32)
    both = lambda shape, f, g: [pl.BlockSpec(shape, f), pl.BlockSpec(shape, g)]
    whole_t = lambda b, i: (b, 0, 0, 0)
    ctx_specs = [pl.BlockSpec((None, tc, 512), whole), pl.BlockSpec((None, None, 512, tc), whole_t),
                 pl.BlockSpec((None, None, 512, tc), whole_t), pl.BlockSpec((None, tc, 1024), whole),
                 pl.BlockSpec((None, tc, 256), whole)]
    return pl.pallas_call(
        functools.partial(_mlstm_kernel, ns=ns),
        out_shape=(o_shape, o_shape),
        grid=(bn, ns),
        scratch_shapes=[pltpu.VMEM((2 * HEADS, DK, DV), F32), pltpu.VMEM((2 * HEADS, 128), F32),
                        pltpu.VMEM((8, 128), F32), pltpu.VMEM((ns, step, HEADS * DV), F32)],
        in_specs=(both((None, step, 512), fwd, bwd) + both((None, None, 512, step), fwd_t, bwd_t)
                  + both((None, None, 512, step), fwd_t, bwd_t) + both((None, step, 1024), fwd, bwd)
                  + both((None, step, 256), fwd, bwd) + ctx_specs
                  + [pl.BlockSpec((1, 256), lambda b, i: (0, 0))]),
        out_specs=_half_specs(ns, step, HEADS * DV),
        compiler_params=_params(("arbitrary", "arbitrary")),
        name="mlstm_scan",
    )(k, k, qt, qt, kt, kt, v, v, tg, tg, *ctx, bgate)


def _head_norm(o, g):
    parts = []
    for h in range(HEADS):
        oh = o[:, h * DV:(h + 1) * DV]
        parts.append(oh * lax.rsqrt(jnp.mean(oh * oh, axis=-1, keepdims=True) + EPS))
    return jnp.concatenate(parts, axis=-1) * g


def _merge_kernel(olo_ref, ohi_ref, hlo_ref, hhi_ref, po_ref, x_ref, g1_ref, gg_ref, gm_ref,
                  wbg_ref, wbm_ref, wo_ref, o_ref, *, per_b):
    groups = [(g * COL_BLOCK // MERGE_GROUPS, (g + 1) * COL_BLOCK // MERGE_GROUPS) for g in range(MERGE_GROUPS)]
    rows = [slice(a * GRID_W, b * GRID_W) for a, b in groups]
    lower = pl.program_id(0) % per_b < per_b // 2
    ys = []
    for (a, b), rs in zip(groups, rows):
        hm = jnp.concatenate([r[:, rl, :] for rl in range(a, b) for r in (hlo_ref, hhi_ref)], axis=0)
        o = jnp.where(lower, olo_ref[rs, :], ohi_ref[rs, :]).astype(F32)
        y_gla = _head_norm(o, gg_ref[...]) * po_ref[rs, 0:1024].astype(F32)
        y_m = _head_norm(hm, gm_ref[...]) * po_ref[rs, 1024:2048].astype(F32)
        ys.append((y_gla.astype(BF16), y_m.astype(BF16)))
    ds = [(_dot(y_gla, wbg_ref[...]), _dot(y_m, wbm_ref[...])) for y_gla, y_m in ys]
    ys = [(po_ref[rs, 2048:3072].astype(F32) * d_g + po_ref[rs, 3072:4096].astype(F32) * d_m).astype(BF16)
          for rs, (d_g, d_m) in zip(rows, ds)]
    mixes = [_dot(y, wo_ref[...]) for y in ys]
    for rs, mix in zip(rows, mixes):
        o_ref[rs, :] = x_ref[rs, :] + g1_ref[...] * mix


def _merge(olo, ohi, hlo4, hhi4, po, x2, mod3, gg, gm, wbg, wbm, wo):
    m = x2.shape[0]
    tm = GRID_W * COL_BLOCK
    per_b = SEQ // tm
    half = per_b // 2
    tok = lambda i: (i, 0)
    hspec = pl.BlockSpec((None, GRID_W // 2, COL_BLOCK, D), lambda i: (i // per_b, 0, i % per_b, 0))
    lo_spec = pl.BlockSpec((None, tm, D), lambda i: (i // per_b, jnp.minimum(i % per_b, half - 1), 0))
    hi_spec = pl.BlockSpec((None, tm, D), lambda i: (i // per_b, jnp.maximum(i % per_b - half, 0), 0))
    return pl.pallas_call(
        functools.partial(_merge_kernel, per_b=per_b),
        out_shape=jax.ShapeDtypeStruct((m, D), F32),
        grid=(m // tm,),
        in_specs=[lo_spec, hi_spec, hspec, hspec,
                  pl.BlockSpec((tm, 4096), tok), pl.BlockSpec((tm, D), tok),
                  pl.BlockSpec((None, 1, D), lambda i: (i // per_b, 0, 2)),
                  pl.BlockSpec((1, D), lambda i: (0, 0)), pl.BlockSpec((1, D), lambda i: (0, 0)),
                  _resident((D, D)), _resident((D, D)), _resident((D, D))],
        out_specs=pl.BlockSpec((tm, D), tok),
        compiler_params=_params(("arbitrary",)),
        name="merge",
    )(olo, ohi, hlo4, hhi4, po, x2, mod3, gg, gm, wbg, wbm, wo)


FF_TILES = ((0, 1280), (1280, 2816))


def _ffn_kernel(x_ref, sh_ref, sc_ref, g2_ref, ng_ref, fg_ref, wi_ref, wo_ref, o_ref):
    x = x_ref[...]
    u = _norm_mod(x, ng_ref[...], sh_ref[...], sc_ref[...])
    acc = None
    for lo, hi in FF_TILES:
        a = _dot(u, wi_ref[:, lo:hi])
        b = _dot(u, wi_ref[:, D_FF + lo:D_FF + hi])
        hid = (a * _sigmoid(a) * b).astype(BF16)
        part = _dot(hid, wo_ref[lo:hi, :])
        acc = part if acc is None else acc + part
    x2 = x + g2_ref[...] * acc
    o_ref[...] = x2 * lax.rsqrt(jnp.mean(x2 * x2, axis=-1, keepdims=True) + EPS) * fg_ref[...]


def _ffn(x1, mod3, ng, fg, wi, wo, tm):
    m = x1.shape[0]
    per_b = SEQ // tm
    tok = lambda i: (i, 0)
    modspec = lambda c: pl.BlockSpec((None, 1, D), lambda i: (i // per_b, 0, c))
    return pl.pallas_call(
        _ffn_kernel,
        out_shape=jax.ShapeDtypeStruct((m, D), F32),
        grid=(m // tm,),
        in_specs=[pl.BlockSpec((tm, D), tok), modspec(3), modspec(4), modspec(5),
                  pl.BlockSpec((1, D), lambda i: (0, 0)), pl.BlockSpec((1, D), lambda i: (0, 0)),
                  _resident((D, 2 * D_FF)), _resident((D_FF, D))],
        out_specs=pl.BlockSpec((tm, D), tok),
        compiler_params=_params(("arbitrary",)),
        name="ffn",
    )(x1, mod3, mod3, mod3, ng, fg, wi, wo)


def _split_w_in_kernel(w_ref, wg_ref, wm_ref, wo_ref, wtl_ref, wtg_ref):
    cols = w_ref.shape[1]
    c = lambda a, b: w_ref[a:b, :].astype(BF16)
    z = lambda n: jnp.zeros((n, cols), BF16)
    wg_ref[...] = c(0, 2048)
    wm_ref[...] = c(3104, 5152)
    wo_ref[0:1024, :] = c(2048, 3072)
    wo_ref[1024:2048, :] = c(5152, 6176)
    wo_ref[2048:4096, :] = c(6192, 8240)
    wtl_ref[...] = jnp.concatenate([c(3072, 3104), z(128 - 2 * RANK)], axis=0)
    wtg_ref[...] = jnp.concatenate(
        [z(GATE_LANE), c(6176, 6192), z(128 - GATE_LANE - 16),
         z(GATE_LANE), c(6180, 6184), z(4), c(6188, 6192), z(128 - GATE_LANE - 12)], axis=0)


def _split_w_in(w_in_t):
    tc = 256
    heights = (2048, 2048, 4096, 128, 256)
    return pl.pallas_call(
        _split_w_in_kernel,
        out_shape=tuple(jax.ShapeDtypeStruct((n, D), BF16) for n in heights),
        grid=(D // tc,),
        in_specs=[pl.BlockSpec((w_in_t.shape[0], tc), lambda i: (0, i))],
        out_specs=tuple(pl.BlockSpec((n, tc), lambda i: (0, i)) for n in heights),
        compiler_params=_params(("arbitrary",)),
        name="split_w_in",
    )(w_in_t)


def kernel(x, c, ctx, c_ctx, w_ada, b_ada, norm1_g, w_in, gla_w_up, gla_b_dec, gla_norm_g,
           mlstm_conv_w, mlstm_conv_b, mlstm_b_gate, mlstm_norm_g, w_br_gla, w_br_mlstm, w_out,
           norm2_g, w_ffn_in, w_ffn_out, final_g):
    bsz = x.shape[0]
    row = lambda a: a.reshape(1, -1)

    cvec = jnp.concatenate([c, c_ctx[None, :], jnp.zeros((8 - bsz - 1, D), F32)], axis=0)
    mod3 = _ada(cvec, w_ada[0], row(b_ada[0])).reshape(8, 1, N_MOD)

    wg, wm, wo, wtl, wtg = _split_w_in(w_in[0].T)
    x2 = x.reshape(bsz * SEQ, D)
    ctx2 = ctx.reshape(bsz * CTX, D)
    g1n = row(norm1_g[0])
    tm = GRID_W * COL_BLOCK
    po, pg, tlr, u_lat = _inproj(x2, mod3, lambda i: i // (SEQ // tm), g1n, (wo, wg, wtl), (BF16, BF16, F32), tm,
                                 acts=(("silu", "sigmoid", "sigmoid", "sigmoid"), None, None), emit_u=True)
    conv_w = mlstm_conv_w[0]
    conv_b = row(mlstm_conv_b[0])
    m_lat = _inproj_m(u_lat.reshape(bsz, SEQ // GRID_W, GRID_W, D), None, None, None,
                      wm, wtg, conv_w, conv_b, tm, True)
    pg_c, tlr_c = _inproj(ctx2, mod3, lambda i: bsz, g1n, (wg, wtl), (BF16, F32), CTX)
    m_ctx = _inproj_m(ctx2, mod3, lambda i: bsz, g1n, wm, wtg, conv_w, conv_b, CTX, False)

    wup = jnp.zeros((2, 128, HEADS * DK), F32)
    wup = wup.at[0, 0:RANK].set(gla_w_up[0, 0]).at[1, RANK:2 * RANK].set(gla_w_up[0, 1]).astype(BF16)
    bdec = gla_b_dec[0].reshape(2, 1, HEADS * DK)
    o_lo, o_hi = _gla_scan(pg.reshape(bsz, SEQ, 2048), tlr.reshape(bsz, SEQ, 128),
                           pg_c.reshape(bsz, CTX, 2048), tlr_c.reshape(bsz, CTX, 128), wup, bdec)

    bgate = mlstm_b_gate[0].reshape(1, 16)
    zg = lambda n: jnp.zeros((1, n), F32)
    bg2 = jnp.concatenate([zg(GATE_LANE), bgate, zg(128 - GATE_LANE - 16),
                           zg(GATE_LANE), bgate[:, 4:8], zg(4), bgate[:, 12:16],
                           zg(128 - GATE_LANE - 12)], axis=1)
    h_lo, h_hi = _mlstm_scan(m_lat, m_ctx, bg2)

    cm4 = lambda a: a.reshape(bsz, GRID_W // 2, SEQ // GRID_W, D)
    x1 = _merge(o_lo, o_hi, cm4(h_lo), cm4(h_hi), po, x2, mod3,
                row(gla_norm_g[0]), row(mlstm_norm_g[0]),
                w_br_gla[0].astype(BF16), w_br_mlstm[0].astype(BF16), w_out[0].astype(BF16))
    out = _ffn(x1, mod3, row(norm2_g[0]), row(final_g),
               w_ffn_in[0].astype(BF16), w_ffn_out[0].astype(BF16), tm)
    return out.reshape(bsz, SEQ, D)
```

```python
import functools
import math

import jax
import jax.numpy as jnp
from jax import lax
from jax.experimental import pallas as pl
from jax.experimental.pallas import tpu as pltpu

D = 1024
SEQ = 4096
CTX = 256
GRID_W = 64
EPS = 1e-6
HEADS = 4
DK = 128
DV = 256
RANK = 16
TAU = 16.0
D_FF = 2816
N_MOD = 6 * D
GATE_LANE = 32
STEP = 512
GLA_CHUNK = 128
MLSTM_CHUNK = 128
COL_BLOCK = 8
MERGE_GROUPS = 2

LOG2E = math.log2(math.e)
LOG2_QSCALE = -0.5 * math.log2(DK)

F32 = jnp.float32
BF16 = jnp.bfloat16
VMEM_LIMIT = 56 * 1024 * 1024


def _dot(a, b):
    return jnp.dot(a, b, preferred_element_type=F32)


def _dot_nt(a, b):
    return lax.dot_general(a, b, (((1,), (1,)), ((), ())), preferred_element_type=F32)


def _dot_tn(a, b):
    return lax.dot_general(a, b, (((0,), (0,)), ((), ())), preferred_element_type=F32)


def _sigmoid(x):
    return 0.5 * jnp.tanh(0.5 * x) + 0.5


def _log2_sigmoid(x):
    return jnp.minimum(x, 0.0) * LOG2E - jnp.log2(1.0 + jnp.exp2(jnp.abs(x) * (-LOG2E)))


def _cumsum_rows(tri, g):
    g1 = g.astype(BF16)
    g2 = (g - g1.astype(F32)).astype(BF16)
    return _dot(tri, g1) + _dot(tri, g2)


def _tri(n, d):
    row = lax.broadcasted_iota(jnp.int32, (n, n), 0)
    col = lax.broadcasted_iota(jnp.int32, (n, n), 1)
    causal = (col <= row) if d == 0 else (col >= row)
    return causal, jnp.where(causal, 1.0, 0.0).astype(BF16)


def _resident(shape):
    n = len(shape)
    return pl.BlockSpec(shape, lambda *_: (0,) * n, pipeline_mode=pl.Buffered(1))


def _params(sem):
    return pltpu.CompilerParams(dimension_semantics=sem, vmem_limit_bytes=VMEM_LIMIT)


def _skewed(units, stages):
    for t in range(len(units) + len(stages) - 1):
        for s_idx, stage in enumerate(stages):
            if 0 <= t - s_idx < len(units):
                stage(*units[t - s_idx])


def _scan_units(n_sub):
    return [(d, j) for jj in range(n_sub) for d, j in ((0, jj), (1, n_sub - 1 - jj))]


def _ada_kernel(c_ref, w_ref, b_ref, o_ref):
    cv = c_ref[...]
    s = (cv * _sigmoid(cv)).astype(BF16)
    o_ref[...] = _dot(s, w_ref[...].astype(BF16)) + b_ref[...]


def _ada(cvec, w_ada, b_ada):
    tn = 1024
    return pl.pallas_call(
        _ada_kernel,
        out_shape=jax.ShapeDtypeStruct((8, N_MOD), F32),
        grid=(N_MOD // tn,),
        in_specs=[pl.BlockSpec((8, D), lambda j: (0, 0)),
                  pl.BlockSpec((D, tn), lambda j: (0, j)),
                  pl.BlockSpec((1, tn), lambda j: (0, j))],
        out_specs=pl.BlockSpec((8, tn), lambda j: (0, j)),
        compiler_params=_params(("arbitrary",)),
        name="adaln",
    )(cvec, w_ada, b_ada)


def _norm_mod_f32(x, g, sh, sc):
    return x * lax.rsqrt(jnp.mean(x * x, axis=-1, keepdims=True) + EPS) * (g * (1.0 + sc)) + sh


def _norm_mod(x, g, sh, sc):
    return _norm_mod_f32(x, g, sh, sc).astype(BF16)


def _act(v, kind):
    if kind is None:
        return v
    v = v.astype(BF16)
    s = _sigmoid(v)
    return v * s if kind == "silu" else s


def _inproj_kernel(x_ref, sh_ref, sc_ref, g_ref, *refs, acts, emit_u):
    uf = _norm_mod_f32(x_ref[...], g_ref[...], sh_ref[...], sc_ref[...])
    if emit_u:
        refs[-1][...] = uf
        refs = refs[:-1]
    n_out = len(refs) // 2
    u = uf.astype(BF16)
    for w_ref, o_ref, act in zip(refs[:n_out], refs[n_out:], acts):
        n = w_ref.shape[0]
        for jc, j in enumerate(range(0, n, 1024)):
            cs = slice(j, min(j + 1024, n))
            o_ref[:, cs] = _act(_dot_nt(u, w_ref[cs, :]), act[jc] if act else None).astype(o_ref.dtype)


def _inproj(x2, mod3, mod_row, norm_g, weights, out_dtypes, tm, acts=None, emit_u=False):
    m = x2.shape[0]
    acts = acts or (None,) * len(weights)
    widths = [w.shape[0] for w in weights] + ([D] if emit_u else [])
    dtypes = list(out_dtypes) + ([F32] if emit_u else [])
    return pl.pallas_call(
        functools.partial(_inproj_kernel, acts=acts, emit_u=emit_u),
        out_shape=tuple(jax.ShapeDtypeStruct((m, n), dt) for n, dt in zip(widths, dtypes)),
        grid=(m // tm,),
        in_specs=[pl.BlockSpec((tm, D), lambda i: (i, 0)),
                  pl.BlockSpec((None, 1, D), lambda i: (mod_row(i), 0, 0)),
                  pl.BlockSpec((None, 1, D), lambda i: (mod_row(i), 0, 1)),
                  pl.BlockSpec((1, D), lambda i: (0, 0))] + [_resident(w.shape) for w in weights],
        out_specs=tuple(pl.BlockSpec((tm, n), lambda i: (i, 0)) for n in widths),
        compiler_params=_params(("arbitrary",)),
        name="inproj",
    )(x2, mod3, mod3, norm_g, *weights)


def _inproj_m_kernel(*refs, colmajor, nblk):
    if colmajor:
        u_ref, hp_ref, hn_ref, wm_ref, wtg_ref, cw_ref, cb_ref = refs[:7]
        u = jnp.concatenate([u_ref[:, cl, :] for cl in range(COL_BLOCK)] + [hp_ref[7], hn_ref[0]],
                            axis=0).astype(BF16)
    else:
        x_ref, sh_ref, sc_ref, g_ref, wm_ref, wtg_ref, cw_ref, cb_ref = refs[:8]
        u = _norm_mod(x_ref[...], g_ref[...], sh_ref[...], sc_ref[...])
    k_ref, qt_ref, kt_ref, v_ref, tg_ref = refs[-5:]
    n = k_ref.shape[0]
    um = u[0:n]
    pres = [_dot_nt(u, wm_ref[c * 512:(c + 1) * 512, :]) for c in range(2)]
    v_ref[...] = _dot_nt(um, wm_ref[1024:2048, :]).astype(BF16)
    tg_ref[...] = _dot_nt(um, wtg_ref[...])
    row8 = lax.broadcasted_iota(jnp.int32, (8, 512), 0)
    j = pl.program_id(0) % nblk
    for c, pre in enumerate(pres):
        cs = slice(c * 512, (c + 1) * 512)
        a = pre[0:n]
        if colmajor:
            prev_row = jnp.where(j > 0, pre[n + 7:n + 8], 0.0)
            next_row = jnp.where(j < nblk - 1, pre[n + 8:n + 9], 0.0)
        else:
            prev_row = next_row = jnp.zeros((1, 512), F32)
        ap = pltpu.roll(a, 1, axis=0)
        ap = jnp.concatenate([jnp.where(row8 == 0, prev_row, ap[0:8]), ap[8:]], axis=0)
        an = pltpu.roll(a, n - 1, axis=0)
        an = jnp.concatenate([an[0:n - 8], jnp.where(row8 == 7, next_row, an[n - 8:])], axis=0)
        conv = ap * cw_ref[0:1, cs] + a * cw_ref[1:2, cs] + an * cw_ref[2:3, cs] + cb_ref[:, cs]
        y = conv * _sigmoid(conv)
        if c == 0:
            qt_ref[...] = y.T.astype(BF16)
        else:
            y = y * (DK ** -0.5)
            k_ref[...] = y.astype(BF16)
            kt_ref[...] = y.T.astype(BF16)


def _inproj_m(xv, mod3, mod_row, norm_g, wm, wtg, conv_w, conv_b, tm, colmajor):
    full = lambda shape: pl.BlockSpec(shape, lambda i: (0,) * len(shape))
    mod_specs = [pl.BlockSpec((None, 1, D), lambda i: (mod_row(i), 0, 0)),
                 pl.BlockSpec((None, 1, D), lambda i: (mod_row(i), 0, 1)), full((1, D))]
    mod_args = (mod3, mod3, norm_g)
    if colmajor:
        mod_specs, mod_args = [], ()
        bsz = xv.shape[0]
        tn = SEQ
        nblk = GRID_W // COL_BLOCK
        blk = (None, GRID_W, COL_BLOCK, D)
        halo = (None, 8, COL_BLOCK, D)
        x_specs = [pl.BlockSpec(blk, lambda i: (i // nblk, 0, i % nblk, 0)),
                   pl.BlockSpec(halo, lambda i: (i // nblk, GRID_W // 8 - 1, jnp.maximum(i % nblk - 1, 0), 0)),
                   pl.BlockSpec(halo, lambda i: (i // nblk, 0, jnp.minimum(i % nblk + 1, nblk - 1), 0))]
        xs = (xv, xv, xv)
    else:
        tn = tm
        bsz = xv.shape[0] // tn
        nblk = 1
        x_specs = [pl.BlockSpec((tm, D), lambda i: (i, 0))]
        xs = (xv,)
    tok = lambda w: pl.BlockSpec((None, tm, w), lambda i: (i // nblk, i % nblk, 0))
    tr = pl.BlockSpec((None, None, 512, tm), lambda i: (i // nblk, i % nblk, 0, 0))
    sds = jax.ShapeDtypeStruct
    return pl.pallas_call(
        functools.partial(_inproj_m_kernel, colmajor=colmajor, nblk=nblk),
        out_shape=(sds((bsz, tn, 512), BF16), sds((bsz, nblk, 512, tm), BF16), sds((bsz, nblk, 512, tm), BF16),
                   sds((bsz, tn, 1024), BF16), sds((bsz, tn, 256), F32)),
        grid=(bsz * nblk,),
        in_specs=x_specs + mod_specs + [_resident(wm.shape), _resident(wtg.shape),
                                        full((3, 1024)), full((1, 1024))],
        out_specs=(tok(512), tr, tr, tok(1024), tok(256)),
        compiler_params=_params(("arbitrary",)),
        name="inproj_m_cm" if colmajor else "inproj_m",
    )(*xs, *mod_args, wm, wtg, conv_w, conv_b)


def _sum_directions(acc_ref, out_ref, blk, rs, cs, val):
    tot = acc_ref[blk, rs, cs] + val
    acc_ref[blk, rs, cs] = tot
    out_ref[rs, cs] = tot.astype(out_ref.dtype)


def _gla_kernel(pf_ref, pb_ref, tf_ref, tb_ref, cp_ref, ct_ref, wup_ref, bdec_ref, olo_ref, ohi_ref,
                st_ref, acc_ref, *, ns):
    i = pl.program_id(1)

    @pl.when(i == 0)
    def _():
        st_ref[...] = jnp.zeros(st_ref.shape, F32)
        acc_ref[...] = jnp.zeros(acc_ref.shape, F32)
        ctx = (cp_ref, ct_ref, None)
        _gla_step((ctx, ctx), wup_ref, bdec_ref, st_ref, None, None, emit_out=False)

    _gla_step(((pf_ref, tf_ref, ohi_ref), (pb_ref, tb_ref, olo_ref)), wup_ref, bdec_ref, st_ref, acc_ref,
              (i, ns - 1 - i), emit_out=True)


def _gla_step(dirs, wup_ref, bdec_ref, st_ref, acc_ref, blks, *, emit_out):
    lc = GLA_CHUNK
    units = _scan_units(dirs[0][0].shape[0] // lc)
    masks =[_tri(lc, d) for d in range(2)]
    gs = []
    for d, (p_ref, t_ref, o_ref) in enumerate(dirs):
        z = _dot(t_ref[...].astype(BF16), wup_ref[d]) + bdec_ref[d]
        gs.append(_log2_sigmoid(z) * (1.0 / TAU))
    bs, ops, sc, us, dcols = {}, {}, {}, {}, {}
    st = {(d, h): st_ref[d * HEADS + h] for d in range(2) for h in range(HEADS)}

    def stage2(d, j):
        bs[d, j] = _cumsum_rows(masks[d][1], gs[d][j * lc:(j + 1) * lc])

    def stage3(d, j):
        p_ref = dirs[d][0]
        rs = slice(j * lc, (j + 1) * lc)
        b = bs[d, j]
        b_last = b[lc - 1:lc, :] if d == 0 else b[0:1, :]
        b_mid = b[lc // 2 - 1:lc // 2, :] if d == 0 else b[lc // 2:lc // 2 + 1, :]
        q = p_ref[rs, 0:512]
        k = p_ref[rs, 512:1024]
        qd = q * jnp.exp2(b - b_mid).astype(BF16)
        kd = k * jnp.exp2((b_mid + LOG2_QSCALE) - b).astype(BF16)
        qi = qd * jnp.exp2(b_mid).astype(BF16)
        kl = kd * jnp.exp2(b_last - b_mid).astype(BF16)
        dec = jnp.exp2(b_last)
        ops[d, j] = (qi, kl, dec, qd, kd)

    def stage4(d, j):
        p_ref = dirs[d][0]
        rs = slice(j * lc, (j + 1) * lc)
        qi, kl, dec, qd, kd = ops[d, j]
        for h in range(HEADS):
            ks = slice(h * DK, (h + 1) * DK)
            v = p_ref[rs, 1024 + h * DV:1024 + (h + 1) * DV]
            if emit_out:
                sc[d, j, h] = jnp.where(masks[d][0], _dot_nt(qd[:, ks], kd[:, ks]), 0.0).astype(BF16)
            us[d, j, h] = _dot_tn(kl[:, ks], v)
            dcols[d, j, h] = jnp.broadcast_to(dec[:, ks], (8, DK)).T[:, 0:1]
    def stage5(d, j):
        p_ref, _, o_ref = dirs[d]
        rs = slice(j * lc, (j + 1) * lc)
        for h in range(HEADS):
            ks = slice(h * DK, (h + 1) * DK)
            if emit_out:
                v = p_ref[rs, 1024 + h * DV:1024 + (h + 1) * DV]
                o = _dot(jnp.concatenate([sc[d, j, h], ops[d, j][0][:, ks]], axis=1),
                         jnp.concatenate([v, st[d, h].astype(BF16)], axis=0))
                _sum_directions(acc_ref, o_ref, blks[d], rs, slice(h * DV, (h + 1) * DV), o)
            st[d, h] = st[d, h] * dcols[d, j, h] + us[d, j, h]

    _skewed(units, (stage2, stage3, stage4, stage5))
    for (d, h), val in st.items():
        st_ref[d * HEADS + h] = val


def _half_specs(ns, step, width):
    half = ns // 2
    lo = pl.BlockSpec((None, step, width), lambda b, i: (b, jnp.minimum(ns - 1 - i, half - 1), 0))
    hi = pl.BlockSpec((None, step, width), lambda b, i: (b, jnp.maximum(i - half, 0), 0))
    return lo, hi


def _gla_scan(pg, tlr, pg_c, tlr_c, wup, bdec):
    bn, tn, _ = pg.shape
    tc = pg_c.shape[1]
    step = min(STEP, tn)
    ns = tn // step
    fwd = lambda b, i: (b, i, 0)
    bwd = lambda b, i: (b, ns - 1 - i, 0)
    ctx = lambda b, i: (b, 0, 0)
    o_shape = jax.ShapeDtypeStruct((bn, tn // 2, HEADS * DV), BF16)
    return pl.pallas_call(
        functools.partial(_gla_kernel, ns=ns),
        out_shape=(o_shape, o_shape),
        grid=(bn, ns),
        scratch_shapes=[pltpu.VMEM((2 * HEADS, DK, DV), F32), pltpu.VMEM((ns, step, HEADS * DV), F32)],
        in_specs=[pl.BlockSpec((None, step, 2048), fwd),
                  pl.BlockSpec((None, step, 2048), bwd),
                  pl.BlockSpec((None, step, 128), fwd),
                  pl.BlockSpec((None, step, 128), bwd),
                  pl.BlockSpec((None, tc, 2048), ctx),
                  pl.BlockSpec((None, tc, 128), ctx),
                  pl.BlockSpec((2, 128, 512), lambda b, i: (0, 0, 0)),
                  pl.BlockSpec((2, 1, 512), lambda b, i: (0, 0, 0))],
        out_specs=_half_specs(ns, step, HEADS * DV),
        compiler_params=_params(("arbitrary", "arbitrary")),
        name="gla_scan",
    )(pg, pg, tlr, tlr, pg_c, tlr_c, wup, bdec)


def _mlstm_kernel(kf_ref, kb_ref, qtf_ref, qtb_ref, ktf_ref, ktb_ref, vf_ref, vb_ref, tf_ref, tb_ref,
                  kc_ref, qtc_ref, ktc_ref, vc_ref, tc_ref, bg_ref, hlo_ref, hhi_ref,
                  c_ref, n_ref, m_ref, acc_ref, *, ns):
    i = pl.program_id(1)

    @pl.when(i == 0)
    def _():
        c_ref[...] = jnp.zeros(c_ref.shape, F32)
        n_ref[...] = jnp.zeros(n_ref.shape, F32)
        m_ref[...] = jnp.zeros(m_ref.shape, F32)
        acc_ref[...] = jnp.zeros(acc_ref.shape, F32)
        ctx = (kc_ref, qtc_ref, ktc_ref, vc_ref, tc_ref, None)
        _mlstm_step((ctx, ctx), bg_ref, c_ref, n_ref, m_ref, None, None, emit_out=False)

    _mlstm_step(((kf_ref, qtf_ref, ktf_ref, vf_ref, tf_ref, hhi_ref),
                 (kb_ref, qtb_ref, ktb_ref, vb_ref, tb_ref, hlo_ref)),
                bg_ref, c_ref, n_ref, m_ref, acc_ref, (i, ns - 1 - i), emit_out=True)


def _mlstm_step(dirs, bg_ref, c_ref, n_ref, m_ref, acc_ref, blks, *, emit_out):
    lc = MLSTM_CHUNK
    units = _scan_units(dirs[0][0].shape[0] // lc)
    tris =[_tri(lc, d) for d in range(2)]
    lane_of = lambda d, h: GATE_LANE + 8 * d + h
    hs = lambda h: slice(h * DK, (h + 1) * DK)
    vs = lambda h: slice(h * DV, (h + 1) * DV)

    mrow = [m_ref[d:d + 1, :] for d in range(2)]
    tiles = {}
    for d, j in units:
        t_ref = dirs[d][4]
        rs = slice(j * lc, (j + 1) * lc)
        ga = (t_ref[rs, 0:128] + bg_ref[:, 0:128]) * LOG2E
        gb = t_ref[rs, 128:256] + bg_ref[:, 128:256]
        bc = _cumsum_rows(tris[d][1], _log2_sigmoid(gb))
        b_last = bc[lc - 1:lc, :] if d == 0 else bc[0:1, :]
        log_key = b_last - bc + ga
        m_new = jnp.maximum(b_last + mrow[d], jnp.max(log_key, axis=0, keepdims=True))
        tiles[d, j] = dict(rmat=ga - bc, bct=bc.T, m_in=mrow[d], wkt=jnp.exp2(log_key - m_new).T,
                           decay=jnp.exp2(b_last + mrow[d] - m_new))
        mrow[d] = m_new
    for d in range(2):
        m_ref[d:d + 1, :] = mrow[d]

    us, ncols = {}, {}
    for d, j in units:
        k_ref, _, kt_ref, v_ref = dirs[d][:4]
        rs = slice(j * lc, (j + 1) * lc)
        for h in range(HEADS):
            lane = lane_of(d, h)
            wk = tiles[d, j]["wkt"][lane:lane + 1, :]
            kwt = kt_ref[hs(h), rs] * wk.astype(BF16)
            us[d, j, h] = _dot(kwt, v_ref[rs, vs(h)])
            ncols[d, j, h] = _dot(jnp.broadcast_to(wk, (16, lc)).astype(BF16), k_ref[rs, hs(h)])[0:1]

    n_in = {}
    for d in range(2):
        for h in range(HEADS):
            idx = d * HEADS + h
            lane = lane_of(d, h)
            nvec = n_ref[idx:idx + 1, :]
            for dd, j in units:
                if dd == d:
                    n_in[d, j, h] = nvec
                    nvec = tiles[d, j]["decay"][:, lane:lane + 1] * nvec + ncols[d, j, h]
            n_ref[idx:idx + 1, :] = nvec

    lhs = {}
    if emit_out:
        for d, j in units:
            k_ref, qt_ref = dirs[d][:2]
            rs = slice(j * lc, (j + 1) * lc)
            t = tiles[d, j]
            causal_t = tris[1 - d][0]
            for h in range(HEADS):
                lane = lane_of(d, h)
                qt = qt_ref[hs(h), rs]
                kq = _dot(jnp.concatenate(
                    [k_ref[rs, hs(h)], jnp.broadcast_to(n_in[d, j, h], (16, DK)).astype(BF16)], axis=0), qt)
                rm = jnp.where(causal_t, t["rmat"][:, lane:lane + 1], -jnp.inf)
                mval = t["m_in"][:, lane:lane + 1]
                mx = jnp.maximum(mval, jnp.max(rm, axis=0, keepdims=True))
                wt = jnp.exp2(rm - mx) * kq[0:lc]
                w_inter = jnp.exp2(mval - mx)
                den = jnp.sum(wt, axis=0, keepdims=True) + w_inter * kq[lc:lc + 1]
                inv = 1.0 / jnp.maximum(jnp.abs(den), jnp.exp2(-(t["bct"][lane:lane + 1, :] + mx)))
                lhs[d, j, h] = jnp.concatenate(
                    [(wt * inv).astype(BF16), qt * (w_inter * inv).astype(BF16)], axis=0)

    for d in range(2):
        v_ref, o_ref = dirs[d][3], dirs[d][5]
        for h in range(HEADS):
            idx = d * HEADS + h
            lane = lane_of(d, h)
            cmat = c_ref[idx]
            for dd, j in units:
                if dd != d:
                    continue
                rs = slice(j * lc, (j + 1) * lc)
                if emit_out:
                    o = _dot_tn(lhs[d, j, h], jnp.concatenate([v_ref[rs, vs(h)], cmat.astype(BF16)], axis=0))
                    _sum_directions(acc_ref, o_ref, blks[d], rs, vs(h), o)
                cmat = tiles[d, j]["decay"][:, lane:lane + 1] * cmat + us[d, j, h]
            c_ref[idx] = cmat


def _mlstm_scan(lat, ctx, bgate):
    k, qt, kt, v, tg = lat
    bn, tn, _ = k.shape
    tc = ctx[0].shape[1]
    step = min(STEP, tn)
    ns = tn // step
    fwd = lambda b, i: (b, i, 0)
    bwd = lambda b, i: (b, ns - 1 - i, 0)
    assert qt.shape[1:] == (ns, 512, step) and ctx[1].shape[1:] == (1, 512, tc)
    fwd_t = lambda b, i: (b, i, 0, 0)
    bwd_t = lambda b, i: (b, ns - 1 - i, 0, 0)
    whole = lambda b, i: (b, 0, 0)
    o_shape = jax.ShapeDtypeStruct((bn, tn // 2, HEADS * DV), F32)
    both = lambda shape, f, g: [pl.BlockSpec(shape, f), pl.BlockSpec(shape, g)]
    whole_t = lambda b, i: (b, 0, 0, 0)
    ctx_specs = [pl.BlockSpec((None, tc, 512), whole), pl.BlockSpec((None, None, 512, tc), whole_t),
                 pl.BlockSpec((None, None, 512, tc), whole_t), pl.BlockSpec((None, tc, 1024), whole),
                 pl.BlockSpec((None, tc, 256), whole)]
    return pl.pallas_call(
        functools.partial(_mlstm_kernel, ns=ns),
        out_shape=(o_shape, o_shape),
        grid=(bn, ns),
        scratch_shapes=[pltpu.VMEM((2 * HEADS, DK, DV), F32), pltpu.VMEM((2 * HEADS, 128), F32),
                        pltpu.VMEM((8, 128), F32), pltpu.VMEM((ns, step, HEADS * DV), F32)],
        in_specs=(both((None, step, 512), fwd, bwd) + both((None, None, 512, step), fwd_t, bwd_t)
                  + both((None, None, 512, step), fwd_t, bwd_t) + both((None, step, 1024), fwd, bwd)
                  + both((None, step, 256), fwd, bwd) + ctx_specs
                  + [pl.BlockSpec((1, 256), lambda b, i: (0, 0))]),
        out_specs=_half_specs(ns, step, HEADS * DV),
        compiler_params=_params(("arbitrary", "arbitrary")),
        name="mlstm_scan",
    )(k, k, qt, qt, kt, kt, v, v, tg, tg, *ctx, bgate)


def _head_norm(o, g):
    parts = []
    for h in range(HEADS):
        oh = o[:, h * DV:(h + 1) * DV]
        parts.append(oh * lax.rsqrt(jnp.mean(oh * oh, axis=-1, keepdims=True) + EPS))
    return jnp.concatenate(parts, axis=-1) * g


def _merge_kernel(olo_ref, ohi_ref, hlo_ref, hhi_ref, po_ref, x_ref, g1_ref, gg_ref, gm_ref,
                  wbg_ref, wbm_ref, wo_ref, o_ref, *, per_b):
    groups = [(g * COL_BLOCK // MERGE_GROUPS, (g + 1) * COL_BLOCK // MERGE_GROUPS) for g in range(MERGE_GROUPS)]
    rows = [slice(a * GRID_W, b * GRID_W) for a, b in groups]
    lower = pl.program_id(0) % per_b < per_b // 2
    ys = []
    for (a, b), rs in zip(groups, rows):
        hm = jnp.concatenate([r[:, rl, :] for rl in range(a, b) for r in (hlo_ref, hhi_ref)], axis=0)
        o = jnp.where(lower, olo_ref[rs, :], ohi_ref[rs, :]).astype(F32)
        y_gla = _head_norm(o, gg_ref[...]) * po_ref[rs, 0:1024].astype(F32)
        y_m = _head_norm(hm, gm_ref[...]) * po_ref[rs, 1024:2048].astype(F32)
        ys.append((y_gla.astype(BF16), y_m.astype(BF16)))
    ds = [(_dot(y_gla, wbg_ref[...]), _dot(y_m, wbm_ref[...])) for y_gla, y_m in ys]
    ys = [(po_ref[rs, 2048:3072].astype(F32) * d_g + po_ref[rs, 3072:4096].astype(F32) * d_m).astype(BF16)
          for rs, (d_g, d_m) in zip(rows, ds)]
    mixes = [_dot(y, wo_ref[...]) for y in ys]
    for rs, mix in zip(rows, mixes):
        o_ref[rs, :] = x_ref[rs, :] + g1_ref[...] * mix


def _merge(olo, ohi, hlo4, hhi4, po, x2, mod3, gg, gm, wbg, wbm, wo):
    m = x2.shape[0]
    tm = GRID_W * COL_BLOCK
    per_b = SEQ // tm
    half = per_b // 2
    tok = lambda i: (i, 0)
    hspec = pl.BlockSpec((None, GRID_W // 2, COL_BLOCK, D), lambda i: (i // per_b, 0, i % per_b, 0))
    lo_spec = pl.BlockSpec((None, tm, D), lambda i: (i // per_b, jnp.minimum(i % per_b, half - 1), 0))
    hi_spec = pl.BlockSpec((None, tm, D), lambda i: (i // per_b, jnp.maximum(i % per_b - half, 0), 0))
    return pl.pallas_call(
        functools.partial(_merge_kernel, per_b=per_b),
        out_shape=jax.ShapeDtypeStruct((m, D), F32),
        grid=(m // tm,),
        in_specs=[lo_spec, hi_spec, hspec, hspec,
                  pl.BlockSpec((tm, 4096), tok), pl.BlockSpec((tm, D), tok),
                  pl.BlockSpec((None, 1, D), lambda i: (i // per_b, 0, 2)),
                  pl.BlockSpec((1, D), lambda i: (0, 0)), pl.BlockSpec((1, D), lambda i: (0, 0)),
                  _resident((D, D)), _resident((D, D)), _resident((D, D))],
        out_specs=pl.BlockSpec((tm, D), tok),
        compiler_params=_params(("arbitrary",)),
        name="merge",
    )(olo, ohi, hlo4, hhi4, po, x2, mod3, gg, gm, wbg, wbm, wo)


FF_TILES = ((0, 1280), (1280, 2816))


def _ffn_kernel(x_ref, sh_ref, sc_ref, g2_ref, ng_ref, fg_ref, wi_ref, wo_ref, o_ref):
    x = x_ref[...]
    u = _norm_mod(x, ng_ref[...], sh_ref[...], sc_ref[...])
    acc = None
    for lo, hi in FF_TILES:
        a = _dot(u, wi_ref[:, lo:hi])
        b = _dot(u, wi_ref[:, D_FF + lo:D_FF + hi])
        hid = (a * _sigmoid(a) * b).astype(BF16)
        part = _dot(hid, wo_ref[lo:hi, :])
        acc = part if acc is None else acc + part
    x2 = x + g2_ref[...] * acc
    o_ref[...] = x2 * lax.rsqrt(jnp.mean(x2 * x2, axis=-1, keepdims=True) + EPS) * fg_ref[...]


def _ffn(x1, mod3, ng, fg, wi, wo, tm):
    m = x1.shape[0]
    per_b = SEQ // tm
    tok = lambda i: (i, 0)
    modspec = lambda c: pl.BlockSpec((None, 1, D), lambda i: (i // per_b, 0, c))
    return pl.pallas_call(
        _ffn_kernel,
        out_shape=jax.ShapeDtypeStruct((m, D), F32),
        grid=(m // tm,),
        in_specs=[pl.BlockSpec((tm, D), tok), modspec(3), modspec(4), modspec(5),
                  pl.BlockSpec((1, D), lambda i: (0, 0)), pl.BlockSpec((1, D), lambda i: (0, 0)),
                  _resident((D, 2 * D_FF)), _resident((D_FF, D))],
        out_specs=pl.BlockSpec((tm, D), tok),
        compiler_params=_params(("arbitrary",)),
        name="ffn",
    )(x1, mod3, mod3, mod3, ng, fg, wi, wo)


def _split_w_in_kernel(w_ref, wg_ref, wm_ref, wo_ref, wtl_ref, wtg_ref):
    cols = w_ref.shape[1]
    c = lambda a, b: w_ref[a:b, :].astype(BF16)
    z = lambda n: jnp.zeros((n, cols), BF16)
    wg_ref[...] = c(0, 2048)
    wm_ref[...] = c(3104, 5152)
    wo_ref[0:1024, :] = c(2048, 3072)
    wo_ref[1024:2048, :] = c(5152, 6176)
    wo_ref[2048:4096, :] = c(6192, 8240)
    wtl_ref[...] = jnp.concatenate([c(3072, 3104), z(128 - 2 * RANK)], axis=0)
    wtg_ref[...] = jnp.concatenate(
        [z(GATE_LANE), c(6176, 6192), z(128 - GATE_LANE - 16),
         z(GATE_LANE), c(6180, 6184), z(4), c(6188, 6192), z(128 - GATE_LANE - 12)], axis=0)


def _split_w_in(w_in_t):
    tc = 256
    heights = (2048, 2048, 4096, 128, 256)
    return pl.pallas_call(
        _split_w_in_kernel,
        out_shape=tuple(jax.ShapeDtypeStruct((n, D), BF16) for n in heights),
        grid=(D // tc,),
        in_specs=[pl.BlockSpec((w_in_t.shape[0], tc), lambda i: (0, i))],
        out_specs=tuple(pl.BlockSpec((n, tc), lambda i: (0, i)) for n in heights),
        compiler_params=_params(("arbitrary",)),
        name="split_w_in",
    )(w_in_t)


def kernel(x, c, ctx, c_ctx, w_ada, b_ada, norm1_g, w_in, gla_w_up, gla_b_dec, gla_norm_g,
           mlstm_conv_w, mlstm_conv_b, mlstm_b_gate, mlstm_norm_g, w_br_gla, w_br_mlstm, w_out,
           norm2_g, w_ffn_in, w_ffn_out, final_g):
    bsz = x.shape[0]
    row = lambda a: a.reshape(1, -1)

    cvec = jnp.concatenate([c, c_ctx[None, :], jnp.zeros((8 - bsz - 1, D), F32)], axis=0)
    mod3 = _ada(cvec, w_ada[0], row(b_ada[0])).reshape(8, 1, N_MOD)

    wg, wm, wo, wtl, wtg = _split_w_in(w_in[0].T)
    x2 = x.reshape(bsz * SEQ, D)
    ctx2 = ctx.reshape(bsz * CTX, D)
    g1n = row(norm1_g[0])
    tm = GRID_W * COL_BLOCK
    po, pg, tlr, u_lat = _inproj(x2, mod3, lambda i: i // (SEQ // tm), g1n, (wo, wg, wtl), (BF16, BF16, F32), tm,
                                 acts=(("silu", "sigmoid", "sigmoid", "sigmoid"), None, None), emit_u=True)
    conv_w = mlstm_conv_w[0]
    conv_b = row(mlstm_conv_b[0])
    m_lat = _inproj_m(u_lat.reshape(bsz, SEQ // GRID_W, GRID_W, D), None, None, None,
                      wm, wtg, conv_w, conv_b, tm, True)
    pg_c, tlr_c = _inproj(ctx2, mod3, lambda i: bsz, g1n, (wg, wtl), (BF16, F32), CTX)
    m_ctx = _inproj_m(ctx2, mod3, lambda i: bsz, g1n, wm, wtg, conv_w, conv_b, CTX, False)

    wup = jnp.zeros((2, 128, HEADS * DK), F32)
    wup = wup.at[0, 0:RANK].set(gla_w_up[0, 0]).at[1, RANK:2 * RANK].set(gla_w_up[0, 1]).astype(BF16)
    bdec = gla_b_dec[0].reshape(2, 1, HEADS * DK)
    o_lo, o_hi = _gla_scan(pg.reshape(bsz, SEQ, 2048), tlr.reshape(bsz, SEQ, 128),
                           pg_c.reshape(bsz, CTX, 2048), tlr_c.reshape(bsz, CTX, 128), wup, bdec)

    bgate = mlstm_b_gate[0].reshape(1, 16)
    zg = lambda n: jnp.zeros((1, n), F32)
    bg2 = jnp.concatenate([zg(GATE_LANE), bgate, zg(128 - GATE_LANE - 16),
                           zg(GATE_LANE), bgate[:, 4:8], zg(4), bgate[:, 12:16],
                           zg(128 - GATE_LANE - 12)], axis=1)
    h_lo, h_hi = _mlstm_scan(m_lat, m_ctx, bg2)

    cm4 = lambda a: a.reshape(bsz, GRID_W // 2, SEQ // GRID_W, D)
    x1 = _merge(o_lo, o_hi, cm4(h_lo), cm4(h_hi), po, x2, mod3,
                row(gla_norm_g[0]), row(mlstm_norm_g[0]),
                w_br_gla[0].astype(BF16), w_br_mlstm[0].astype(BF16), w_out[0].astype(BF16))
    out = _ffn(x1, mod3, row(norm2_g[0]), row(final_g),
               w_ffn_in[0].astype(BF16), w_ffn_out[0].astype(BF16), tm)
    return out.reshape(bsz, SEQ, D)
```

```python
import functools
import math

import jax
import jax.numpy as jnp
from jax import lax
from jax.experimental import pallas as pl
from jax.experimental.pallas import tpu as pltpu

D = 1024
SEQ = 4096
CTX = 256
GRID_W = 64
EPS = 1e-6
HEADS = 4
DK = 128
DV = 256
RANK = 16
TAU = 16.0
D_FF = 2816
N_MOD = 6 * D
GATE_LANE = 32
STEP = 512
GLA_CHUNK = 128
MLSTM_CHUNK = 128
COL_BLOCK = 8
MERGE_GROUPS = 2
FFN_GROUPS = 2

LOG2E = math.log2(math.e)
LOG2_QSCALE = -0.5 * math.log2(DK)

F32 = jnp.float32
BF16 = jnp.bfloat16
VMEM_LIMIT = 56 * 1024 * 1024


def _dot(a, b):
    return jnp.dot(a, b, preferred_element_type=F32)


def _dot_nt(a, b):
    return lax.dot_general(a, b, (((1,), (1,)), ((), ())), preferred_element_type=F32)


def _dot_tn(a, b):
    return lax.dot_general(a, b, (((0,), (0,)), ((), ())), preferred_element_type=F32)


def _sigmoid(x):
    return 0.5 * jnp.tanh(0.5 * x) + 0.5


def _log2_sigmoid(x):
    return jnp.minimum(x, 0.0) * LOG2E - jnp.log2(1.0 + jnp.exp2(jnp.abs(x) * (-LOG2E)))


def _cumsum_rows(tri, g):
    g1 = g.astype(BF16)
    g2 = (g - g1.astype(F32)).astype(BF16)
    return _dot(tri, g1) + _dot(tri, g2)


def _tri(n, d):
    row = lax.broadcasted_iota(jnp.int32, (n, n), 0)
    col = lax.broadcasted_iota(jnp.int32, (n, n), 1)
    causal = (col <= row) if d == 0 else (col >= row)
    return causal, jnp.where(causal, 1.0, 0.0).astype(BF16)


def _resident(shape):
    n = len(shape)
    return pl.BlockSpec(shape, lambda *_: (0,) * n, pipeline_mode=pl.Buffered(1))


def _params(sem):
    return pltpu.CompilerParams(dimension_semantics=sem, vmem_limit_bytes=VMEM_LIMIT)


def _skewed(units, stages):
    for t in range(len(units) + len(stages) - 1):
        for s_idx, stage in enumerate(stages):
            if 0 <= t - s_idx < len(units):
                stage(*units[t - s_idx])


def _scan_units(n_sub):
    return [(d, j) for jj in range(n_sub) for d, j in ((0, jj), (1, n_sub - 1 - jj))]


def _ada_kernel(c_ref, w_ref, b_ref, o_ref):
    cv = c_ref[...]
    s = (cv * _sigmoid(cv)).astype(BF16)
    o_ref[...] = _dot(s, w_ref[...].astype(BF16)) + b_ref[...]


def _ada(cvec, w_ada, b_ada):
    tn = 1024
    return pl.pallas_call(
        _ada_kernel,
        out_shape=jax.ShapeDtypeStruct((8, N_MOD), F32),
        grid=(N_MOD // tn,),
        in_specs=[pl.BlockSpec((8, D), lambda j: (0, 0)),
                  pl.BlockSpec((D, tn), lambda j: (0, j)),
                  pl.BlockSpec((1, tn), lambda j: (0, j))],
        out_specs=pl.BlockSpec((8, tn), lambda j: (0, j)),
        compiler_params=_params(("arbitrary",)),
        name="adaln",
    )(cvec, w_ada, b_ada)


def _norm_mod_f32(x, g, sh, sc):
    return x * lax.rsqrt(jnp.mean(x * x, axis=-1, keepdims=True) + EPS) * (g * (1.0 + sc)) + sh


def _norm_mod(x, g, sh, sc):
    return _norm_mod_f32(x, g, sh, sc).astype(BF16)


def _act(v, kind):
    if kind is None:
        return v
    v = v.astype(BF16)
    s = _sigmoid(v)
    return v * s if kind == "silu" else s


def _inproj_kernel(x_ref, sh_ref, sc_ref, g_ref, *refs, acts, emit_u):
    uf = _norm_mod_f32(x_ref[...], g_ref[...], sh_ref[...], sc_ref[...])
    if emit_u:
        refs[-1][...] = uf
        refs = refs[:-1]
    n_out = len(refs) // 2
    u = uf.astype(BF16)
    for w_ref, o_ref, act in zip(refs[:n_out], refs[n_out:], acts):
        n = w_ref.shape[0]
        for jc, j in enumerate(range(0, n, 1024)):
            cs = slice(j, min(j + 1024, n))
            o_ref[:, cs] = _act(_dot_nt(u, w_ref[cs, :]), act[jc] if act else None).astype(o_ref.dtype)


def _inproj(x2, mod3, mod_row, norm_g, weights, out_dtypes, tm, acts=None, emit_u=False):
    m = x2.shape[0]
    acts = acts or (None,) * len(weights)
    widths = [w.shape[0] for w in weights] + ([D] if emit_u else [])
    dtypes = list(out_dtypes) + ([F32] if emit_u else [])
    return pl.pallas_call(
        functools.partial(_inproj_kernel, acts=acts, emit_u=emit_u),
        out_shape=tuple(jax.ShapeDtypeStruct((m, n), dt) for n, dt in zip(widths, dtypes)),
        grid=(m // tm,),
        in_specs=[pl.BlockSpec((tm, D), lambda i: (i, 0)),
                  pl.BlockSpec((None, 1, D), lambda i: (mod_row(i), 0, 0)),
                  pl.BlockSpec((None, 1, D), lambda i: (mod_row(i), 0, 1)),
                  pl.BlockSpec((1, D), lambda i: (0, 0))] + [_resident(w.shape) for w in weights],
        out_specs=tuple(pl.BlockSpec((tm, n), lambda i: (i, 0)) for n in widths),
        compiler_params=_params(("arbitrary",)),
        name="inproj",
    )(x2, mod3, mod3, norm_g, *weights)


def _inproj_m_kernel(*refs, colmajor, nblk):
    if colmajor:
        u_ref, hp_ref, hn_ref, wm_ref, wtg_ref, cw_ref, cb_ref = refs[:7]
        u = jnp.concatenate([u_ref[:, cl, :] for cl in range(COL_BLOCK)] + [hp_ref[7], hn_ref[0]],
                            axis=0).astype(BF16)
        k_ref, qt_ref, kt_ref, v_ref, tg_ref = refs[7:]
    else:
        x_ref, sh_ref, sc_ref, g_ref, wm_ref, wtg_ref, cw_ref, cb_ref, wg_ref, wtl_ref = refs[:10]
        k_ref, qt_ref, kt_ref, v_ref, tg_ref, pg_ref, tlr_ref = refs[10:]
        u = _norm_mod(x_ref[...], g_ref[...], sh_ref[...], sc_ref[...])
        pg_ref[...] = _dot_nt(u, wg_ref[...]).astype(BF16)
        tlr_ref[...] = _dot_nt(u, wtl_ref[...])
    n = k_ref.shape[0]
    um = u[0:n]
    pres = [_dot_nt(u, wm_ref[c * 512:(c + 1) * 512, :]) for c in range(2)]
    v_ref[...] = _dot_nt(um, wm_ref[1024:2048, :]).astype(BF16)
    tg_ref[...] = _dot_nt(um, wtg_ref[...])
    row8 = lax.broadcasted_iota(jnp.int32, (8, 512), 0)
    j = pl.program_id(0) % nblk
    for c, pre in enumerate(pres):
        cs = slice(c * 512, (c + 1) * 512)
        a = pre[0:n]
        if colmajor:
            prev_row = jnp.where(j > 0, pre[n + 7:n + 8], 0.0)
            next_row = jnp.where(j < nblk - 1, pre[n + 8:n + 9], 0.0)
        else:
            prev_row = next_row = jnp.zeros((1, 512), F32)
        ap = pltpu.roll(a, 1, axis=0)
        ap = jnp.concatenate([jnp.where(row8 == 0, prev_row, ap[0:8]), ap[8:]], axis=0)
        an = pltpu.roll(a, n - 1, axis=0)
        an = jnp.concatenate([an[0:n - 8], jnp.where(row8 == 7, next_row, an[n - 8:])], axis=0)
        conv = ap * cw_ref[0:1, cs] + a * cw_ref[1:2, cs] + an * cw_ref[2:3, cs] + cb_ref[:, cs]
        y = conv * _sigmoid(conv)
        if c == 0:
            qt_ref[...] = y.T.astype(BF16)
        else:
            y = y * (DK ** -0.5)
            k_ref[...] = y.astype(BF16)
            kt_ref[...] = y.T.astype(BF16)


def _inproj_m(xv, mod3, mod_row, norm_g, wm, wtg, conv_w, conv_b, tm, colmajor, gla_weights=()):
    full = lambda shape: pl.BlockSpec(shape, lambda i: (0,) * len(shape))
    mod_specs = [pl.BlockSpec((None, 1, D), lambda i: (mod_row(i), 0, 0)),
                 pl.BlockSpec((None, 1, D), lambda i: (mod_row(i), 0, 1)), full((1, D))]
    mod_args = (mod3, mod3, norm_g)
    if colmajor:
        mod_specs, mod_args = [], ()
        bsz = xv.shape[0]
        tn = SEQ
        nblk = GRID_W // COL_BLOCK
        blk = (None, GRID_W, COL_BLOCK, D)
        halo = (None, 8, COL_BLOCK, D)
        x_specs = [pl.BlockSpec(blk, lambda i: (i // nblk, 0, i % nblk, 0)),
                   pl.BlockSpec(halo, lambda i: (i // nblk, GRID_W // 8 - 1, jnp.maximum(i % nblk - 1, 0), 0)),
                   pl.BlockSpec(halo, lambda i: (i // nblk, 0, jnp.minimum(i % nblk + 1, nblk - 1), 0))]
        xs = (xv, xv, xv)
    else:
        tn = tm
        bsz = xv.shape[0] // tn
        nblk = 1
        x_specs = [pl.BlockSpec((tm, D), lambda i: (i, 0))]
        xs = (xv,)
    extra_shapes = tuple(jax.ShapeDtypeStruct((bsz, tn, w.shape[0]), dt) for w, dt in zip(gla_weights, (BF16, F32)))
    tok = lambda w: pl.BlockSpec((None, tm, w), lambda i: (i // nblk, i % nblk, 0))
    tr = pl.BlockSpec((None, None, 512, tm), lambda i: (i // nblk, i % nblk, 0, 0))
    sds = jax.ShapeDtypeStruct
    return pl.pallas_call(
        functools.partial(_inproj_m_kernel, colmajor=colmajor, nblk=nblk),
        out_shape=(sds((bsz, tn, 512), BF16), sds((bsz, nblk, 512, tm), BF16), sds((bsz, nblk, 512, tm), BF16),
                   sds((bsz, tn, 1024), BF16), sds((bsz, tn, 256), F32)) + extra_shapes,
        grid=(bsz * nblk,),
        in_specs=x_specs + mod_specs + [_resident(wm.shape), _resident(wtg.shape), full((3, 1024)),
                                        full((1, 1024))] + [_resident(w.shape) for w in gla_weights],
        out_specs=(tok(512), tr, tr, tok(1024), tok(256)) + tuple(tok(w.shape[0]) for w in gla_weights),
        compiler_params=_params(("arbitrary",)),
        name="inproj_m_cm" if colmajor else "inproj_m",
    )(*xs, *mod_args, wm, wtg, conv_w, conv_b, *gla_weights)


def _sum_directions(acc_ref, out_ref, blk, rs, cs, val):
    tot = acc_ref[blk, rs, cs] + val
    acc_ref[blk, rs, cs] = tot
    out_ref[rs, cs] = tot.astype(out_ref.dtype)


def _gla_kernel(pf_ref, pb_ref, tf_ref, tb_ref, cp_ref, ct_ref, wup_ref, bdec_ref, olo_ref, ohi_ref,
                st_ref, acc_ref, *, ns):
    i = pl.program_id(1)

    @pl.when(i == 0)
    def _():
        st_ref[...] = jnp.zeros(st_ref.shape, F32)
        acc_ref[...] = jnp.zeros(acc_ref.shape, F32)
        ctx = (cp_ref, ct_ref, None)
        _gla_step((ctx, ctx), wup_ref, bdec_ref, st_ref, None, None, emit_out=False)

    _gla_step(((pf_ref, tf_ref, ohi_ref), (pb_ref, tb_ref, olo_ref)), wup_ref, bdec_ref, st_ref, acc_ref,
              (i, ns - 1 - i), emit_out=True)


def _gla_step(dirs, wup_ref, bdec_ref, st_ref, acc_ref, blks, *, emit_out):
    lc = GLA_CHUNK
    units = _scan_units(dirs[0][0].shape[0] // lc)
    masks =[_tri(lc, d) for d in range(2)]
    gs = []
    for d, (p_ref, t_ref, o_ref) in enumerate(dirs):
        z = _dot(t_ref[...].astype(BF16), wup_ref[d]) + bdec_ref[d]
        gs.append(_log2_sigmoid(z) * (1.0 / TAU))
    bs, ops, sc, us, dcols = {}, {}, {}, {}, {}
    st = {(d, h): st_ref[d * HEADS + h] for d in range(2) for h in range(HEADS)}

    def stage2(d, j):
        bs[d, j] = _cumsum_rows(masks[d][1], gs[d][j * lc:(j + 1) * lc])

    def stage3(d, j):
        p_ref = dirs[d][0]
        rs = slice(j * lc, (j + 1) * lc)
        b = bs[d, j]
        b_last = b[lc - 1:lc, :] if d == 0 else b[0:1, :]
        b_mid = b[lc // 2 - 1:lc // 2, :] if d == 0 else b[lc // 2:lc // 2 + 1, :]
        q = p_ref[rs, 0:512]
        k = p_ref[rs, 512:1024]
        qd = q * jnp.exp2(b - b_mid).astype(BF16)
        kd = k * jnp.exp2((b_mid + LOG2_QSCALE) - b).astype(BF16)
        qi = qd * jnp.exp2(b_mid).astype(BF16)
        kl = kd * jnp.exp2(b_last - b_mid).astype(BF16)
        dec = jnp.exp2(b_last)
        ops[d, j] = (qi, kl, dec, qd, kd)

    def stage4(d, j):
        p_ref = dirs[d][0]
        rs = slice(j * lc, (j + 1) * lc)
        qi, kl, dec, qd, kd = ops[d, j]
        for h in range(HEADS):
            ks = slice(h * DK, (h + 1) * DK)
            v = p_ref[rs, 1024 + h * DV:1024 + (h + 1) * DV]
            if emit_out:
                sc[d, j, h] = jnp.where(masks[d][0], _dot_nt(qd[:, ks], kd[:, ks]), 0.0).astype(BF16)
            us[d, j, h] = _dot_tn(kl[:, ks], v)
            dcols[d, j, h] = jnp.broadcast_to(dec[:, ks], (8, DK)).T[:, 0:1]
    def stage5(d, j):
        p_ref, _, o_ref = dirs[d]
        rs = slice(j * lc, (j + 1) * lc)
        for h in range(HEADS):
            ks = slice(h * DK, (h + 1) * DK)
            if emit_out:
                v = p_ref[rs, 1024 + h * DV:1024 + (h + 1) * DV]
                o = _dot(jnp.concatenate([sc[d, j, h], ops[d, j][0][:, ks]], axis=1),
                         jnp.concatenate([v, st[d, h].astype(BF16)], axis=0))
                _sum_directions(acc_ref, o_ref, blks[d], rs, slice(h * DV, (h + 1) * DV), o)
            st[d, h] = st[d, h] * dcols[d, j, h] + us[d, j, h]

    _skewed(units, (stage2, stage3, stage4, stage5))
    for (d, h), val in st.items():
        st_ref[d * HEADS + h] = val


def _half_specs(ns, step, width):
    half = ns // 2
    lo = pl.BlockSpec((None, step, width), lambda b, i: (b, jnp.minimum(ns - 1 - i, half - 1), 0))
    hi = pl.BlockSpec((None, step, width), lambda b, i: (b, jnp.maximum(i - half, 0), 0))
    return lo, hi


def _gla_scan(pg, tlr, pg_c, tlr_c, wup, bdec):
    bn, tn, _ = pg.shape
    tc = pg_c.shape[1]
    step = min(STEP, tn)
    ns = tn // step
    fwd = lambda b, i: (b, i, 0)
    bwd = lambda b, i: (b, ns - 1 - i, 0)
    ctx = lambda b, i: (b, 0, 0)
    o_shape = jax.ShapeDtypeStruct((bn, tn // 2, HEADS * DV), BF16)
    return pl.pallas_call(
        functools.partial(_gla_kernel, ns=ns),
        out_shape=(o_shape, o_shape),
        grid=(bn, ns),
        scratch_shapes=[pltpu.VMEM((2 * HEADS, DK, DV), F32), pltpu.VMEM((ns, step, HEADS * DV), F32)],
        in_specs=[pl.BlockSpec((None, step, 2048), fwd),
                  pl.BlockSpec((None, step, 2048), bwd),
                  pl.BlockSpec((None, step, 128), fwd),
                  pl.BlockSpec((None, step, 128), bwd),
                  pl.BlockSpec((None, tc, 2048), ctx),
                  pl.BlockSpec((None, tc, 128), ctx),
                  pl.BlockSpec((2, 128, 512), lambda b, i: (0, 0, 0)),
                  pl.BlockSpec((2, 1, 512), lambda b, i: (0, 0, 0))],
        out_specs=_half_specs(ns, step, HEADS * DV),
        compiler_params=_params(("arbitrary", "arbitrary")),
        name="gla_scan",
    )(pg, pg, tlr, tlr, pg_c, tlr_c, wup, bdec)


def _mlstm_kernel(kf_ref, kb_ref, qtf_ref, qtb_ref, ktf_ref, ktb_ref, vf_ref, vb_ref, tf_ref, tb_ref,
                  kc_ref, qtc_ref, ktc_ref, vc_ref, tc_ref, bg_ref, hlo_ref, hhi_ref,
                  c_ref, n_ref, m_ref, acc_ref, *, ns):
    i = pl.program_id(1)

    @pl.when(i == 0)
    def _():
        c_ref[...] = jnp.zeros(c_ref.shape, F32)
        n_ref[...] = jnp.zeros(n_ref.shape, F32)
        m_ref[...] = jnp.zeros(m_ref.shape, F32)
        acc_ref[...] = jnp.zeros(acc_ref.shape, F32)
        ctx = (kc_ref, qtc_ref, ktc_ref, vc_ref, tc_ref, None)
        _mlstm_step((ctx, ctx), bg_ref, c_ref, n_ref, m_ref, None, None, emit_out=False)

    _mlstm_step(((kf_ref, qtf_ref, ktf_ref, vf_ref, tf_ref, hhi_ref),
                 (kb_ref, qtb_ref, ktb_ref, vb_ref, tb_ref, hlo_ref)),
                bg_ref, c_ref, n_ref, m_ref, acc_ref, (i, ns - 1 - i), emit_out=True)


def _mlstm_step(dirs, bg_ref, c_ref, n_ref, m_ref, acc_ref, blks, *, emit_out):
    lc = MLSTM_CHUNK
    units = _scan_units(dirs[0][0].shape[0] // lc)
    tris =[_tri(lc, d) for d in range(2)]
    lane_of = lambda d, h: GATE_LANE + 8 * d + h
    hs = lambda h: slice(h * DK, (h + 1) * DK)
    vs = lambda h: slice(h * DV, (h + 1) * DV)

    mrow = [m_ref[d:d + 1, :] for d in range(2)]
    tiles = {}
    for d, j in units:
        t_ref = dirs[d][4]
        rs = slice(j * lc, (j + 1) * lc)
        ga = (t_ref[rs, 0:128] + bg_ref[:, 0:128]) * LOG2E
        gb = t_ref[rs, 128:256] + bg_ref[:, 128:256]
        bc = _cumsum_rows(tris[d][1], _log2_sigmoid(gb))
        b_last = bc[lc - 1:lc, :] if d == 0 else bc[0:1, :]
        log_key = b_last - bc + ga
        m_new = jnp.maximum(b_last + mrow[d], jnp.max(log_key, axis=0, keepdims=True))
        tiles[d, j] = dict(rmat=ga - bc, bct=bc.T, m_in=mrow[d], wkt=jnp.exp2(log_key - m_new).T,
                           decay=jnp.exp2(b_last + mrow[d] - m_new))
        mrow[d] = m_new
    for d in range(2):
        m_ref[d:d + 1, :] = mrow[d]

    us, ncols = {}, {}
    for d, j in units:
        k_ref, _, kt_ref, v_ref = dirs[d][:4]
        rs = slice(j * lc, (j + 1) * lc)
        for h in range(HEADS):
            lane = lane_of(d, h)
            wk = tiles[d, j]["wkt"][lane:lane + 1, :]
            kwt = kt_ref[hs(h), rs] * wk.astype(BF16)
            us[d, j, h] = _dot(kwt, v_ref[rs, vs(h)])
            ncols[d, j, h] = _dot(jnp.broadcast_to(wk, (16, lc)).astype(BF16), k_ref[rs, hs(h)])[0:1]

    n_in = {}
    for d in range(2):
        for h in range(HEADS):
            idx = d * HEADS + h
            lane = lane_of(d, h)
            nvec = n_ref[idx:idx + 1, :]
            for dd, j in units:
                if dd == d:
                    n_in[d, j, h] = nvec
                    nvec = tiles[d, j]["decay"][:, lane:lane + 1] * nvec + ncols[d, j, h]
            n_ref[idx:idx + 1, :] = nvec

    lhs = {}
    if emit_out:
        for d, j in units:
            k_ref, qt_ref = dirs[d][:2]
            rs = slice(j * lc, (j + 1) * lc)
            t = tiles[d, j]
            causal_t = tris[1 - d][0]
            for h in range(HEADS):
                lane = lane_of(d, h)
                qt = qt_ref[hs(h), rs]
                kq = _dot(jnp.concatenate(
                    [k_ref[rs, hs(h)], jnp.broadcast_to(n_in[d, j, h], (16, DK)).astype(BF16)], axis=0), qt)
                rm = jnp.where(causal_t, t["rmat"][:, lane:lane + 1], -jnp.inf)
                mval = t["m_in"][:, lane:lane + 1]
                mx = jnp.maximum(mval, jnp.max(rm, axis=0, keepdims=True))
                wt = jnp.exp2(rm - mx) * kq[0:lc]
                w_inter = jnp.exp2(mval - mx)
                den = jnp.sum(wt, axis=0, keepdims=True) + w_inter * kq[lc:lc + 1]
                inv = 1.0 / jnp.maximum(jnp.abs(den), jnp.exp2(-(t["bct"][lane:lane + 1, :] + mx)))
                lhs[d, j, h] = jnp.concatenate(
                    [(wt * inv).astype(BF16), qt * (w_inter * inv).astype(BF16)], axis=0)

    for d in range(2):
        v_ref, o_ref = dirs[d][3], dirs[d][5]
        for h in range(HEADS):
            idx = d * HEADS + h
            lane = lane_of(d, h)
            cmat = c_ref[idx]
            for dd, j in units:
                if dd != d:
                    continue
                rs = slice(j * lc, (j + 1) * lc)
                if emit_out:
                    o = _dot_tn(lhs[d, j, h], jnp.concatenate([v_ref[rs, vs(h)], cmat.astype(BF16)], axis=0))
                    _sum_directions(acc_ref, o_ref, blks[d], rs, vs(h), o)
                cmat = tiles[d, j]["decay"][:, lane:lane + 1] * cmat + us[d, j, h]
            c_ref[idx] = cmat


def _mlstm_scan(lat, ctx, bgate):
    k, qt, kt, v, tg = lat
    bn, tn, _ = k.shape
    tc = ctx[0].shape[1]
    step = min(STEP, tn)
    ns = tn // step
    fwd = lambda b, i: (b, i, 0)
    bwd = lambda b, i: (b, ns - 1 - i, 0)
    assert qt.shape[1:] == (ns, 512, step) and ctx[1].shape[1:] == (1, 512, tc)
    fwd_t = lambda b, i: (b, i, 0, 0)
    bwd_t = lambda b, i: (b, ns - 1 - i, 0, 0)
    whole = lambda b, i: (b, 0, 0)
    o_shape = jax.ShapeDtypeStruct((bn, tn // 2, HEADS * DV), F32)
    both = lambda shape, f, g: [pl.BlockSpec(shape, f), pl.BlockSpec(shape, g)]
    whole_t = lambda b, i: (b, 0, 0, 0)
    ctx_specs = [pl.BlockSpec((None, tc, 512), whole), pl.BlockSpec((None, None, 512, tc), whole_t),
                 pl.BlockSpec((None, None, 512, tc), whole_t), pl.BlockSpec((None, tc, 1024), whole),
                 pl.BlockSpec((None, tc, 256), whole)]
    return pl.pallas_call(
        functools.partial(_mlstm_kernel, ns=ns),
        out_shape=(o_shape, o_shape),
        grid=(bn, ns),
        scratch_shapes=[pltpu.VMEM((2 * HEADS, DK, DV), F32), pltpu.VMEM((2 * HEADS, 128), F32),
                        pltpu.VMEM((8, 128), F32), pltpu.VMEM((ns, step, HEADS * DV), F32)],
        in_specs=(both((None, step, 512), fwd, bwd) + both((None, None, 512, step), fwd_t, bwd_t)
                  + both((None, None, 512, step), fwd_t, bwd_t) + both((None, step, 1024), fwd, bwd)
                  + both((None, step, 256), fwd, bwd) + ctx_specs
                  + [pl.BlockSpec((1, 256), lambda b, i: (0, 0))]),
        out_specs=_half_specs(ns, step, HEADS * DV),
        compiler_params=_params(("arbitrary", "arbitrary")),
        name="mlstm_scan",
    )(k, k, qt, qt, kt, kt, v, v, tg, tg, *ctx, bgate)


def _head_norm(o, g):
    parts = []
    for h in range(HEADS):
        oh = o[:, h * DV:(h + 1) * DV]
        parts.append(oh * lax.rsqrt(jnp.mean(oh * oh, axis=-1, keepdims=True) + EPS))
    return jnp.concatenate(parts, axis=-1) * g


def _merge_kernel(olo_ref, ohi_ref, hlo_ref, hhi_ref, po_ref, x_ref, g1_ref, gg_ref, gm_ref,
                  wbg_ref, wbm_ref, wo_ref, o_ref, *, per_b):
    groups = [(g * COL_BLOCK // MERGE_GROUPS, (g + 1) * COL_BLOCK // MERGE_GROUPS) for g in range(MERGE_GROUPS)]
    rows = [slice(a * GRID_W, b * GRID_W) for a, b in groups]
    lower = pl.program_id(0) % per_b < per_b // 2
    ys = []
    for (a, b), rs in zip(groups, rows):
        hm = jnp.concatenate([r[:, rl, :] for rl in range(a, b) for r in (hlo_ref, hhi_ref)], axis=0)
        o = jnp.where(lower, olo_ref[rs, :], ohi_ref[rs, :]).astype(F32)
        y_gla = _head_norm(o, gg_ref[...]) * po_ref[rs, 0:1024].astype(F32)
        y_m = _head_norm(hm, gm_ref[...]) * po_ref[rs, 1024:2048].astype(F32)
        ys.append((y_gla.astype(BF16), y_m.astype(BF16)))
    ds = [(_dot(y_gla, wbg_ref[...]), _dot(y_m, wbm_ref[...])) for y_gla, y_m in ys]
    ys = [(po_ref[rs, 2048:3072].astype(F32) * d_g + po_ref[rs, 3072:4096].astype(F32) * d_m).astype(BF16)
          for rs, (d_g, d_m) in zip(rows, ds)]
    mixes = [_dot(y, wo_ref[...]) for y in ys]
    for rs, mix in zip(rows, mixes):
        o_ref[rs, :] = x_ref[rs, :] + g1_ref[...] * mix


def _merge(olo, ohi, hlo4, hhi4, po, x2, mod3, gg, gm, wbg, wbm, wo):
    m = x2.shape[0]
    tm = GRID_W * COL_BLOCK
    per_b = SEQ // tm
    half = per_b // 2
    tok = lambda i: (i, 0)
    hspec = pl.BlockSpec((None, GRID_W // 2, COL_BLOCK, D), lambda i: (i // per_b, 0, i % per_b, 0))
    lo_spec = pl.BlockSpec((None, tm, D), lambda i: (i // per_b, jnp.minimum(i % per_b, half - 1), 0))
    hi_spec = pl.BlockSpec((None, tm, D), lambda i: (i // per_b, jnp.maximum(i % per_b - half, 0), 0))
    return pl.pallas_call(
        functools.partial(_merge_kernel, per_b=per_b),
        out_shape=jax.ShapeDtypeStruct((m, D), F32),
        grid=(m // tm,),
        in_specs=[lo_spec, hi_spec, hspec, hspec,
                  pl.BlockSpec((tm, 4096), tok), pl.BlockSpec((tm, D), tok),
                  pl.BlockSpec((None, 1, D), lambda i: (i // per_b, 0, 2)),
                  pl.BlockSpec((1, D), lambda i: (0, 0)), pl.BlockSpec((1, D), lambda i: (0, 0)),
                  _resident((D, D)), _resident((D, D)), _resident((D, D))],
        out_specs=pl.BlockSpec((tm, D), tok),
        compiler_params=_params(("arbitrary",)),
        name="merge",
    )(olo, ohi, hlo4, hhi4, po, x2, mod3, gg, gm, wbg, wbm, wo)


FF_TILES = ((0, 1280), (1280, 2816))


def _ffn_kernel(x_ref, sh_ref, sc_ref, g2_ref, ng_ref, fg_ref, wi_ref, wo_ref, o_ref):
    tm = x_ref.shape[0]
    rows = [slice(g * tm // FFN_GROUPS, (g + 1) * tm // FFN_GROUPS) for g in range(FFN_GROUPS)]
    us = [_norm_mod(x_ref[rs, :], ng_ref[...], sh_ref[...], sc_ref[...]) for rs in rows]
    accs = [None] * FFN_GROUPS
    for lo, hi in FF_TILES:
        abs_ = [(_dot(u, wi_ref[:, lo:hi]), _dot(u, wi_ref[:, D_FF + lo:D_FF + hi])) for u in us]
        hids = [(a * _sigmoid(a) * b).astype(BF16) for a, b in abs_]
        parts = [_dot(hid, wo_ref[lo:hi, :]) for hid in hids]
        accs = [p if acc is None else acc + p for acc, p in zip(accs, parts)]
    for rs, acc in zip(rows, accs):
        x2 = x_ref[rs, :] + g2_ref[...] * acc
        o_ref[rs, :] = x2 * lax.rsqrt(jnp.mean(x2 * x2, axis=-1, keepdims=True) + EPS) * fg_ref[...]


def _ffn(x1, mod3, ng, fg, wi, wo, tm):
    m = x1.shape[0]
    per_b = SEQ // tm
    tok = lambda i: (i, 0)
    modspec = lambda c: pl.BlockSpec((None, 1, D), lambda i: (i // per_b, 0, c))
    return pl.pallas_call(
        _ffn_kernel,
        out_shape=jax.ShapeDtypeStruct((m, D), F32),
        grid=(m // tm,),
        in_specs=[pl.BlockSpec((tm, D), tok), modspec(3), modspec(4), modspec(5),
                  pl.BlockSpec((1, D), lambda i: (0, 0)), pl.BlockSpec((1, D), lambda i: (0, 0)),
                  _resident((D, 2 * D_FF)), _resident((D_FF, D))],
        out_specs=pl.BlockSpec((tm, D), tok),
        compiler_params=_params(("arbitrary",)),
        name="ffn",
    )(x1, mod3, mod3, mod3, ng, fg, wi, wo)


def _split_w_in_kernel(w_ref, wg_ref, wm_ref, wo_ref, wtl_ref, wtg_ref):
    cols = w_ref.shape[1]
    c = lambda a, b: w_ref[a:b, :].astype(BF16)
    z = lambda n: jnp.zeros((n, cols), BF16)
    wg_ref[...] = c(0, 2048)
    wm_ref[...] = c(3104, 5152)
    wo_ref[0:1024, :] = c(2048, 3072)
    wo_ref[1024:2048, :] = c(5152, 6176)
    wo_ref[2048:4096, :] = c(6192, 8240)
    wtl_ref[...] = jnp.concatenate([c(3072, 3104), z(128 - 2 * RANK)], axis=0)
    wtg_ref[...] = jnp.concatenate(
        [z(GATE_LANE), c(6176, 6192), z(128 - GATE_LANE - 16),
         z(GATE_LANE), c(6180, 6184), z(4), c(6188, 6192), z(128 - GATE_LANE - 12)], axis=0)


def _split_w_in(w_in_t):
    tc = 256
    heights = (2048, 2048, 4096, 128, 256)
    return pl.pallas_call(
        _split_w_in_kernel,
        out_shape=tuple(jax.ShapeDtypeStruct((n, D), BF16) for n in heights),
        grid=(D // tc,),
        in_specs=[pl.BlockSpec((w_in_t.shape[0], tc), lambda i: (0, i))],
        out_specs=tuple(pl.BlockSpec((n, tc), lambda i: (0, i)) for n in heights),
        compiler_params=_params(("arbitrary",)),
        name="split_w_in",
    )(w_in_t)


def kernel(x, c, ctx, c_ctx, w_ada, b_ada, norm1_g, w_in, gla_w_up, gla_b_dec, gla_norm_g,
           mlstm_conv_w, mlstm_conv_b, mlstm_b_gate, mlstm_norm_g, w_br_gla, w_br_mlstm, w_out,
           norm2_g, w_ffn_in, w_ffn_out, final_g):
    bsz = x.shape[0]
    row = lambda a: a.reshape(1, -1)

    cvec = jnp.concatenate([c, c_ctx[None, :], jnp.zeros((8 - bsz - 1, D), F32)], axis=0)
    mod3 = _ada(cvec, w_ada[0], row(b_ada[0])).reshape(8, 1, N_MOD)

    wg, wm, wo, wtl, wtg = _split_w_in(w_in[0].T)
    x2 = x.reshape(bsz * SEQ, D)
    ctx2 = ctx.reshape(bsz * CTX, D)
    g1n = row(norm1_g[0])
    tm = GRID_W * COL_BLOCK
    po, pg, tlr, u_lat = _inproj(x2, mod3, lambda i: i // (SEQ // tm), g1n, (wo, wg, wtl), (BF16, BF16, F32), tm,
                                 acts=(("silu", "sigmoid", "sigmoid", "sigmoid"), None, None), emit_u=True)
    conv_w = mlstm_conv_w[0]
    conv_b = row(mlstm_conv_b[0])
    m_lat = _inproj_m(u_lat.reshape(bsz, SEQ // GRID_W, GRID_W, D), None, None, None,
                      wm, wtg, conv_w, conv_b, tm, True)
    *m_ctx, pg_c, tlr_c = _inproj_m(ctx2, mod3, lambda i: bsz, g1n, wm, wtg, conv_w, conv_b, CTX, False,
                                    gla_weights=(wg, wtl))

    wup = jnp.zeros((2, 128, HEADS * DK), F32)
    wup = wup.at[0, 0:RANK].set(gla_w_up[0, 0]).at[1, RANK:2 * RANK].set(gla_w_up[0, 1]).astype(BF16)
    bdec = gla_b_dec[0].reshape(2, 1, HEADS * DK)
    o_lo, o_hi = _gla_scan(pg.reshape(bsz, SEQ, 2048), tlr.reshape(bsz, SEQ, 128),
                           pg_c.reshape(bsz, CTX, 2048), tlr_c.reshape(bsz, CTX, 128), wup, bdec)

    bgate = mlstm_b_gate[0].reshape(1, 16)
    zg = lambda n: jnp.zeros((1, n), F32)
    bg2 = jnp.concatenate([zg(GATE_LANE), bgate, zg(128 - GATE_LANE - 16),
                           zg(GATE_LANE), bgate[:, 4:8], zg(4), bgate[:, 12:16],
                           zg(128 - GATE_LANE - 12)], axis=1)
    h_lo, h_hi = _mlstm_scan(m_lat, m_ctx, bg2)

    cm4 = lambda a: a.reshape(bsz, GRID_W // 2, SEQ // GRID_W, D)
    x1 = _merge(o_lo, o_hi, cm4(h_lo), cm4(h_hi), po, x2, mod3,
                row(gla_norm_g[0]), row(mlstm_norm_g[0]),
                w_br_gla[0].astype(BF16), w_br_mlstm[0].astype(BF16), w_out[0].astype(BF16))
    out = _ffn(x1, mod3, row(norm2_g[0]), row(final_g),
               w_ffn_in[0].astype(BF16), w_ffn_out[0].astype(BF16), tm)
    return out.reshape(bsz, SEQ, D)
```

```python
import functools
import math

import jax
import jax.numpy as jnp
from jax import lax
from jax.experimental import pallas as pl
from jax.experimental.pallas import tpu as pltpu

D = 1024
SEQ = 4096
CTX = 256
GRID_W = 64
EPS = 1e-6
HEADS = 4
DK = 128
DV = 256
RANK = 16
TAU = 16.0
D_FF = 2816
N_MOD = 6 * D
GATE_LANE = 32
STEP = 512
GLA_CHUNK = 128
MLSTM_CHUNK = 128
COL_BLOCK = 8
MERGE_GROUPS = 2
FFN_GROUPS = 2

LOG2E = math.log2(math.e)
LOG2_QSCALE = -0.5 * math.log2(DK)

F32 = jnp.float32
BF16 = jnp.bfloat16
VMEM_BYTES_V7X = 64 * 1024 * 1024
VMEM_CAP_BYTES = VMEM_BYTES_V7X - 8 * 1024 * 1024
VMEM_BODY_BYTES = 20 * 1024 * 1024


def _dot(a, b):
    return jnp.dot(a, b, preferred_element_type=F32)


def _dot_nt(a, b):
    return lax.dot_general(a, b, (((1,), (1,)), ((), ())), preferred_element_type=F32)


def _dot_tn(a, b):
    return lax.dot_general(a, b, (((0,), (0,)), ((), ())), preferred_element_type=F32)


def _sigmoid(x):
    return 0.5 * jnp.tanh(0.5 * x) + 0.5


def _log2_sigmoid(x):
    return jnp.minimum(x, 0.0) * LOG2E - jnp.log2(1.0 + jnp.exp2(jnp.abs(x) * (-LOG2E)))


def _cumsum_rows(tri, g):
    g1 = g.astype(BF16)
    g2 = (g - g1.astype(F32)).astype(BF16)
    return _dot(tri, g1) + _dot(tri, g2)


def _tri(n, d):
    row = lax.broadcasted_iota(jnp.int32, (n, n), 0)
    col = lax.broadcasted_iota(jnp.int32, (n, n), 1)
    causal = (col <= row) if d == 0 else (col >= row)
    return causal, jnp.where(causal, 1.0, 0.0).astype(BF16)


def _resident(shape):
    n = len(shape)
    return pl.BlockSpec(shape, lambda *_: (0,) * n, pipeline_mode=pl.Buffered(1))


def _block_bytes(spec, dtype):
    n = 1
    for dim in spec.block_shape:
        n *= 1 if dim is None else dim
    buffers = 2 if spec.pipeline_mode is None else spec.pipeline_mode.buffer_count
    return buffers * n * jnp.dtype(dtype).itemsize


def _pallas(kernel, *, name, grid, in_specs, operands, out_shape, out_specs, scratch=()):
    outs = out_shape if isinstance(out_shape, (tuple, list)) else (out_shape,)
    ospecs = out_specs if isinstance(out_specs, (tuple, list)) else (out_specs,)
    need = (sum(_block_bytes(s, a.dtype) for s, a in zip(in_specs, operands, strict=True))
            + sum(_block_bytes(s, o.dtype) for s, o in zip(ospecs, outs, strict=True))
            + sum(math.prod(s.shape) * jnp.dtype(s.dtype).itemsize for s in scratch))
    params = pltpu.CompilerParams(dimension_semantics=("arbitrary",) * len(grid),
                                  vmem_limit_bytes=min(need + VMEM_BODY_BYTES, VMEM_CAP_BYTES))
    return pl.pallas_call(kernel, out_shape=out_shape, grid=grid, in_specs=list(in_specs), out_specs=out_specs,
                          scratch_shapes=list(scratch), compiler_params=params, name=name)(*operands)


def _skewed(units, stages):
    for t in range(len(units) + len(stages) - 1):
        for s_idx, stage in enumerate(stages):
            if 0 <= t - s_idx < len(units):
                stage(*units[t - s_idx])


def _scan_units(n_sub):
    return [(d, j) for jj in range(n_sub) for d, j in ((0, jj), (1, n_sub - 1 - jj))]


def _ada_kernel(c_ref, w_ref, b_ref, o_ref):
    cv = c_ref[...]
    s = (cv * _sigmoid(cv)).astype(BF16)
    o_ref[...] = _dot(s, w_ref[...].astype(BF16)) + b_ref[...]


def _ada(cvec, w_ada, b_ada):
    tn = 1024
    return _pallas(
        _ada_kernel, name="adaln",
        grid=(N_MOD // tn,),
        in_specs=[pl.BlockSpec((8, D), lambda j: (0, 0)),
                  pl.BlockSpec((D, tn), lambda j: (0, j)),
                  pl.BlockSpec((1, tn), lambda j: (0, j))],
        operands=(cvec, w_ada, b_ada),
        out_shape=jax.ShapeDtypeStruct((8, N_MOD), F32),
        out_specs=pl.BlockSpec((8, tn), lambda j: (0, j)))


def _norm_mod_f32(x, g, sh, sc):
    return x * lax.rsqrt(jnp.mean(x * x, axis=-1, keepdims=True) + EPS) * (g * (1.0 + sc)) + sh


def _norm_mod(x, g, sh, sc):
    return _norm_mod_f32(x, g, sh, sc).astype(BF16)


def _act(v, kind):
    if kind is None:
        return v
    v = v.astype(BF16)
    s = _sigmoid(v)
    return v * s if kind == "silu" else s


def _inproj_kernel(x_ref, sh_ref, sc_ref, g_ref, *refs, acts, emit_u):
    uf = _norm_mod_f32(x_ref[...], g_ref[...], sh_ref[...], sc_ref[...])
    if emit_u:
        refs[-1][...] = uf
        refs = refs[:-1]
    n_out = len(refs) // 2
    u = uf.astype(BF16)
    for w_ref, o_ref, act in zip(refs[:n_out], refs[n_out:], acts):
        n = w_ref.shape[0]
        for jc, j in enumerate(range(0, n, 1024)):
            cs = slice(j, min(j + 1024, n))
            o_ref[:, cs] = _act(_dot_nt(u, w_ref[cs, :]), act[jc] if act else None).astype(o_ref.dtype)


def _inproj(x2, mod3, mod_row, norm_g, weights, out_dtypes, tm, acts=None, emit_u=False):
    m = x2.shape[0]
    acts = acts or (None,) * len(weights)
    widths = [w.shape[0] for w in weights] + ([D] if emit_u else [])
    dtypes = list(out_dtypes) + ([F32] if emit_u else [])
    return _pallas(
        functools.partial(_inproj_kernel, acts=acts, emit_u=emit_u), name="inproj",
        grid=(m // tm,),
        in_specs=[pl.BlockSpec((tm, D), lambda i: (i, 0)),
                  pl.BlockSpec((None, 1, D), lambda i: (mod_row(i), 0, 0)),
                  pl.BlockSpec((None, 1, D), lambda i: (mod_row(i), 0, 1)),
                  pl.BlockSpec((1, D), lambda i: (0, 0))] + [_resident(w.shape) for w in weights],
        operands=(x2, mod3, mod3, norm_g, *weights),
        out_shape=tuple(jax.ShapeDtypeStruct((m, n), dt) for n, dt in zip(widths, dtypes)),
        out_specs=tuple(pl.BlockSpec((tm, n), lambda i: (i, 0)) for n in widths))


def _inproj_m_kernel(*refs, colmajor, nblk):
    if colmajor:
        u_ref, hp_ref, hn_ref, wm_ref, wtg_ref, cw_ref, cb_ref = refs[:7]
        u = jnp.concatenate([u_ref[:, cl, :] for cl in range(COL_BLOCK)] + [hp_ref[7], hn_ref[0]],
                            axis=0).astype(BF16)
        k_ref, qt_ref, kt_ref, v_ref, tg_ref = refs[7:]
    else:
        x_ref, sh_ref, sc_ref, g_ref, wm_ref, wtg_ref, cw_ref, cb_ref, wg_ref, wtl_ref = refs[:10]
        k_ref, qt_ref, kt_ref, v_ref, tg_ref, pg_ref, tlr_ref = refs[10:]
        u = _norm_mod(x_ref[...], g_ref[...], sh_ref[...], sc_ref[...])
        pg_ref[...] = _dot_nt(u, wg_ref[...]).astype(BF16)
        tlr_ref[...] = _dot_nt(u, wtl_ref[...])
    n = k_ref.shape[0]
    um = u[0:n]
    pres = [_dot_nt(u, wm_ref[c * 512:(c + 1) * 512, :]) for c in range(2)]
    v_ref[...] = _dot_nt(um, wm_ref[1024:2048, :]).astype(BF16)
    tg_ref[...] = _dot_nt(um, wtg_ref[...])
    row8 = lax.broadcasted_iota(jnp.int32, (8, 512), 0)
    j = pl.program_id(0) % nblk
    for c, pre in enumerate(pres):
        cs = slice(c * 512, (c + 1) * 512)
        a = pre[0:n]
        if colmajor:
            prev_row = jnp.where(j > 0, pre[n + 7:n + 8], 0.0)
            next_row = jnp.where(j < nblk - 1, pre[n + 8:n + 9], 0.0)
        else:
            prev_row = next_row = jnp.zeros((1, 512), F32)
        ap = pltpu.roll(a, 1, axis=0)
        ap = jnp.concatenate([jnp.where(row8 == 0, prev_row, ap[0:8]), ap[8:]], axis=0)
        an = pltpu.roll(a, n - 1, axis=0)
        an = jnp.concatenate([an[0:n - 8], jnp.where(row8 == 7, next_row, an[n - 8:])], axis=0)
        conv = ap * cw_ref[0:1, cs] + a * cw_ref[1:2, cs] + an * cw_ref[2:3, cs] + cb_ref[:, cs]
        y = conv * _sigmoid(conv)
        if c == 0:
            qt_ref[...] = y.T.astype(BF16)
        else:
            y = y * (DK ** -0.5)
            k_ref[...] = y.astype(BF16)
            kt_ref[...] = y.T.astype(BF16)


def _inproj_m(xv, mod3, mod_row, norm_g, wm, wtg, conv_w, conv_b, tm, colmajor, gla_weights=()):
    full = lambda shape: pl.BlockSpec(shape, lambda i: (0,) * len(shape))
    mod_specs = [pl.BlockSpec((None, 1, D), lambda i: (mod_row(i), 0, 0)),
                 pl.BlockSpec((None, 1, D), lambda i: (mod_row(i), 0, 1)), full((1, D))]
    mod_args = (mod3, mod3, norm_g)
    if colmajor:
        mod_specs, mod_args = [], ()
        bsz = xv.shape[0]
        tn = SEQ
        nblk = GRID_W // COL_BLOCK
        blk = (None, GRID_W, COL_BLOCK, D)
        halo = (None, 8, COL_BLOCK, D)
        x_specs = [pl.BlockSpec(blk, lambda i: (i // nblk, 0, i % nblk, 0)),
                   pl.BlockSpec(halo, lambda i: (i // nblk, GRID_W // 8 - 1, jnp.maximum(i % nblk - 1, 0), 0)),
                   pl.BlockSpec(halo, lambda i: (i // nblk, 0, jnp.minimum(i % nblk + 1, nblk - 1), 0))]
        xs = (xv, xv, xv)
    else:
        tn = tm
        bsz = xv.shape[0] // tn
        nblk = 1
        x_specs = [pl.BlockSpec((tm, D), lambda i: (i, 0))]
        xs = (xv,)
    extra_shapes = tuple(jax.ShapeDtypeStruct((bsz, tn, w.shape[0]), dt) for w, dt in zip(gla_weights, (BF16, F32)))
    tok = lambda w: pl.BlockSpec((None, tm, w), lambda i: (i // nblk, i % nblk, 0))
    tr = pl.BlockSpec((None, None, 512, tm), lambda i: (i // nblk, i % nblk, 0, 0))
    sds = jax.ShapeDtypeStruct
    return _pallas(
        functools.partial(_inproj_m_kernel, colmajor=colmajor, nblk=nblk),
        name="inproj_m_cm" if colmajor else "inproj_m",
        grid=(bsz * nblk,),
        in_specs=x_specs + mod_specs + [_resident(wm.shape), _resident(wtg.shape), full((3, 1024)),
                                        full((1, 1024))] + [_resident(w.shape) for w in gla_weights],
        operands=(*xs, *mod_args, wm, wtg, conv_w, conv_b, *gla_weights),
        out_shape=(sds((bsz, tn, 512), BF16), sds((bsz, nblk, 512, tm), BF16), sds((bsz, nblk, 512, tm), BF16),
                   sds((bsz, tn, 1024), BF16), sds((bsz, tn, 256), F32)) + extra_shapes,
        out_specs=(tok(512), tr, tr, tok(1024), tok(256)) + tuple(tok(w.shape[0]) for w in gla_weights))


def _sum_directions(acc_ref, out_ref, blk, rs, cs, val):
    tot = acc_ref[blk, rs, cs] + val
    acc_ref[blk, rs, cs] = tot
    out_ref[rs, cs] = tot.astype(out_ref.dtype)


def _gla_kernel(pf_ref, pb_ref, tf_ref, tb_ref, cp_ref, ct_ref, wup_ref, bdec_ref, olo_ref, ohi_ref,
                st_ref, acc_ref, *, ns):
    i = pl.program_id(1)

    @pl.when(i == 0)
    def _():
        st_ref[...] = jnp.zeros(st_ref.shape, F32)
        acc_ref[...] = jnp.zeros(acc_ref.shape, F32)
        ctx = (cp_ref, ct_ref, None)
        _gla_step((ctx, ctx), wup_ref, bdec_ref, st_ref, None, None, emit_out=False)

    _gla_step(((pf_ref, tf_ref, ohi_ref), (pb_ref, tb_ref, olo_ref)), wup_ref, bdec_ref, st_ref, acc_ref,
              (i, ns - 1 - i), emit_out=True)


def _gla_step(dirs, wup_ref, bdec_ref, st_ref, acc_ref, blks, *, emit_out):
    lc = GLA_CHUNK
    units = _scan_units(dirs[0][0].shape[0] // lc)
    masks =[_tri(lc, d) for d in range(2)]
    gs = []
    for d, (p_ref, t_ref, o_ref) in enumerate(dirs):
        z = _dot(t_ref[...].astype(BF16), wup_ref[d]) + bdec_ref[d]
        gs.append(_log2_sigmoid(z) * (1.0 / TAU))
    bs, ops, sc, us, dcols = {}, {}, {}, {}, {}
    st = {(d, h): st_ref[d * HEADS + h] for d in range(2) for h in range(HEADS)}

    def stage2(d, j):
        bs[d, j] = _cumsum_rows(masks[d][1], gs[d][j * lc:(j + 1) * lc])

    def stage3(d, j):
        p_ref = dirs[d][0]
        rs = slice(j * lc, (j + 1) * lc)
        b = bs[d, j]
        b_last = b[lc - 1:lc, :] if d == 0 else b[0:1, :]
        b_mid = b[lc // 2 - 1:lc // 2, :] if d == 0 else b[lc // 2:lc // 2 + 1, :]
        q = p_ref[rs, 0:512]
        k = p_ref[rs, 512:1024]
        qd = q * jnp.exp2(b - b_mid).astype(BF16)
        kd = k * jnp.exp2((b_mid + LOG2_QSCALE) - b).astype(BF16)
        qi = qd * jnp.exp2(b_mid).astype(BF16)
        kl = kd * jnp.exp2(b_last - b_mid).astype(BF16)
        dec = jnp.exp2(b_last)
        ops[d, j] = (qi, kl, dec, qd, kd)

    def stage4(d, j):
        p_ref = dirs[d][0]
        rs = slice(j * lc, (j + 1) * lc)
        qi, kl, dec, qd, kd = ops[d, j]
        for h in range(HEADS):
            ks = slice(h * DK, (h + 1) * DK)
            v = p_ref[rs, 1024 + h * DV:1024 + (h + 1) * DV]
            if emit_out:
                sc[d, j, h] = jnp.where(masks[d][0], _dot_nt(qd[:, ks], kd[:, ks]), 0.0).astype(BF16)
            us[d, j, h] = _dot_tn(kl[:, ks], v)
            dcols[d, j, h] = jnp.broadcast_to(dec[:, ks], (8, DK)).T[:, 0:1]
    def stage5(d, j):
        p_ref, _, o_ref = dirs[d]
        rs = slice(j * lc, (j + 1) * lc)
        for h in range(HEADS):
            ks = slice(h * DK, (h + 1) * DK)
            if emit_out:
                v = p_ref[rs, 1024 + h * DV:1024 + (h + 1) * DV]
                o = _dot(jnp.concatenate([sc[d, j, h], ops[d, j][0][:, ks]], axis=1),
                         jnp.concatenate([v, st[d, h].astype(BF16)], axis=0))
                _sum_directions(acc_ref, o_ref, blks[d], rs, slice(h * DV, (h + 1) * DV), o)
            st[d, h] = st[d, h] * dcols[d, j, h] + us[d, j, h]

    _skewed(units, (stage2, stage3, stage4, stage5))
    for (d, h), val in st.items():
        st_ref[d * HEADS + h] = val


def _half_specs(ns, step, width):
    half = ns // 2
    lo = pl.BlockSpec((None, step, width), lambda b, i: (b, jnp.minimum(ns - 1 - i, half - 1), 0))
    hi = pl.BlockSpec((None, step, width), lambda b, i: (b, jnp.maximum(i - half, 0), 0))
    return lo, hi


def _gla_scan(pg, tlr, pg_c, tlr_c, wup, bdec):
    bn, tn, _ = pg.shape
    tc = pg_c.shape[1]
    step = min(STEP, tn)
    ns = tn // step
    fwd = lambda b, i: (b, i, 0)
    bwd = lambda b, i: (b, ns - 1 - i, 0)
    ctx = lambda b, i: (b, 0, 0)
    o_shape = jax.ShapeDtypeStruct((bn, tn // 2, HEADS * DV), BF16)
    return _pallas(
        functools.partial(_gla_kernel, ns=ns), name="gla_scan",
        grid=(bn, ns),
        in_specs=[pl.BlockSpec((None, step, 2048), fwd),
                  pl.BlockSpec((None, step, 2048), bwd),
                  pl.BlockSpec((None, step, 128), fwd),
                  pl.BlockSpec((None, step, 128), bwd),
                  pl.BlockSpec((None, tc, 2048), ctx),
                  pl.BlockSpec((None, tc, 128), ctx),
                  pl.BlockSpec((2, 128, 512), lambda b, i: (0, 0, 0)),
                  pl.BlockSpec((2, 1, 512), lambda b, i: (0, 0, 0))],
        operands=(pg, pg, tlr, tlr, pg_c, tlr_c, wup, bdec),
        out_shape=(o_shape, o_shape),
        out_specs=_half_specs(ns, step, HEADS * DV),
        scratch=[pltpu.VMEM((2 * HEADS, DK, DV), F32), pltpu.VMEM((ns, step, HEADS * DV), F32)])


def _mlstm_kernel(kf_ref, kb_ref, qtf_ref, qtb_ref, ktf_ref, ktb_ref, vf_ref, vb_ref, tf_ref, tb_ref,
                  kc_ref, qtc_ref, ktc_ref, vc_ref, tc_ref, bg_ref, hlo_ref, hhi_ref,
                  c_ref, n_ref, m_ref, acc_ref, *, ns):
    i = pl.program_id(1)

    @pl.when(i == 0)
    def _():
        c_ref[...] = jnp.zeros(c_ref.shape, F32)
        n_ref[...] = jnp.zeros(n_ref.shape, F32)
        m_ref[...] = jnp.zeros(m_ref.shape, F32)
        acc_ref[...] = jnp.zeros(acc_ref.shape, F32)
        ctx = (kc_ref, qtc_ref, ktc_ref, vc_ref, tc_ref, None)
        _mlstm_step((ctx, ctx), bg_ref, c_ref, n_ref, m_ref, None, None, emit_out=False)

    _mlstm_step(((kf_ref, qtf_ref, ktf_ref, vf_ref, tf_ref, hhi_ref),
                 (kb_ref, qtb_ref, ktb_ref, vb_ref, tb_ref, hlo_ref)),
                bg_ref, c_ref, n_ref, m_ref, acc_ref, (i, ns - 1 - i), emit_out=True)


def _mlstm_step(dirs, bg_ref, c_ref, n_ref, m_ref, acc_ref, blks, *, emit_out):
    lc = MLSTM_CHUNK
    units = _scan_units(dirs[0][0].shape[0] // lc)
    tris =[_tri(lc, d) for d in range(2)]
    lane_of = lambda d, h: GATE_LANE + 8 * d + h
    hs = lambda h: slice(h * DK, (h + 1) * DK)
    vs = lambda h: slice(h * DV, (h + 1) * DV)

    mrow = [m_ref[d:d + 1, :] for d in range(2)]
    tiles = {}
    for d, j in units:
        t_ref = dirs[d][4]
        rs = slice(j * lc, (j + 1) * lc)
        ga = (t_ref[rs, 0:128] + bg_ref[:, 0:128]) * LOG2E
        gb = t_ref[rs, 128:256] + bg_ref[:, 128:256]
        bc = _cumsum_rows(tris[d][1], _log2_sigmoid(gb))
        b_last = bc[lc - 1:lc, :] if d == 0 else bc[0:1, :]
        log_key = b_last - bc + ga
        m_new = jnp.maximum(b_last + mrow[d], jnp.max(log_key, axis=0, keepdims=True))
        tiles[d, j] = dict(rmat=ga - bc, bct=bc.T, m_in=mrow[d], wkt=jnp.exp2(log_key - m_new).T,
                           decay=jnp.exp2(b_last + mrow[d] - m_new))
        mrow[d] = m_new
    for d in range(2):
        m_ref[d:d + 1, :] = mrow[d]

    us, ncols = {}, {}
    for d, j in units:
        k_ref, _, kt_ref, v_ref = dirs[d][:4]
        rs = slice(j * lc, (j + 1) * lc)
        for h in range(HEADS):
            lane = lane_of(d, h)
            wk = tiles[d, j]["wkt"][lane:lane + 1, :]
            kwt = kt_ref[hs(h), rs] * wk.astype(BF16)
            us[d, j, h] = _dot(kwt, v_ref[rs, vs(h)])
            ncols[d, j, h] = _dot(jnp.broadcast_to(wk, (16, lc)).astype(BF16), k_ref[rs, hs(h)])[0:1]

    n_in = {}
    for d in range(2):
        for h in range(HEADS):
            idx = d * HEADS + h
            lane = lane_of(d, h)
            nvec = n_ref[idx:idx + 1, :]
            for dd, j in units:
                if dd == d:
                    n_in[d, j, h] = nvec
                    nvec = tiles[d, j]["decay"][:, lane:lane + 1] * nvec + ncols[d, j, h]
            n_ref[idx:idx + 1, :] = nvec

    lhs = {}
    if emit_out:
        for d, j in units:
            k_ref, qt_ref = dirs[d][:2]
            rs = slice(j * lc, (j + 1) * lc)
            t = tiles[d, j]
            causal_t = tris[1 - d][0]
            for h in range(HEADS):
                lane = lane_of(d, h)
                qt = qt_ref[hs(h), rs]
                kq = _dot(jnp.concatenate(
                    [k_ref[rs, hs(h)], jnp.broadcast_to(n_in[d, j, h], (16, DK)).astype(BF16)], axis=0), qt)
                rm = jnp.where(causal_t, t["rmat"][:, lane:lane + 1], -jnp.inf)
                mval = t["m_in"][:, lane:lane + 1]
                mx = jnp.maximum(mval, jnp.max(rm, axis=0, keepdims=True))
                wt = jnp.exp2(rm - mx) * kq[0:lc]
                w_inter = jnp.exp2(mval - mx)
                den = jnp.sum(wt, axis=0, keepdims=True) + w_inter * kq[lc:lc + 1]
                inv = 1.0 / jnp.maximum(jnp.abs(den), jnp.exp2(-(t["bct"][lane:lane + 1, :] + mx)))
                lhs[d, j, h] = jnp.concatenate(
                    [(wt * inv).astype(BF16), qt * (w_inter * inv).astype(BF16)], axis=0)

    for d in range(2):
        v_ref, o_ref = dirs[d][3], dirs[d][5]
        for h in range(HEADS):
            idx = d * HEADS + h
            lane = lane_of(d, h)
            cmat = c_ref[idx]
            for dd, j in units:
                if dd != d:
                    continue
                rs = slice(j * lc, (j + 1) * lc)
                if emit_out:
                    o = _dot_tn(lhs[d, j, h], jnp.concatenate([v_ref[rs, vs(h)], cmat.astype(BF16)], axis=0))
                    _sum_directions(acc_ref, o_ref, blks[d], rs, vs(h), o)
                cmat = tiles[d, j]["decay"][:, lane:lane + 1] * cmat + us[d, j, h]
            c_ref[idx] = cmat


def _mlstm_scan(lat, ctx, bgate):
    k, qt, kt, v, tg = lat
    bn, tn, _ = k.shape
    tc = ctx[0].shape[1]
    step = min(STEP, tn)
    ns = tn // step
    fwd = lambda b, i: (b, i, 0)
    bwd = lambda b, i: (b, ns - 1 - i, 0)
    assert qt.shape[1:] == (ns, 512, step) and ctx[1].shape[1:] == (1, 512, tc)
    fwd_t = lambda b, i: (b, i, 0, 0)
    bwd_t = lambda b, i: (b, ns - 1 - i, 0, 0)
    whole = lambda b, i: (b, 0, 0)
    o_shape = jax.ShapeDtypeStruct((bn, tn // 2, HEADS * DV), F32)
    both = lambda shape, f, g: [pl.BlockSpec(shape, f), pl.BlockSpec(shape, g)]
    whole_t = lambda b, i: (b, 0, 0, 0)
    ctx_specs = [pl.BlockSpec((None, tc, 512), whole), pl.BlockSpec((None, None, 512, tc), whole_t),
                 pl.BlockSpec((None, None, 512, tc), whole_t), pl.BlockSpec((None, tc, 1024), whole),
                 pl.BlockSpec((None, tc, 256), whole)]
    return _pallas(
        functools.partial(_mlstm_kernel, ns=ns), name="mlstm_scan",
        grid=(bn, ns),
        in_specs=(both((None, step, 512), fwd, bwd) + both((None, None, 512, step), fwd_t, bwd_t)
                  + both((None, None, 512, step), fwd_t, bwd_t) + both((None, step, 1024), fwd, bwd)
                  + both((None, step, 256), fwd, bwd) + ctx_specs
                  + [pl.BlockSpec((1, 256), lambda b, i: (0, 0))]),
        operands=(k, k, qt, qt, kt, kt, v, v, tg, tg, *ctx, bgate),
        out_shape=(o_shape, o_shape),
        out_specs=_half_specs(ns, step, HEADS * DV),
        scratch=[pltpu.VMEM((2 * HEADS, DK, DV), F32), pltpu.VMEM((2 * HEADS, 128), F32),
                 pltpu.VMEM((8, 128), F32), pltpu.VMEM((ns, step, HEADS * DV), F32)])


def _head_norm(o, g):
    parts = []
    for h in range(HEADS):
        oh = o[:, h * DV:(h + 1) * DV]
        parts.append(oh * lax.rsqrt(jnp.mean(oh * oh, axis=-1, keepdims=True) + EPS))
    return jnp.concatenate(parts, axis=-1) * g


def _merge_kernel(olo_ref, ohi_ref, hlo_ref, hhi_ref, po_ref, x_ref, g1_ref, gg_ref, gm_ref,
                  wbg_ref, wbm_ref, wo_ref, o_ref, *, per_b):
    groups = [(g * COL_BLOCK // MERGE_GROUPS, (g + 1) * COL_BLOCK // MERGE_GROUPS) for g in range(MERGE_GROUPS)]
    rows = [slice(a * GRID_W, b * GRID_W) for a, b in groups]
    lower = pl.program_id(0) % per_b < per_b // 2
    ys = []
    for (a, b), rs in zip(groups, rows):
        hm = jnp.concatenate([r[:, rl, :] for rl in range(a, b) for r in (hlo_ref, hhi_ref)], axis=0)
        o = jnp.where(lower, olo_ref[rs, :], ohi_ref[rs, :]).astype(F32)
        y_gla = _head_norm(o, gg_ref[...]) * po_ref[rs, 0:1024].astype(F32)
        y_m = _head_norm(hm, gm_ref[...]) * po_ref[rs, 1024:2048].astype(F32)
        ys.append((y_gla.astype(BF16), y_m.astype(BF16)))
    ds = [(_dot(y_gla, wbg_ref[...]), _dot(y_m, wbm_ref[...])) for y_gla, y_m in ys]
    ys = [(po_ref[rs, 2048:3072].astype(F32) * d_g + po_ref[rs, 3072:4096].astype(F32) * d_m).astype(BF16)
          for rs, (d_g, d_m) in zip(rows, ds)]
    mixes = [_dot(y, wo_ref[...]) for y in ys]
    for rs, mix in zip(rows, mixes):
        o_ref[rs, :] = x_ref[rs, :] + g1_ref[...] * mix


def _merge(olo, ohi, hlo4, hhi4, po, x2, mod3, gg, gm, wbg, wbm, wo):
    m = x2.shape[0]
    tm = GRID_W * COL_BLOCK
    per_b = SEQ // tm
    half = per_b // 2
    tok = lambda i: (i, 0)
    hspec = pl.BlockSpec((None, GRID_W // 2, COL_BLOCK, D), lambda i: (i // per_b, 0, i % per_b, 0))
    lo_spec = pl.BlockSpec((None, tm, D), lambda i: (i // per_b, jnp.minimum(i % per_b, half - 1), 0))
    hi_spec = pl.BlockSpec((None, tm, D), lambda i: (i // per_b, jnp.maximum(i % per_b - half, 0), 0))
    return _pallas(
        functools.partial(_merge_kernel, per_b=per_b), name="merge",
        grid=(m // tm,),
        in_specs=[lo_spec, hi_spec, hspec, hspec,
                  pl.BlockSpec((tm, 4096), tok), pl.BlockSpec((tm, D), tok),
                  pl.BlockSpec((None, 1, D), lambda i: (i // per_b, 0, 2)),
                  pl.BlockSpec((1, D), lambda i: (0, 0)), pl.BlockSpec((1, D), lambda i: (0, 0)),
                  _resident((D, D)), _resident((D, D)), _resident((D, D))],
        operands=(olo, ohi, hlo4, hhi4, po, x2, mod3, gg, gm, wbg, wbm, wo),
        out_shape=jax.ShapeDtypeStruct((m, D), F32),
        out_specs=pl.BlockSpec((tm, D), tok))


FF_TILES = ((0, 1280), (1280, 2816))


def _ffn_kernel(x_ref, sh_ref, sc_ref, g2_ref, ng_ref, fg_ref, wi_ref, wo_ref, o_ref):
    tm = x_ref.shape[0]
    rows = [slice(g * tm // FFN_GROUPS, (g + 1) * tm // FFN_GROUPS) for g in range(FFN_GROUPS)]
    us = [_norm_mod(x_ref[rs, :], ng_ref[...], sh_ref[...], sc_ref[...]) for rs in rows]
    accs = [None] * FFN_GROUPS
    for lo, hi in FF_TILES:
        abs_ = [(_dot(u, wi_ref[:, lo:hi]), _dot(u, wi_ref[:, D_FF + lo:D_FF + hi])) for u in us]
        hids = [(a * _sigmoid(a) * b).astype(BF16) for a, b in abs_]
        parts = [_dot(hid, wo_ref[lo:hi, :]) for hid in hids]
        accs = [p if acc is None else acc + p for acc, p in zip(accs, parts)]
    for rs, acc in zip(rows, accs):
        x2 = x_ref[rs, :] + g2_ref[...] * acc
        o_ref[rs, :] = x2 * lax.rsqrt(jnp.mean(x2 * x2, axis=-1, keepdims=True) + EPS) * fg_ref[...]


def _ffn(x1, mod3, ng, fg, wi, wo, tm):
    m = x1.shape[0]
    per_b = SEQ // tm
    tok = lambda i: (i, 0)
    modspec = lambda c: pl.BlockSpec((None, 1, D), lambda i: (i // per_b, 0, c))
    return _pallas(
        _ffn_kernel, name="ffn",
        grid=(m // tm,),
        in_specs=[pl.BlockSpec((tm, D), tok), modspec(3), modspec(4), modspec(5),
                  pl.BlockSpec((1, D), lambda i: (0, 0)), pl.BlockSpec((1, D), lambda i: (0, 0)),
                  _resident((D, 2 * D_FF)), _resident((D_FF, D))],
        operands=(x1, mod3, mod3, mod3, ng, fg, wi, wo),
        out_shape=jax.ShapeDtypeStruct((m, D), F32),
        out_specs=pl.BlockSpec((tm, D), tok))


def _split_w_in_kernel(w_ref, wg_ref, wm_ref, wo_ref, wtl_ref, wtg_ref):
    cols = w_ref.shape[1]
    c = lambda a, b: w_ref[a:b, :].astype(BF16)
    z = lambda n: jnp.zeros((n, cols), BF16)
    wg_ref[...] = c(0, 2048)
    wm_ref[...] = c(3104, 5152)
    wo_ref[0:1024, :] = c(2048, 3072)
    wo_ref[1024:2048, :] = c(5152, 6176)
    wo_ref[2048:4096, :] = c(6192, 8240)
    wtl_ref[...] = jnp.concatenate([c(3072, 3104), z(128 - 2 * RANK)], axis=0)
    wtg_ref[...] = jnp.concatenate(
        [z(GATE_LANE), c(6176, 6192), z(128 - GATE_LANE - 16),
         z(GATE_LANE), c(6180, 6184), z(4), c(6188, 6192), z(128 - GATE_LANE - 12)], axis=0)


def _split_w_in(w_in_t):
    tc = 256
    heights = (2048, 2048, 4096, 128, 256)
    return _pallas(
        _split_w_in_kernel, name="split_w_in",
        grid=(D // tc,),
        in_specs=[pl.BlockSpec((w_in_t.shape[0], tc), lambda i: (0, i))],
        operands=(w_in_t,),
        out_shape=tuple(jax.ShapeDtypeStruct((n, D), BF16) for n in heights),
        out_specs=tuple(pl.BlockSpec((n, tc), lambda i: (0, i)) for n in heights))


def kernel(x, c, ctx, c_ctx, w_ada, b_ada, norm1_g, w_in, gla_w_up, gla_b_dec, gla_norm_g,
           mlstm_conv_w, mlstm_conv_b, mlstm_b_gate, mlstm_norm_g, w_br_gla, w_br_mlstm, w_out,
           norm2_g, w_ffn_in, w_ffn_out, final_g):
    bsz = x.shape[0]
    row = lambda a: a.reshape(1, -1)

    cvec = jnp.concatenate([c, c_ctx[None, :], jnp.zeros((8 - bsz - 1, D), F32)], axis=0)
    mod3 = _ada(cvec, w_ada[0], row(b_ada[0])).reshape(8, 1, N_MOD)

    wg, wm, wo, wtl, wtg = _split_w_in(w_in[0].T)
    x2 = x.reshape(bsz * SEQ, D)
    ctx2 = ctx.reshape(bsz * CTX, D)
    g1n = row(norm1_g[0])
    tm = GRID_W * COL_BLOCK
    po, pg, tlr, u_lat = _inproj(x2, mod3, lambda i: i // (SEQ // tm), g1n, (wo, wg, wtl), (BF16, BF16, F32), tm,
                                 acts=(("silu", "sigmoid", "sigmoid", "sigmoid"), None, None), emit_u=True)
    conv_w = mlstm_conv_w[0]
    conv_b = row(mlstm_conv_b[0])
    m_lat = _inproj_m(u_lat.reshape(bsz, SEQ // GRID_W, GRID_W, D), None, None, None,
                      wm, wtg, conv_w, conv_b, tm, True)
    *m_ctx, pg_c, tlr_c = _inproj_m(ctx2, mod3, lambda i: bsz, g1n, wm, wtg, conv_w, conv_b, CTX, False,
                                    gla_weights=(wg, wtl))

    wup = jnp.zeros((2, 128, HEADS * DK), F32)
    wup = wup.at[0, 0:RANK].set(gla_w_up[0, 0]).at[1, RANK:2 * RANK].set(gla_w_up[0, 1]).astype(BF16)
    bdec = gla_b_dec[0].reshape(2, 1, HEADS * DK)
    o_lo, o_hi = _gla_scan(pg.reshape(bsz, SEQ, 2048), tlr.reshape(bsz, SEQ, 128),
                           pg_c.reshape(bsz, CTX, 2048), tlr_c.reshape(bsz, CTX, 128), wup, bdec)

    bgate = mlstm_b_gate[0].reshape(1, 16)
    zg = lambda n: jnp.zeros((1, n), F32)
    bg2 = jnp.concatenate([zg(GATE_LANE), bgate, zg(128 - GATE_LANE - 16),
                           zg(GATE_LANE), bgate[:, 4:8], zg(4), bgate[:, 12:16],
                           zg(128 - GATE_LANE - 12)], axis=1)
    h_lo, h_hi = _mlstm_scan(m_lat, m_ctx, bg2)

    cm4 = lambda a: a.reshape(bsz, GRID_W // 2, SEQ // GRID_W, D)
    x1 = _merge(o_lo, o_hi, cm4(h_lo), cm4(h_hi), po, x2, mod3,
                row(gla_norm_g[0]), row(mlstm_norm_g[0]),
                w_br_gla[0].astype(BF16), w_br_mlstm[0].astype(BF16), w_out[0].astype(BF16))
    out = _ffn(x1, mod3, row(norm2_g[0]), row(final_g),
               w_ffn_in[0].astype(BF16), w_ffn_out[0].astype(BF16), tm)
    return out.reshape(bsz, SEQ, D)
```

```python
import functools
import math

import jax
import jax.numpy as jnp
from jax import lax
from jax.experimental import pallas as pl
from jax.experimental.pallas import tpu as pltpu

D = 1024
SEQ = 4096
CTX = 256
GRID_W = 64
EPS = 1e-6
HEADS = 4
DK = 128
DV = 256
RANK = 16
TAU = 16.0
D_FF = 2816
N_MOD = 6 * D
GATE_LANE = 32
STEP = 512
GLA_CHUNK = 128
MLSTM_CHUNK = 128
COL_BLOCK = 8
MERGE_GROUPS = 2
FFN_ROWS = 1024
FFN_GROUPS = 4

LOG2E = math.log2(math.e)
LOG2_QSCALE = -0.5 * math.log2(DK)

F32 = jnp.float32
BF16 = jnp.bfloat16
VMEM_BYTES_V7X = 64 * 1024 * 1024
VMEM_CAP_BYTES = VMEM_BYTES_V7X - 8 * 1024 * 1024
VMEM_BODY_BYTES = 20 * 1024 * 1024


def _dot(a, b):
    return jnp.dot(a, b, preferred_element_type=F32)


def _dot_nt(a, b):
    return lax.dot_general(a, b, (((1,), (1,)), ((), ())), preferred_element_type=F32)


def _dot_tn(a, b):
    return lax.dot_general(a, b, (((0,), (0,)), ((), ())), preferred_element_type=F32)


def _sigmoid(x):
    return 0.5 * jnp.tanh(0.5 * x) + 0.5


def _log2_sigmoid(x):
    return jnp.minimum(x, 0.0) * LOG2E - jnp.log2(1.0 + jnp.exp2(jnp.abs(x) * (-LOG2E)))


def _cumsum_rows(tri, g):
    g1 = g.astype(BF16)
    g2 = (g - g1.astype(F32)).astype(BF16)
    return _dot(tri, g1) + _dot(tri, g2)


def _tri(n, d):
    row = lax.broadcasted_iota(jnp.int32, (n, n), 0)
    col = lax.broadcasted_iota(jnp.int32, (n, n), 1)
    causal = (col <= row) if d == 0 else (col >= row)
    return causal, jnp.where(causal, 1.0, 0.0).astype(BF16)


def _resident(shape):
    n = len(shape)
    return pl.BlockSpec(shape, lambda *_: (0,) * n, pipeline_mode=pl.Buffered(1))


def _block_bytes(spec, dtype):
    n = 1
    for dim in spec.block_shape:
        n *= 1 if dim is None else dim
    buffers = 2 if spec.pipeline_mode is None else spec.pipeline_mode.buffer_count
    return buffers * n * jnp.dtype(dtype).itemsize


def _pallas(kernel, *, name, grid, in_specs, operands, out_shape, out_specs, scratch=()):
    outs = out_shape if isinstance(out_shape, (tuple, list)) else (out_shape,)
    ospecs = out_specs if isinstance(out_specs, (tuple, list)) else (out_specs,)
    need = (sum(_block_bytes(s, a.dtype) for s, a in zip(in_specs, operands, strict=True))
            + sum(_block_bytes(s, o.dtype) for s, o in zip(ospecs, outs, strict=True))
            + sum(math.prod(s.shape) * jnp.dtype(s.dtype).itemsize for s in scratch))
    params = pltpu.CompilerParams(dimension_semantics=("arbitrary",) * len(grid),
                                  vmem_limit_bytes=min(need + VMEM_BODY_BYTES, VMEM_CAP_BYTES))
    return pl.pallas_call(kernel, out_shape=out_shape, grid=grid, in_specs=list(in_specs), out_specs=out_specs,
                          scratch_shapes=list(scratch), compiler_params=params, name=name)(*operands)


def _skewed(units, stages):
    for t in range(len(units) + len(stages) - 1):
        for s_idx, stage in enumerate(stages):
            if 0 <= t - s_idx < len(units):
                stage(*units[t - s_idx])


def _scan_units(n_sub):
    return [(d, j) for jj in range(n_sub) for d, j in ((0, jj), (1, n_sub - 1 - jj))]


def _ada_kernel(c_ref, w_ref, b_ref, o_ref):
    cv = c_ref[...]
    s = (cv * _sigmoid(cv)).astype(BF16)
    o_ref[...] = _dot(s, w_ref[...].astype(BF16)) + b_ref[...]


def _ada(cvec, w_ada, b_ada):
    tn = 1024
    return _pallas(
        _ada_kernel, name="adaln",
        grid=(N_MOD // tn,),
        in_specs=[pl.BlockSpec((8, D), lambda j: (0, 0)),
                  pl.BlockSpec((D, tn), lambda j: (0, j)),
                  pl.BlockSpec((1, tn), lambda j: (0, j))],
        operands=(cvec, w_ada, b_ada),
        out_shape=jax.ShapeDtypeStruct((8, N_MOD), F32),
        out_specs=pl.BlockSpec((8, tn), lambda j: (0, j)))


def _norm_mod_f32(x, g, sh, sc):
    return x * lax.rsqrt(jnp.mean(x * x, axis=-1, keepdims=True) + EPS) * (g * (1.0 + sc)) + sh


def _norm_mod(x, g, sh, sc):
    return _norm_mod_f32(x, g, sh, sc).astype(BF16)


def _act(v, kind):
    if kind is None:
        return v
    v = v.astype(BF16)
    s = _sigmoid(v)
    return v * s if kind == "silu" else s


def _inproj_kernel(x_ref, sh_ref, sc_ref, g_ref, *refs, acts, emit_u):
    uf = _norm_mod_f32(x_ref[...], g_ref[...], sh_ref[...], sc_ref[...])
    if emit_u:
        refs[-1][...] = uf
        refs = refs[:-1]
    n_out = len(refs) // 2
    u = uf.astype(BF16)
    for w_ref, o_ref, act in zip(refs[:n_out], refs[n_out:], acts):
        n = w_ref.shape[0]
        for jc, j in enumerate(range(0, n, 1024)):
            cs = slice(j, min(j + 1024, n))
            o_ref[:, cs] = _act(_dot_nt(u, w_ref[cs, :]), act[jc] if act else None).astype(o_ref.dtype)


def _inproj(x2, mod3, mod_row, norm_g, weights, out_dtypes, tm, acts=None, emit_u=False):
    m = x2.shape[0]
    acts = acts or (None,) * len(weights)
    widths = [w.shape[0] for w in weights] + ([D] if emit_u else [])
    dtypes = list(out_dtypes) + ([F32] if emit_u else [])
    return _pallas(
        functools.partial(_inproj_kernel, acts=acts, emit_u=emit_u), name="inproj",
        grid=(m // tm,),
        in_specs=[pl.BlockSpec((tm, D), lambda i: (i, 0)),
                  pl.BlockSpec((None, 1, D), lambda i: (mod_row(i), 0, 0)),
                  pl.BlockSpec((None, 1, D), lambda i: (mod_row(i), 0, 1)),
                  pl.BlockSpec((1, D), lambda i: (0, 0))] + [_resident(w.shape) for w in weights],
        operands=(x2, mod3, mod3, norm_g, *weights),
        out_shape=tuple(jax.ShapeDtypeStruct((m, n), dt) for n, dt in zip(widths, dtypes)),
        out_specs=tuple(pl.BlockSpec((tm, n), lambda i: (i, 0)) for n in widths))


def _inproj_m_kernel(*refs, colmajor, nblk):
    if colmajor:
        u_ref, hp_ref, hn_ref, wm_ref, wtg_ref, cw_ref, cb_ref = refs[:7]
        u = jnp.concatenate([u_ref[:, cl, :] for cl in range(COL_BLOCK)] + [hp_ref[7], hn_ref[0]],
                            axis=0).astype(BF16)
        k_ref, qt_ref, kt_ref, v_ref, tg_ref = refs[7:]
    else:
        x_ref, sh_ref, sc_ref, g_ref, wm_ref, wtg_ref, cw_ref, cb_ref, wg_ref, wtl_ref = refs[:10]
        k_ref, qt_ref, kt_ref, v_ref, tg_ref, pg_ref, tlr_ref = refs[10:]
        u = _norm_mod(x_ref[...], g_ref[...], sh_ref[...], sc_ref[...])
        pg_ref[...] = _dot_nt(u, wg_ref[...]).astype(BF16)
        tlr_ref[...] = _dot_nt(u, wtl_ref[...])
    n = k_ref.shape[0]
    um = u[0:n]
    pres = [_dot_nt(u, wm_ref[c * 512:(c + 1) * 512, :]) for c in range(2)]
    v_ref[...] = _dot_nt(um, wm_ref[1024:2048, :]).astype(BF16)
    tg_ref[...] = _dot_nt(um, wtg_ref[...])
    row8 = lax.broadcasted_iota(jnp.int32, (8, 512), 0)
    j = pl.program_id(0) % nblk
    for c, pre in enumerate(pres):
        cs = slice(c * 512, (c + 1) * 512)
        a = pre[0:n]
        if colmajor:
            prev_row = jnp.where(j > 0, pre[n + 7:n + 8], 0.0)
            next_row = jnp.where(j < nblk - 1, pre[n + 8:n + 9], 0.0)
        else:
            prev_row = next_row = jnp.zeros((1, 512), F32)
        ap = pltpu.roll(a, 1, axis=0)
        ap = jnp.concatenate([jnp.where(row8 == 0, prev_row, ap[0:8]), ap[8:]], axis=0)
        an = pltpu.roll(a, n - 1, axis=0)
        an = jnp.concatenate([an[0:n - 8], jnp.where(row8 == 7, next_row, an[n - 8:])], axis=0)
        conv = ap * cw_ref[0:1, cs] + a * cw_ref[1:2, cs] + an * cw_ref[2:3, cs] + cb_ref[:, cs]
        y = conv * _sigmoid(conv)
        if c == 0:
            qt_ref[...] = y.T.astype(BF16)
        else:
            y = y * (DK ** -0.5)
            k_ref[...] = y.astype(BF16)
            kt_ref[...] = y.T.astype(BF16)


def _inproj_m(xv, mod3, mod_row, norm_g, wm, wtg, conv_w, conv_b, tm, colmajor, gla_weights=()):
    full = lambda shape: pl.BlockSpec(shape, lambda i: (0,) * len(shape))
    mod_specs = [pl.BlockSpec((None, 1, D), lambda i: (mod_row(i), 0, 0)),
                 pl.BlockSpec((None, 1, D), lambda i: (mod_row(i), 0, 1)), full((1, D))]
    mod_args = (mod3, mod3, norm_g)
    if colmajor:
        mod_specs, mod_args = [], ()
        bsz = xv.shape[0]
        tn = SEQ
        nblk = GRID_W // COL_BLOCK
        blk = (None, GRID_W, COL_BLOCK, D)
        halo = (None, 8, COL_BLOCK, D)
        x_specs = [pl.BlockSpec(blk, lambda i: (i // nblk, 0, i % nblk, 0)),
                   pl.BlockSpec(halo, lambda i: (i // nblk, GRID_W // 8 - 1, jnp.maximum(i % nblk - 1, 0), 0)),
                   pl.BlockSpec(halo, lambda i: (i // nblk, 0, jnp.minimum(i % nblk + 1, nblk - 1), 0))]
        xs = (xv, xv, xv)
    else:
        tn = tm
        bsz = xv.shape[0] // tn
        nblk = 1
        x_specs = [pl.BlockSpec((tm, D), lambda i: (i, 0))]
        xs = (xv,)
    extra_shapes = tuple(jax.ShapeDtypeStruct((bsz, tn, w.shape[0]), dt) for w, dt in zip(gla_weights, (BF16, F32)))
    tok = lambda w: pl.BlockSpec((None, tm, w), lambda i: (i // nblk, i % nblk, 0))
    tr = pl.BlockSpec((None, None, 512, tm), lambda i: (i // nblk, i % nblk, 0, 0))
    sds = jax.ShapeDtypeStruct
    return _pallas(
        functools.partial(_inproj_m_kernel, colmajor=colmajor, nblk=nblk),
        name="inproj_m_cm" if colmajor else "inproj_m",
        grid=(bsz * nblk,),
        in_specs=x_specs + mod_specs + [_resident(wm.shape), _resident(wtg.shape), full((3, 1024)),
                                        full((1, 1024))] + [_resident(w.shape) for w in gla_weights],
        operands=(*xs, *mod_args, wm, wtg, conv_w, conv_b, *gla_weights),
        out_shape=(sds((bsz, tn, 512), BF16), sds((bsz, nblk, 512, tm), BF16), sds((bsz, nblk, 512, tm), BF16),
                   sds((bsz, tn, 1024), BF16), sds((bsz, tn, 256), F32)) + extra_shapes,
        out_specs=(tok(512), tr, tr, tok(1024), tok(256)) + tuple(tok(w.shape[0]) for w in gla_weights))


def _sum_directions(acc_ref, out_ref, blk, rs, cs, val):
    tot = acc_ref[blk, rs, cs] + val
    acc_ref[blk, rs, cs] = tot
    out_ref[rs, cs] = tot.astype(out_ref.dtype)


def _gla_kernel(pf_ref, pb_ref, tf_ref, tb_ref, cp_ref, ct_ref, wup_ref, bdec_ref, olo_ref, ohi_ref,
                st_ref, acc_ref, *, ns):
    i = pl.program_id(1)

    @pl.when(i == 0)
    def _():
        st_ref[...] = jnp.zeros(st_ref.shape, F32)
        acc_ref[...] = jnp.zeros(acc_ref.shape, F32)
        ctx = (cp_ref, ct_ref, None)
        _gla_step((ctx, ctx), wup_ref, bdec_ref, st_ref, None, None, emit_out=False)

    _gla_step(((pf_ref, tf_ref, ohi_ref), (pb_ref, tb_ref, olo_ref)), wup_ref, bdec_ref, st_ref, acc_ref,
              (i, ns - 1 - i), emit_out=True)


def _gla_step(dirs, wup_ref, bdec_ref, st_ref, acc_ref, blks, *, emit_out):
    lc = GLA_CHUNK
    units = _scan_units(dirs[0][0].shape[0] // lc)
    masks =[_tri(lc, d) for d in range(2)]
    gs = []
    for d, (p_ref, t_ref, o_ref) in enumerate(dirs):
        z = _dot(t_ref[...].astype(BF16), wup_ref[d]) + bdec_ref[d]
        gs.append(_log2_sigmoid(z) * (1.0 / TAU))
    bs, ops, sc, us, dcols = {}, {}, {}, {}, {}
    st = {(d, h): st_ref[d * HEADS + h] for d in range(2) for h in range(HEADS)}

    def stage2(d, j):
        bs[d, j] = _cumsum_rows(masks[d][1], gs[d][j * lc:(j + 1) * lc])

    def stage3(d, j):
        p_ref = dirs[d][0]
        rs = slice(j * lc, (j + 1) * lc)
        b = bs[d, j]
        b_last = b[lc - 1:lc, :] if d == 0 else b[0:1, :]
        b_mid = b[lc // 2 - 1:lc // 2, :] if d == 0 else b[lc // 2:lc // 2 + 1, :]
        q = p_ref[rs, 0:512]
        k = p_ref[rs, 512:1024]
        qd = q * jnp.exp2(b - b_mid).astype(BF16)
        kd = k * jnp.exp2((b_mid + LOG2_QSCALE) - b).astype(BF16)
        qi = qd * jnp.exp2(b_mid).astype(BF16)
        kl = kd * jnp.exp2(b_last - b_mid).astype(BF16)
        dec = jnp.exp2(b_last)
        ops[d, j] = (qi, kl, dec, qd, kd)

    def stage4(d, j):
        p_ref = dirs[d][0]
        rs = slice(j * lc, (j + 1) * lc)
        qi, kl, dec, qd, kd = ops[d, j]
        for h in range(HEADS):
            ks = slice(h * DK, (h + 1) * DK)
            v = p_ref[rs, 1024 + h * DV:1024 + (h + 1) * DV]
            if emit_out:
                sc[d, j, h] = jnp.where(masks[d][0], _dot_nt(qd[:, ks], kd[:, ks]), 0.0).astype(BF16)
            us[d, j, h] = _dot_tn(kl[:, ks], v)
            dcols[d, j, h] = jnp.broadcast_to(dec[:, ks], (8, DK)).T[:, 0:1]
    def stage5(d, j):
        p_ref, _, o_ref = dirs[d]
        rs = slice(j * lc, (j + 1) * lc)
        for h in range(HEADS):
            ks = slice(h * DK, (h + 1) * DK)
            if emit_out:
                v = p_ref[rs, 1024 + h * DV:1024 + (h + 1) * DV]
                o = _dot(jnp.concatenate([sc[d, j, h], ops[d, j][0][:, ks]], axis=1),
                         jnp.concatenate([v, st[d, h].astype(BF16)], axis=0))
                _sum_directions(acc_ref, o_ref, blks[d], rs, slice(h * DV, (h + 1) * DV), o)
            st[d, h] = st[d, h] * dcols[d, j, h] + us[d, j, h]

    _skewed(units, (stage2, stage3, stage4, stage5))
    for (d, h), val in st.items():
        st_ref[d * HEADS + h] = val


def _half_specs(ns, step, width):
    half = ns // 2
    lo = pl.BlockSpec((None, step, width), lambda b, i: (b, jnp.minimum(ns - 1 - i, half - 1), 0))
    hi = pl.BlockSpec((None, step, width), lambda b, i: (b, jnp.maximum(i - half, 0), 0))
    return lo, hi


def _gla_scan(pg, tlr, pg_c, tlr_c, wup, bdec):
    bn, tn, _ = pg.shape
    tc = pg_c.shape[1]
    step = min(STEP, tn)
    ns = tn // step
    fwd = lambda b, i: (b, i, 0)
    bwd = lambda b, i: (b, ns - 1 - i, 0)
    ctx = lambda b, i: (b, 0, 0)
    o_shape = jax.ShapeDtypeStruct((bn, tn // 2, HEADS * DV), BF16)
    return _pallas(
        functools.partial(_gla_kernel, ns=ns), name="gla_scan",
        grid=(bn, ns),
        in_specs=[pl.BlockSpec((None, step, 2048), fwd),
                  pl.BlockSpec((None, step, 2048), bwd),
                  pl.BlockSpec((None, step, 128), fwd),
                  pl.BlockSpec((None, step, 128), bwd),
                  pl.BlockSpec((None, tc, 2048), ctx),
                  pl.BlockSpec((None, tc, 128), ctx),
                  pl.BlockSpec((2, 128, 512), lambda b, i: (0, 0, 0)),
                  pl.BlockSpec((2, 1, 512), lambda b, i: (0, 0, 0))],
        operands=(pg, pg, tlr, tlr, pg_c, tlr_c, wup, bdec),
        out_shape=(o_shape, o_shape),
        out_specs=_half_specs(ns, step, HEADS * DV),
        scratch=[pltpu.VMEM((2 * HEADS, DK, DV), F32), pltpu.VMEM((ns, step, HEADS * DV), F32)])


def _mlstm_kernel(kf_ref, kb_ref, qtf_ref, qtb_ref, ktf_ref, ktb_ref, vf_ref, vb_ref, tf_ref, tb_ref,
                  kc_ref, qtc_ref, ktc_ref, vc_ref, tc_ref, bg_ref, hlo_ref, hhi_ref,
                  c_ref, n_ref, m_ref, acc_ref, *, ns):
    i = pl.program_id(1)

    @pl.when(i == 0)
    def _():
        c_ref[...] = jnp.zeros(c_ref.shape, F32)
        n_ref[...] = jnp.zeros(n_ref.shape, F32)
        m_ref[...] = jnp.zeros(m_ref.shape, F32)
        acc_ref[...] = jnp.zeros(acc_ref.shape, F32)
        ctx = (kc_ref, qtc_ref, ktc_ref, vc_ref, tc_ref, None)
        _mlstm_step((ctx, ctx), bg_ref, c_ref, n_ref, m_ref, None, None, emit_out=False)

    _mlstm_step(((kf_ref, qtf_ref, ktf_ref, vf_ref, tf_ref, hhi_ref),
                 (kb_ref, qtb_ref, ktb_ref, vb_ref, tb_ref, hlo_ref)),
                bg_ref, c_ref, n_ref, m_ref, acc_ref, (i, ns - 1 - i), emit_out=True)


def _mlstm_step(dirs, bg_ref, c_ref, n_ref, m_ref, acc_ref, blks, *, emit_out):
    lc = MLSTM_CHUNK
    units = _scan_units(dirs[0][0].shape[0] // lc)
    tris =[_tri(lc, d) for d in range(2)]
    lane_of = lambda d, h: GATE_LANE + 8 * d + h
    hs = lambda h: slice(h * DK, (h + 1) * DK)
    vs = lambda h: slice(h * DV, (h + 1) * DV)

    mrow = [m_ref[d:d + 1, :] for d in range(2)]
    tiles = {}
    for d, j in units:
        t_ref = dirs[d][4]
        rs = slice(j * lc, (j + 1) * lc)
        ga = (t_ref[rs, 0:128] + bg_ref[:, 0:128]) * LOG2E
        gb = t_ref[rs, 128:256] + bg_ref[:, 128:256]
        bc = _cumsum_rows(tris[d][1], _log2_sigmoid(gb))
        b_last = bc[lc - 1:lc, :] if d == 0 else bc[0:1, :]
        log_key = b_last - bc + ga
        m_new = jnp.maximum(b_last + mrow[d], jnp.max(log_key, axis=0, keepdims=True))
        tiles[d, j] = dict(rmat=ga - bc, bct=bc.T, m_in=mrow[d], wkt=jnp.exp2(log_key - m_new).T,
                           decay=jnp.exp2(b_last + mrow[d] - m_new))
        mrow[d] = m_new
    for d in range(2):
        m_ref[d:d + 1, :] = mrow[d]

    us, ncols = {}, {}
    for d, j in units:
        k_ref, _, kt_ref, v_ref = dirs[d][:4]
        rs = slice(j * lc, (j + 1) * lc)
        for h in range(HEADS):
            lane = lane_of(d, h)
            wk = tiles[d, j]["wkt"][lane:lane + 1, :]
            kwt = kt_ref[hs(h), rs] * wk.astype(BF16)
            us[d, j, h] = _dot(kwt, v_ref[rs, vs(h)])
            ncols[d, j, h] = _dot(jnp.broadcast_to(wk, (16, lc)).astype(BF16), k_ref[rs, hs(h)])[0:1]

    n_in = {}
    for d in range(2):
        for h in range(HEADS):
            idx = d * HEADS + h
            lane = lane_of(d, h)
            nvec = n_ref[idx:idx + 1, :]
            for dd, j in units:
                if dd == d:
                    n_in[d, j, h] = nvec
                    nvec = tiles[d, j]["decay"][:, lane:lane + 1] * nvec + ncols[d, j, h]
            n_ref[idx:idx + 1, :] = nvec

    lhs = {}
    if emit_out:
        for d, j in units:
            k_ref, qt_ref = dirs[d][:2]
            rs = slice(j * lc, (j + 1) * lc)
            t = tiles[d, j]
            causal_t = tris[1 - d][0]
            for h in range(HEADS):
                lane = lane_of(d, h)
                qt = qt_ref[hs(h), rs]
                kq = _dot(jnp.concatenate(
                    [k_ref[rs, hs(h)], jnp.broadcast_to(n_in[d, j, h], (16, DK)).astype(BF16)], axis=0), qt)
                rm = jnp.where(causal_t, t["rmat"][:, lane:lane + 1], -jnp.inf)
                mval = t["m_in"][:, lane:lane + 1]
                mx = jnp.maximum(mval, jnp.max(rm, axis=0, keepdims=True))
                wt = jnp.exp2(rm - mx) * kq[0:lc]
                w_inter = jnp.exp2(mval - mx)
                den = jnp.sum(wt, axis=0, keepdims=True) + w_inter * kq[lc:lc + 1]
                inv = 1.0 / jnp.maximum(jnp.abs(den), jnp.exp2(-(t["bct"][lane:lane + 1, :] + mx)))
                lhs[d, j, h] = jnp.concatenate(
                    [(wt * inv).astype(BF16), qt * (w_inter * inv).astype(BF16)], axis=0)

    for d in range(2):
        v_ref, o_ref = dirs[d][3], dirs[d][5]
        for h in range(HEADS):
            idx = d * HEADS + h
            lane = lane_of(d, h)
            cmat = c_ref[idx]
            for dd, j in units:
                if dd != d:
                    continue
                rs = slice(j * lc, (j + 1) * lc)
                if emit_out:
                    o = _dot_tn(lhs[d, j, h], jnp.concatenate([v_ref[rs, vs(h)], cmat.astype(BF16)], axis=0))
                    _sum_directions(acc_ref, o_ref, blks[d], rs, vs(h), o)
                cmat = tiles[d, j]["decay"][:, lane:lane + 1] * cmat + us[d, j, h]
            c_ref[idx] = cmat


def _mlstm_scan(lat, ctx, bgate):
    k, qt, kt, v, tg = lat
    bn, tn, _ = k.shape
    tc = ctx[0].shape[1]
    step = min(STEP, tn)
    ns = tn // step
    fwd = lambda b, i: (b, i, 0)
    bwd = lambda b, i: (b, ns - 1 - i, 0)
    assert qt.shape[1:] == (ns, 512, step) and ctx[1].shape[1:] == (1, 512, tc)
    fwd_t = lambda b, i: (b, i, 0, 0)
    bwd_t = lambda b, i: (b, ns - 1 - i, 0, 0)
    whole = lambda b, i: (b, 0, 0)
    o_shape = jax.ShapeDtypeStruct((bn, tn // 2, HEADS * DV), F32)
    both = lambda shape, f, g: [pl.BlockSpec(shape, f), pl.BlockSpec(shape, g)]
    whole_t = lambda b, i: (b, 0, 0, 0)
    ctx_specs = [pl.BlockSpec((None, tc, 512), whole), pl.BlockSpec((None, None, 512, tc), whole_t),
                 pl.BlockSpec((None, None, 512, tc), whole_t), pl.BlockSpec((None, tc, 1024), whole),
                 pl.BlockSpec((None, tc, 256), whole)]
    return _pallas(
        functools.partial(_mlstm_kernel, ns=ns), name="mlstm_scan",
        grid=(bn, ns),
        in_specs=(both((None, step, 512), fwd, bwd) + both((None, None, 512, step), fwd_t, bwd_t)
                  + both((None, None, 512, step), fwd_t, bwd_t) + both((None, step, 1024), fwd, bwd)
                  + both((None, step, 256), fwd, bwd) + ctx_specs
                  + [pl.BlockSpec((1, 256), lambda b, i: (0, 0))]),
        operands=(k, k, qt, qt, kt, kt, v, v, tg, tg, *ctx, bgate),
        out_shape=(o_shape, o_shape),
        out_specs=_half_specs(ns, step, HEADS * DV),
        scratch=[pltpu.VMEM((2 * HEADS, DK, DV), F32), pltpu.VMEM((2 * HEADS, 128), F32),
                 pltpu.VMEM((8, 128), F32), pltpu.VMEM((ns, step, HEADS * DV), F32)])


def _head_norm(o, g):
    parts = []
    for h in range(HEADS):
        oh = o[:, h * DV:(h + 1) * DV]
        parts.append(oh * lax.rsqrt(jnp.mean(oh * oh, axis=-1, keepdims=True) + EPS))
    return jnp.concatenate(parts, axis=-1) * g


def _merge_kernel(olo_ref, ohi_ref, hlo_ref, hhi_ref, po_ref, x_ref, g1_ref, gg_ref, gm_ref,
                  wbg_ref, wbm_ref, wo_ref, o_ref, *, per_b):
    groups = [(g * COL_BLOCK // MERGE_GROUPS, (g + 1) * COL_BLOCK // MERGE_GROUPS) for g in range(MERGE_GROUPS)]
    rows = [slice(a * GRID_W, b * GRID_W) for a, b in groups]
    lower = pl.program_id(0) % per_b < per_b // 2
    ys = []
    for (a, b), rs in zip(groups, rows):
        hm = jnp.concatenate([r[:, rl, :] for rl in range(a, b) for r in (hlo_ref, hhi_ref)], axis=0)
        o = jnp.where(lower, olo_ref[rs, :], ohi_ref[rs, :]).astype(F32)
        y_gla = _head_norm(o, gg_ref[...]) * po_ref[rs, 0:1024].astype(F32)
        y_m = _head_norm(hm, gm_ref[...]) * po_ref[rs, 1024:2048].astype(F32)
        ys.append((y_gla.astype(BF16), y_m.astype(BF16)))
    ds = [(_dot(y_gla, wbg_ref[...]), _dot(y_m, wbm_ref[...])) for y_gla, y_m in ys]
    ys = [(po_ref[rs, 2048:3072].astype(F32) * d_g + po_ref[rs, 3072:4096].astype(F32) * d_m).astype(BF16)
          for rs, (d_g, d_m) in zip(rows, ds)]
    mixes = [_dot(y, wo_ref[...]) for y in ys]
    for rs, mix in zip(rows, mixes):
        o_ref[rs, :] = x_ref[rs, :] + g1_ref[...] * mix


def _merge(olo, ohi, hlo4, hhi4, po, x2, mod3, gg, gm, wbg, wbm, wo):
    m = x2.shape[0]
    tm = GRID_W * COL_BLOCK
    per_b = SEQ // tm
    half = per_b // 2
    tok = lambda i: (i, 0)
    hspec = pl.BlockSpec((None, GRID_W // 2, COL_BLOCK, D), lambda i: (i // per_b, 0, i % per_b, 0))
    lo_spec = pl.BlockSpec((None, tm, D), lambda i: (i // per_b, jnp.minimum(i % per_b, half - 1), 0))
    hi_spec = pl.BlockSpec((None, tm, D), lambda i: (i // per_b, jnp.maximum(i % per_b - half, 0), 0))
    return _pallas(
        functools.partial(_merge_kernel, per_b=per_b), name="merge",
        grid=(m // tm,),
        in_specs=[lo_spec, hi_spec, hspec, hspec,
                  pl.BlockSpec((tm, 4096), tok), pl.BlockSpec((tm, D), tok),
                  pl.BlockSpec((None, 1, D), lambda i: (i // per_b, 0, 2)),
                  pl.BlockSpec((1, D), lambda i: (0, 0)), pl.BlockSpec((1, D), lambda i: (0, 0)),
                  _resident((D, D)), _resident((D, D)), _resident((D, D))],
        operands=(olo, ohi, hlo4, hhi4, po, x2, mod3, gg, gm, wbg, wbm, wo),
        out_shape=jax.ShapeDtypeStruct((m, D), F32),
        out_specs=pl.BlockSpec((tm, D), tok))


FF_TILES = ((0, 1280), (1280, 2816))


def _ffn_kernel(x_ref, sh_ref, sc_ref, g2_ref, ng_ref, fg_ref, wi_ref, wo_ref, o_ref):
    tm = x_ref.shape[0]
    rows = [slice(g * tm // FFN_GROUPS, (g + 1) * tm // FFN_GROUPS) for g in range(FFN_GROUPS)]
    us = [_norm_mod(x_ref[rs, :], ng_ref[...], sh_ref[...], sc_ref[...]) for rs in rows]
    accs = [None] * FFN_GROUPS
    for lo, hi in FF_TILES:
        abs_ = [(_dot(u, wi_ref[:, lo:hi]), _dot(u, wi_ref[:, D_FF + lo:D_FF + hi])) for u in us]
        hids = [(a * _sigmoid(a) * b).astype(BF16) for a, b in abs_]
        parts = [_dot(hid, wo_ref[lo:hi, :]) for hid in hids]
        accs = [p if acc is None else acc + p for acc, p in zip(accs, parts)]
    for rs, acc in zip(rows, accs):
        x2 = x_ref[rs, :] + g2_ref[...] * acc
        o_ref[rs, :] = x2 * lax.rsqrt(jnp.mean(x2 * x2, axis=-1, keepdims=True) + EPS) * fg_ref[...]


def _ffn(x1, mod3, ng, fg, wi, wo, tm):
    m = x1.shape[0]
    per_b = SEQ // tm
    tok = lambda i: (i, 0)
    modspec = lambda c: pl.BlockSpec((None, 1, D), lambda i: (i // per_b, 0, c))
    return _pallas(
        _ffn_kernel, name="ffn",
        grid=(m // tm,),
        in_specs=[pl.BlockSpec((tm, D), tok), modspec(3), modspec(4), modspec(5),
                  pl.BlockSpec((1, D), lambda i: (0, 0)), pl.BlockSpec((1, D), lambda i: (0, 0)),
                  _resident((D, 2 * D_FF)), _resident((D_FF, D))],
        operands=(x1, mod3, mod3, mod3, ng, fg, wi, wo),
        out_shape=jax.ShapeDtypeStruct((m, D), F32),
        out_specs=pl.BlockSpec((tm, D), tok))


def _split_w_in_kernel(w_ref, wg_ref, wm_ref, wo_ref, wtl_ref, wtg_ref):
    cols = w_ref.shape[1]
    c = lambda a, b: w_ref[a:b, :].astype(BF16)
    z = lambda n: jnp.zeros((n, cols), BF16)
    wg_ref[...] = c(0, 2048)
    wm_ref[...] = c(3104, 5152)
    wo_ref[0:1024, :] = c(2048, 3072)
    wo_ref[1024:2048, :] = c(5152, 6176)
    wo_ref[2048:4096, :] = c(6192, 8240)
    wtl_ref[...] = jnp.concatenate([c(3072, 3104), z(128 - 2 * RANK)], axis=0)
    wtg_ref[...] = jnp.concatenate(
        [z(GATE_LANE), c(6176, 6192), z(128 - GATE_LANE - 16),
         z(GATE_LANE), c(6180, 6184), z(4), c(6188, 6192), z(128 - GATE_LANE - 12)], axis=0)


def _split_w_in(w_in_t):
    tc = 256
    heights = (2048, 2048, 4096, 128, 256)
    return _pallas(
        _split_w_in_kernel, name="split_w_in",
        grid=(D // tc,),
        in_specs=[pl.BlockSpec((w_in_t.shape[0], tc), lambda i: (0, i))],
        operands=(w_in_t,),
        out_shape=tuple(jax.ShapeDtypeStruct((n, D), BF16) for n in heights),
        out_specs=tuple(pl.BlockSpec((n, tc), lambda i: (0, i)) for n in heights))


def kernel(x, c, ctx, c_ctx, w_ada, b_ada, norm1_g, w_in, gla_w_up, gla_b_dec, gla_norm_g,
           mlstm_conv_w, mlstm_conv_b, mlstm_b_gate, mlstm_norm_g, w_br_gla, w_br_mlstm, w_out,
           norm2_g, w_ffn_in, w_ffn_out, final_g):
    bsz = x.shape[0]
    row = lambda a: a.reshape(1, -1)

    cvec = jnp.concatenate([c, c_ctx[None, :], jnp.zeros((8 - bsz - 1, D), F32)], axis=0)
    mod3 = _ada(cvec, w_ada[0], row(b_ada[0])).reshape(8, 1, N_MOD)

    wg, wm, wo, wtl, wtg = _split_w_in(w_in[0].T)
    x2 = x.reshape(bsz * SEQ, D)
    ctx2 = ctx.reshape(bsz * CTX, D)
    g1n = row(norm1_g[0])
    tm = GRID_W * COL_BLOCK
    po, pg, tlr, u_lat = _inproj(x2, mod3, lambda i: i // (SEQ // tm), g1n, (wo, wg, wtl), (BF16, BF16, F32), tm,
                                 acts=(("silu", "sigmoid", "sigmoid", "sigmoid"), None, None), emit_u=True)
    conv_w = mlstm_conv_w[0]
    conv_b = row(mlstm_conv_b[0])
    m_lat = _inproj_m(u_lat.reshape(bsz, SEQ // GRID_W, GRID_W, D), None, None, None,
                      wm, wtg, conv_w, conv_b, tm, True)
    *m_ctx, pg_c, tlr_c = _inproj_m(ctx2, mod3, lambda i: bsz, g1n, wm, wtg, conv_w, conv_b, CTX, False,
                                    gla_weights=(wg, wtl))

    wup = jnp.zeros((2, 128, HEADS * DK), F32)
    wup = wup.at[0, 0:RANK].set(gla_w_up[0, 0]).at[1, RANK:2 * RANK].set(gla_w_up[0, 1]).astype(BF16)
    bdec = gla_b_dec[0].reshape(2, 1, HEADS * DK)
    o_lo, o_hi = _gla_scan(pg.reshape(bsz, SEQ, 2048), tlr.reshape(bsz, SEQ, 128),
                           pg_c.reshape(bsz, CTX, 2048), tlr_c.reshape(bsz, CTX, 128), wup, bdec)

    bgate = mlstm_b_gate[0].reshape(1, 16)
    zg = lambda n: jnp.zeros((1, n), F32)
    bg2 = jnp.concatenate([zg(GATE_LANE), bgate, zg(128 - GATE_LANE - 16),
                           zg(GATE_LANE), bgate[:, 4:8], zg(4), bgate[:, 12:16],
                           zg(128 - GATE_LANE - 12)], axis=1)
    h_lo, h_hi = _mlstm_scan(m_lat, m_ctx, bg2)

    cm4 = lambda a: a.reshape(bsz, GRID_W // 2, SEQ // GRID_W, D)
    x1 = _merge(o_lo, o_hi, cm4(h_lo), cm4(h_hi), po, x2, mod3,
                row(gla_norm_g[0]), row(mlstm_norm_g[0]),
                w_br_gla[0].astype(BF16), w_br_mlstm[0].astype(BF16), w_out[0].astype(BF16))
    out = _ffn(x1, mod3, row(norm2_g[0]), row(final_g),
               w_ffn_in[0].astype(BF16), w_ffn_out[0].astype(BF16), FFN_ROWS)
    return out.reshape(bsz, SEQ, D)
```

```python
import functools
import math

import jax
import jax.numpy as jnp
from jax import lax
from jax.experimental import pallas as pl
from jax.experimental.pallas import tpu as pltpu

D = 1024
SEQ = 4096
CTX = 256
GRID_W = 64
EPS = 1e-6
HEADS = 4
DK = 128
DV = 256
RANK = 16
TAU = 16.0
D_FF = 2816
N_MOD = 6 * D
GATE_LANE = 32
STEP = 512
GLA_STEP = 1024
GLA_CHUNK = 128
MLSTM_CHUNK = 128
COL_BLOCK = 8
MERGE_GROUPS = 2
FFN_ROWS = 1024
FFN_GROUPS = 4

LOG2E = math.log2(math.e)
LOG2_QSCALE = -0.5 * math.log2(DK)

F32 = jnp.float32
BF16 = jnp.bfloat16
VMEM_BYTES_V7X = 64 * 1024 * 1024
VMEM_CAP_BYTES = VMEM_BYTES_V7X - 8 * 1024 * 1024
VMEM_BODY_BYTES = 20 * 1024 * 1024


def _dot(a, b):
    return jnp.dot(a, b, preferred_element_type=F32)


def _dot_nt(a, b):
    return lax.dot_general(a, b, (((1,), (1,)), ((), ())), preferred_element_type=F32)


def _dot_tn(a, b):
    return lax.dot_general(a, b, (((0,), (0,)), ((), ())), preferred_element_type=F32)


def _sigmoid(x):
    return 0.5 * jnp.tanh(0.5 * x) + 0.5


def _log2_sigmoid(x):
    return jnp.minimum(x, 0.0) * LOG2E - jnp.log2(1.0 + jnp.exp2(jnp.abs(x) * (-LOG2E)))


def _cumsum_rows(tri, g):
    g1 = g.astype(BF16)
    g2 = (g - g1.astype(F32)).astype(BF16)
    return _dot(tri, g1) + _dot(tri, g2)


def _tri(n, d):
    row = lax.broadcasted_iota(jnp.int32, (n, n), 0)
    col = lax.broadcasted_iota(jnp.int32, (n, n), 1)
    causal = (col <= row) if d == 0 else (col >= row)
    return causal, jnp.where(causal, 1.0, 0.0).astype(BF16)


def _resident(shape):
    n = len(shape)
    return pl.BlockSpec(shape, lambda *_: (0,) * n, pipeline_mode=pl.Buffered(1))


def _block_bytes(spec, dtype):
    n = 1
    for dim in spec.block_shape:
        n *= 1 if dim is None else dim
    buffers = 2 if spec.pipeline_mode is None else spec.pipeline_mode.buffer_count
    return buffers * n * jnp.dtype(dtype).itemsize


def _pallas(kernel, *, name, grid, in_specs, operands, out_shape, out_specs, scratch=()):
    outs = out_shape if isinstance(out_shape, (tuple, list)) else (out_shape,)
    ospecs = out_specs if isinstance(out_specs, (tuple, list)) else (out_specs,)
    need = (sum(_block_bytes(s, a.dtype) for s, a in zip(in_specs, operands, strict=True))
            + sum(_block_bytes(s, o.dtype) for s, o in zip(ospecs, outs, strict=True))
            + sum(math.prod(s.shape) * jnp.dtype(s.dtype).itemsize for s in scratch))
    params = pltpu.CompilerParams(dimension_semantics=("arbitrary",) * len(grid),
                                  vmem_limit_bytes=min(need + VMEM_BODY_BYTES, VMEM_CAP_BYTES))
    return pl.pallas_call(kernel, out_shape=out_shape, grid=grid, in_specs=list(in_specs), out_specs=out_specs,
                          scratch_shapes=list(scratch), compiler_params=params, name=name)(*operands)


def _skewed(units, stages):
    for t in range(len(units) + len(stages) - 1):
        for s_idx, stage in enumerate(stages):
            if 0 <= t - s_idx < len(units):
                stage(*units[t - s_idx])


def _scan_units(n_sub):
    return [(d, j) for jj in range(n_sub) for d, j in ((0, jj), (1, n_sub - 1 - jj))]


def _ada_kernel(c_ref, w_ref, b_ref, o_ref):
    cv = c_ref[...]
    s = (cv * _sigmoid(cv)).astype(BF16)
    o_ref[...] = _dot(s, w_ref[...].astype(BF16)) + b_ref[...]


def _ada(cvec, w_ada, b_ada):
    tn = 1024
    return _pallas(
        _ada_kernel, name="adaln",
        grid=(N_MOD // tn,),
        in_specs=[pl.BlockSpec((8, D), lambda j: (0, 0)),
                  pl.BlockSpec((D, tn), lambda j: (0, j)),
                  pl.BlockSpec((1, tn), lambda j: (0, j))],
        operands=(cvec, w_ada, b_ada),
        out_shape=jax.ShapeDtypeStruct((8, N_MOD), F32),
        out_specs=pl.BlockSpec((8, tn), lambda j: (0, j)))


def _norm_mod_f32(x, g, sh, sc):
    return x * lax.rsqrt(jnp.mean(x * x, axis=-1, keepdims=True) + EPS) * (g * (1.0 + sc)) + sh


def _norm_mod(x, g, sh, sc):
    return _norm_mod_f32(x, g, sh, sc).astype(BF16)


def _act(v, kind):
    if kind is None:
        return v
    v = v.astype(BF16)
    s = _sigmoid(v)
    return v * s if kind == "silu" else s


def _inproj_kernel(x_ref, sh_ref, sc_ref, g_ref, *refs, acts, emit_u, n_cast):
    n_w = len(acts)
    w_refs, cast_in, outs = refs[:n_w], refs[n_w:n_w + n_cast], refs[n_w + n_cast:]
    for i_ref, o_ref in zip(cast_in, outs[len(outs) - n_cast:]):
        o_ref[...] = i_ref[...].astype(o_ref.dtype)
    uf = _norm_mod_f32(x_ref[...], g_ref[...], sh_ref[...], sc_ref[...])
    if emit_u:
        outs[n_w][...] = uf
    u = uf.astype(BF16)
    for w_ref, o_ref, act in zip(w_refs, outs[:n_w], acts):
        n = w_ref.shape[0]
        for jc, j in enumerate(range(0, n, 1024)):
            cs = slice(j, min(j + 1024, n))
            o_ref[:, cs] = _act(_dot_nt(u, w_ref[cs, :]), act[jc] if act else None).astype(o_ref.dtype)


def _inproj(x2, mod3, mod_row, norm_g, weights, out_dtypes, tm, acts=None, emit_u=False, casts=()):
    m = x2.shape[0]
    steps = m // tm
    acts = acts or (None,) * len(weights)
    widths = [w.shape[0] for w in weights] + ([D] if emit_u else [])
    dtypes = list(out_dtypes) + ([F32] if emit_u else [])
    slab = lambda a, rows: pl.BlockSpec((rows, a.shape[1]), lambda i: (i * (a.shape[0] // rows) // steps, 0))
    cast_specs = [slab(a, rows) for a, rows in casts]
    return _pallas(
        functools.partial(_inproj_kernel, acts=acts, emit_u=emit_u, n_cast=len(casts)), name="inproj",
        grid=(steps,),
        in_specs=[pl.BlockSpec((tm, D), lambda i: (i, 0)),
                  pl.BlockSpec((None, 1, D), lambda i: (mod_row(i), 0, 0)),
                  pl.BlockSpec((None, 1, D), lambda i: (mod_row(i), 0, 1)),
                  pl.BlockSpec((1, D), lambda i: (0, 0))] + [_resident(w.shape) for w in weights] + cast_specs,
        operands=(x2, mod3, mod3, norm_g, *weights, *(a for a, _ in casts)),
        out_shape=(tuple(jax.ShapeDtypeStruct((m, n), dt) for n, dt in zip(widths, dtypes))
                   + tuple(jax.ShapeDtypeStruct(a.shape, BF16) for a, _ in casts)),
        out_specs=tuple(pl.BlockSpec((tm, n), lambda i: (i, 0)) for n in widths) + tuple(cast_specs))


def _inproj_m_kernel(*refs, colmajor, nblk):
    if colmajor:
        u_ref, hp_ref, hn_ref, wm_ref, wtg_ref, cw_ref, cb_ref = refs[:7]
        u = jnp.concatenate([u_ref[:, cl, :] for cl in range(COL_BLOCK)] + [hp_ref[7], hn_ref[0]],
                            axis=0).astype(BF16)
        k_ref, qt_ref, kt_ref, v_ref, tg_ref = refs[7:]
    else:
        x_ref, sh_ref, sc_ref, g_ref, wm_ref, wtg_ref, cw_ref, cb_ref, wg_ref, wtl_ref = refs[:10]
        k_ref, qt_ref, kt_ref, v_ref, tg_ref, pg_ref, tlr_ref = refs[10:]
        u = _norm_mod(x_ref[...], g_ref[...], sh_ref[...], sc_ref[...])
        pg_ref[...] = _dot_nt(u, wg_ref[...]).astype(BF16)
        tlr_ref[...] = _dot_nt(u, wtl_ref[...])
    n = k_ref.shape[0]
    um = u[0:n]
    pres = [_dot_nt(u, wm_ref[c * 512:(c + 1) * 512, :]) for c in range(2)]
    v_ref[...] = _dot_nt(um, wm_ref[1024:2048, :]).astype(BF16)
    tg_ref[...] = _dot_nt(um, wtg_ref[...])
    row8 = lax.broadcasted_iota(jnp.int32, (8, 512), 0)
    j = pl.program_id(0) % nblk
    for c, pre in enumerate(pres):
        cs = slice(c * 512, (c + 1) * 512)
        a = pre[0:n]
        if colmajor:
            prev_row = jnp.where(j > 0, pre[n + 7:n + 8], 0.0)
            next_row = jnp.where(j < nblk - 1, pre[n + 8:n + 9], 0.0)
        else:
            prev_row = next_row = jnp.zeros((1, 512), F32)
        ap = pltpu.roll(a, 1, axis=0)
        ap = jnp.concatenate([jnp.where(row8 == 0, prev_row, ap[0:8]), ap[8:]], axis=0)
        an = pltpu.roll(a, n - 1, axis=0)
        an = jnp.concatenate([an[0:n - 8], jnp.where(row8 == 7, next_row, an[n - 8:])], axis=0)
        conv = ap * cw_ref[0:1, cs] + a * cw_ref[1:2, cs] + an * cw_ref[2:3, cs] + cb_ref[:, cs]
        y = conv * _sigmoid(conv)
        if c == 0:
            qt_ref[...] = y.T.astype(BF16)
        else:
            y = y * (DK ** -0.5)
            k_ref[...] = y.astype(BF16)
            kt_ref[...] = y.T.astype(BF16)


def _inproj_m(xv, mod3, mod_row, norm_g, wm, wtg, conv_w, conv_b, tm, colmajor, gla_weights=()):
    full = lambda shape: pl.BlockSpec(shape, lambda i: (0,) * len(shape))
    mod_specs = [pl.BlockSpec((None, 1, D), lambda i: (mod_row(i), 0, 0)),
                 pl.BlockSpec((None, 1, D), lambda i: (mod_row(i), 0, 1)), full((1, D))]
    mod_args = (mod3, mod3, norm_g)
    if colmajor:
        mod_specs, mod_args = [], ()
        bsz = xv.shape[0]
        tn = SEQ
        nblk = GRID_W // COL_BLOCK
        blk = (None, GRID_W, COL_BLOCK, D)
        halo = (None, 8, COL_BLOCK, D)
        x_specs = [pl.BlockSpec(blk, lambda i: (i // nblk, 0, i % nblk, 0)),
                   pl.BlockSpec(halo, lambda i: (i // nblk, GRID_W // 8 - 1, jnp.maximum(i % nblk - 1, 0), 0)),
                   pl.BlockSpec(halo, lambda i: (i // nblk, 0, jnp.minimum(i % nblk + 1, nblk - 1), 0))]
        xs = (xv, xv, xv)
    else:
        tn = tm
        bsz = xv.shape[0] // tn
        nblk = 1
        x_specs = [pl.BlockSpec((tm, D), lambda i: (i, 0))]
        xs = (xv,)
    extra_shapes = tuple(jax.ShapeDtypeStruct((bsz, tn, w.shape[0]), dt) for w, dt in zip(gla_weights, (BF16, F32)))
    tok = lambda w: pl.BlockSpec((None, tm, w), lambda i: (i // nblk, i % nblk, 0))
    tr = pl.BlockSpec((None, None, 512, tm), lambda i: (i // nblk, i % nblk, 0, 0))
    sds = jax.ShapeDtypeStruct
    return _pallas(
        functools.partial(_inproj_m_kernel, colmajor=colmajor, nblk=nblk),
        name="inproj_m_cm" if colmajor else "inproj_m",
        grid=(bsz * nblk,),
        in_specs=x_specs + mod_specs + [_resident(wm.shape), _resident(wtg.shape), full((3, 1024)),
                                        full((1, 1024))] + [_resident(w.shape) for w in gla_weights],
        operands=(*xs, *mod_args, wm, wtg, conv_w, conv_b, *gla_weights),
        out_shape=(sds((bsz, tn, 512), BF16), sds((bsz, nblk, 512, tm), BF16), sds((bsz, nblk, 512, tm), BF16),
                   sds((bsz, tn, 1024), BF16), sds((bsz, tn, 256), F32)) + extra_shapes,
        out_specs=(tok(512), tr, tr, tok(1024), tok(256)) + tuple(tok(w.shape[0]) for w in gla_weights))


def _sum_directions(acc_ref, out_ref, blk, rs, cs, val):
    tot = acc_ref[blk, rs, cs] + val
    acc_ref[blk, rs, cs] = tot
    out_ref[rs, cs] = tot.astype(out_ref.dtype)


def _gla_kernel(pf_ref, pb_ref, tf_ref, tb_ref, cp_ref, ct_ref, wup_ref, bdec_ref, olo_ref, ohi_ref,
                st_ref, acc_ref, *, ns):
    i = pl.program_id(1)

    @pl.when(i == 0)
    def _():
        st_ref[...] = jnp.zeros(st_ref.shape, F32)
        acc_ref[...] = jnp.zeros(acc_ref.shape, F32)
        ctx = (cp_ref, ct_ref, None)
        _gla_step((ctx, ctx), wup_ref, bdec_ref, st_ref, None, None, emit_out=False)

    _gla_step(((pf_ref, tf_ref, ohi_ref), (pb_ref, tb_ref, olo_ref)), wup_ref, bdec_ref, st_ref, acc_ref,
              (i, ns - 1 - i), emit_out=True)


def _gla_step(dirs, wup_ref, bdec_ref, st_ref, acc_ref, blks, *, emit_out):
    lc = GLA_CHUNK
    units = _scan_units(dirs[0][0].shape[0] // lc)
    masks =[_tri(lc, d) for d in range(2)]
    gs = []
    for d, (p_ref, t_ref, o_ref) in enumerate(dirs):
        z = _dot(t_ref[...].astype(BF16), wup_ref[d]) + bdec_ref[d]
        gs.append(_log2_sigmoid(z) * (1.0 / TAU))
    bs, ops, sc, us, dcols = {}, {}, {}, {}, {}
    st = {(d, h): st_ref[d * HEADS + h] for d in range(2) for h in range(HEADS)}

    def stage2(d, j):
        bs[d, j] = _cumsum_rows(masks[d][1], gs[d][j * lc:(j + 1) * lc])

    def stage3(d, j):
        p_ref = dirs[d][0]
        rs = slice(j * lc, (j + 1) * lc)
        b = bs[d, j]
        b_last = b[lc - 1:lc, :] if d == 0 else b[0:1, :]
        b_mid = b[lc // 2 - 1:lc // 2, :] if d == 0 else b[lc // 2:lc // 2 + 1, :]
        q = p_ref[rs, 0:512]
        k = p_ref[rs, 512:1024]
        qd = q * jnp.exp2(b - b_mid).astype(BF16)
        kd = k * jnp.exp2((b_mid + LOG2_QSCALE) - b).astype(BF16)
        qi = qd * jnp.exp2(b_mid).astype(BF16)
        kl = kd * jnp.exp2(b_last - b_mid).astype(BF16)
        dec = jnp.exp2(b_last)
        ops[d, j] = (qi, kl, dec, qd, kd)

    def stage4(d, j):
        p_ref = dirs[d][0]
        rs = slice(j * lc, (j + 1) * lc)
        qi, kl, dec, qd, kd = ops[d, j]
        for h in range(HEADS):
            ks = slice(h * DK, (h + 1) * DK)
            v = p_ref[rs, 1024 + h * DV:1024 + (h + 1) * DV]
            if emit_out:
                sc[d, j, h] = jnp.where(masks[d][0], _dot_nt(qd[:, ks], kd[:, ks]), 0.0).astype(BF16)
            us[d, j, h] = _dot_tn(kl[:, ks], v)
            dcols[d, j, h] = jnp.broadcast_to(dec[:, ks], (8, DK)).T[:, 0:1]
    def stage5(d, j):
        p_ref, _, o_ref = dirs[d]
        rs = slice(j * lc, (j + 1) * lc)
        for h in range(HEADS):
            ks = slice(h * DK, (h + 1) * DK)
            if emit_out:
                v = p_ref[rs, 1024 + h * DV:1024 + (h + 1) * DV]
                o = _dot(jnp.concatenate([sc[d, j, h], ops[d, j][0][:, ks]], axis=1),
                         jnp.concatenate([v, st[d, h].astype(BF16)], axis=0))
                _sum_directions(acc_ref, o_ref, blks[d], rs, slice(h * DV, (h + 1) * DV), o)
            st[d, h] = st[d, h] * dcols[d, j, h] + us[d, j, h]

    _skewed(units, (stage2, stage3, stage4, stage5))
    for (d, h), val in st.items():
        st_ref[d * HEADS + h] = val


def _half_specs(ns, step, width):
    half = ns // 2
    lo = pl.BlockSpec((None, step, width), lambda b, i: (b, jnp.minimum(ns - 1 - i, half - 1), 0))
    hi = pl.BlockSpec((None, step, width), lambda b, i: (b, jnp.maximum(i - half, 0), 0))
    return lo, hi


def _gla_scan(pg, tlr, pg_c, tlr_c, wup, bdec):
    bn, tn, _ = pg.shape
    tc = pg_c.shape[1]
    step = min(GLA_STEP, tn)
    ns = tn // step
    fwd = lambda b, i: (b, i, 0)
    bwd = lambda b, i: (b, ns - 1 - i, 0)
    ctx = lambda b, i: (b, 0, 0)
    o_shape = jax.ShapeDtypeStruct((bn, tn // 2, HEADS * DV), BF16)
    return _pallas(
        functools.partial(_gla_kernel, ns=ns), name="gla_scan",
        grid=(bn, ns),
        in_specs=[pl.BlockSpec((None, step, 2048), fwd),
                  pl.BlockSpec((None, step, 2048), bwd),
                  pl.BlockSpec((None, step, 128), fwd),
                  pl.BlockSpec((None, step, 128), bwd),
                  pl.BlockSpec((None, tc, 2048), ctx),
                  pl.BlockSpec((None, tc, 128), ctx),
                  pl.BlockSpec((2, 128, 512), lambda b, i: (0, 0, 0)),
                  pl.BlockSpec((2, 1, 512), lambda b, i: (0, 0, 0))],
        operands=(pg, pg, tlr, tlr, pg_c, tlr_c, wup, bdec),
        out_shape=(o_shape, o_shape),
        out_specs=_half_specs(ns, step, HEADS * DV),
        scratch=[pltpu.VMEM((2 * HEADS, DK, DV), F32), pltpu.VMEM((ns, step, HEADS * DV), F32)])


def _mlstm_kernel(kf_ref, kb_ref, qtf_ref, qtb_ref, ktf_ref, ktb_ref, vf_ref, vb_ref, tf_ref, tb_ref,
                  kc_ref, qtc_ref, ktc_ref, vc_ref, tc_ref, bg_ref, hlo_ref, hhi_ref,
                  c_ref, n_ref, m_ref, acc_ref, *, ns):
    i = pl.program_id(1)

    @pl.when(i == 0)
    def _():
        c_ref[...] = jnp.zeros(c_ref.shape, F32)
        n_ref[...] = jnp.zeros(n_ref.shape, F32)
        m_ref[...] = jnp.zeros(m_ref.shape, F32)
        acc_ref[...] = jnp.zeros(acc_ref.shape, F32)
        ctx = (kc_ref, qtc_ref, ktc_ref, vc_ref, tc_ref, None)
        _mlstm_step((ctx, ctx), bg_ref, c_ref, n_ref, m_ref, None, None, emit_out=False)

    _mlstm_step(((kf_ref, qtf_ref, ktf_ref, vf_ref, tf_ref, hhi_ref),
                 (kb_ref, qtb_ref, ktb_ref, vb_ref, tb_ref, hlo_ref)),
                bg_ref, c_ref, n_ref, m_ref, acc_ref, (i, ns - 1 - i), emit_out=True)


def _mlstm_step(dirs, bg_ref, c_ref, n_ref, m_ref, acc_ref, blks, *, emit_out):
    lc = MLSTM_CHUNK
    units = _scan_units(dirs[0][0].shape[0] // lc)
    tris =[_tri(lc, d) for d in range(2)]
    lane_of = lambda d, h: GATE_LANE + 8 * d + h
    hs = lambda h: slice(h * DK, (h + 1) * DK)
    vs = lambda h: slice(h * DV, (h + 1) * DV)

    mrow = [m_ref[d:d + 1, :] for d in range(2)]
    tiles = {}
    for d, j in units:
        t_ref = dirs[d][4]
        rs = slice(j * lc, (j + 1) * lc)
        ga = (t_ref[rs, 0:128] + bg_ref[:, 0:128]) * LOG2E
        gb = t_ref[rs, 128:256] + bg_ref[:, 128:256]
        bc = _cumsum_rows(tris[d][1], _log2_sigmoid(gb))
        b_last = bc[lc - 1:lc, :] if d == 0 else bc[0:1, :]
        log_key = b_last - bc + ga
        m_new = jnp.maximum(b_last + mrow[d], jnp.max(log_key, axis=0, keepdims=True))
        tiles[d, j] = dict(rmat=ga - bc, bct=bc.T, m_in=mrow[d], wkt=jnp.exp2(log_key - m_new).T,
                           decay=jnp.exp2(b_last + mrow[d] - m_new))
        mrow[d] = m_new
    for d in range(2):
        m_ref[d:d + 1, :] = mrow[d]

    us, ncols = {}, {}
    for d, j in units:
        k_ref, _, kt_ref, v_ref = dirs[d][:4]
        rs = slice(j * lc, (j + 1) * lc)
        for h in range(HEADS):
            lane = lane_of(d, h)
            wk = tiles[d, j]["wkt"][lane:lane + 1, :]
            kwt = kt_ref[hs(h), rs] * wk.astype(BF16)
            us[d, j, h] = _dot(kwt, v_ref[rs, vs(h)])
            ncols[d, j, h] = _dot(jnp.broadcast_to(wk, (16, lc)).astype(BF16), k_ref[rs, hs(h)])[0:1]

    n_in = {}
    for d in range(2):
        for h in range(HEADS):
            idx = d * HEADS + h
            lane = lane_of(d, h)
            nvec = n_ref[idx:idx + 1, :]
            for dd, j in units:
                if dd == d:
                    n_in[d, j, h] = nvec
                    nvec = tiles[d, j]["decay"][:, lane:lane + 1] * nvec + ncols[d, j, h]
            n_ref[idx:idx + 1, :] = nvec

    lhs = {}
    if emit_out:
        for d, j in units:
            k_ref, qt_ref = dirs[d][:2]
            rs = slice(j * lc, (j + 1) * lc)
            t = tiles[d, j]
            causal_t = tris[1 - d][0]
            for h in range(HEADS):
                lane = lane_of(d, h)
                qt = qt_ref[hs(h), rs]
                kq = _dot(jnp.concatenate(
                    [k_ref[rs, hs(h)], jnp.broadcast_to(n_in[d, j, h], (16, DK)).astype(BF16)], axis=0), qt)
                rm = jnp.where(causal_t, t["rmat"][:, lane:lane + 1], -jnp.inf)
                mval = t["m_in"][:, lane:lane + 1]
                mx = jnp.maximum(mval, jnp.max(rm, axis=0, keepdims=True))
                wt = jnp.exp2(rm - mx) * kq[0:lc]
                w_inter = jnp.exp2(mval - mx)
                den = jnp.sum(wt, axis=0, keepdims=True) + w_inter * kq[lc:lc + 1]
                inv = 1.0 / jnp.maximum(jnp.abs(den), jnp.exp2(-(t["bct"][lane:lane + 1, :] + mx)))
                lhs[d, j, h] = jnp.concatenate(
                    [(wt * inv).astype(BF16), qt * (w_inter * inv).astype(BF16)], axis=0)

    for d in range(2):
        v_ref, o_ref = dirs[d][3], dirs[d][5]
        for h in range(HEADS):
            idx = d * HEADS + h
            lane = lane_of(d, h)
            cmat = c_ref[idx]
            for dd, j in units:
                if dd != d:
                    continue
                rs = slice(j * lc, (j + 1) * lc)
                if emit_out:
                    o = _dot_tn(lhs[d, j, h], jnp.concatenate([v_ref[rs, vs(h)], cmat.astype(BF16)], axis=0))
                    _sum_directions(acc_ref, o_ref, blks[d], rs, vs(h), o)
                cmat = tiles[d, j]["decay"][:, lane:lane + 1] * cmat + us[d, j, h]
            c_ref[idx] = cmat


def _mlstm_scan(lat, ctx, bgate):
    k, qt, kt, v, tg = lat
    bn, tn, _ = k.shape
    tc = ctx[0].shape[1]
    step = min(STEP, tn)
    ns = tn // step
    fwd = lambda b, i: (b, i, 0)
    bwd = lambda b, i: (b, ns - 1 - i, 0)
    assert qt.shape[1:] == (ns, 512, step) and ctx[1].shape[1:] == (1, 512, tc)
    fwd_t = lambda b, i: (b, i, 0, 0)
    bwd_t = lambda b, i: (b, ns - 1 - i, 0, 0)
    whole = lambda b, i: (b, 0, 0)
    o_shape = jax.ShapeDtypeStruct((bn, tn // 2, HEADS * DV), F32)
    both = lambda shape, f, g: [pl.BlockSpec(shape, f), pl.BlockSpec(shape, g)]
    whole_t = lambda b, i: (b, 0, 0, 0)
    ctx_specs = [pl.BlockSpec((None, tc, 512), whole), pl.BlockSpec((None, None, 512, tc), whole_t),
                 pl.BlockSpec((None, None, 512, tc), whole_t), pl.BlockSpec((None, tc, 1024), whole),
                 pl.BlockSpec((None, tc, 256), whole)]
    return _pallas(
        functools.partial(_mlstm_kernel, ns=ns), name="mlstm_scan",
        grid=(bn, ns),
        in_specs=(both((None, step, 512), fwd, bwd) + both((None, None, 512, step), fwd_t, bwd_t)
                  + both((None, None, 512, step), fwd_t, bwd_t) + both((None, step, 1024), fwd, bwd)
                  + both((None, step, 256), fwd, bwd) + ctx_specs
                  + [pl.BlockSpec((1, 256), lambda b, i: (0, 0))]),
        operands=(k, k, qt, qt, kt, kt, v, v, tg, tg, *ctx, bgate),
        out_shape=(o_shape, o_shape),
        out_specs=_half_specs(ns, step, HEADS * DV),
        scratch=[pltpu.VMEM((2 * HEADS, DK, DV), F32), pltpu.VMEM((2 * HEADS, 128), F32),
                 pltpu.VMEM((8, 128), F32), pltpu.VMEM((ns, step, HEADS * DV), F32)])


def _head_norm(o, g):
    parts = []
    for h in range(HEADS):
        oh = o[:, h * DV:(h + 1) * DV]
        parts.append(oh * lax.rsqrt(jnp.mean(oh * oh, axis=-1, keepdims=True) + EPS))
    return jnp.concatenate(parts, axis=-1) * g


def _merge_kernel(olo_ref, ohi_ref, hlo_ref, hhi_ref, po_ref, x_ref, g1_ref, gg_ref, gm_ref,
                  wbg_ref, wbm_ref, wo_ref, o_ref, *, per_b):
    groups = [(g * COL_BLOCK // MERGE_GROUPS, (g + 1) * COL_BLOCK // MERGE_GROUPS) for g in range(MERGE_GROUPS)]
    rows = [slice(a * GRID_W, b * GRID_W) for a, b in groups]
    lower = pl.program_id(0) % per_b < per_b // 2
    ys = []
    for (a, b), rs in zip(groups, rows):
        hm = jnp.concatenate([r[:, rl, :] for rl in range(a, b) for r in (hlo_ref, hhi_ref)], axis=0)
        o = jnp.where(lower, olo_ref[rs, :], ohi_ref[rs, :]).astype(F32)
        y_gla = _head_norm(o, gg_ref[...]) * po_ref[rs, 0:1024].astype(F32)
        y_m = _head_norm(hm, gm_ref[...]) * po_ref[rs, 1024:2048].astype(F32)
        ys.append((y_gla.astype(BF16), y_m.astype(BF16)))
    ds = [(_dot(y_gla, wbg_ref[...]), _dot(y_m, wbm_ref[...])) for y_gla, y_m in ys]
    ys = [(po_ref[rs, 2048:3072].astype(F32) * d_g + po_ref[rs, 3072:4096].astype(F32) * d_m).astype(BF16)
          for rs, (d_g, d_m) in zip(rows, ds)]
    mixes = [_dot(y, wo_ref[...]) for y in ys]
    for rs, mix in zip(rows, mixes):
        o_ref[rs, :] = x_ref[rs, :] + g1_ref[...] * mix


def _merge(olo, ohi, hlo4, hhi4, po, x2, mod3, gg, gm, wbg, wbm, wo):
    m = x2.shape[0]
    tm = GRID_W * COL_BLOCK
    per_b = SEQ // tm
    half = per_b // 2
    tok = lambda i: (i, 0)
    hspec = pl.BlockSpec((None, GRID_W // 2, COL_BLOCK, D), lambda i: (i // per_b, 0, i % per_b, 0))
    lo_spec = pl.BlockSpec((None, tm, D), lambda i: (i // per_b, jnp.minimum(i % per_b, half - 1), 0))
    hi_spec = pl.BlockSpec((None, tm, D), lambda i: (i // per_b, jnp.maximum(i % per_b - half, 0), 0))
    return _pallas(
        functools.partial(_merge_kernel, per_b=per_b), name="merge",
        grid=(m // tm,),
        in_specs=[lo_spec, hi_spec, hspec, hspec,
                  pl.BlockSpec((tm, 4096), tok), pl.BlockSpec((tm, D), tok),
                  pl.BlockSpec((None, 1, D), lambda i: (i // per_b, 0, 2)),
                  pl.BlockSpec((1, D), lambda i: (0, 0)), pl.BlockSpec((1, D), lambda i: (0, 0)),
                  _resident((D, D)), _resident((D, D)), _resident((D, D))],
        operands=(olo, ohi, hlo4, hhi4, po, x2, mod3, gg, gm, wbg, wbm, wo),
        out_shape=jax.ShapeDtypeStruct((m, D), F32),
        out_specs=pl.BlockSpec((tm, D), tok))


FF_TILES = ((0, 1280), (1280, 2816))


def _ffn_kernel(x_ref, sh_ref, sc_ref, g2_ref, ng_ref, fg_ref, wi_ref, wo_ref, o_ref):
    tm = x_ref.shape[0]
    rows = [slice(g * tm // FFN_GROUPS, (g + 1) * tm // FFN_GROUPS) for g in range(FFN_GROUPS)]
    us = [_norm_mod(x_ref[rs, :], ng_ref[...], sh_ref[...], sc_ref[...]) for rs in rows]
    accs = [None] * FFN_GROUPS
    for lo, hi in FF_TILES:
        abs_ = [(_dot(u, wi_ref[:, lo:hi]), _dot(u, wi_ref[:, D_FF + lo:D_FF + hi])) for u in us]
        hids = [(a * _sigmoid(a) * b).astype(BF16) for a, b in abs_]
        parts = [_dot(hid, wo_ref[lo:hi, :]) for hid in hids]
        accs = [p if acc is None else acc + p for acc, p in zip(accs, parts)]
    for rs, acc in zip(rows, accs):
        x2 = x_ref[rs, :] + g2_ref[...] * acc
        o_ref[rs, :] = x2 * lax.rsqrt(jnp.mean(x2 * x2, axis=-1, keepdims=True) + EPS) * fg_ref[...]


def _ffn(x1, mod3, ng, fg, wi, wo, tm):
    m = x1.shape[0]
    per_b = SEQ // tm
    tok = lambda i: (i, 0)
    modspec = lambda c: pl.BlockSpec((None, 1, D), lambda i: (i // per_b, 0, c))
    return _pallas(
        _ffn_kernel, name="ffn",
        grid=(m // tm,),
        in_specs=[pl.BlockSpec((tm, D), tok), modspec(3), modspec(4), modspec(5),
                  pl.BlockSpec((1, D), lambda i: (0, 0)), pl.BlockSpec((1, D), lambda i: (0, 0)),
                  _resident((D, 2 * D_FF)), _resident((D_FF, D))],
        operands=(x1, mod3, mod3, mod3, ng, fg, wi, wo),
        out_shape=jax.ShapeDtypeStruct((m, D), F32),
        out_specs=pl.BlockSpec((tm, D), tok))


def _split_w_in_kernel(w_ref, wg_ref, wm_ref, wo_ref, wtl_ref, wtg_ref):
    cols = w_ref.shape[1]
    c = lambda a, b: w_ref[a:b, :].astype(BF16)
    z = lambda n: jnp.zeros((n, cols), BF16)
    wg_ref[...] = c(0, 2048)
    wm_ref[...] = c(3104, 5152)
    wo_ref[0:1024, :] = c(2048, 3072)
    wo_ref[1024:2048, :] = c(5152, 6176)
    wo_ref[2048:4096, :] = c(6192, 8240)
    wtl_ref[...] = jnp.concatenate([c(3072, 3104), z(128 - 2 * RANK)], axis=0)
    wtg_ref[...] = jnp.concatenate(
        [z(GATE_LANE), c(6176, 6192), z(128 - GATE_LANE - 16),
         z(GATE_LANE), c(6180, 6184), z(4), c(6188, 6192), z(128 - GATE_LANE - 12)], axis=0)


def _split_w_in(w_in_t):
    tc = 256
    heights = (2048, 2048, 4096, 128, 256)
    return _pallas(
        _split_w_in_kernel, name="split_w_in",
        grid=(D // tc,),
        in_specs=[pl.BlockSpec((w_in_t.shape[0], tc), lambda i: (0, i))],
        operands=(w_in_t,),
        out_shape=tuple(jax.ShapeDtypeStruct((n, D), BF16) for n in heights),
        out_specs=tuple(pl.BlockSpec((n, tc), lambda i: (0, i)) for n in heights))


def kernel(x, c, ctx, c_ctx, w_ada, b_ada, norm1_g, w_in, gla_w_up, gla_b_dec, gla_norm_g,
           mlstm_conv_w, mlstm_conv_b, mlstm_b_gate, mlstm_norm_g, w_br_gla, w_br_mlstm, w_out,
           norm2_g, w_ffn_in, w_ffn_out, final_g):
    bsz = x.shape[0]
    row = lambda a: a.reshape(1, -1)

    cvec = jnp.concatenate([c, c_ctx[None, :], jnp.zeros((8 - bsz - 1, D), F32)], axis=0)
    mod3 = _ada(cvec, w_ada[0], row(b_ada[0])).reshape(8, 1, N_MOD)

    wg, wm, wo, wtl, wtg = _split_w_in(w_in[0].T)
    x2 = x.reshape(bsz * SEQ, D)
    ctx2 = ctx.reshape(bsz * CTX, D)
    g1n = row(norm1_g[0])
    tm = GRID_W * COL_BLOCK
    steps = bsz * SEQ // tm
    casts = ((w_br_gla[0], D // steps), (w_br_mlstm[0], D // steps), (w_out[0], D // steps),
             (w_ffn_in[0], D // steps), (w_ffn_out[0], 2 * D_FF // steps))
    po, pg, tlr, u_lat, wbg, wbm, wout, wfi, wfo = _inproj(
        x2, mod3, lambda i: i // (SEQ // tm), g1n, (wo, wg, wtl), (BF16, BF16, F32), tm,
        acts=(("silu", "sigmoid", "sigmoid", "sigmoid"), None, None), emit_u=True, casts=casts)
    conv_w = mlstm_conv_w[0]
    conv_b = row(mlstm_conv_b[0])
    m_lat = _inproj_m(u_lat.reshape(bsz, SEQ // GRID_W, GRID_W, D), None, None, None,
                      wm, wtg, conv_w, conv_b, tm, True)
    *m_ctx, pg_c, tlr_c = _inproj_m(ctx2, mod3, lambda i: bsz, g1n, wm, wtg, conv_w, conv_b, CTX, False,
                                    gla_weights=(wg, wtl))

    wup = jnp.zeros((2, 128, HEADS * DK), F32)
    wup = wup.at[0, 0:RANK].set(gla_w_up[0, 0]).at[1, RANK:2 * RANK].set(gla_w_up[0, 1]).astype(BF16)
    bdec = gla_b_dec[0].reshape(2, 1, HEADS * DK)
    o_lo, o_hi = _gla_scan(pg.reshape(bsz, SEQ, 2048), tlr.reshape(bsz, SEQ, 128),
                           pg_c.reshape(bsz, CTX, 2048), tlr_c.reshape(bsz, CTX, 128), wup, bdec)

    bgate = mlstm_b_gate[0].reshape(1, 16)
    zg = lambda n: jnp.zeros((1, n), F32)
    bg2 = jnp.concatenate([zg(GATE_LANE), bgate, zg(128 - GATE_LANE - 16),
                           zg(GATE_LANE), bgate[:, 4:8], zg(4), bgate[:, 12:16],
                           zg(128 - GATE_LANE - 12)], axis=1)
    h_lo, h_hi = _mlstm_scan(m_lat, m_ctx, bg2)

    cm4 = lambda a: a.reshape(bsz, GRID_W // 2, SEQ // GRID_W, D)
    x1 = _merge(o_lo, o_hi, cm4(h_lo), cm4(h_hi), po, x2, mod3,
                row(gla_norm_g[0]), row(mlstm_norm_g[0]), wbg, wbm, wout)
    out = _ffn(x1, mod3, row(norm2_g[0]), row(final_g), wfi, wfo, FFN_ROWS)
    return out.reshape(bsz, SEQ, D)
```

```python
import functools
import math

import jax
import jax.numpy as jnp
from jax import lax
from jax.experimental import pallas as pl
from jax.experimental.pallas import tpu as pltpu

D = 1024
SEQ = 4096
CTX = 256
GRID_W = 64
EPS = 1e-6
HEADS = 4
DK = 128
DV = 256
RANK = 16
TAU = 16.0
D_FF = 2816
N_MOD = 6 * D
GATE_LANE = 32
STEP = 512
GLA_STEP = 512
GLA_CHUNK = 128
MLSTM_CHUNK = 128
COL_BLOCK = 8
MERGE_GROUPS = 2
FFN_ROWS = 1024
FFN_GROUPS = 4

LOG2E = math.log2(math.e)
LOG2_QSCALE = -0.5 * math.log2(DK)

F32 = jnp.float32
BF16 = jnp.bfloat16
VMEM_BYTES_V7X = 64 * 1024 * 1024
VMEM_CAP_BYTES = VMEM_BYTES_V7X - 8 * 1024 * 1024
VMEM_BODY_BYTES = 48 * 1024 * 1024


def _dot(a, b):
    return jnp.dot(a, b, preferred_element_type=F32)


def _dot_nt(a, b):
    return lax.dot_general(a, b, (((1,), (1,)), ((), ())), preferred_element_type=F32)


def _dot_tn(a, b):
    return lax.dot_general(a, b, (((0,), (0,)), ((), ())), preferred_element_type=F32)


def _sigmoid(x):
    return 0.5 * jnp.tanh(0.5 * x) + 0.5


def _log2_sigmoid(x):
    return jnp.minimum(x, 0.0) * LOG2E - jnp.log2(1.0 + jnp.exp2(jnp.abs(x) * (-LOG2E)))


def _cumsum_rows(tri, g):
    g1 = g.astype(BF16)
    g2 = (g - g1.astype(F32)).astype(BF16)
    return _dot(tri, g1) + _dot(tri, g2)


def _tri(n, d):
    row = lax.broadcasted_iota(jnp.int32, (n, n), 0)
    col = lax.broadcasted_iota(jnp.int32, (n, n), 1)
    causal = (col <= row) if d == 0 else (col >= row)
    return causal, jnp.where(causal, 1.0, 0.0).astype(BF16)


def _resident(shape):
    n = len(shape)
    return pl.BlockSpec(shape, lambda *_: (0,) * n, pipeline_mode=pl.Buffered(1))


def _block_bytes(spec, dtype):
    n = 1
    for dim in spec.block_shape:
        n *= 1 if dim is None else dim
    buffers = 2 if spec.pipeline_mode is None else spec.pipeline_mode.buffer_count
    return buffers * n * jnp.dtype(dtype).itemsize


def _pallas(kernel, *, name, grid, in_specs, operands, out_shape, out_specs, scratch=()):
    outs = out_shape if isinstance(out_shape, (tuple, list)) else (out_shape,)
    ospecs = out_specs if isinstance(out_specs, (tuple, list)) else (out_specs,)
    need = (sum(_block_bytes(s, a.dtype) for s, a in zip(in_specs, operands, strict=True))
            + sum(_block_bytes(s, o.dtype) for s, o in zip(ospecs, outs, strict=True))
            + sum(math.prod(s.shape) * jnp.dtype(s.dtype).itemsize for s in scratch))
    params = pltpu.CompilerParams(dimension_semantics=("arbitrary",) * len(grid),
                                  vmem_limit_bytes=min(need + VMEM_BODY_BYTES, VMEM_CAP_BYTES))
    return pl.pallas_call(kernel, out_shape=out_shape, grid=grid, in_specs=list(in_specs), out_specs=out_specs,
                          scratch_shapes=list(scratch), compiler_params=params, name=name)(*operands)


def _skewed(units, stages):
    for t in range(len(units) + len(stages) - 1):
        for s_idx, stage in enumerate(stages):
            if 0 <= t - s_idx < len(units):
                stage(*units[t - s_idx])


def _scan_units(n_sub):
    return [(d, j) for jj in range(n_sub) for d, j in ((0, jj), (1, n_sub - 1 - jj))]


def _ada_kernel(c_ref, w_ref, b_ref, o_ref):
    cv = c_ref[...]
    s = (cv * _sigmoid(cv)).astype(BF16)
    o_ref[...] = _dot(s, w_ref[...].astype(BF16)) + b_ref[...]


def _ada(cvec, w_ada, b_ada):
    tn = 1024
    return _pallas(
        _ada_kernel, name="adaln",
        grid=(N_MOD // tn,),
        in_specs=[pl.BlockSpec((8, D), lambda j: (0, 0)),
                  pl.BlockSpec((D, tn), lambda j: (0, j)),
                  pl.BlockSpec((1, tn), lambda j: (0, j))],
        operands=(cvec, w_ada, b_ada),
        out_shape=jax.ShapeDtypeStruct((8, N_MOD), F32),
        out_specs=pl.BlockSpec((8, tn), lambda j: (0, j)))


def _norm_mod_f32(x, g, sh, sc):
    return x * lax.rsqrt(jnp.mean(x * x, axis=-1, keepdims=True) + EPS) * (g * (1.0 + sc)) + sh


def _norm_mod(x, g, sh, sc):
    return _norm_mod_f32(x, g, sh, sc).astype(BF16)


def _act(v, kind):
    if kind is None:
        return v
    v = v.astype(BF16)
    s = _sigmoid(v)
    return v * s if kind == "silu" else s


def _inproj_kernel(x_ref, sh_ref, sc_ref, g_ref, *refs, acts, emit_u, n_cast):
    n_w = len(acts)
    w_refs, cast_in, outs = refs[:n_w], refs[n_w:n_w + n_cast], refs[n_w + n_cast:]
    for i_ref, o_ref in zip(cast_in, outs[len(outs) - n_cast:]):
        o_ref[...] = i_ref[...].astype(o_ref.dtype)
    uf = _norm_mod_f32(x_ref[...], g_ref[...], sh_ref[...], sc_ref[...])
    if emit_u:
        outs[n_w][...] = uf
    u = uf.astype(BF16)
    for w_ref, o_ref, act in zip(w_refs, outs[:n_w], acts):
        n = w_ref.shape[0]
        for jc, j in enumerate(range(0, n, 1024)):
            cs = slice(j, min(j + 1024, n))
            o_ref[:, cs] = _act(_dot_nt(u, w_ref[cs, :]), act[jc] if act else None).astype(o_ref.dtype)


def _inproj(x2, mod3, mod_row, norm_g, weights, out_dtypes, tm, acts=None, emit_u=False, casts=()):
    m = x2.shape[0]
    steps = m // tm
    acts = acts or (None,) * len(weights)
    widths = [w.shape[0] for w in weights] + ([D] if emit_u else [])
    dtypes = list(out_dtypes) + ([F32] if emit_u else [])
    slab = lambda a, rows: pl.BlockSpec((rows, a.shape[1]), lambda i: (i * (a.shape[0] // rows) // steps, 0))
    cast_specs = [slab(a, rows) for a, rows in casts]
    return _pallas(
        functools.partial(_inproj_kernel, acts=acts, emit_u=emit_u, n_cast=len(casts)), name="inproj",
        grid=(steps,),
        in_specs=[pl.BlockSpec((tm, D), lambda i: (i, 0)),
                  pl.BlockSpec((None, 1, D), lambda i: (mod_row(i), 0, 0)),
                  pl.BlockSpec((None, 1, D), lambda i: (mod_row(i), 0, 1)),
                  pl.BlockSpec((1, D), lambda i: (0, 0))] + [_resident(w.shape) for w in weights] + cast_specs,
        operands=(x2, mod3, mod3, norm_g, *weights, *(a for a, _ in casts)),
        out_shape=(tuple(jax.ShapeDtypeStruct((m, n), dt) for n, dt in zip(widths, dtypes))
                   + tuple(jax.ShapeDtypeStruct(a.shape, BF16) for a, _ in casts)),
        out_specs=tuple(pl.BlockSpec((tm, n), lambda i: (i, 0)) for n in widths) + tuple(cast_specs))


def _inproj_m_kernel(*refs, colmajor, nblk):
    if colmajor:
        u_ref, hp_ref, hn_ref, wm_ref, wtg_ref, cw_ref, cb_ref = refs[:7]
        u = jnp.concatenate([u_ref[:, cl, :] for cl in range(COL_BLOCK)] + [hp_ref[7], hn_ref[0]],
                            axis=0).astype(BF16)
        k_ref, qt_ref, kt_ref, v_ref, tg_ref = refs[7:]
    else:
        x_ref, sh_ref, sc_ref, g_ref, wm_ref, wtg_ref, cw_ref, cb_ref, wg_ref, wtl_ref = refs[:10]
        k_ref, qt_ref, kt_ref, v_ref, tg_ref, pg_ref, tlr_ref = refs[10:]
        u = _norm_mod(x_ref[...], g_ref[...], sh_ref[...], sc_ref[...])
        pg_ref[...] = _dot_nt(u, wg_ref[...]).astype(BF16)
        tlr_ref[...] = _dot_nt(u, wtl_ref[...])
    n = k_ref.shape[0]
    um = u[0:n]
    pres = [_dot_nt(u, wm_ref[c * 512:(c + 1) * 512, :]) for c in range(2)]
    v_ref[...] = _dot_nt(um, wm_ref[1024:2048, :]).astype(BF16)
    tg_ref[...] = _dot_nt(um, wtg_ref[...])
    row8 = lax.broadcasted_iota(jnp.int32, (8, 512), 0)
    j = pl.program_id(0) % nblk
    for c, pre in enumerate(pres):
        cs = slice(c * 512, (c + 1) * 512)
        a = pre[0:n]
        if colmajor:
            prev_row = jnp.where(j > 0, pre[n + 7:n + 8], 0.0)
            next_row = jnp.where(j < nblk - 1, pre[n + 8:n + 9], 0.0)
        else:
            prev_row = next_row = jnp.zeros((1, 512), F32)
        ap = pltpu.roll(a, 1, axis=0)
        ap = jnp.concatenate([jnp.where(row8 == 0, prev_row, ap[0:8]), ap[8:]], axis=0)
        an = pltpu.roll(a, n - 1, axis=0)
        an = jnp.concatenate([an[0:n - 8], jnp.where(row8 == 7, next_row, an[n - 8:])], axis=0)
        conv = ap * cw_ref[0:1, cs] + a * cw_ref[1:2, cs] + an * cw_ref[2:3, cs] + cb_ref[:, cs]
        y = conv * _sigmoid(conv)
        if c == 0:
            qt_ref[...] = y.T.astype(BF16)
        else:
            y = y * (DK ** -0.5)
            k_ref[...] = y.astype(BF16)
            kt_ref[...] = y.T.astype(BF16)


def _inproj_m(xv, mod3, mod_row, norm_g, wm, wtg, conv_w, conv_b, tm, colmajor, gla_weights=()):
    full = lambda shape: pl.BlockSpec(shape, lambda i: (0,) * len(shape))
    mod_specs = [pl.BlockSpec((None, 1, D), lambda i: (mod_row(i), 0, 0)),
                 pl.BlockSpec((None, 1, D), lambda i: (mod_row(i), 0, 1)), full((1, D))]
    mod_args = (mod3, mod3, norm_g)
    if colmajor:
        mod_specs, mod_args = [], ()
        bsz = xv.shape[0]
        tn = SEQ
        nblk = GRID_W // COL_BLOCK
        blk = (None, GRID_W, COL_BLOCK, D)
        halo = (None, 8, COL_BLOCK, D)
        x_specs = [pl.BlockSpec(blk, lambda i: (i // nblk, 0, i % nblk, 0)),
                   pl.BlockSpec(halo, lambda i: (i // nblk, GRID_W // 8 - 1, jnp.maximum(i % nblk - 1, 0), 0)),
                   pl.BlockSpec(halo, lambda i: (i // nblk, 0, jnp.minimum(i % nblk + 1, nblk - 1), 0))]
        xs = (xv, xv, xv)
    else:
        tn = tm
        bsz = xv.shape[0] // tn
        nblk = 1
        x_specs = [pl.BlockSpec((tm, D), lambda i: (i, 0))]
        xs = (xv,)
    extra_shapes = tuple(jax.ShapeDtypeStruct((bsz, tn, w.shape[0]), dt) for w, dt in zip(gla_weights, (BF16, F32)))
    tok = lambda w: pl.BlockSpec((None, tm, w), lambda i: (i // nblk, i % nblk, 0))
    tr = pl.BlockSpec((None, None, 512, tm), lambda i: (i // nblk, i % nblk, 0, 0))
    sds = jax.ShapeDtypeStruct
    return _pallas(
        functools.partial(_inproj_m_kernel, colmajor=colmajor, nblk=nblk),
        name="inproj_m_cm" if colmajor else "inproj_m",
        grid=(bsz * nblk,),
        in_specs=x_specs + mod_specs + [_resident(wm.shape), _resident(wtg.shape), full((3, 1024)),
                                        full((1, 1024))] + [_resident(w.shape) for w in gla_weights],
        operands=(*xs, *mod_args, wm, wtg, conv_w, conv_b, *gla_weights),
        out_shape=(sds((bsz, tn, 512), BF16), sds((bsz, nblk, 512, tm), BF16), sds((bsz, nblk, 512, tm), BF16),
                   sds((bsz, tn, 1024), BF16), sds((bsz, tn, 256), F32)) + extra_shapes,
        out_specs=(tok(512), tr, tr, tok(1024), tok(256)) + tuple(tok(w.shape[0]) for w in gla_weights))


def _sum_directions(acc_ref, out_ref, blk, rs, cs, val):
    tot = acc_ref[blk, rs, cs] + val
    acc_ref[blk, rs, cs] = tot
    out_ref[rs, cs] = tot.astype(out_ref.dtype)


def _gla_kernel(pf_ref, pb_ref, tf_ref, tb_ref, cp_ref, ct_ref, wup_ref, bdec_ref, olo_ref, ohi_ref,
                st_ref, acc_ref, *, ns):
    i = pl.program_id(1)

    @pl.when(i == 0)
    def _():
        st_ref[...] = jnp.zeros(st_ref.shape, F32)
        acc_ref[...] = jnp.zeros(acc_ref.shape, F32)
        ctx = (cp_ref, ct_ref, None)
        _gla_step((ctx, ctx), wup_ref, bdec_ref, st_ref, None, None, emit_out=False)

    _gla_step(((pf_ref, tf_ref, ohi_ref), (pb_ref, tb_ref, olo_ref)), wup_ref, bdec_ref, st_ref, acc_ref,
              (i, ns - 1 - i), emit_out=True)


def _gla_step(dirs, wup_ref, bdec_ref, st_ref, acc_ref, blks, *, emit_out):
    lc = GLA_CHUNK
    units = _scan_units(dirs[0][0].shape[0] // lc)
    masks =[_tri(lc, d) for d in range(2)]
    gs = []
    for d, (p_ref, t_ref, o_ref) in enumerate(dirs):
        z = _dot(t_ref[...].astype(BF16), wup_ref[d]) + bdec_ref[d]
        gs.append(_log2_sigmoid(z) * (1.0 / TAU))
    bs, ops, sc, us, dcols = {}, {}, {}, {}, {}
    st = {(d, h): st_ref[d * HEADS + h] for d in range(2) for h in range(HEADS)}

    def stage2(d, j):
        bs[d, j] = _cumsum_rows(masks[d][1], gs[d][j * lc:(j + 1) * lc])

    def stage3(d, j):
        p_ref = dirs[d][0]
        rs = slice(j * lc, (j + 1) * lc)
        b = bs[d, j]
        b_last = b[lc - 1:lc, :] if d == 0 else b[0:1, :]
        b_mid = b[lc // 2 - 1:lc // 2, :] if d == 0 else b[lc // 2:lc // 2 + 1, :]
        q = p_ref[rs, 0:512]
        k = p_ref[rs, 512:1024]
        qd = q * jnp.exp2(b - b_mid).astype(BF16)
        kd = k * jnp.exp2((b_mid + LOG2_QSCALE) - b).astype(BF16)
        qi = qd * jnp.exp2(b_mid).astype(BF16)
        kl = kd * jnp.exp2(b_last - b_mid).astype(BF16)
        dec = jnp.exp2(b_last)
        ops[d, j] = (qi, kl, dec, qd, kd)

    def stage4(d, j):
        p_ref = dirs[d][0]
        rs = slice(j * lc, (j + 1) * lc)
        qi, kl, dec, qd, kd = ops[d, j]
        for h in range(HEADS):
            ks = slice(h * DK, (h + 1) * DK)
            v = p_ref[rs, 1024 + h * DV:1024 + (h + 1) * DV]
            if emit_out:
                sc[d, j, h] = jnp.where(masks[d][0], _dot_nt(qd[:, ks], kd[:, ks]), 0.0).astype(BF16)
            us[d, j, h] = _dot_tn(kl[:, ks], v)
            dcols[d, j, h] = jnp.broadcast_to(dec[:, ks], (8, DK)).T[:, 0:1]
    def stage5(d, j):
        p_ref, _, o_ref = dirs[d]
        rs = slice(j * lc, (j + 1) * lc)
        for h in range(HEADS):
            ks = slice(h * DK, (h + 1) * DK)
            if emit_out:
                v = p_ref[rs, 1024 + h * DV:1024 + (h + 1) * DV]
                o = _dot(jnp.concatenate([sc[d, j, h], ops[d, j][0][:, ks]], axis=1),
                         jnp.concatenate([v, st[d, h].astype(BF16)], axis=0))
                _sum_directions(acc_ref, o_ref, blks[d], rs, slice(h * DV, (h + 1) * DV), o)
            st[d, h] = st[d, h] * dcols[d, j, h] + us[d, j, h]

    _skewed(units, (stage2, stage3, stage4, stage5))
    for (d, h), val in st.items():
        st_ref[d * HEADS + h] = val


def _half_specs(ns, step, width):
    half = ns // 2
    lo = pl.BlockSpec((None, step, width), lambda b, i: (b, jnp.minimum(ns - 1 - i, half - 1), 0))
    hi = pl.BlockSpec((None, step, width), lambda b, i: (b, jnp.maximum(i - half, 0), 0))
    return lo, hi


def _gla_scan(pg, tlr, pg_c, tlr_c, wup, bdec):
    bn, tn, _ = pg.shape
    tc = pg_c.shape[1]
    step = min(GLA_STEP, tn)
    ns = tn // step
    fwd = lambda b, i: (b, i, 0)
    bwd = lambda b, i: (b, ns - 1 - i, 0)
    ctx = lambda b, i: (b, 0, 0)
    o_shape = jax.ShapeDtypeStruct((bn, tn // 2, HEADS * DV), BF16)
    return _pallas(
        functools.partial(_gla_kernel, ns=ns), name="gla_scan",
        grid=(bn, ns),
        in_specs=[pl.BlockSpec((None, step, 2048), fwd),
                  pl.BlockSpec((None, step, 2048), bwd),
                  pl.BlockSpec((None, step, 128), fwd),
                  pl.BlockSpec((None, step, 128), bwd),
                  pl.BlockSpec((None, tc, 2048), ctx),
                  pl.BlockSpec((None, tc, 128), ctx),
                  pl.BlockSpec((2, 128, 512), lambda b, i: (0, 0, 0)),
                  pl.BlockSpec((2, 1, 512), lambda b, i: (0, 0, 0))],
        operands=(pg, pg, tlr, tlr, pg_c, tlr_c, wup, bdec),
        out_shape=(o_shape, o_shape),
        out_specs=_half_specs(ns, step, HEADS * DV),
        scratch=[pltpu.VMEM((2 * HEADS, DK, DV), F32), pltpu.VMEM((ns, step, HEADS * DV), F32)])


def _mlstm_kernel(kf_ref, kb_ref, qtf_ref, qtb_ref, ktf_ref, ktb_ref, vf_ref, vb_ref, tf_ref, tb_ref,
                  kc_ref, qtc_ref, ktc_ref, vc_ref, tc_ref, bg_ref, hlo_ref, hhi_ref,
                  c_ref, n_ref, m_ref, acc_ref, *, ns):
    i = pl.program_id(1)

    @pl.when(i == 0)
    def _():
        c_ref[...] = jnp.zeros(c_ref.shape, F32)
        n_ref[...] = jnp.zeros(n_ref.shape, F32)
        m_ref[...] = jnp.zeros(m_ref.shape, F32)
        acc_ref[...] = jnp.zeros(acc_ref.shape, F32)
        ctx = (kc_ref, qtc_ref, ktc_ref, vc_ref, tc_ref, None)
        _mlstm_step((ctx, ctx), bg_ref, c_ref, n_ref, m_ref, None, None, emit_out=False)

    _mlstm_step(((kf_ref, qtf_ref, ktf_ref, vf_ref, tf_ref, hhi_ref),
                 (kb_ref, qtb_ref, ktb_ref, vb_ref, tb_ref, hlo_ref)),
                bg_ref, c_ref, n_ref, m_ref, acc_ref, (i, ns - 1 - i), emit_out=True)


def _mlstm_step(dirs, bg_ref, c_ref, n_ref, m_ref, acc_ref, blks, *, emit_out):
    lc = MLSTM_CHUNK
    units = _scan_units(dirs[0][0].shape[0] // lc)
    tris =[_tri(lc, d) for d in range(2)]
    lane_of = lambda d, h: GATE_LANE + 8 * d + h
    hs = lambda h: slice(h * DK, (h + 1) * DK)
    vs = lambda h: slice(h * DV, (h + 1) * DV)

    mrow = [m_ref[d:d + 1, :] for d in range(2)]
    tiles = {}
    for d, j in units:
        t_ref = dirs[d][4]
        rs = slice(j * lc, (j + 1) * lc)
        ga = (t_ref[rs, 0:128] + bg_ref[:, 0:128]) * LOG2E
        gb = t_ref[rs, 128:256] + bg_ref[:, 128:256]
        bc = _cumsum_rows(tris[d][1], _log2_sigmoid(gb))
        b_last = bc[lc - 1:lc, :] if d == 0 else bc[0:1, :]
        log_key = b_last - bc + ga
        m_new = jnp.maximum(b_last + mrow[d], jnp.max(log_key, axis=0, keepdims=True))
        tiles[d, j] = dict(rmat=ga - bc, bct=bc.T, m_in=mrow[d], wkt=jnp.exp2(log_key - m_new).T,
                           decay=jnp.exp2(b_last + mrow[d] - m_new))
        mrow[d] = m_new
    for d in range(2):
        m_ref[d:d + 1, :] = mrow[d]

    us, ncols = {}, {}
    for d, j in units:
        k_ref, _, kt_ref, v_ref = dirs[d][:4]
        rs = slice(j * lc, (j + 1) * lc)
        for h in range(HEADS):
            lane = lane_of(d, h)
            wk = tiles[d, j]["wkt"][lane:lane + 1, :]
            kwt = kt_ref[hs(h), rs] * wk.astype(BF16)
            us[d, j, h] = _dot(kwt, v_ref[rs, vs(h)])
            ncols[d, j, h] = _dot(jnp.broadcast_to(wk, (16, lc)).astype(BF16), k_ref[rs, hs(h)])[0:1]

    n_in = {}
    for d in range(2):
        for h in range(HEADS):
            idx = d * HEADS + h
            lane = lane_of(d, h)
            nvec = n_ref[idx:idx + 1, :]
            for dd, j in units:
                if dd == d:
                    n_in[d, j, h] = nvec
                    nvec = tiles[d, j]["decay"][:, lane:lane + 1] * nvec + ncols[d, j, h]
            n_ref[idx:idx + 1, :] = nvec

    lhs = {}
    if emit_out:
        for d, j in units:
            k_ref, qt_ref = dirs[d][:2]
            rs = slice(j * lc, (j + 1) * lc)
            t = tiles[d, j]
            causal_t = tris[1 - d][0]
            for h in range(HEADS):
                lane = lane_of(d, h)
                qt = qt_ref[hs(h), rs]
                kq = _dot(jnp.concatenate(
                    [k_ref[rs, hs(h)], jnp.broadcast_to(n_in[d, j, h], (16, DK)).astype(BF16)], axis=0), qt)
                rm = jnp.where(causal_t, t["rmat"][:, lane:lane + 1], -jnp.inf)
                mval = t["m_in"][:, lane:lane + 1]
                mx = jnp.maximum(mval, jnp.max(rm, axis=0, keepdims=True))
                wt = jnp.exp2(rm - mx) * kq[0:lc]
                w_inter = jnp.exp2(mval - mx)
                den = jnp.sum(wt, axis=0, keepdims=True) + w_inter * kq[lc:lc + 1]
                inv = 1.0 / jnp.maximum(jnp.abs(den), jnp.exp2(-(t["bct"][lane:lane + 1, :] + mx)))
                lhs[d, j, h] = jnp.concatenate(
                    [(wt * inv).astype(BF16), qt * (w_inter * inv).astype(BF16)], axis=0)

    for d in range(2):
        v_ref, o_ref = dirs[d][3], dirs[d][5]
        for h in range(HEADS):
            idx = d * HEADS + h
            lane = lane_of(d, h)
            cmat = c_ref[idx]
            for dd, j in units:
                if dd != d:
                    continue
                rs = slice(j * lc, (j + 1) * lc)
                if emit_out:
                    o = _dot_tn(lhs[d, j, h], jnp.concatenate([v_ref[rs, vs(h)], cmat.astype(BF16)], axis=0))
                    _sum_directions(acc_ref, o_ref, blks[d], rs, vs(h), o)
                cmat = tiles[d, j]["decay"][:, lane:lane + 1] * cmat + us[d, j, h]
            c_ref[idx] = cmat


def _mlstm_scan(lat, ctx, bgate):
    k, qt, kt, v, tg = lat
    bn, tn, _ = k.shape
    tc = ctx[0].shape[1]
    step = min(STEP, tn)
    ns = tn // step
    fwd = lambda b, i: (b, i, 0)
    bwd = lambda b, i: (b, ns - 1 - i, 0)
    assert qt.shape[1:] == (ns, 512, step) and ctx[1].shape[1:] == (1, 512, tc)
    fwd_t = lambda b, i: (b, i, 0, 0)
    bwd_t = lambda b, i: (b, ns - 1 - i, 0, 0)
    whole = lambda b, i: (b, 0, 0)
    o_shape = jax.ShapeDtypeStruct((bn, tn // 2, HEADS * DV), F32)
    both = lambda shape, f, g: [pl.BlockSpec(shape, f), pl.BlockSpec(shape, g)]
    whole_t = lambda b, i: (b, 0, 0, 0)
    ctx_specs = [pl.BlockSpec((None, tc, 512), whole), pl.BlockSpec((None, None, 512, tc), whole_t),
                 pl.BlockSpec((None, None, 512, tc), whole_t), pl.BlockSpec((None, tc, 1024), whole),
                 pl.BlockSpec((None, tc, 256), whole)]
    return _pallas(
        functools.partial(_mlstm_kernel, ns=ns), name="mlstm_scan",
        grid=(bn, ns),
        in_specs=(both((None, step, 512), fwd, bwd) + both((None, None, 512, step), fwd_t, bwd_t)
                  + both((None, None, 512, step), fwd_t, bwd_t) + both((None, step, 1024), fwd, bwd)
                  + both((None, step, 256), fwd, bwd) + ctx_specs
                  + [pl.BlockSpec((1, 256), lambda b, i: (0, 0))]),
        operands=(k, k, qt, qt, kt, kt, v, v, tg, tg, *ctx, bgate),
        out_shape=(o_shape, o_shape),
        out_specs=_half_specs(ns, step, HEADS * DV),
        scratch=[pltpu.VMEM((2 * HEADS, DK, DV), F32), pltpu.VMEM((2 * HEADS, 128), F32),
                 pltpu.VMEM((8, 128), F32), pltpu.VMEM((ns, step, HEADS * DV), F32)])


def _head_norm(o, g):
    parts = []
    for h in range(HEADS):
        oh = o[:, h * DV:(h + 1) * DV]
        parts.append(oh * lax.rsqrt(jnp.mean(oh * oh, axis=-1, keepdims=True) + EPS))
    return jnp.concatenate(parts, axis=-1) * g


def _merge_kernel(olo_ref, ohi_ref, hlo_ref, hhi_ref, po_ref, x_ref, g1_ref, gg_ref, gm_ref,
                  wbg_ref, wbm_ref, wo_ref, o_ref, *, per_b):
    groups = [(g * COL_BLOCK // MERGE_GROUPS, (g + 1) * COL_BLOCK // MERGE_GROUPS) for g in range(MERGE_GROUPS)]
    rows = [slice(a * GRID_W, b * GRID_W) for a, b in groups]
    lower = pl.program_id(0) % per_b < per_b // 2
    ys = []
    for (a, b), rs in zip(groups, rows):
        hm = jnp.concatenate([r[:, rl, :] for rl in range(a, b) for r in (hlo_ref, hhi_ref)], axis=0)
        o = jnp.where(lower, olo_ref[rs, :], ohi_ref[rs, :]).astype(F32)
        y_gla = _head_norm(o, gg_ref[...]) * po_ref[rs, 0:1024].astype(F32)
        y_m = _head_norm(hm, gm_ref[...]) * po_ref[rs, 1024:2048].astype(F32)
        ys.append((y_gla.astype(BF16), y_m.astype(BF16)))
    ds = [(_dot(y_gla, wbg_ref[...]), _dot(y_m, wbm_ref[...])) for y_gla, y_m in ys]
    ys = [(po_ref[rs, 2048:3072].astype(F32) * d_g + po_ref[rs, 3072:4096].astype(F32) * d_m).astype(BF16)
          for rs, (d_g, d_m) in zip(rows, ds)]
    mixes = [_dot(y, wo_ref[...]) for y in ys]
    for rs, mix in zip(rows, mixes):
        o_ref[rs, :] = x_ref[rs, :] + g1_ref[...] * mix


def _merge(olo, ohi, hlo4, hhi4, po, x2, mod3, gg, gm, wbg, wbm, wo):
    m = x2.shape[0]
    tm = GRID_W * COL_BLOCK
    per_b = SEQ // tm
    half = per_b // 2
    tok = lambda i: (i, 0)
    hspec = pl.BlockSpec((None, GRID_W // 2, COL_BLOCK, D), lambda i: (i // per_b, 0, i % per_b, 0))
    lo_spec = pl.BlockSpec((None, tm, D), lambda i: (i // per_b, jnp.minimum(i % per_b, half - 1), 0))
    hi_spec = pl.BlockSpec((None, tm, D), lambda i: (i // per_b, jnp.maximum(i % per_b - half, 0), 0))
    return _pallas(
        functools.partial(_merge_kernel, per_b=per_b), name="merge",
        grid=(m // tm,),
        in_specs=[lo_spec, hi_spec, hspec, hspec,
                  pl.BlockSpec((tm, 4096), tok), pl.BlockSpec((tm, D), tok),
                  pl.BlockSpec((None, 1, D), lambda i: (i // per_b, 0, 2)),
                  pl.BlockSpec((1, D), lambda i: (0, 0)), pl.BlockSpec((1, D), lambda i: (0, 0)),
                  _resident((D, D)), _resident((D, D)), _resident((D, D))],
        operands=(olo, ohi, hlo4, hhi4, po, x2, mod3, gg, gm, wbg, wbm, wo),
        out_shape=jax.ShapeDtypeStruct((m, D), F32),
        out_specs=pl.BlockSpec((tm, D), tok))


FF_TILES = ((0, 1280), (1280, 2816))


def _ffn_kernel(x_ref, sh_ref, sc_ref, g2_ref, ng_ref, fg_ref, wi_ref, wo_ref, o_ref):
    tm = x_ref.shape[0]
    rows = [slice(g * tm // FFN_GROUPS, (g + 1) * tm // FFN_GROUPS) for g in range(FFN_GROUPS)]
    us = [_norm_mod(x_ref[rs, :], ng_ref[...], sh_ref[...], sc_ref[...]) for rs in rows]
    accs = [None] * FFN_GROUPS
    for lo, hi in FF_TILES:
        abs_ = [(_dot(u, wi_ref[:, lo:hi]), _dot(u, wi_ref[:, D_FF + lo:D_FF + hi])) for u in us]
        hids = [(a * _sigmoid(a) * b).astype(BF16) for a, b in abs_]
        parts = [_dot(hid, wo_ref[lo:hi, :]) for hid in hids]
        accs = [p if acc is None else acc + p for acc, p in zip(accs, parts)]
    for rs, acc in zip(rows, accs):
        x2 = x_ref[rs, :] + g2_ref[...] * acc
        o_ref[rs, :] = x2 * lax.rsqrt(jnp.mean(x2 * x2, axis=-1, keepdims=True) + EPS) * fg_ref[...]


def _ffn(x1, mod3, ng, fg, wi, wo, tm):
    m = x1.shape[0]
    per_b = SEQ // tm
    tok = lambda i: (i, 0)
    modspec = lambda c: pl.BlockSpec((None, 1, D), lambda i: (i // per_b, 0, c))
    return _pallas(
        _ffn_kernel, name="ffn",
        grid=(m // tm,),
        in_specs=[pl.BlockSpec((tm, D), tok), modspec(3), modspec(4), modspec(5),
                  pl.BlockSpec((1, D), lambda i: (0, 0)), pl.BlockSpec((1, D), lambda i: (0, 0)),
                  _resident((D, 2 * D_FF)), _resident((D_FF, D))],
        operands=(x1, mod3, mod3, mod3, ng, fg, wi, wo),
        out_shape=jax.ShapeDtypeStruct((m, D), F32),
        out_specs=pl.BlockSpec((tm, D), tok))


def _split_w_in_kernel(w_ref, wg_ref, wm_ref, wo_ref, wtl_ref, wtg_ref):
    cols = w_ref.shape[1]
    c = lambda a, b: w_ref[a:b, :].astype(BF16)
    z = lambda n: jnp.zeros((n, cols), BF16)
    wg_ref[...] = c(0, 2048)
    wm_ref[...] = c(3104, 5152)
    wo_ref[0:1024, :] = c(2048, 3072)
    wo_ref[1024:2048, :] = c(5152, 6176)
    wo_ref[2048:4096, :] = c(6192, 8240)
    wtl_ref[...] = jnp.concatenate([c(3072, 3104), z(128 - 2 * RANK)], axis=0)
    wtg_ref[...] = jnp.concatenate(
        [z(GATE_LANE), c(6176, 6192), z(128 - GATE_LANE - 16),
         z(GATE_LANE), c(6180, 6184), z(4), c(6188, 6192), z(128 - GATE_LANE - 12)], axis=0)


def _split_w_in(w_in_t):
    tc = 256
    heights = (2048, 2048, 4096, 128, 256)
    return _pallas(
        _split_w_in_kernel, name="split_w_in",
        grid=(D // tc,),
        in_specs=[pl.BlockSpec((w_in_t.shape[0], tc), lambda i: (0, i))],
        operands=(w_in_t,),
        out_shape=tuple(jax.ShapeDtypeStruct((n, D), BF16) for n in heights),
        out_specs=tuple(pl.BlockSpec((n, tc), lambda i: (0, i)) for n in heights))


def kernel(x, c, ctx, c_ctx, w_ada, b_ada, norm1_g, w_in, gla_w_up, gla_b_dec, gla_norm_g,
           mlstm_conv_w, mlstm_conv_b, mlstm_b_gate, mlstm_norm_g, w_br_gla, w_br_mlstm, w_out,
           norm2_g, w_ffn_in, w_ffn_out, final_g):
    bsz = x.shape[0]
    row = lambda a: a.reshape(1, -1)

    cvec = jnp.concatenate([c, c_ctx[None, :], jnp.zeros((8 - bsz - 1, D), F32)], axis=0)
    mod3 = _ada(cvec, w_ada[0], row(b_ada[0])).reshape(8, 1, N_MOD)

    wg, wm, wo, wtl, wtg = _split_w_in(w_in[0].T)
    x2 = x.reshape(bsz * SEQ, D)
    ctx2 = ctx.reshape(bsz * CTX, D)
    g1n = row(norm1_g[0])
    tm = GRID_W * COL_BLOCK
    steps = bsz * SEQ // tm
    casts = ((w_br_gla[0], D // steps), (w_br_mlstm[0], D // steps), (w_out[0], D // steps),
             (w_ffn_in[0], D // steps), (w_ffn_out[0], 2 * D_FF // steps))
    po, pg, tlr, u_lat, wbg, wbm, wout, wfi, wfo = _inproj(
        x2, mod3, lambda i: i // (SEQ // tm), g1n, (wo, wg, wtl), (BF16, BF16, F32), tm,
        acts=(("silu", "sigmoid", "sigmoid", "sigmoid"), None, None), emit_u=True, casts=casts)
    conv_w = mlstm_conv_w[0]
    conv_b = row(mlstm_conv_b[0])
    m_lat = _inproj_m(u_lat.reshape(bsz, SEQ // GRID_W, GRID_W, D), None, None, None,
                      wm, wtg, conv_w, conv_b, tm, True)
    *m_ctx, pg_c, tlr_c = _inproj_m(ctx2, mod3, lambda i: bsz, g1n, wm, wtg, conv_w, conv_b, CTX, False,
                                    gla_weights=(wg, wtl))

    wup = jnp.zeros((2, 128, HEADS * DK), F32)
    wup = wup.at[0, 0:RANK].set(gla_w_up[0, 0]).at[1, RANK:2 * RANK].set(gla_w_up[0, 1]).astype(BF16)
    bdec = gla_b_dec[0].reshape(2, 1, HEADS * DK)
    o_lo, o_hi = _gla_scan(pg.reshape(bsz, SEQ, 2048), tlr.reshape(bsz, SEQ, 128),
                           pg_c.reshape(bsz, CTX, 2048), tlr_c.reshape(bsz, CTX, 128), wup, bdec)

    bgate = mlstm_b_gate[0].reshape(1, 16)
    zg = lambda n: jnp.zeros((1, n), F32)
    bg2 = jnp.concatenate([zg(GATE_LANE), bgate, zg(128 - GATE_LANE - 16),
                           zg(GATE_LANE), bgate[:, 4:8], zg(4), bgate[:, 12:16],
                           zg(128 - GATE_LANE - 12)], axis=1)
    h_lo, h_hi = _mlstm_scan(m_lat, m_ctx, bg2)

    cm4 = lambda a: a.reshape(bsz, GRID_W // 2, SEQ // GRID_W, D)
    x1 = _merge(o_lo, o_hi, cm4(h_lo), cm4(h_hi), po, x2, mod3,
                row(gla_norm_g[0]), row(mlstm_norm_g[0]), wbg, wbm, wout)
    out = _ffn(x1, mod3, row(norm2_g[0]), row(final_g), wfi, wfo, FFN_ROWS)
    return out.reshape(bsz, SEQ, D)
```

```python
import functools
import math

import jax
import jax.numpy as jnp
from jax import lax
from jax.experimental import pallas as pl
from jax.experimental.pallas import tpu as pltpu

D = 1024
SEQ = 4096
CTX = 256
GRID_W = 64
EPS = 1e-6
HEADS = 4
DK = 128
DV = 256
RANK = 16
TAU = 16.0
D_FF = 2816
N_MOD = 6 * D
GATE_LANE = 32
STEP = 512
GLA_STEP = 512
GLA_CHUNK = 128
MLSTM_CHUNK = 128
COL_BLOCK = 8
MERGE_GROUPS = 4
FFN_ROWS = 1024
FFN_GROUPS = 4

LOG2E = math.log2(math.e)
LOG2_QSCALE = -0.5 * math.log2(DK)

F32 = jnp.float32
BF16 = jnp.bfloat16
VMEM_BYTES_V7X = 64 * 1024 * 1024
VMEM_CAP_BYTES = VMEM_BYTES_V7X - 8 * 1024 * 1024
VMEM_BODY_BYTES = 48 * 1024 * 1024


def _dot(a, b):
    return jnp.dot(a, b, preferred_element_type=F32)


def _dot_nt(a, b):
    return lax.dot_general(a, b, (((1,), (1,)), ((), ())), preferred_element_type=F32)


def _dot_tn(a, b):
    return lax.dot_general(a, b, (((0,), (0,)), ((), ())), preferred_element_type=F32)


def _sigmoid(x):
    return 0.5 * jnp.tanh(0.5 * x) + 0.5


def _log2_sigmoid(x):
    return jnp.minimum(x, 0.0) * LOG2E - jnp.log2(1.0 + jnp.exp2(jnp.abs(x) * (-LOG2E)))


def _cumsum_rows(tri, g):
    g1 = g.astype(BF16)
    g2 = (g - g1.astype(F32)).astype(BF16)
    return _dot(tri, g1) + _dot(tri, g2)


def _tri(n, d):
    row = lax.broadcasted_iota(jnp.int32, (n, n), 0)
    col = lax.broadcasted_iota(jnp.int32, (n, n), 1)
    causal = (col <= row) if d == 0 else (col >= row)
    return causal, jnp.where(causal, 1.0, 0.0).astype(BF16)


def _resident(shape):
    n = len(shape)
    return pl.BlockSpec(shape, lambda *_: (0,) * n, pipeline_mode=pl.Buffered(1))


def _block_bytes(spec, dtype):
    n = 1
    for dim in spec.block_shape:
        n *= 1 if dim is None else dim
    buffers = 2 if spec.pipeline_mode is None else spec.pipeline_mode.buffer_count
    return buffers * n * jnp.dtype(dtype).itemsize


def _pallas(kernel, *, name, grid, in_specs, operands, out_shape, out_specs, scratch=()):
    outs = out_shape if isinstance(out_shape, (tuple, list)) else (out_shape,)
    ospecs = out_specs if isinstance(out_specs, (tuple, list)) else (out_specs,)
    need = (sum(_block_bytes(s, a.dtype) for s, a in zip(in_specs, operands, strict=True))
            + sum(_block_bytes(s, o.dtype) for s, o in zip(ospecs, outs, strict=True))
            + sum(math.prod(s.shape) * jnp.dtype(s.dtype).itemsize for s in scratch))
    params = pltpu.CompilerParams(dimension_semantics=("arbitrary",) * len(grid),
                                  vmem_limit_bytes=min(need + VMEM_BODY_BYTES, VMEM_CAP_BYTES))
    return pl.pallas_call(kernel, out_shape=out_shape, grid=grid, in_specs=list(in_specs), out_specs=out_specs,
                          scratch_shapes=list(scratch), compiler_params=params, name=name)(*operands)


def _skewed(units, stages):
    for t in range(len(units) + len(stages) - 1):
        for s_idx, stage in enumerate(stages):
            if 0 <= t - s_idx < len(units):
                stage(*units[t - s_idx])


def _scan_units(n_sub):
    return [(d, j) for jj in range(n_sub) for d, j in ((0, jj), (1, n_sub - 1 - jj))]


def _ada_kernel(c_ref, w_ref, b_ref, o_ref):
    cv = c_ref[...]
    s = (cv * _sigmoid(cv)).astype(BF16)
    o_ref[...] = _dot(s, w_ref[...].astype(BF16)) + b_ref[...]


def _ada(cvec, w_ada, b_ada):
    tn = 1024
    return _pallas(
        _ada_kernel, name="adaln",
        grid=(N_MOD // tn,),
        in_specs=[pl.BlockSpec((8, D), lambda j: (0, 0)),
                  pl.BlockSpec((D, tn), lambda j: (0, j)),
                  pl.BlockSpec((1, tn), lambda j: (0, j))],
        operands=(cvec, w_ada, b_ada),
        out_shape=jax.ShapeDtypeStruct((8, N_MOD), F32),
        out_specs=pl.BlockSpec((8, tn), lambda j: (0, j)))


def _norm_mod_f32(x, g, sh, sc):
    return x * lax.rsqrt(jnp.mean(x * x, axis=-1, keepdims=True) + EPS) * (g * (1.0 + sc)) + sh


def _norm_mod(x, g, sh, sc):
    return _norm_mod_f32(x, g, sh, sc).astype(BF16)


def _act(v, kind):
    if kind is None:
        return v
    v = v.astype(BF16)
    s = _sigmoid(v)
    return v * s if kind == "silu" else s


def _inproj_kernel(x_ref, sh_ref, sc_ref, g_ref, *refs, acts, emit_u, n_cast):
    n_w = len(acts)
    w_refs, cast_in, outs = refs[:n_w], refs[n_w:n_w + n_cast], refs[n_w + n_cast:]
    for i_ref, o_ref in zip(cast_in, outs[len(outs) - n_cast:]):
        o_ref[...] = i_ref[...].astype(o_ref.dtype)
    uf = _norm_mod_f32(x_ref[...], g_ref[...], sh_ref[...], sc_ref[...])
    if emit_u:
        outs[n_w][...] = uf
    u = uf.astype(BF16)
    for w_ref, o_ref, act in zip(w_refs, outs[:n_w], acts):
        n = w_ref.shape[0]
        for jc, j in enumerate(range(0, n, 1024)):
            cs = slice(j, min(j + 1024, n))
            o_ref[:, cs] = _act(_dot_nt(u, w_ref[cs, :]), act[jc] if act else None).astype(o_ref.dtype)


def _inproj(x2, mod3, mod_row, norm_g, weights, out_dtypes, tm, acts=None, emit_u=False, casts=()):
    m = x2.shape[0]
    steps = m // tm
    acts = acts or (None,) * len(weights)
    widths = [w.shape[0] for w in weights] + ([D] if emit_u else [])
    dtypes = list(out_dtypes) + ([F32] if emit_u else [])
    slab = lambda a, rows: pl.BlockSpec((rows, a.shape[1]), lambda i: (i * (a.shape[0] // rows) // steps, 0))
    cast_specs = [slab(a, rows) for a, rows in casts]
    return _pallas(
        functools.partial(_inproj_kernel, acts=acts, emit_u=emit_u, n_cast=len(casts)), name="inproj",
        grid=(steps,),
        in_specs=[pl.BlockSpec((tm, D), lambda i: (i, 0)),
                  pl.BlockSpec((None, 1, D), lambda i: (mod_row(i), 0, 0)),
                  pl.BlockSpec((None, 1, D), lambda i: (mod_row(i), 0, 1)),
                  pl.BlockSpec((1, D), lambda i: (0, 0))] + [_resident(w.shape) for w in weights] + cast_specs,
        operands=(x2, mod3, mod3, norm_g, *weights, *(a for a, _ in casts)),
        out_shape=(tuple(jax.ShapeDtypeStruct((m, n), dt) for n, dt in zip(widths, dtypes))
                   + tuple(jax.ShapeDtypeStruct(a.shape, BF16) for a, _ in casts)),
        out_specs=tuple(pl.BlockSpec((tm, n), lambda i: (i, 0)) for n in widths) + tuple(cast_specs))


def _inproj_m_kernel(*refs, colmajor, nblk):
    if colmajor:
        u_ref, hp_ref, hn_ref, wm_ref, wtg_ref, cw_ref, cb_ref = refs[:7]
        u = jnp.concatenate([u_ref[:, cl, :] for cl in range(COL_BLOCK)] + [hp_ref[7], hn_ref[0]],
                            axis=0).astype(BF16)
        k_ref, qt_ref, kt_ref, v_ref, tg_ref = refs[7:]
    else:
        x_ref, sh_ref, sc_ref, g_ref, wm_ref, wtg_ref, cw_ref, cb_ref, wg_ref, wtl_ref = refs[:10]
        k_ref, qt_ref, kt_ref, v_ref, tg_ref, pg_ref, tlr_ref = refs[10:]
        u = _norm_mod(x_ref[...], g_ref[...], sh_ref[...], sc_ref[...])
        pg_ref[...] = _dot_nt(u, wg_ref[...]).astype(BF16)
        tlr_ref[...] = _dot_nt(u, wtl_ref[...])
    n = k_ref.shape[0]
    um = u[0:n]
    pres = [_dot_nt(u, wm_ref[c * 512:(c + 1) * 512, :]) for c in range(2)]
    v_ref[...] = _dot_nt(um, wm_ref[1024:2048, :]).astype(BF16)
    tg_ref[...] = _dot_nt(um, wtg_ref[...])
    row8 = lax.broadcasted_iota(jnp.int32, (8, 512), 0)
    j = pl.program_id(0) % nblk
    for c, pre in enumerate(pres):
        cs = slice(c * 512, (c + 1) * 512)
        a = pre[0:n]
        if colmajor:
            prev_row = jnp.where(j > 0, pre[n + 7:n + 8], 0.0)
            next_row = jnp.where(j < nblk - 1, pre[n + 8:n + 9], 0.0)
        else:
            prev_row = next_row = jnp.zeros((1, 512), F32)
        ap = pltpu.roll(a, 1, axis=0)
        ap = jnp.concatenate([jnp.where(row8 == 0, prev_row, ap[0:8]), ap[8:]], axis=0)
        an = pltpu.roll(a, n - 1, axis=0)
        an = jnp.concatenate([an[0:n - 8], jnp.where(row8 == 7, next_row, an[n - 8:])], axis=0)
        conv = ap * cw_ref[0:1, cs] + a * cw_ref[1:2, cs] + an * cw_ref[2:3, cs] + cb_ref[:, cs]
        y = conv * _sigmoid(conv)
        if c == 0:
            qt_ref[...] = y.T.astype(BF16)
        else:
            y = y * (DK ** -0.5)
            k_ref[...] = y.astype(BF16)
            kt_ref[...] = y.T.astype(BF16)


def _inproj_m(xv, mod3, mod_row, norm_g, wm, wtg, conv_w, conv_b, tm, colmajor, gla_weights=()):
    full = lambda shape: pl.BlockSpec(shape, lambda i: (0,) * len(shape))
    mod_specs = [pl.BlockSpec((None, 1, D), lambda i: (mod_row(i), 0, 0)),
                 pl.BlockSpec((None, 1, D), lambda i: (mod_row(i), 0, 1)), full((1, D))]
    mod_args = (mod3, mod3, norm_g)
    if colmajor:
        mod_specs, mod_args = [], ()
        bsz = xv.shape[0]
        tn = SEQ
        nblk = GRID_W // COL_BLOCK
        blk = (None, GRID_W, COL_BLOCK, D)
        halo = (None, 8, COL_BLOCK, D)
        x_specs = [pl.BlockSpec(blk, lambda i: (i // nblk, 0, i % nblk, 0)),
                   pl.BlockSpec(halo, lambda i: (i // nblk, GRID_W // 8 - 1, jnp.maximum(i % nblk - 1, 0), 0)),
                   pl.BlockSpec(halo, lambda i: (i // nblk, 0, jnp.minimum(i % nblk + 1, nblk - 1), 0))]
        xs = (xv, xv, xv)
    else:
        tn = tm
        bsz = xv.shape[0] // tn
        nblk = 1
        x_specs = [pl.BlockSpec((tm, D), lambda i: (i, 0))]
        xs = (xv,)
    extra_shapes = tuple(jax.ShapeDtypeStruct((bsz, tn, w.shape[0]), dt) for w, dt in zip(gla_weights, (BF16, F32)))
    tok = lambda w: pl.BlockSpec((None, tm, w), lambda i: (i // nblk, i % nblk, 0))
    tr = pl.BlockSpec((None, None, 512, tm), lambda i: (i // nblk, i % nblk, 0, 0))
    sds = jax.ShapeDtypeStruct
    return _pallas(
        functools.partial(_inproj_m_kernel, colmajor=colmajor, nblk=nblk),
        name="inproj_m_cm" if colmajor else "inproj_m",
        grid=(bsz * nblk,),
        in_specs=x_specs + mod_specs + [_resident(wm.shape), _resident(wtg.shape), full((3, 1024)),
                                        full((1, 1024))] + [_resident(w.shape) for w in gla_weights],
        operands=(*xs, *mod_args, wm, wtg, conv_w, conv_b, *gla_weights),
        out_shape=(sds((bsz, tn, 512), BF16), sds((bsz, nblk, 512, tm), BF16), sds((bsz, nblk, 512, tm), BF16),
                   sds((bsz, tn, 1024), BF16), sds((bsz, tn, 256), F32)) + extra_shapes,
        out_specs=(tok(512), tr, tr, tok(1024), tok(256)) + tuple(tok(w.shape[0]) for w in gla_weights))


def _sum_directions(acc_ref, out_ref, blk, rs, cs, val):
    tot = acc_ref[blk, rs, cs] + val
    acc_ref[blk, rs, cs] = tot
    out_ref[rs, cs] = tot.astype(out_ref.dtype)


def _gla_kernel(pf_ref, pb_ref, tf_ref, tb_ref, cp_ref, ct_ref, wup_ref, bdec_ref, olo_ref, ohi_ref,
                st_ref, acc_ref, *, ns):
    i = pl.program_id(1)

    @pl.when(i == 0)
    def _():
        st_ref[...] = jnp.zeros(st_ref.shape, F32)
        acc_ref[...] = jnp.zeros(acc_ref.shape, F32)
        ctx = (cp_ref, ct_ref, None)
        _gla_step((ctx, ctx), wup_ref, bdec_ref, st_ref, None, None, emit_out=False)

    _gla_step(((pf_ref, tf_ref, ohi_ref), (pb_ref, tb_ref, olo_ref)), wup_ref, bdec_ref, st_ref, acc_ref,
              (i, ns - 1 - i), emit_out=True)


def _gla_step(dirs, wup_ref, bdec_ref, st_ref, acc_ref, blks, *, emit_out):
    lc = GLA_CHUNK
    units = _scan_units(dirs[0][0].shape[0] // lc)
    masks =[_tri(lc, d) for d in range(2)]
    gs = []
    for d, (p_ref, t_ref, o_ref) in enumerate(dirs):
        z = _dot(t_ref[...].astype(BF16), wup_ref[d]) + bdec_ref[d]
        gs.append(_log2_sigmoid(z) * (1.0 / TAU))
    bs, ops, sc, us, dcols = {}, {}, {}, {}, {}
    st = {(d, h): st_ref[d * HEADS + h] for d in range(2) for h in range(HEADS)}

    def stage2(d, j):
        bs[d, j] = _cumsum_rows(masks[d][1], gs[d][j * lc:(j + 1) * lc])

    def stage3(d, j):
        p_ref = dirs[d][0]
        rs = slice(j * lc, (j + 1) * lc)
        b = bs[d, j]
        b_last = b[lc - 1:lc, :] if d == 0 else b[0:1, :]
        b_mid = b[lc // 2 - 1:lc // 2, :] if d == 0 else b[lc // 2:lc // 2 + 1, :]
        q = p_ref[rs, 0:512]
        k = p_ref[rs, 512:1024]
        qd = q * jnp.exp2(b - b_mid).astype(BF16)
        kd = k * jnp.exp2((b_mid + LOG2_QSCALE) - b).astype(BF16)
        qi = qd * jnp.exp2(b_mid).astype(BF16)
        kl = kd * jnp.exp2(b_last - b_mid).astype(BF16)
        dec = jnp.exp2(b_last)
        ops[d, j] = (qi, kl, dec, qd, kd)

    def stage4(d, j):
        p_ref = dirs[d][0]
        rs = slice(j * lc, (j + 1) * lc)
        qi, kl, dec, qd, kd = ops[d, j]
        for h in range(HEADS):
            ks = slice(h * DK, (h + 1) * DK)
            v = p_ref[rs, 1024 + h * DV:1024 + (h + 1) * DV]
            if emit_out:
                sc[d, j, h] = jnp.where(masks[d][0], _dot_nt(qd[:, ks], kd[:, ks]), 0.0).astype(BF16)
            us[d, j, h] = _dot_tn(kl[:, ks], v)
            dcols[d, j, h] = jnp.broadcast_to(dec[:, ks], (8, DK)).T[:, 0:1]
    def stage5(d, j):
        p_ref, _, o_ref = dirs[d]
        rs = slice(j * lc, (j + 1) * lc)
        for h in range(HEADS):
            ks = slice(h * DK, (h + 1) * DK)
            if emit_out:
                v = p_ref[rs, 1024 + h * DV:1024 + (h + 1) * DV]
                o = _dot(jnp.concatenate([sc[d, j, h], ops[d, j][0][:, ks]], axis=1),
                         jnp.concatenate([v, st[d, h].astype(BF16)], axis=0))
                _sum_directions(acc_ref, o_ref, blks[d], rs, slice(h * DV, (h + 1) * DV), o)
            st[d, h] = st[d, h] * dcols[d, j, h] + us[d, j, h]

    _skewed(units, (stage2, stage3, stage4, stage5))
    for (d, h), val in st.items():
        st_ref[d * HEADS + h] = val


def _half_specs(ns, step, width):
    half = ns // 2
    lo = pl.BlockSpec((None, step, width), lambda b, i: (b, jnp.minimum(ns - 1 - i, half - 1), 0))
    hi = pl.BlockSpec((None, step, width), lambda b, i: (b, jnp.maximum(i - half, 0), 0))
    return lo, hi


def _gla_scan(pg, tlr, pg_c, tlr_c, wup, bdec):
    bn, tn, _ = pg.shape
    tc = pg_c.shape[1]
    step = min(GLA_STEP, tn)
    ns = tn // step
    fwd = lambda b, i: (b, i, 0)
    bwd = lambda b, i: (b, ns - 1 - i, 0)
    ctx = lambda b, i: (b, 0, 0)
    o_shape = jax.ShapeDtypeStruct((bn, tn // 2, HEADS * DV), BF16)
    return _pallas(
        functools.partial(_gla_kernel, ns=ns), name="gla_scan",
        grid=(bn, ns),
        in_specs=[pl.BlockSpec((None, step, 2048), fwd),
                  pl.BlockSpec((None, step, 2048), bwd),
                  pl.BlockSpec((None, step, 128), fwd),
                  pl.BlockSpec((None, step, 128), bwd),
                  pl.BlockSpec((None, tc, 2048), ctx),
                  pl.BlockSpec((None, tc, 128), ctx),
                  pl.BlockSpec((2, 128, 512), lambda b, i: (0, 0, 0)),
                  pl.BlockSpec((2, 1, 512), lambda b, i: (0, 0, 0))],
        operands=(pg, pg, tlr, tlr, pg_c, tlr_c, wup, bdec),
        out_shape=(o_shape, o_shape),
        out_specs=_half_specs(ns, step, HEADS * DV),
        scratch=[pltpu.VMEM((2 * HEADS, DK, DV), F32), pltpu.VMEM((ns, step, HEADS * DV), F32)])


def _mlstm_kernel(kf_ref, kb_ref, qtf_ref, qtb_ref, ktf_ref, ktb_ref, vf_ref, vb_ref, tf_ref, tb_ref,
                  kc_ref, qtc_ref, ktc_ref, vc_ref, tc_ref, bg_ref, hlo_ref, hhi_ref,
                  c_ref, n_ref, m_ref, acc_ref, *, ns):
    i = pl.program_id(1)

    @pl.when(i == 0)
    def _():
        c_ref[...] = jnp.zeros(c_ref.shape, F32)
        n_ref[...] = jnp.zeros(n_ref.shape, F32)
        m_ref[...] = jnp.zeros(m_ref.shape, F32)
        acc_ref[...] = jnp.zeros(acc_ref.shape, F32)
        ctx = (kc_ref, qtc_ref, ktc_ref, vc_ref, tc_ref, None)
        _mlstm_step((ctx, ctx), bg_ref, c_ref, n_ref, m_ref, None, None, emit_out=False)

    _mlstm_step(((kf_ref, qtf_ref, ktf_ref, vf_ref, tf_ref, hhi_ref),
                 (kb_ref, qtb_ref, ktb_ref, vb_ref, tb_ref, hlo_ref)),
                bg_ref, c_ref, n_ref, m_ref, acc_ref, (i, ns - 1 - i), emit_out=True)


def _mlstm_step(dirs, bg_ref, c_ref, n_ref, m_ref, acc_ref, blks, *, emit_out):
    lc = MLSTM_CHUNK
    units = _scan_units(dirs[0][0].shape[0] // lc)
    tris =[_tri(lc, d) for d in range(2)]
    lane_of = lambda d, h: GATE_LANE + 8 * d + h
    hs = lambda h: slice(h * DK, (h + 1) * DK)
    vs = lambda h: slice(h * DV, (h + 1) * DV)

    mrow = [m_ref[d:d + 1, :] for d in range(2)]
    tiles = {}
    for d, j in units:
        t_ref = dirs[d][4]
        rs = slice(j * lc, (j + 1) * lc)
        ga = (t_ref[rs, 0:128] + bg_ref[:, 0:128]) * LOG2E
        gb = t_ref[rs, 128:256] + bg_ref[:, 128:256]
        bc = _cumsum_rows(tris[d][1], _log2_sigmoid(gb))
        b_last = bc[lc - 1:lc, :] if d == 0 else bc[0:1, :]
        log_key = b_last - bc + ga
        m_new = jnp.maximum(b_last + mrow[d], jnp.max(log_key, axis=0, keepdims=True))
        tiles[d, j] = dict(rmat=ga - bc, bct=bc.T, m_in=mrow[d], wkt=jnp.exp2(log_key - m_new).T,
                           decay=jnp.exp2(b_last + mrow[d] - m_new))
        mrow[d] = m_new
    for d in range(2):
        m_ref[d:d + 1, :] = mrow[d]

    us, ncols = {}, {}
    for d, j in units:
        k_ref, _, kt_ref, v_ref = dirs[d][:4]
        rs = slice(j * lc, (j + 1) * lc)
        for h in range(HEADS):
            lane = lane_of(d, h)
            wk = tiles[d, j]["wkt"][lane:lane + 1, :]
            kwt = kt_ref[hs(h), rs] * wk.astype(BF16)
            us[d, j, h] = _dot(kwt, v_ref[rs, vs(h)])
            ncols[d, j, h] = _dot(jnp.broadcast_to(wk, (16, lc)).astype(BF16), k_ref[rs, hs(h)])[0:1]

    n_in = {}
    for d in range(2):
        for h in range(HEADS):
            idx = d * HEADS + h
            lane = lane_of(d, h)
            nvec = n_ref[idx:idx + 1, :]
            for dd, j in units:
                if dd == d:
                    n_in[d, j, h] = nvec
                    nvec = tiles[d, j]["decay"][:, lane:lane + 1] * nvec + ncols[d, j, h]
            n_ref[idx:idx + 1, :] = nvec

    lhs = {}
    if emit_out:
        for d, j in units:
            k_ref, qt_ref = dirs[d][:2]
            rs = slice(j * lc, (j + 1) * lc)
            t = tiles[d, j]
            causal_t = tris[1 - d][0]
            for h in range(HEADS):
                lane = lane_of(d, h)
                qt = qt_ref[hs(h), rs]
                kq = _dot(jnp.concatenate(
                    [k_ref[rs, hs(h)], jnp.broadcast_to(n_in[d, j, h], (16, DK)).astype(BF16)], axis=0), qt)
                rm = jnp.where(causal_t, t["rmat"][:, lane:lane + 1], -jnp.inf)
                mval = t["m_in"][:, lane:lane + 1]
                mx = jnp.maximum(mval, jnp.max(rm, axis=0, keepdims=True))
                wt = jnp.exp2(rm - mx) * kq[0:lc]
                w_inter = jnp.exp2(mval - mx)
                den = jnp.sum(wt, axis=0, keepdims=True) + w_inter * kq[lc:lc + 1]
                inv = 1.0 / jnp.maximum(jnp.abs(den), jnp.exp2(-(t["bct"][lane:lane + 1, :] + mx)))
                lhs[d, j, h] = jnp.concatenate(
                    [(wt * inv).astype(BF16), qt * (w_inter * inv).astype(BF16)], axis=0)

    for d in range(2):
        v_ref, o_ref = dirs[d][3], dirs[d][5]
        for h in range(HEADS):
            idx = d * HEADS + h
            lane = lane_of(d, h)
            cmat = c_ref[idx]
            for dd, j in units:
                if dd != d:
                    continue
                rs = slice(j * lc, (j + 1) * lc)
                if emit_out:
                    o = _dot_tn(lhs[d, j, h], jnp.concatenate([v_ref[rs, vs(h)], cmat.astype(BF16)], axis=0))
                    _sum_directions(acc_ref, o_ref, blks[d], rs, vs(h), o)
                cmat = tiles[d, j]["decay"][:, lane:lane + 1] * cmat + us[d, j, h]
            c_ref[idx] = cmat


def _mlstm_scan(lat, ctx, bgate):
    k, qt, kt, v, tg = lat
    bn, tn, _ = k.shape
    tc = ctx[0].shape[1]
    step = min(STEP, tn)
    ns = tn // step
    fwd = lambda b, i: (b, i, 0)
    bwd = lambda b, i: (b, ns - 1 - i, 0)
    assert qt.shape[1:] == (ns, 512, step) and ctx[1].shape[1:] == (1, 512, tc)
    fwd_t = lambda b, i: (b, i, 0, 0)
    bwd_t = lambda b, i: (b, ns - 1 - i, 0, 0)
    whole = lambda b, i: (b, 0, 0)
    o_shape = jax.ShapeDtypeStruct((bn, tn // 2, HEADS * DV), F32)
    both = lambda shape, f, g: [pl.BlockSpec(shape, f), pl.BlockSpec(shape, g)]
    whole_t = lambda b, i: (b, 0, 0, 0)
    ctx_specs = [pl.BlockSpec((None, tc, 512), whole), pl.BlockSpec((None, None, 512, tc), whole_t),
                 pl.BlockSpec((None, None, 512, tc), whole_t), pl.BlockSpec((None, tc, 1024), whole),
                 pl.BlockSpec((None, tc, 256), whole)]
    return _pallas(
        functools.partial(_mlstm_kernel, ns=ns), name="mlstm_scan",
        grid=(bn, ns),
        in_specs=(both((None, step, 512), fwd, bwd) + both((None, None, 512, step), fwd_t, bwd_t)
                  + both((None, None, 512, step), fwd_t, bwd_t) + both((None, step, 1024), fwd, bwd)
                  + both((None, step, 256), fwd, bwd) + ctx_specs
                  + [pl.BlockSpec((1, 256), lambda b, i: (0, 0))]),
        operands=(k, k, qt, qt, kt, kt, v, v, tg, tg, *ctx, bgate),
        out_shape=(o_shape, o_shape),
        out_specs=_half_specs(ns, step, HEADS * DV),
        scratch=[pltpu.VMEM((2 * HEADS, DK, DV), F32), pltpu.VMEM((2 * HEADS, 128), F32),
                 pltpu.VMEM((8, 128), F32), pltpu.VMEM((ns, step, HEADS * DV), F32)])


def _head_norm(o, g):
    parts = []
    for h in range(HEADS):
        oh = o[:, h * DV:(h + 1) * DV]
        parts.append(oh * lax.rsqrt(jnp.mean(oh * oh, axis=-1, keepdims=True) + EPS))
    return jnp.concatenate(parts, axis=-1) * g


def _merge_kernel(olo_ref, ohi_ref, hlo_ref, hhi_ref, po_ref, x_ref, g1_ref, gg_ref, gm_ref,
                  wbg_ref, wbm_ref, wo_ref, o_ref, *, per_b):
    groups = [(g * COL_BLOCK // MERGE_GROUPS, (g + 1) * COL_BLOCK // MERGE_GROUPS) for g in range(MERGE_GROUPS)]
    rows = [slice(a * GRID_W, b * GRID_W) for a, b in groups]
    lower = pl.program_id(0) % per_b < per_b // 2
    ys = []
    for (a, b), rs in zip(groups, rows):
        hm = jnp.concatenate([r[:, rl, :] for rl in range(a, b) for r in (hlo_ref, hhi_ref)], axis=0)
        o = jnp.where(lower, olo_ref[rs, :], ohi_ref[rs, :]).astype(F32)
        y_gla = _head_norm(o, gg_ref[...]) * po_ref[rs, 0:1024].astype(F32)
        y_m = _head_norm(hm, gm_ref[...]) * po_ref[rs, 1024:2048].astype(F32)
        ys.append((y_gla.astype(BF16), y_m.astype(BF16)))
    ds = [(_dot(y_gla, wbg_ref[...]), _dot(y_m, wbm_ref[...])) for y_gla, y_m in ys]
    ys = [(po_ref[rs, 2048:3072].astype(F32) * d_g + po_ref[rs, 3072:4096].astype(F32) * d_m).astype(BF16)
          for rs, (d_g, d_m) in zip(rows, ds)]
    mixes = [_dot(y, wo_ref[...]) for y in ys]
    for rs, mix in zip(rows, mixes):
        o_ref[rs, :] = x_ref[rs, :] + g1_ref[...] * mix


def _merge(olo, ohi, hlo4, hhi4, po, x2, mod3, gg, gm, wbg, wbm, wo):
    m = x2.shape[0]
    tm = GRID_W * COL_BLOCK
    per_b = SEQ // tm
    half = per_b // 2
    tok = lambda i: (i, 0)
    hspec = pl.BlockSpec((None, GRID_W // 2, COL_BLOCK, D), lambda i: (i // per_b, 0, i % per_b, 0))
    lo_spec = pl.BlockSpec((None, tm, D), lambda i: (i // per_b, jnp.minimum(i % per_b, half - 1), 0))
    hi_spec = pl.BlockSpec((None, tm, D), lambda i: (i // per_b, jnp.maximum(i % per_b - half, 0), 0))
    return _pallas(
        functools.partial(_merge_kernel, per_b=per_b), name="merge",
        grid=(m // tm,),
        in_specs=[lo_spec, hi_spec, hspec, hspec,
                  pl.BlockSpec((tm, 4096), tok), pl.BlockSpec((tm, D), tok),
                  pl.BlockSpec((None, 1, D), lambda i: (i // per_b, 0, 2)),
                  pl.BlockSpec((1, D), lambda i: (0, 0)), pl.BlockSpec((1, D), lambda i: (0, 0)),
                  _resident((D, D)), _resident((D, D)), _resident((D, D))],
        operands=(olo, ohi, hlo4, hhi4, po, x2, mod3, gg, gm, wbg, wbm, wo),
        out_shape=jax.ShapeDtypeStruct((m, D), F32),
        out_specs=pl.BlockSpec((tm, D), tok))


FF_TILES = ((0, 1280), (1280, 2816))


def _ffn_kernel(x_ref, sh_ref, sc_ref, g2_ref, ng_ref, fg_ref, wi_ref, wo_ref, o_ref):
    tm = x_ref.shape[0]
    rows = [slice(g * tm // FFN_GROUPS, (g + 1) * tm // FFN_GROUPS) for g in range(FFN_GROUPS)]
    us = [_norm_mod(x_ref[rs, :], ng_ref[...], sh_ref[...], sc_ref[...]) for rs in rows]
    accs = [None] * FFN_GROUPS
    for lo, hi in FF_TILES:
        abs_ = [(_dot(u, wi_ref[:, lo:hi]), _dot(u, wi_ref[:, D_FF + lo:D_FF + hi])) for u in us]
        hids = [(a * _sigmoid(a) * b).astype(BF16) for a, b in abs_]
        parts = [_dot(hid, wo_ref[lo:hi, :]) for hid in hids]
        accs = [p if acc is None else acc + p for acc, p in zip(accs, parts)]
    for rs, acc in zip(rows, accs):
        x2 = x_ref[rs, :] + g2_ref[...] * acc
        o_ref[rs, :] = x2 * lax.rsqrt(jnp.mean(x2 * x2, axis=-1, keepdims=True) + EPS) * fg_ref[...]


def _ffn(x1, mod3, ng, fg, wi, wo, tm):
    m = x1.shape[0]
    per_b = SEQ // tm
    tok = lambda i: (i, 0)
    modspec = lambda c: pl.BlockSpec((None, 1, D), lambda i: (i // per_b, 0, c))
    return _pallas(
        _ffn_kernel, name="ffn",
        grid=(m // tm,),
        in_specs=[pl.BlockSpec((tm, D), tok), modspec(3), modspec(4), modspec(5),
                  pl.BlockSpec((1, D), lambda i: (0, 0)), pl.BlockSpec((1, D), lambda i: (0, 0)),
                  _resident((D, 2 * D_FF)), _resident((D_FF, D))],
        operands=(x1, mod3, mod3, mod3, ng, fg, wi, wo),
        out_shape=jax.ShapeDtypeStruct((m, D), F32),
        out_specs=pl.BlockSpec((tm, D), tok))


def _split_w_in_kernel(w_ref, wg_ref, wm_ref, wo_ref, wtl_ref, wtg_ref):
    cols = w_ref.shape[1]
    c = lambda a, b: w_ref[a:b, :].astype(BF16)
    z = lambda n: jnp.zeros((n, cols), BF16)
    wg_ref[...] = c(0, 2048)
    wm_ref[...] = c(3104, 5152)
    wo_ref[0:1024, :] = c(2048, 3072)
    wo_ref[1024:2048, :] = c(5152, 6176)
    wo_ref[2048:4096, :] = c(6192, 8240)
    wtl_ref[...] = jnp.concatenate([c(3072, 3104), z(128 - 2 * RANK)], axis=0)
    wtg_ref[...] = jnp.concatenate(
        [z(GATE_LANE), c(6176, 6192), z(128 - GATE_LANE - 16),
         z(GATE_LANE), c(6180, 6184), z(4), c(6188, 6192), z(128 - GATE_LANE - 12)], axis=0)


def _split_w_in(w_in_t):
    tc = 256
    heights = (2048, 2048, 4096, 128, 256)
    return _pallas(
        _split_w_in_kernel, name="split_w_in",
        grid=(D // tc,),
        in_specs=[pl.BlockSpec((w_in_t.shape[0], tc), lambda i: (0, i))],
        operands=(w_in_t,),
        out_shape=tuple(jax.ShapeDtypeStruct((n, D), BF16) for n in heights),
        out_specs=tuple(pl.BlockSpec((n, tc), lambda i: (0, i)) for n in heights))


def kernel(x, c, ctx, c_ctx, w_ada, b_ada, norm1_g, w_in, gla_w_up, gla_b_dec, gla_norm_g,
           mlstm_conv_w, mlstm_conv_b, mlstm_b_gate, mlstm_norm_g, w_br_gla, w_br_mlstm, w_out,
           norm2_g, w_ffn_in, w_ffn_out, final_g):
    bsz = x.shape[0]
    assert x.shape == (bsz, SEQ, D) and ctx.shape == (bsz, CTX, D) and bsz + 1 <= 8
    assert w_in.shape == (1, D, 8240) and w_ffn_in.shape == (1, D, 2 * D_FF)
    row = lambda a: a.reshape(1, -1)

    cvec = jnp.concatenate([c, c_ctx[None, :], jnp.zeros((8 - bsz - 1, D), F32)], axis=0)
    mod3 = _ada(cvec, w_ada[0], row(b_ada[0])).reshape(8, 1, N_MOD)

    wg, wm, wo, wtl, wtg = _split_w_in(w_in[0].T)
    x2 = x.reshape(bsz * SEQ, D)
    ctx2 = ctx.reshape(bsz * CTX, D)
    g1n = row(norm1_g[0])
    tm = GRID_W * COL_BLOCK
    steps = bsz * SEQ // tm
    casts = ((w_br_gla[0], D // steps), (w_br_mlstm[0], D // steps), (w_out[0], D // steps),
             (w_ffn_in[0], D // steps), (w_ffn_out[0], 2 * D_FF // steps))
    po, pg, tlr, u_lat, wbg, wbm, wout, wfi, wfo = _inproj(
        x2, mod3, lambda i: i // (SEQ // tm), g1n, (wo, wg, wtl), (BF16, BF16, F32), tm,
        acts=(("silu", "sigmoid", "sigmoid", "sigmoid"), None, None), emit_u=True, casts=casts)
    conv_w = mlstm_conv_w[0]
    conv_b = row(mlstm_conv_b[0])
    m_lat = _inproj_m(u_lat.reshape(bsz, SEQ // GRID_W, GRID_W, D), None, None, None,
                      wm, wtg, conv_w, conv_b, tm, True)
    *m_ctx, pg_c, tlr_c = _inproj_m(ctx2, mod3, lambda i: bsz, g1n, wm, wtg, conv_w, conv_b, CTX, False,
                                    gla_weights=(wg, wtl))

    wup = jnp.zeros((2, 128, HEADS * DK), F32)
    wup = wup.at[0, 0:RANK].set(gla_w_up[0, 0]).at[1, RANK:2 * RANK].set(gla_w_up[0, 1]).astype(BF16)
    bdec = gla_b_dec[0].reshape(2, 1, HEADS * DK)
    o_lo, o_hi = _gla_scan(pg.reshape(bsz, SEQ, 2048), tlr.reshape(bsz, SEQ, 128),
                           pg_c.reshape(bsz, CTX, 2048), tlr_c.reshape(bsz, CTX, 128), wup, bdec)

    bgate = mlstm_b_gate[0].reshape(1, 16)
    zg = lambda n: jnp.zeros((1, n), F32)
    bg2 = jnp.concatenate([zg(GATE_LANE), bgate, zg(128 - GATE_LANE - 16),
                           zg(GATE_LANE), bgate[:, 4:8], zg(4), bgate[:, 12:16],
                           zg(128 - GATE_LANE - 12)], axis=1)
    h_lo, h_hi = _mlstm_scan(m_lat, m_ctx, bg2)

    cm4 = lambda a: a.reshape(bsz, GRID_W // 2, SEQ // GRID_W, D)
    x1 = _merge(o_lo, o_hi, cm4(h_lo), cm4(h_hi), po, x2, mod3,
                row(gla_norm_g[0]), row(mlstm_norm_g[0]), wbg, wbm, wout)
    out = _ffn(x1, mod3, row(norm2_g[0]), row(final_g), wfi, wfo, FFN_ROWS)
    return out.reshape(bsz, SEQ, D)
```

```python
import functools
import math

import jax
import jax.numpy as jnp
from jax import lax
from jax.experimental import pallas as pl
from jax.experimental.pallas import tpu as pltpu

D = 1024
SEQ = 4096
CTX = 256
GRID_W = 64
EPS = 1e-6
HEADS = 4
DK = 128
DV = 256
RANK = 16
TAU = 16.0
D_FF = 2816
N_MOD = 6 * D
GATE_LANE = 32
STEP = 512
GLA_STEP = 512
GLA_CHUNK = 128
MLSTM_CHUNK = 128
COL_BLOCK = 8
MERGE_GROUPS = 2
FFN_ROWS = 1024
FFN_GROUPS = 4

LOG2E = math.log2(math.e)
LOG2_QSCALE = -0.5 * math.log2(DK)

F32 = jnp.float32
BF16 = jnp.bfloat16
VMEM_BYTES_V7X = 64 * 1024 * 1024
VMEM_CAP_BYTES = VMEM_BYTES_V7X - 8 * 1024 * 1024
VMEM_BODY_BYTES = 48 * 1024 * 1024


def _dot(a, b):
    return jnp.dot(a, b, preferred_element_type=F32)


def _dot_nt(a, b):
    return lax.dot_general(a, b, (((1,), (1,)), ((), ())), preferred_element_type=F32)


def _dot_tn(a, b):
    return lax.dot_general(a, b, (((0,), (0,)), ((), ())), preferred_element_type=F32)


def _sigmoid(x):
    return 0.5 * jnp.tanh(0.5 * x) + 0.5


def _log2_sigmoid(x):
    return jnp.minimum(x, 0.0) * LOG2E - jnp.log2(1.0 + jnp.exp2(jnp.abs(x) * (-LOG2E)))


def _cumsum_rows(tri, g):
    g1 = g.astype(BF16)
    g2 = (g - g1.astype(F32)).astype(BF16)
    return _dot(tri, g1) + _dot(tri, g2)


def _tri(n, d):
    row = lax.broadcasted_iota(jnp.int32, (n, n), 0)
    col = lax.broadcasted_iota(jnp.int32, (n, n), 1)
    causal = (col <= row) if d == 0 else (col >= row)
    return causal, jnp.where(causal, 1.0, 0.0).astype(BF16)


def _resident(shape):
    n = len(shape)
    return pl.BlockSpec(shape, lambda *_: (0,) * n, pipeline_mode=pl.Buffered(1))


def _block_bytes(spec, dtype):
    n = 1
    for dim in spec.block_shape:
        n *= 1 if dim is None else dim
    buffers = 2 if spec.pipeline_mode is None else spec.pipeline_mode.buffer_count
    return buffers * n * jnp.dtype(dtype).itemsize


def _pallas(kernel, *, name, grid, in_specs, operands, out_shape, out_specs, scratch=()):
    outs = out_shape if isinstance(out_shape, (tuple, list)) else (out_shape,)
    ospecs = out_specs if isinstance(out_specs, (tuple, list)) else (out_specs,)
    need = (sum(_block_bytes(s, a.dtype) for s, a in zip(in_specs, operands, strict=True))
            + sum(_block_bytes(s, o.dtype) for s, o in zip(ospecs, outs, strict=True))
            + sum(math.prod(s.shape) * jnp.dtype(s.dtype).itemsize for s in scratch))
    params = pltpu.CompilerParams(dimension_semantics=("arbitrary",) * len(grid),
                                  vmem_limit_bytes=min(need + VMEM_BODY_BYTES, VMEM_CAP_BYTES))
    return pl.pallas_call(kernel, out_shape=out_shape, grid=grid, in_specs=list(in_specs), out_specs=out_specs,
                          scratch_shapes=list(scratch), compiler_params=params, name=name)(*operands)


def _skewed(units, stages):
    for t in range(len(units) + len(stages) - 1):
        for s_idx, stage in enumerate(stages):
            if 0 <= t - s_idx < len(units):
                stage(*units[t - s_idx])


def _scan_units(n_sub):
    return [(d, j) for jj in range(n_sub) for d, j in ((0, jj), (1, n_sub - 1 - jj))]


def _ada_kernel(c_ref, w_ref, b_ref, o_ref):
    cv = c_ref[...]
    s = (cv * _sigmoid(cv)).astype(BF16)
    o_ref[...] = _dot(s, w_ref[...].astype(BF16)) + b_ref[...]


def _ada(cvec, w_ada, b_ada):
    tn = 1024
    return _pallas(
        _ada_kernel, name="adaln",
        grid=(N_MOD // tn,),
        in_specs=[pl.BlockSpec((8, D), lambda j: (0, 0)),
                  pl.BlockSpec((D, tn), lambda j: (0, j)),
                  pl.BlockSpec((1, tn), lambda j: (0, j))],
        operands=(cvec, w_ada, b_ada),
        out_shape=jax.ShapeDtypeStruct((8, N_MOD), F32),
        out_specs=pl.BlockSpec((8, tn), lambda j: (0, j)))


def _norm_mod_f32(x, g, sh, sc):
    return x * lax.rsqrt(jnp.mean(x * x, axis=-1, keepdims=True) + EPS) * (g * (1.0 + sc)) + sh


def _norm_mod(x, g, sh, sc):
    return _norm_mod_f32(x, g, sh, sc).astype(BF16)


def _act(v, kind):
    if kind is None:
        return v
    v = v.astype(BF16)
    s = _sigmoid(v)
    return v * s if kind == "silu" else s


def _inproj_kernel(x_ref, sh_ref, sc_ref, g_ref, *refs, acts, emit_u, n_cast):
    n_w = len(acts)
    w_refs, cast_in, outs = refs[:n_w], refs[n_w:n_w + n_cast], refs[n_w + n_cast:]
    for i_ref, o_ref in zip(cast_in, outs[len(outs) - n_cast:]):
        o_ref[...] = i_ref[...].astype(o_ref.dtype)
    uf = _norm_mod_f32(x_ref[...], g_ref[...], sh_ref[...], sc_ref[...])
    if emit_u:
        outs[n_w][...] = uf
    u = uf.astype(BF16)
    for w_ref, o_ref, act in zip(w_refs, outs[:n_w], acts):
        n = w_ref.shape[0]
        for jc, j in enumerate(range(0, n, 1024)):
            cs = slice(j, min(j + 1024, n))
            o_ref[:, cs] = _act(_dot_nt(u, w_ref[cs, :]), act[jc] if act else None).astype(o_ref.dtype)


def _inproj(x2, mod3, mod_row, norm_g, weights, out_dtypes, tm, acts=None, emit_u=False, casts=()):
    m = x2.shape[0]
    steps = m // tm
    acts = acts or (None,) * len(weights)
    widths = [w.shape[0] for w in weights] + ([D] if emit_u else [])
    dtypes = list(out_dtypes) + ([F32] if emit_u else [])
    slab = lambda a, rows: pl.BlockSpec((rows, a.shape[1]), lambda i: (i * (a.shape[0] // rows) // steps, 0))
    cast_specs = [slab(a, rows) for a, rows in casts]
    return _pallas(
        functools.partial(_inproj_kernel, acts=acts, emit_u=emit_u, n_cast=len(casts)), name="inproj",
        grid=(steps,),
        in_specs=[pl.BlockSpec((tm, D), lambda i: (i, 0)),
                  pl.BlockSpec((None, 1, D), lambda i: (mod_row(i), 0, 0)),
                  pl.BlockSpec((None, 1, D), lambda i: (mod_row(i), 0, 1)),
                  pl.BlockSpec((1, D), lambda i: (0, 0))] + [_resident(w.shape) for w in weights] + cast_specs,
        operands=(x2, mod3, mod3, norm_g, *weights, *(a for a, _ in casts)),
        out_shape=(tuple(jax.ShapeDtypeStruct((m, n), dt) for n, dt in zip(widths, dtypes))
                   + tuple(jax.ShapeDtypeStruct(a.shape, BF16) for a, _ in casts)),
        out_specs=tuple(pl.BlockSpec((tm, n), lambda i: (i, 0)) for n in widths) + tuple(cast_specs))


def _inproj_m_kernel(*refs, colmajor, nblk):
    if colmajor:
        u_ref, hp_ref, hn_ref, wm_ref, wtg_ref, cw_ref, cb_ref = refs[:7]
        u = jnp.concatenate([u_ref[:, cl, :] for cl in range(COL_BLOCK)] + [hp_ref[7], hn_ref[0]],
                            axis=0).astype(BF16)
        k_ref, qt_ref, kt_ref, v_ref, tg_ref = refs[7:]
    else:
        x_ref, sh_ref, sc_ref, g_ref, wm_ref, wtg_ref, cw_ref, cb_ref, wg_ref, wtl_ref = refs[:10]
        k_ref, qt_ref, kt_ref, v_ref, tg_ref, pg_ref, tlr_ref = refs[10:]
        u = _norm_mod(x_ref[...], g_ref[...], sh_ref[...], sc_ref[...])
        pg_ref[...] = _dot_nt(u, wg_ref[...]).astype(BF16)
        tlr_ref[...] = _dot_nt(u, wtl_ref[...])
    n = k_ref.shape[0]
    um = u[0:n]
    pres = [_dot_nt(u, wm_ref[c * 512:(c + 1) * 512, :]) for c in range(2)]
    v_ref[...] = _dot_nt(um, wm_ref[1024:2048, :]).astype(BF16)
    tg_ref[...] = _dot_nt(um, wtg_ref[...])
    row8 = lax.broadcasted_iota(jnp.int32, (8, 512), 0)
    j = pl.program_id(0) % nblk
    for c, pre in enumerate(pres):
        cs = slice(c * 512, (c + 1) * 512)
        a = pre[0:n]
        if colmajor:
            prev_row = jnp.where(j > 0, pre[n + 7:n + 8], 0.0)
            next_row = jnp.where(j < nblk - 1, pre[n + 8:n + 9], 0.0)
        else:
            prev_row = next_row = jnp.zeros((1, 512), F32)
        ap = pltpu.roll(a, 1, axis=0)
        ap = jnp.concatenate([jnp.where(row8 == 0, prev_row, ap[0:8]), ap[8:]], axis=0)
        an = pltpu.roll(a, n - 1, axis=0)
        an = jnp.concatenate([an[0:n - 8], jnp.where(row8 == 7, next_row, an[n - 8:])], axis=0)
        conv = ap * cw_ref[0:1, cs] + a * cw_ref[1:2, cs] + an * cw_ref[2:3, cs] + cb_ref[:, cs]
        y = conv * _sigmoid(conv)
        if c == 0:
            qt_ref[...] = y.T.astype(BF16)
        else:
            y = y * (DK ** -0.5)
            k_ref[...] = y.astype(BF16)
            kt_ref[...] = y.T.astype(BF16)


def _inproj_m(xv, mod3, mod_row, norm_g, wm, wtg, conv_w, conv_b, tm, colmajor, gla_weights=()):
    full = lambda shape: pl.BlockSpec(shape, lambda i: (0,) * len(shape))
    mod_specs = [pl.BlockSpec((None, 1, D), lambda i: (mod_row(i), 0, 0)),
                 pl.BlockSpec((None, 1, D), lambda i: (mod_row(i), 0, 1)), full((1, D))]
    mod_args = (mod3, mod3, norm_g)
    if colmajor:
        mod_specs, mod_args = [], ()
        bsz = xv.shape[0]
        tn = SEQ
        nblk = GRID_W // COL_BLOCK
        blk = (None, GRID_W, COL_BLOCK, D)
        halo = (None, 8, COL_BLOCK, D)
        x_specs = [pl.BlockSpec(blk, lambda i: (i // nblk, 0, i % nblk, 0)),
                   pl.BlockSpec(halo, lambda i: (i // nblk, GRID_W // 8 - 1, jnp.maximum(i % nblk - 1, 0), 0)),
                   pl.BlockSpec(halo, lambda i: (i // nblk, 0, jnp.minimum(i % nblk + 1, nblk - 1), 0))]
        xs = (xv, xv, xv)
    else:
        tn = tm
        bsz = xv.shape[0] // tn
        nblk = 1
        x_specs = [pl.BlockSpec((tm, D), lambda i: (i, 0))]
        xs = (xv,)
    extra_shapes = tuple(jax.ShapeDtypeStruct((bsz, tn, w.shape[0]), dt) for w, dt in zip(gla_weights, (BF16, F32)))
    tok = lambda w: pl.BlockSpec((None, tm, w), lambda i: (i // nblk, i % nblk, 0))
    tr = pl.BlockSpec((None, None, 512, tm), lambda i: (i // nblk, i % nblk, 0, 0))
    sds = jax.ShapeDtypeStruct
    return _pallas(
        functools.partial(_inproj_m_kernel, colmajor=colmajor, nblk=nblk),
        name="inproj_m_cm" if colmajor else "inproj_m",
        grid=(bsz * nblk,),
        in_specs=x_specs + mod_specs + [_resident(wm.shape), _resident(wtg.shape), full((3, 1024)),
                                        full((1, 1024))] + [_resident(w.shape) for w in gla_weights],
        operands=(*xs, *mod_args, wm, wtg, conv_w, conv_b, *gla_weights),
        out_shape=(sds((bsz, tn, 512), BF16), sds((bsz, nblk, 512, tm), BF16), sds((bsz, nblk, 512, tm), BF16),
                   sds((bsz, tn, 1024), BF16), sds((bsz, tn, 256), F32)) + extra_shapes,
        out_specs=(tok(512), tr, tr, tok(1024), tok(256)) + tuple(tok(w.shape[0]) for w in gla_weights))


def _sum_directions(acc_ref, out_ref, blk, rs, cs, val):
    tot = acc_ref[blk, rs, cs] + val
    acc_ref[blk, rs, cs] = tot
    out_ref[rs, cs] = tot.astype(out_ref.dtype)


def _gla_kernel(pf_ref, pb_ref, tf_ref, tb_ref, cp_ref, ct_ref, wup_ref, bdec_ref, olo_ref, ohi_ref,
                st_ref, acc_ref, *, ns):
    i = pl.program_id(1)

    @pl.when(i == 0)
    def _():
        st_ref[...] = jnp.zeros(st_ref.shape, F32)
        acc_ref[...] = jnp.zeros(acc_ref.shape, F32)
        ctx = (cp_ref, ct_ref, None)
        _gla_step((ctx, ctx), wup_ref, bdec_ref, st_ref, None, None, emit_out=False)

    _gla_step(((pf_ref, tf_ref, ohi_ref), (pb_ref, tb_ref, olo_ref)), wup_ref, bdec_ref, st_ref, acc_ref,
              (i, ns - 1 - i), emit_out=True)


def _gla_step(dirs, wup_ref, bdec_ref, st_ref, acc_ref, blks, *, emit_out):
    lc = GLA_CHUNK
    units = _scan_units(dirs[0][0].shape[0] // lc)
    masks =[_tri(lc, d) for d in range(2)]
    gs = []
    for d, (p_ref, t_ref, o_ref) in enumerate(dirs):
        z = _dot(t_ref[...].astype(BF16), wup_ref[d]) + bdec_ref[d]
        gs.append(_log2_sigmoid(z) * (1.0 / TAU))
    bs, ops, sc, us, dcols = {}, {}, {}, {}, {}
    st = {(d, h): st_ref[d * HEADS + h] for d in range(2) for h in range(HEADS)}

    def stage2(d, j):
        bs[d, j] = _cumsum_rows(masks[d][1], gs[d][j * lc:(j + 1) * lc])

    def stage3(d, j):
        p_ref = dirs[d][0]
        rs = slice(j * lc, (j + 1) * lc)
        b = bs[d, j]
        b_last = b[lc - 1:lc, :] if d == 0 else b[0:1, :]
        b_mid = b[lc // 2 - 1:lc // 2, :] if d == 0 else b[lc // 2:lc // 2 + 1, :]
        q = p_ref[rs, 0:512]
        k = p_ref[rs, 512:1024]
        qd = q * jnp.exp2(b - b_mid).astype(BF16)
        kd = k * jnp.exp2((b_mid + LOG2_QSCALE) - b).astype(BF16)
        qi = qd * jnp.exp2(b_mid).astype(BF16)
        kl = kd * jnp.exp2(b_last - b_mid).astype(BF16)
        dec = jnp.exp2(b_last)
        ops[d, j] = (qi, kl, dec, qd, kd)

    def stage4(d, j):
        p_ref = dirs[d][0]
        rs = slice(j * lc, (j + 1) * lc)
        qi, kl, dec, qd, kd = ops[d, j]
        for h in range(HEADS):
            ks = slice(h * DK, (h + 1) * DK)
            v = p_ref[rs, 1024 + h * DV:1024 + (h + 1) * DV]
            if emit_out:
                sc[d, j, h] = jnp.where(masks[d][0], _dot_nt(qd[:, ks], kd[:, ks]), 0.0).astype(BF16)
            us[d, j, h] = _dot_tn(kl[:, ks], v)
            dcols[d, j, h] = jnp.broadcast_to(dec[:, ks], (8, DK)).T[:, 0:1]
    def stage5(d, j):
        p_ref, _, o_ref = dirs[d]
        rs = slice(j * lc, (j + 1) * lc)
        for h in range(HEADS):
            ks = slice(h * DK, (h + 1) * DK)
            if emit_out:
                v = p_ref[rs, 1024 + h * DV:1024 + (h + 1) * DV]
                o = _dot(jnp.concatenate([sc[d, j, h], ops[d, j][0][:, ks]], axis=1),
                         jnp.concatenate([v, st[d, h].astype(BF16)], axis=0))
                _sum_directions(acc_ref, o_ref, blks[d], rs, slice(h * DV, (h + 1) * DV), o)
            st[d, h] = st[d, h] * dcols[d, j, h] + us[d, j, h]

    _skewed(units, (stage2, stage3, stage4, stage5))
    for (d, h), val in st.items():
        st_ref[d * HEADS + h] = val


def _half_specs(ns, step, width):
    half = ns // 2
    lo = pl.BlockSpec((None, step, width), lambda b, i: (b, jnp.minimum(ns - 1 - i, half - 1), 0))
    hi = pl.BlockSpec((None, step, width), lambda b, i: (b, jnp.maximum(i - half, 0), 0))
    return lo, hi


def _gla_scan(pg, tlr, pg_c, tlr_c, wup, bdec):
    bn, tn, _ = pg.shape
    tc = pg_c.shape[1]
    step = min(GLA_STEP, tn)
    ns = tn // step
    fwd = lambda b, i: (b, i, 0)
    bwd = lambda b, i: (b, ns - 1 - i, 0)
    ctx = lambda b, i: (b, 0, 0)
    o_shape = jax.ShapeDtypeStruct((bn, tn // 2, HEADS * DV), BF16)
    return _pallas(
        functools.partial(_gla_kernel, ns=ns), name="gla_scan",
        grid=(bn, ns),
        in_specs=[pl.BlockSpec((None, step, 2048), fwd),
                  pl.BlockSpec((None, step, 2048), bwd),
                  pl.BlockSpec((None, step, 128), fwd),
                  pl.BlockSpec((None, step, 128), bwd),
                  pl.BlockSpec((None, tc, 2048), ctx),
                  pl.BlockSpec((None, tc, 128), ctx),
                  pl.BlockSpec((2, 128, 512), lambda b, i: (0, 0, 0)),
                  pl.BlockSpec((2, 1, 512), lambda b, i: (0, 0, 0))],
        operands=(pg, pg, tlr, tlr, pg_c, tlr_c, wup, bdec),
        out_shape=(o_shape, o_shape),
        out_specs=_half_specs(ns, step, HEADS * DV),
        scratch=[pltpu.VMEM((2 * HEADS, DK, DV), F32), pltpu.VMEM((ns, step, HEADS * DV), F32)])


def _mlstm_kernel(kf_ref, kb_ref, qtf_ref, qtb_ref, ktf_ref, ktb_ref, vf_ref, vb_ref, tf_ref, tb_ref,
                  kc_ref, qtc_ref, ktc_ref, vc_ref, tc_ref, bg_ref, hlo_ref, hhi_ref,
                  c_ref, n_ref, m_ref, acc_ref, *, ns):
    i = pl.program_id(1)

    @pl.when(i == 0)
    def _():
        c_ref[...] = jnp.zeros(c_ref.shape, F32)
        n_ref[...] = jnp.zeros(n_ref.shape, F32)
        m_ref[...] = jnp.zeros(m_ref.shape, F32)
        acc_ref[...] = jnp.zeros(acc_ref.shape, F32)
        ctx = (kc_ref, qtc_ref, ktc_ref, vc_ref, tc_ref, None)
        _mlstm_step((ctx, ctx), bg_ref, c_ref, n_ref, m_ref, None, None, emit_out=False)

    _mlstm_step(((kf_ref, qtf_ref, ktf_ref, vf_ref, tf_ref, hhi_ref),
                 (kb_ref, qtb_ref, ktb_ref, vb_ref, tb_ref, hlo_ref)),
                bg_ref, c_ref, n_ref, m_ref, acc_ref, (i, ns - 1 - i), emit_out=True)


def _mlstm_step(dirs, bg_ref, c_ref, n_ref, m_ref, acc_ref, blks, *, emit_out):
    lc = MLSTM_CHUNK
    units = _scan_units(dirs[0][0].shape[0] // lc)
    tris =[_tri(lc, d) for d in range(2)]
    lane_of = lambda d, h: GATE_LANE + 8 * d + h
    hs = lambda h: slice(h * DK, (h + 1) * DK)
    vs = lambda h: slice(h * DV, (h + 1) * DV)

    mrow = [m_ref[d:d + 1, :] for d in range(2)]
    tiles = {}
    for d, j in units:
        t_ref = dirs[d][4]
        rs = slice(j * lc, (j + 1) * lc)
        ga = (t_ref[rs, 0:128] + bg_ref[:, 0:128]) * LOG2E
        gb = t_ref[rs, 128:256] + bg_ref[:, 128:256]
        bc = _cumsum_rows(tris[d][1], _log2_sigmoid(gb))
        b_last = bc[lc - 1:lc, :] if d == 0 else bc[0:1, :]
        log_key = b_last - bc + ga
        m_new = jnp.maximum(b_last + mrow[d], jnp.max(log_key, axis=0, keepdims=True))
        tiles[d, j] = dict(rmat=ga - bc, bct=bc.T, m_in=mrow[d], wkt=jnp.exp2(log_key - m_new).T,
                           decay=jnp.exp2(b_last + mrow[d] - m_new))
        mrow[d] = m_new
    for d in range(2):
        m_ref[d:d + 1, :] = mrow[d]

    us, ncols = {}, {}
    for d, j in units:
        k_ref, _, kt_ref, v_ref = dirs[d][:4]
        rs = slice(j * lc, (j + 1) * lc)
        for h in range(HEADS):
            lane = lane_of(d, h)
            wk = tiles[d, j]["wkt"][lane:lane + 1, :]
            kwt = kt_ref[hs(h), rs] * wk.astype(BF16)
            us[d, j, h] = _dot(kwt, v_ref[rs, vs(h)])
            ncols[d, j, h] = _dot(jnp.broadcast_to(wk, (16, lc)).astype(BF16), k_ref[rs, hs(h)])[0:1]

    n_in = {}
    for d in range(2):
        for h in range(HEADS):
            idx = d * HEADS + h
            lane = lane_of(d, h)
            nvec = n_ref[idx:idx + 1, :]
            for dd, j in units:
                if dd == d:
                    n_in[d, j, h] = nvec
                    nvec = tiles[d, j]["decay"][:, lane:lane + 1] * nvec + ncols[d, j, h]
            n_ref[idx:idx + 1, :] = nvec

    lhs = {}
    if emit_out:
        for d, j in units:
            k_ref, qt_ref = dirs[d][:2]
            rs = slice(j * lc, (j + 1) * lc)
            t = tiles[d, j]
            causal_t = tris[1 - d][0]
            for h in range(HEADS):
                lane = lane_of(d, h)
                qt = qt_ref[hs(h), rs]
                kq = _dot(jnp.concatenate(
                    [k_ref[rs, hs(h)], jnp.broadcast_to(n_in[d, j, h], (16, DK)).astype(BF16)], axis=0), qt)
                rm = jnp.where(causal_t, t["rmat"][:, lane:lane + 1], -jnp.inf)
                mval = t["m_in"][:, lane:lane + 1]
                mx = jnp.maximum(mval, jnp.max(rm, axis=0, keepdims=True))
                wt = jnp.exp2(rm - mx) * kq[0:lc]
                w_inter = jnp.exp2(mval - mx)
                den = jnp.sum(wt, axis=0, keepdims=True) + w_inter * kq[lc:lc + 1]
                inv = 1.0 / jnp.maximum(jnp.abs(den), jnp.exp2(-(t["bct"][lane:lane + 1, :] + mx)))
                lhs[d, j, h] = jnp.concatenate(
                    [(wt * inv).astype(BF16), qt * (w_inter * inv).astype(BF16)], axis=0)

    for d in range(2):
        v_ref, o_ref = dirs[d][3], dirs[d][5]
        for h in range(HEADS):
            idx = d * HEADS + h
            lane = lane_of(d, h)
            cmat = c_ref[idx]
            for dd, j in units:
                if dd != d:
                    continue
                rs = slice(j * lc, (j + 1) * lc)
                if emit_out:
                    o = _dot_tn(lhs[d, j, h], jnp.concatenate([v_ref[rs, vs(h)], cmat.astype(BF16)], axis=0))
                    _sum_directions(acc_ref, o_ref, blks[d], rs, vs(h), o)
                cmat = tiles[d, j]["decay"][:, lane:lane + 1] * cmat + us[d, j, h]
            c_ref[idx] = cmat


def _mlstm_scan(lat, ctx, bgate):
    k, qt, kt, v, tg = lat
    bn, tn, _ = k.shape
    tc = ctx[0].shape[1]
    step = min(STEP, tn)
    ns = tn // step
    fwd = lambda b, i: (b, i, 0)
    bwd = lambda b, i: (b, ns - 1 - i, 0)
    assert qt.shape[1:] == (ns, 512, step) and ctx[1].shape[1:] == (1, 512, tc)
    fwd_t = lambda b, i: (b, i, 0, 0)
    bwd_t = lambda b, i: (b, ns - 1 - i, 0, 0)
    whole = lambda b, i: (b, 0, 0)
    o_shape = jax.ShapeDtypeStruct((bn, tn // 2, HEADS * DV), F32)
    both = lambda shape, f, g: [pl.BlockSpec(shape, f), pl.BlockSpec(shape, g)]
    whole_t = lambda b, i: (b, 0, 0, 0)
    ctx_specs = [pl.BlockSpec((None, tc, 512), whole), pl.BlockSpec((None, None, 512, tc), whole_t),
                 pl.BlockSpec((None, None, 512, tc), whole_t), pl.BlockSpec((None, tc, 1024), whole),
                 pl.BlockSpec((None, tc, 256), whole)]
    return _pallas(
        functools.partial(_mlstm_kernel, ns=ns), name="mlstm_scan",
        grid=(bn, ns),
        in_specs=(both((None, step, 512), fwd, bwd) + both((None, None, 512, step), fwd_t, bwd_t)
                  + both((None, None, 512, step), fwd_t, bwd_t) + both((None, step, 1024), fwd, bwd)
                  + both((None, step, 256), fwd, bwd) + ctx_specs
                  + [pl.BlockSpec((1, 256), lambda b, i: (0, 0))]),
        operands=(k, k, qt, qt, kt, kt, v, v, tg, tg, *ctx, bgate),
        out_shape=(o_shape, o_shape),
        out_specs=_half_specs(ns, step, HEADS * DV),
        scratch=[pltpu.VMEM((2 * HEADS, DK, DV), F32), pltpu.VMEM((2 * HEADS, 128), F32),
                 pltpu.VMEM((8, 128), F32), pltpu.VMEM((ns, step, HEADS * DV), F32)])


def _head_norm(o, g):
    parts = []
    for h in range(HEADS):
        oh = o[:, h * DV:(h + 1) * DV]
        parts.append(oh * lax.rsqrt(jnp.mean(oh * oh, axis=-1, keepdims=True) + EPS))
    return jnp.concatenate(parts, axis=-1) * g


def _merge_kernel(olo_ref, ohi_ref, hlo_ref, hhi_ref, po_ref, x_ref, g1_ref, gg_ref, gm_ref,
                  wbg_ref, wbm_ref, wo_ref, o_ref, *, per_b):
    groups = [(g * COL_BLOCK // MERGE_GROUPS, (g + 1) * COL_BLOCK // MERGE_GROUPS) for g in range(MERGE_GROUPS)]
    rows = [slice(a * GRID_W, b * GRID_W) for a, b in groups]
    lower = pl.program_id(0) % per_b < per_b // 2
    ys = []
    for (a, b), rs in zip(groups, rows):
        hm = jnp.concatenate([r[:, rl, :] for rl in range(a, b) for r in (hlo_ref, hhi_ref)], axis=0)
        o = jnp.where(lower, olo_ref[rs, :], ohi_ref[rs, :]).astype(F32)
        y_gla = _head_norm(o, gg_ref[...]) * po_ref[rs, 0:1024].astype(F32)
        y_m = _head_norm(hm, gm_ref[...]) * po_ref[rs, 1024:2048].astype(F32)
        ys.append((y_gla.astype(BF16), y_m.astype(BF16)))
    ds = [(_dot(y_gla, wbg_ref[...]), _dot(y_m, wbm_ref[...])) for y_gla, y_m in ys]
    ys = [(po_ref[rs, 2048:3072].astype(F32) * d_g + po_ref[rs, 3072:4096].astype(F32) * d_m).astype(BF16)
          for rs, (d_g, d_m) in zip(rows, ds)]
    mixes = [_dot(y, wo_ref[...]) for y in ys]
    for rs, mix in zip(rows, mixes):
        o_ref[rs, :] = x_ref[rs, :] + g1_ref[...] * mix


def _merge(olo, ohi, hlo4, hhi4, po, x2, mod3, gg, gm, wbg, wbm, wo):
    m = x2.shape[0]
    tm = GRID_W * COL_BLOCK
    per_b = SEQ // tm
    half = per_b // 2
    tok = lambda i: (i, 0)
    hspec = pl.BlockSpec((None, GRID_W // 2, COL_BLOCK, D), lambda i: (i // per_b, 0, i % per_b, 0))
    lo_spec = pl.BlockSpec((None, tm, D), lambda i: (i // per_b, jnp.minimum(i % per_b, half - 1), 0))
    hi_spec = pl.BlockSpec((None, tm, D), lambda i: (i // per_b, jnp.maximum(i % per_b - half, 0), 0))
    return _pallas(
        functools.partial(_merge_kernel, per_b=per_b), name="merge",
        grid=(m // tm,),
        in_specs=[lo_spec, hi_spec, hspec, hspec,
                  pl.BlockSpec((tm, 4096), tok), pl.BlockSpec((tm, D), tok),
                  pl.BlockSpec((None, 1, D), lambda i: (i // per_b, 0, 2)),
                  pl.BlockSpec((1, D), lambda i: (0, 0)), pl.BlockSpec((1, D), lambda i: (0, 0)),
                  _resident((D, D)), _resident((D, D)), _resident((D, D))],
        operands=(olo, ohi, hlo4, hhi4, po, x2, mod3, gg, gm, wbg, wbm, wo),
        out_shape=jax.ShapeDtypeStruct((m, D), F32),
        out_specs=pl.BlockSpec((tm, D), tok))


FF_TILES = ((0, 1280), (1280, 2816))


def _ffn_kernel(x_ref, sh_ref, sc_ref, g2_ref, ng_ref, fg_ref, wi_ref, wo_ref, o_ref):
    tm = x_ref.shape[0]
    rows = [slice(g * tm // FFN_GROUPS, (g + 1) * tm // FFN_GROUPS) for g in range(FFN_GROUPS)]
    us = [_norm_mod(x_ref[rs, :], ng_ref[...], sh_ref[...], sc_ref[...]) for rs in rows]
    accs = [None] * FFN_GROUPS
    for lo, hi in FF_TILES:
        abs_ = [(_dot(u, wi_ref[:, lo:hi]), _dot(u, wi_ref[:, D_FF + lo:D_FF + hi])) for u in us]
        hids = [(a * _sigmoid(a) * b).astype(BF16) for a, b in abs_]
        parts = [_dot(hid, wo_ref[lo:hi, :]) for hid in hids]
        accs = [p if acc is None else acc + p for acc, p in zip(accs, parts)]
    for rs, acc in zip(rows, accs):
        x2 = x_ref[rs, :] + g2_ref[...] * acc
        o_ref[rs, :] = x2 * lax.rsqrt(jnp.mean(x2 * x2, axis=-1, keepdims=True) + EPS) * fg_ref[...]


def _ffn(x1, mod3, ng, fg, wi, wo, tm):
    m = x1.shape[0]
    per_b = SEQ // tm
    tok = lambda i: (i, 0)
    modspec = lambda c: pl.BlockSpec((None, 1, D), lambda i: (i // per_b, 0, c))
    return _pallas(
        _ffn_kernel, name="ffn",
        grid=(m // tm,),
        in_specs=[pl.BlockSpec((tm, D), tok), modspec(3), modspec(4), modspec(5),
                  pl.BlockSpec((1, D), lambda i: (0, 0)), pl.BlockSpec((1, D), lambda i: (0, 0)),
                  _resident((D, 2 * D_FF)), _resident((D_FF, D))],
        operands=(x1, mod3, mod3, mod3, ng, fg, wi, wo),
        out_shape=jax.ShapeDtypeStruct((m, D), F32),
        out_specs=pl.BlockSpec((tm, D), tok))


def _split_w_in_kernel(w_ref, wg_ref, wm_ref, wo_ref, wtl_ref, wtg_ref):
    cols = w_ref.shape[1]
    c = lambda a, b: w_ref[a:b, :].astype(BF16)
    z = lambda n: jnp.zeros((n, cols), BF16)
    wg_ref[...] = c(0, 2048)
    wm_ref[...] = c(3104, 5152)
    wo_ref[0:1024, :] = c(2048, 3072)
    wo_ref[1024:2048, :] = c(5152, 6176)
    wo_ref[2048:4096, :] = c(6192, 8240)
    wtl_ref[...] = jnp.concatenate([c(3072, 3104), z(128 - 2 * RANK)], axis=0)
    wtg_ref[...] = jnp.concatenate(
        [z(GATE_LANE), c(6176, 6192), z(128 - GATE_LANE - 16),
         z(GATE_LANE), c(6180, 6184), z(4), c(6188, 6192), z(128 - GATE_LANE - 12)], axis=0)


def _split_w_in(w_in_t):
    tc = 256
    heights = (2048, 2048, 4096, 128, 256)
    return _pallas(
        _split_w_in_kernel, name="split_w_in",
        grid=(D // tc,),
        in_specs=[pl.BlockSpec((w_in_t.shape[0], tc), lambda i: (0, i))],
        operands=(w_in_t,),
        out_shape=tuple(jax.ShapeDtypeStruct((n, D), BF16) for n in heights),
        out_specs=tuple(pl.BlockSpec((n, tc), lambda i: (0, i)) for n in heights))


def kernel(x, c, ctx, c_ctx, w_ada, b_ada, norm1_g, w_in, gla_w_up, gla_b_dec, gla_norm_g,
           mlstm_conv_w, mlstm_conv_b, mlstm_b_gate, mlstm_norm_g, w_br_gla, w_br_mlstm, w_out,
           norm2_g, w_ffn_in, w_ffn_out, final_g):
    bsz = x.shape[0]
    assert x.shape == (bsz, SEQ, D) and ctx.shape == (bsz, CTX, D) and bsz + 1 <= 8
    assert w_in.shape == (1, D, 8240) and w_ffn_in.shape == (1, D, 2 * D_FF)
    row = lambda a: a.reshape(1, -1)

    cvec = jnp.concatenate([c, c_ctx[None, :], jnp.zeros((8 - bsz - 1, D), F32)], axis=0)
    mod3 = _ada(cvec, w_ada[0], row(b_ada[0])).reshape(8, 1, N_MOD)

    wg, wm, wo, wtl, wtg = _split_w_in(w_in[0].T)
    x2 = x.reshape(bsz * SEQ, D)
    ctx2 = ctx.reshape(bsz * CTX, D)
    g1n = row(norm1_g[0])
    tm = GRID_W * COL_BLOCK
    steps = bsz * SEQ // tm
    casts = ((w_br_gla[0], D // steps), (w_br_mlstm[0], D // steps), (w_out[0], D // steps),
             (w_ffn_in[0], D // steps), (w_ffn_out[0], 2 * D_FF // steps))
    po, pg, tlr, u_lat, wbg, wbm, wout, wfi, wfo = _inproj(
        x2, mod3, lambda i: i // (SEQ // tm), g1n, (wo, wg, wtl), (BF16, BF16, F32), tm,
        acts=(("silu", "sigmoid", "sigmoid", "sigmoid"), None, None), emit_u=True, casts=casts)
    conv_w = mlstm_conv_w[0]
    conv_b = row(mlstm_conv_b[0])
    m_lat = _inproj_m(u_lat.reshape(bsz, SEQ // GRID_W, GRID_W, D), None, None, None,
                      wm, wtg, conv_w, conv_b, tm, True)
    *m_ctx, pg_c, tlr_c = _inproj_m(ctx2, mod3, lambda i: bsz, g1n, wm, wtg, conv_w, conv_b, CTX, False,
                                    gla_weights=(wg, wtl))

    wup = jnp.zeros((2, 128, HEADS * DK), F32)
    wup = wup.at[0, 0:RANK].set(gla_w_up[0, 0]).at[1, RANK:2 * RANK].set(gla_w_up[0, 1]).astype(BF16)
    bdec = gla_b_dec[0].reshape(2, 1, HEADS * DK)
    o_lo, o_hi = _gla_scan(pg.reshape(bsz, SEQ, 2048), tlr.reshape(bsz, SEQ, 128),
                           pg_c.reshape(bsz, CTX, 2048), tlr_c.reshape(bsz, CTX, 128), wup, bdec)

    bgate = mlstm_b_gate[0].reshape(1, 16)
    zg = lambda n: jnp.zeros((1, n), F32)
    bg2 = jnp.concatenate([zg(GATE_LANE), bgate, zg(128 - GATE_LANE - 16),
                           zg(GATE_LANE), bgate[:, 4:8], zg(4), bgate[:, 12:16],
                           zg(128 - GATE_LANE - 12)], axis=1)
    h_lo, h_hi = _mlstm_scan(m_lat, m_ctx, bg2)

    cm4 = lambda a: a.reshape(bsz, GRID_W // 2, SEQ // GRID_W, D)
    x1 = _merge(o_lo, o_hi, cm4(h_lo), cm4(h_hi), po, x2, mod3,
                row(gla_norm_g[0]), row(mlstm_norm_g[0]), wbg, wbm, wout)
    out = _ffn(x1, mod3, row(norm2_g[0]), row(final_g), wfi, wfo, FFN_ROWS)
    return out.reshape(bsz, SEQ, D)
```
